```python
import jax, jax.numpy as jnp
from jax import lax
import numpy as np

D_MODEL = 2048
BATCH = 8
SEQ = 4096
DEPTH = 1

CHUNK = 64
Q_BLOCK = 128
PLE_DIM = 256
FOX_HEADS = 8
FOX_HEAD_DIM = D_MODEL // 16
GLA_HEADS = 4
GLA_KEY_DIM = D_MODEL // 16
GLA_VAL_DIM = D_MODEL // 8
GLA_GATE_RANK = 16
GLA_GATE_TAU = 16.0
D_FF = ((8 * D_MODEL // 3 + 255) // 256) * 256
EPS = 1e-6

FOX_W = FOX_HEADS * FOX_HEAD_DIM
GLA_KW = GLA_HEADS * GLA_KEY_DIM
GLA_VW = GLA_HEADS * GLA_VAL_DIM
IN_SPLITS = (FOX_W, FOX_W, FOX_W, FOX_HEADS, GLA_KW, GLA_KW, GLA_VW, GLA_VW, GLA_GATE_RANK)
D_IN = sum(IN_SPLITS)

kernel_name = "hybrid_fox_gla_macaron_ple"


def rms_norm(x, g):
    xf = x.astype(jnp.float32)
    y = xf * lax.rsqrt(jnp.mean(xf * xf, axis=-1, keepdims=True) + EPS)
    return (y * g.astype(jnp.float32)).astype(x.dtype)


def swiglu(x, w_gate, w_up, w_down):
    return (jax.nn.silu(x @ w_gate) * (x @ w_up)) @ w_down


def split_cols(t, sizes):
    out, start = [], 0
    for s in sizes:
        out.append(t[..., start:start + s])
        start += s
    return out


def forgetting_attention(q, k, v, log_f):
    b, s, h, d = q.shape
    nb = s // Q_BLOCK
    f_cum = jnp.cumsum(log_f.astype(jnp.float32), axis=1).transpose(0, 2, 1)
    q_blocks = q.reshape(b, nb, Q_BLOCK, h, d).transpose(1, 0, 2, 3, 4)
    fq_blocks = f_cum.reshape(b, h, nb, Q_BLOCK).transpose(2, 0, 1, 3)
    key_pos = jnp.arange(s)
    scale = d ** -0.5

    def block(args):
        q_i, fq_i, i = args
        logits = jnp.einsum('bqhd,bkhd->bhqk', q_i, k).astype(jnp.float32) * scale
        logits = logits + fq_i[..., :, None] - f_cum[:, :, None, :]
        q_pos = i * Q_BLOCK + jnp.arange(Q_BLOCK)
        causal = key_pos[None, :] <= q_pos[:, None]
        logits = jnp.where(causal, logits, -jnp.inf)
        probs = jax.nn.softmax(logits, axis=-1).astype(v.dtype)
        return jnp.einsum('bhqk,bkhd->bqhd', probs, v)

    out = lax.map(block, (q_blocks, fq_blocks, jnp.arange(nb)))
    return out.transpose(1, 0, 2, 3, 4).reshape(b, s, h * d)


def gla_chunk_causal(q, k, v, log_a):
    b, s, h, dk = q.shape
    dv = v.shape[-1]
    nc = s // CHUNK

    def chunks(t):
        return t.astype(jnp.float32).reshape(b, nc, CHUNK, h, t.shape[-1]).transpose(1, 0, 2, 3, 4)

    qc, kc, vc, ac = chunks(q), chunks(k), chunks(v), chunks(log_a)
    a_cum = jnp.cumsum(ac, axis=2)
    a_tot = a_cum[:, :, -1]
    k_dec = kc * jnp.exp(a_tot[:, :, None] - a_cum)
    qc = qc * (dk ** -0.5)

    def step(state, inp):
        q_c, k_c, v_c, a_c = inp
        state = jnp.exp(a_c)[..., None] * state + jnp.einsum('bchk,bchv->bhkv', k_c, v_c)
        o = jnp.einsum('bchk,bhkv->bchv', q_c, state)
        return state, o

    state0 = jnp.zeros((b, h, dk, dv), jnp.float32)
    _, o = lax.scan(step, state0, (qc, k_dec, vc, a_tot))
    return o.transpose(1, 0, 2, 3, 4).reshape(b, s, h, dv)


def _fwd_setup_inputs(seed: int = 0) -> dict:
    key = jax.random.key(seed)
    ks = iter(jax.random.split(key, 32))
    f32 = jnp.float32

    def w(shape, fan_in):
        return jax.random.normal(next(ks), (DEPTH,) + shape, f32) * (fan_in ** -0.5)

    def gain(shape):
        return 1.0 + 0.05 * jax.random.normal(next(ks), shape, f32)

    def bias(shape, mean=0.0, std=0.01):
        return mean + std * jax.random.normal(next(ks), shape, f32)

    return {
        "x": jax.random.normal(next(ks), (BATCH, SEQ, D_MODEL), f32),
        "p": jax.random.normal(next(ks), (DEPTH, BATCH, SEQ, PLE_DIM), f32),
        "ffn1_norm": gain((DEPTH, D_MODEL)),
        "ffn1_w_gate": w((D_MODEL, D_FF), D_MODEL),
        "ffn1_w_up": w((D_MODEL, D_FF), D_MODEL),
        "ffn1_w_down": w((D_FF, D_MODEL), D_FF),
        "mix_norm": gain((DEPTH, D_MODEL)),
        "w_in": w((D_MODEL, D_IN), D_MODEL),
        "fox_forget_bias": bias((DEPTH, FOX_HEADS), mean=3.0, std=0.1),
        "gla_gate_up": w((GLA_GATE_RANK, GLA_KW), GLA_GATE_RANK),
        "gla_gate_bias": bias((DEPTH, GLA_KW)),
        "gla_head_norm": gain((DEPTH, GLA_VAL_DIM)),
        "w_branch_fox": w((FOX_W, D_MODEL), FOX_W),
        "w_branch_gla": w((GLA_VW, D_MODEL), GLA_VW),
        "w_merge_gate": w((D_MODEL, 2 * D_MODEL), D_MODEL),
        "b_merge_gate": bias((DEPTH, 2 * D_MODEL)),
        "w_out": w((D_MODEL, D_MODEL), D_MODEL),
        "ffn2_norm": gain((DEPTH, D_MODEL)),
        "ffn2_w_gate": w((D_MODEL, D_FF), D_MODEL),
        "ffn2_w_up": w((D_MODEL, D_FF), D_MODEL),
        "ffn2_w_down": w((D_FF, D_MODEL), D_FF),
        "ple_norm": gain((DEPTH, D_MODEL)),
        "w_ple_proj": w((PLE_DIM, D_MODEL), PLE_DIM),
        "w_ple_gate": w((D_MODEL, D_MODEL), D_MODEL),
        "final_norm": gain((D_MODEL,)),
    }


def _fwd_reference(x, p, ffn1_norm, ffn1_w_gate, ffn1_w_up, ffn1_w_down, mix_norm, w_in,
              fox_forget_bias, gla_gate_up, gla_gate_bias, gla_head_norm,
              w_branch_fox, w_branch_gla, w_merge_gate, b_merge_gate, w_out,
              ffn2_norm, ffn2_w_gate, ffn2_w_up, ffn2_w_down,
              ple_norm, w_ple_proj, w_ple_gate, final_norm):
    b, s, _ = x.shape
    h = x
    for i in range(DEPTH):
        h = h + 0.5 * swiglu(rms_norm(h, ffn1_norm[i]), ffn1_w_gate[i], ffn1_w_up[i], ffn1_w_down[i])

        u = rms_norm(h, mix_norm[i])
        (fq, fk, fv, f_logit, gq, gk, gv, gr, g_down) = split_cols(u @ w_in[i], IN_SPLITS)

        log_f = jax.nn.log_sigmoid((f_logit + fox_forget_bias[i]).astype(jnp.float32))
        y_fox = forgetting_attention(
            fq.reshape(b, s, FOX_HEADS, FOX_HEAD_DIM),
            fk.reshape(b, s, FOX_HEADS, FOX_HEAD_DIM),
            fv.reshape(b, s, FOX_HEADS, FOX_HEAD_DIM),
            log_f.astype(x.dtype))

        log_a = jax.nn.log_sigmoid((g_down @ gla_gate_up[i] + gla_gate_bias[i]).astype(jnp.float32)) / GLA_GATE_TAU
        o_gla = gla_chunk_causal(
            gq.reshape(b, s, GLA_HEADS, GLA_KEY_DIM),
            gk.reshape(b, s, GLA_HEADS, GLA_KEY_DIM),
            gv.reshape(b, s, GLA_HEADS, GLA_VAL_DIM),
            log_a.reshape(b, s, GLA_HEADS, GLA_KEY_DIM))
        o_gla = rms_norm(o_gla, gla_head_norm[i]).reshape(b, s, GLA_VW).astype(x.dtype)
        y_gla = o_gla * jax.nn.silu(gr)

        gates = jax.nn.sigmoid(u @ w_merge_gate[i] + b_merge_gate[i])
        g_fox, g_gla = gates[..., :D_MODEL], gates[..., D_MODEL:]
        merged = g_fox * (y_fox @ w_branch_fox[i]) + g_gla * (y_gla @ w_branch_gla[i])
        h = h + merged @ w_out[i]

        h = h + 0.5 * swiglu(rms_norm(h, ffn2_norm[i]), ffn2_w_gate[i], ffn2_w_up[i], ffn2_w_down[i])

        ple_gate = jax.nn.sigmoid(rms_norm(h, ple_norm[i]) @ w_ple_gate[i])
        h = h + ple_gate * (p[i].astype(h.dtype) @ w_ple_proj[i])
    return rms_norm(h, final_norm)


import jax as _jax
import jax.numpy as _jnp

TWIN_FORMAT = 'train_step'
FWD_PARAMS = ['x', 'p', 'ffn1_norm', 'ffn1_w_gate', 'ffn1_w_up', 'ffn1_w_down', 'mix_norm', 'w_in', 'fox_forget_bias', 'gla_gate_up', 'gla_gate_bias', 'gla_head_norm', 'w_branch_fox', 'w_branch_gla', 'w_merge_gate', 'b_merge_gate', 'w_out', 'ffn2_norm', 'ffn2_w_gate', 'ffn2_w_up', 'ffn2_w_down', 'ple_norm', 'w_ple_proj', 'w_ple_gate', 'final_norm']
TWIN_WEIGHTS = ['ffn1_norm', 'ffn1_w_gate', 'ffn1_w_up', 'ffn1_w_down', 'mix_norm', 'w_in', 'fox_forget_bias', 'gla_gate_up', 'gla_gate_bias', 'gla_head_norm', 'w_branch_fox', 'w_branch_gla', 'w_merge_gate', 'b_merge_gate', 'w_out', 'ffn2_norm', 'ffn2_w_gate', 'ffn2_w_up', 'ffn2_w_down', 'ple_norm', 'w_ple_proj', 'w_ple_gate', 'final_norm']
TWIN_DIFF_INPUT = 'x'
TWIN_INPUTS = ['x', 'p', 'ffn1_norm', 'ffn1_w_gate', 'ffn1_w_up', 'ffn1_w_down', 'mix_norm', 'w_in', 'fox_forget_bias', 'gla_gate_up', 'gla_gate_bias', 'gla_head_norm', 'w_branch_fox', 'w_branch_gla', 'w_merge_gate', 'b_merge_gate', 'w_out', 'ffn2_norm', 'ffn2_w_gate', 'ffn2_w_up', 'ffn2_w_down', 'ple_norm', 'w_ple_proj', 'w_ple_gate', 'final_norm', 'loss_target', 'm_ffn1_norm', 'm_ffn1_w_gate', 'm_ffn1_w_up', 'm_ffn1_w_down', 'm_mix_norm', 'm_w_in', 'm_fox_forget_bias', 'm_gla_gate_up', 'm_gla_gate_bias', 'm_gla_head_norm', 'm_w_branch_fox', 'm_w_branch_gla', 'm_w_merge_gate', 'm_b_merge_gate', 'm_w_out', 'm_ffn2_norm', 'm_ffn2_w_gate', 'm_ffn2_w_up', 'm_ffn2_w_down', 'm_ple_norm', 'm_w_ple_proj', 'm_w_ple_gate', 'm_final_norm', 'v_ffn1_norm', 'v_ffn1_w_gate', 'v_ffn1_w_up', 'v_ffn1_w_down', 'v_mix_norm', 'v_w_in', 'v_fox_forget_bias', 'v_gla_gate_up', 'v_gla_gate_bias', 'v_gla_head_norm', 'v_w_branch_fox', 'v_w_branch_gla', 'v_w_merge_gate', 'v_b_merge_gate', 'v_w_out', 'v_ffn2_norm', 'v_ffn2_w_gate', 'v_ffn2_w_up', 'v_ffn2_w_down', 'v_ple_norm', 'v_w_ple_proj', 'v_w_ple_gate', 'v_final_norm']
TWIN_OUTPUTS = ['loss', 'grad_x', 'grad_ffn1_norm', 'grad_ffn1_w_gate', 'grad_ffn1_w_up', 'grad_ffn1_w_down', 'grad_mix_norm', 'grad_w_in', 'grad_fox_forget_bias', 'grad_gla_gate_up', 'grad_gla_gate_bias', 'grad_gla_head_norm', 'grad_w_branch_fox', 'grad_w_branch_gla', 'grad_w_merge_gate', 'grad_b_merge_gate', 'grad_w_out', 'grad_ffn2_norm', 'grad_ffn2_w_gate', 'grad_ffn2_w_up', 'grad_ffn2_w_down', 'grad_ple_norm', 'grad_w_ple_proj', 'grad_w_ple_gate', 'grad_final_norm', 'delta_ffn1_norm', 'delta_ffn1_w_gate', 'delta_ffn1_w_up', 'delta_ffn1_w_down', 'delta_mix_norm', 'delta_w_in', 'delta_fox_forget_bias', 'delta_gla_gate_up', 'delta_gla_gate_bias', 'delta_gla_head_norm', 'delta_w_branch_fox', 'delta_w_branch_gla', 'delta_w_merge_gate', 'delta_b_merge_gate', 'delta_w_out', 'delta_ffn2_norm', 'delta_ffn2_w_gate', 'delta_ffn2_w_up', 'delta_ffn2_w_down', 'delta_ple_norm', 'delta_w_ple_proj', 'delta_w_ple_gate', 'delta_final_norm', 'new_m_ffn1_norm', 'new_m_ffn1_w_gate', 'new_m_ffn1_w_up', 'new_m_ffn1_w_down', 'new_m_mix_norm', 'new_m_w_in', 'new_m_fox_forget_bias', 'new_m_gla_gate_up', 'new_m_gla_gate_bias', 'new_m_gla_head_norm', 'new_m_w_branch_fox', 'new_m_w_branch_gla', 'new_m_w_merge_gate', 'new_m_b_merge_gate', 'new_m_w_out', 'new_m_ffn2_norm', 'new_m_ffn2_w_gate', 'new_m_ffn2_w_up', 'new_m_ffn2_w_down', 'new_m_ple_norm', 'new_m_w_ple_proj', 'new_m_w_ple_gate', 'new_m_final_norm', 'new_v_ffn1_norm', 'new_v_ffn1_w_gate', 'new_v_ffn1_w_up', 'new_v_ffn1_w_down', 'new_v_mix_norm', 'new_v_w_in', 'new_v_fox_forget_bias', 'new_v_gla_gate_up', 'new_v_gla_gate_bias', 'new_v_gla_head_norm', 'new_v_w_branch_fox', 'new_v_w_branch_gla', 'new_v_w_merge_gate', 'new_v_b_merge_gate', 'new_v_w_out', 'new_v_ffn2_norm', 'new_v_ffn2_w_gate', 'new_v_ffn2_w_up', 'new_v_ffn2_w_down', 'new_v_ple_norm', 'new_v_w_ple_proj', 'new_v_w_ple_gate', 'new_v_final_norm']
TWIN_LEAF_KINDS = {'loss': 'loss', 'grad_x': 'grad_x', 'grad_ffn1_norm': 'grad_w', 'grad_ffn1_w_gate': 'grad_w', 'grad_ffn1_w_up': 'grad_w', 'grad_ffn1_w_down': 'grad_w', 'grad_mix_norm': 'grad_w', 'grad_w_in': 'grad_w', 'grad_fox_forget_bias': 'grad_w', 'grad_gla_gate_up': 'grad_w', 'grad_gla_gate_bias': 'grad_w', 'grad_gla_head_norm': 'grad_w', 'grad_w_branch_fox': 'grad_w', 'grad_w_branch_gla': 'grad_w', 'grad_w_merge_gate': 'grad_w', 'grad_b_merge_gate': 'grad_w', 'grad_w_out': 'grad_w', 'grad_ffn2_norm': 'grad_w', 'grad_ffn2_w_gate': 'grad_w', 'grad_ffn2_w_up': 'grad_w', 'grad_ffn2_w_down': 'grad_w', 'grad_ple_norm': 'grad_w', 'grad_w_ple_proj': 'grad_w', 'grad_w_ple_gate': 'grad_w', 'grad_final_norm': 'grad_w', 'delta_ffn1_norm': 'delta_w', 'delta_ffn1_w_gate': 'delta_w', 'delta_ffn1_w_up': 'delta_w', 'delta_ffn1_w_down': 'delta_w', 'delta_mix_norm': 'delta_w', 'delta_w_in': 'delta_w', 'delta_fox_forget_bias': 'delta_w', 'delta_gla_gate_up': 'delta_w', 'delta_gla_gate_bias': 'delta_w', 'delta_gla_head_norm': 'delta_w', 'delta_w_branch_fox': 'delta_w', 'delta_w_branch_gla': 'delta_w', 'delta_w_merge_gate': 'delta_w', 'delta_b_merge_gate': 'delta_w', 'delta_w_out': 'delta_w', 'delta_ffn2_norm': 'delta_w', 'delta_ffn2_w_gate': 'delta_w', 'delta_ffn2_w_up': 'delta_w', 'delta_ffn2_w_down': 'delta_w', 'delta_ple_norm': 'delta_w', 'delta_w_ple_proj': 'delta_w', 'delta_w_ple_gate': 'delta_w', 'delta_final_norm': 'delta_w', 'new_m_ffn1_norm': 'new_m', 'new_m_ffn1_w_gate': 'new_m', 'new_m_ffn1_w_up': 'new_m', 'new_m_ffn1_w_down': 'new_m', 'new_m_mix_norm': 'new_m', 'new_m_w_in': 'new_m', 'new_m_fox_forget_bias': 'new_m', 'new_m_gla_gate_up': 'new_m', 'new_m_gla_gate_bias': 'new_m', 'new_m_gla_head_norm': 'new_m', 'new_m_w_branch_fox': 'new_m', 'new_m_w_branch_gla': 'new_m', 'new_m_w_merge_gate': 'new_m', 'new_m_b_merge_gate': 'new_m', 'new_m_w_out': 'new_m', 'new_m_ffn2_norm': 'new_m', 'new_m_ffn2_w_gate': 'new_m', 'new_m_ffn2_w_up': 'new_m', 'new_m_ffn2_w_down': 'new_m', 'new_m_ple_norm': 'new_m', 'new_m_w_ple_proj': 'new_m', 'new_m_w_ple_gate': 'new_m', 'new_m_final_norm': 'new_m', 'new_v_ffn1_norm': 'new_v', 'new_v_ffn1_w_gate': 'new_v', 'new_v_ffn1_w_up': 'new_v', 'new_v_ffn1_w_down': 'new_v', 'new_v_mix_norm': 'new_v', 'new_v_w_in': 'new_v', 'new_v_fox_forget_bias': 'new_v', 'new_v_gla_gate_up': 'new_v', 'new_v_gla_gate_bias': 'new_v', 'new_v_gla_head_norm': 'new_v', 'new_v_w_branch_fox': 'new_v', 'new_v_w_branch_gla': 'new_v', 'new_v_w_merge_gate': 'new_v', 'new_v_b_merge_gate': 'new_v', 'new_v_w_out': 'new_v', 'new_v_ffn2_norm': 'new_v', 'new_v_ffn2_w_gate': 'new_v', 'new_v_ffn2_w_up': 'new_v', 'new_v_ffn2_w_down': 'new_v', 'new_v_ple_norm': 'new_v', 'new_v_w_ple_proj': 'new_v', 'new_v_w_ple_gate': 'new_v', 'new_v_final_norm': 'new_v'}


def _forward(args):
    return _fwd_reference(*[args[k] for k in FWD_PARAMS])


def _output_shape():
    def fwd():
        inp = _fwd_setup_inputs(0)
        return _fwd_reference(*[inp[k] for k in FWD_PARAMS])
    out = _jax.eval_shape(fwd)
    return out.shape, out.dtype

N_MICROBATCH = 1
ADAM_LR = 0.001
ADAM_B1 = 0.9
ADAM_B2 = 0.999
ADAM_EPS = 1e-08
ADAM_WD = 0.01
ADAM_STEP = 10
PER_EXAMPLE_BATCH_AXIS = {'x': 0, 'p': 1, 'loss_target': 0}
SHARED_INPUTS = []
_WEIGHT_DTYPES = {'ffn1_norm': _jnp.float32, 'ffn1_w_gate': _jnp.float32, 'ffn1_w_up': _jnp.float32, 'ffn1_w_down': _jnp.float32, 'mix_norm': _jnp.float32, 'w_in': _jnp.float32, 'fox_forget_bias': _jnp.float32, 'gla_gate_up': _jnp.float32, 'gla_gate_bias': _jnp.float32, 'gla_head_norm': _jnp.float32, 'w_branch_fox': _jnp.float32, 'w_branch_gla': _jnp.float32, 'w_merge_gate': _jnp.float32, 'b_merge_gate': _jnp.float32, 'w_out': _jnp.float32, 'ffn2_norm': _jnp.float32, 'ffn2_w_gate': _jnp.float32, 'ffn2_w_up': _jnp.float32, 'ffn2_w_down': _jnp.float32, 'ple_norm': _jnp.float32, 'w_ple_proj': _jnp.float32, 'w_ple_gate': _jnp.float32, 'final_norm': _jnp.float32}
MOMENT_SCALE = {'ffn1_norm': 4.292669e-02, 'ffn1_w_gate': 1.747002e-02, 'ffn1_w_up': 1.694849e-02, 'ffn1_w_down': 2.810427e-02, 'mix_norm': 5.668412e-02, 'w_in': 3.169905e-02, 'fox_forget_bias': 1.699843e-01, 'gla_gate_up': 5.809344e-03, 'gla_gate_bias': 2.285552e-02, 'gla_head_norm': 7.004120e-02, 'w_branch_fox': 1.325184e-02, 'w_branch_gla': 2.508634e-02, 'w_merge_gate': 7.806356e-03, 'b_merge_gate': 7.918090e-03, 'w_out': 2.821276e-02, 'ffn2_norm': 3.139075e-02, 'ffn2_w_gate': 1.339355e-02, 'ffn2_w_up': 1.299776e-02, 'ffn2_w_down': 2.157979e-02, 'ple_norm': 1.515543e-02, 'w_ple_proj': 4.021700e-02, 'w_ple_gate': 1.501038e-02, 'final_norm': 1.604465e+01}


def _to_microbatches(a, axis):
    t = _jnp.moveaxis(a, axis, 0)
    t = t.reshape((N_MICROBATCH, t.shape[0] // N_MICROBATCH) + t.shape[1:])
    return _jnp.moveaxis(t, 1, axis + 1)


def setup_inputs(seed: int = 0) -> dict:
    inp = _fwd_setup_inputs(seed)
    key = _jax.random.fold_in(_jax.random.key(seed), 7919)
    shape, _ = _output_shape()
    out = dict(inp)
    out["loss_target"] = _jax.random.normal(_jax.random.fold_in(key, 0), shape, _jnp.float32)
    for i, name in enumerate(TWIN_WEIGHTS):
        w = inp[name].astype(_jnp.float32)
        if MOMENT_SCALE is None:
            s = _jnp.sqrt(_jnp.mean(_jnp.square(w)) + 1e-30)
        else:
            s = MOMENT_SCALE[name]
        km, kv = _jax.random.split(_jax.random.fold_in(key, i + 1))
        out[name] = w
        out["m_" + name] = s * _jax.random.normal(km, w.shape, _jnp.float32)
        out["v_" + name] = (s * s) * _jax.random.uniform(kv, w.shape, _jnp.float32, 0.5, 1.5)
    if N_MICROBATCH > 1:
        for name, axis in PER_EXAMPLE_BATCH_AXIS.items():
            out[name] = _to_microbatches(out[name], axis)
    return {'x': out['x'], 'p': out['p'], 'ffn1_norm': out['ffn1_norm'], 'ffn1_w_gate': out['ffn1_w_gate'], 'ffn1_w_up': out['ffn1_w_up'], 'ffn1_w_down': out['ffn1_w_down'], 'mix_norm': out['mix_norm'], 'w_in': out['w_in'], 'fox_forget_bias': out['fox_forget_bias'], 'gla_gate_up': out['gla_gate_up'], 'gla_gate_bias': out['gla_gate_bias'], 'gla_head_norm': out['gla_head_norm'], 'w_branch_fox': out['w_branch_fox'], 'w_branch_gla': out['w_branch_gla'], 'w_merge_gate': out['w_merge_gate'], 'b_merge_gate': out['b_merge_gate'], 'w_out': out['w_out'], 'ffn2_norm': out['ffn2_norm'], 'ffn2_w_gate': out['ffn2_w_gate'], 'ffn2_w_up': out['ffn2_w_up'], 'ffn2_w_down': out['ffn2_w_down'], 'ple_norm': out['ple_norm'], 'w_ple_proj': out['w_ple_proj'], 'w_ple_gate': out['w_ple_gate'], 'final_norm': out['final_norm'], 'loss_target': out['loss_target'], 'm_ffn1_norm': out['m_ffn1_norm'], 'm_ffn1_w_gate': out['m_ffn1_w_gate'], 'm_ffn1_w_up': out['m_ffn1_w_up'], 'm_ffn1_w_down': out['m_ffn1_w_down'], 'm_mix_norm': out['m_mix_norm'], 'm_w_in': out['m_w_in'], 'm_fox_forget_bias': out['m_fox_forget_bias'], 'm_gla_gate_up': out['m_gla_gate_up'], 'm_gla_gate_bias': out['m_gla_gate_bias'], 'm_gla_head_norm': out['m_gla_head_norm'], 'm_w_branch_fox': out['m_w_branch_fox'], 'm_w_branch_gla': out['m_w_branch_gla'], 'm_w_merge_gate': out['m_w_merge_gate'], 'm_b_merge_gate': out['m_b_merge_gate'], 'm_w_out': out['m_w_out'], 'm_ffn2_norm': out['m_ffn2_norm'], 'm_ffn2_w_gate': out['m_ffn2_w_gate'], 'm_ffn2_w_up': out['m_ffn2_w_up'], 'm_ffn2_w_down': out['m_ffn2_w_down'], 'm_ple_norm': out['m_ple_norm'], 'm_w_ple_proj': out['m_w_ple_proj'], 'm_w_ple_gate': out['m_w_ple_gate'], 'm_final_norm': out['m_final_norm'], 'v_ffn1_norm': out['v_ffn1_norm'], 'v_ffn1_w_gate': out['v_ffn1_w_gate'], 'v_ffn1_w_up': out['v_ffn1_w_up'], 'v_ffn1_w_down': out['v_ffn1_w_down'], 'v_mix_norm': out['v_mix_norm'], 'v_w_in': out['v_w_in'], 'v_fox_forget_bias': out['v_fox_forget_bias'], 'v_gla_gate_up': out['v_gla_gate_up'], 'v_gla_gate_bias': out['v_gla_gate_bias'], 'v_gla_head_norm': out['v_gla_head_norm'], 'v_w_branch_fox': out['v_w_branch_fox'], 'v_w_branch_gla': out['v_w_branch_gla'], 'v_w_merge_gate': out['v_w_merge_gate'], 'v_b_merge_gate': out['v_b_merge_gate'], 'v_w_out': out['v_w_out'], 'v_ffn2_norm': out['v_ffn2_norm'], 'v_ffn2_w_gate': out['v_ffn2_w_gate'], 'v_ffn2_w_up': out['v_ffn2_w_up'], 'v_ffn2_w_down': out['v_ffn2_w_down'], 'v_ple_norm': out['v_ple_norm'], 'v_w_ple_proj': out['v_w_ple_proj'], 'v_w_ple_gate': out['v_w_ple_gate'], 'v_final_norm': out['v_final_norm']}


def _loss(weights, diff, rest, loss_target):
    with _jax.named_scope("forward"):
        args = {**rest, TWIN_DIFF_INPUT: diff, **{k: w.astype(_WEIGHT_DTYPES[k]) for k, w in weights.items()}}
        y = _forward(args)
    with _jax.named_scope("loss_head"):
        err = _jnp.square(y.astype(_jnp.float32) - loss_target)
        return 0.5 * _jnp.sum(_jnp.mean(err, axis=-1)) if err.ndim else 0.5 * err


def _adamw(w, g, m, v):
    m = ADAM_B1 * m + (1.0 - ADAM_B1) * g
    v = ADAM_B2 * v + (1.0 - ADAM_B2) * _jnp.square(g)
    m_hat = m / (1.0 - ADAM_B1 ** ADAM_STEP)
    v_hat = v / (1.0 - ADAM_B2 ** ADAM_STEP)
    delta = -ADAM_LR * (m_hat / (_jnp.sqrt(v_hat) + ADAM_EPS) + ADAM_WD * w)
    return delta, m, v


def reference(x, p, ffn1_norm, ffn1_w_gate, ffn1_w_up, ffn1_w_down, mix_norm, w_in, fox_forget_bias, gla_gate_up, gla_gate_bias, gla_head_norm, w_branch_fox, w_branch_gla, w_merge_gate, b_merge_gate, w_out, ffn2_norm, ffn2_w_gate, ffn2_w_up, ffn2_w_down, ple_norm, w_ple_proj, w_ple_gate, final_norm, loss_target, m_ffn1_norm, m_ffn1_w_gate, m_ffn1_w_up, m_ffn1_w_down, m_mix_norm, m_w_in, m_fox_forget_bias, m_gla_gate_up, m_gla_gate_bias, m_gla_head_norm, m_w_branch_fox, m_w_branch_gla, m_w_merge_gate, m_b_merge_gate, m_w_out, m_ffn2_norm, m_ffn2_w_gate, m_ffn2_w_up, m_ffn2_w_down, m_ple_norm, m_w_ple_proj, m_w_ple_gate, m_final_norm, v_ffn1_norm, v_ffn1_w_gate, v_ffn1_w_up, v_ffn1_w_down, v_mix_norm, v_w_in, v_fox_forget_bias, v_gla_gate_up, v_gla_gate_bias, v_gla_head_norm, v_w_branch_fox, v_w_branch_gla, v_w_merge_gate, v_b_merge_gate, v_w_out, v_ffn2_norm, v_ffn2_w_gate, v_ffn2_w_up, v_ffn2_w_down, v_ple_norm, v_w_ple_proj, v_w_ple_gate, v_final_norm):
    given = dict(x=x, p=p, ffn1_norm=ffn1_norm, ffn1_w_gate=ffn1_w_gate, ffn1_w_up=ffn1_w_up, ffn1_w_down=ffn1_w_down, mix_norm=mix_norm, w_in=w_in, fox_forget_bias=fox_forget_bias, gla_gate_up=gla_gate_up, gla_gate_bias=gla_gate_bias, gla_head_norm=gla_head_norm, w_branch_fox=w_branch_fox, w_branch_gla=w_branch_gla, w_merge_gate=w_merge_gate, b_merge_gate=b_merge_gate, w_out=w_out, ffn2_norm=ffn2_norm, ffn2_w_gate=ffn2_w_gate, ffn2_w_up=ffn2_w_up, ffn2_w_down=ffn2_w_down, ple_norm=ple_norm, w_ple_proj=w_ple_proj, w_ple_gate=w_ple_gate, final_norm=final_norm, loss_target=loss_target, m_ffn1_norm=m_ffn1_norm, m_ffn1_w_gate=m_ffn1_w_gate, m_ffn1_w_up=m_ffn1_w_up, m_ffn1_w_down=m_ffn1_w_down, m_mix_norm=m_mix_norm, m_w_in=m_w_in, m_fox_forget_bias=m_fox_forget_bias, m_gla_gate_up=m_gla_gate_up, m_gla_gate_bias=m_gla_gate_bias, m_gla_head_norm=m_gla_head_norm, m_w_branch_fox=m_w_branch_fox, m_w_branch_gla=m_w_branch_gla, m_w_merge_gate=m_w_merge_gate, m_b_merge_gate=m_b_merge_gate, m_w_out=m_w_out, m_ffn2_norm=m_ffn2_norm, m_ffn2_w_gate=m_ffn2_w_gate, m_ffn2_w_up=m_ffn2_w_up, m_ffn2_w_down=m_ffn2_w_down, m_ple_norm=m_ple_norm, m_w_ple_proj=m_w_ple_proj, m_w_ple_gate=m_w_ple_gate, m_final_norm=m_final_norm, v_ffn1_norm=v_ffn1_norm, v_ffn1_w_gate=v_ffn1_w_gate, v_ffn1_w_up=v_ffn1_w_up, v_ffn1_w_down=v_ffn1_w_down, v_mix_norm=v_mix_norm, v_w_in=v_w_in, v_fox_forget_bias=v_fox_forget_bias, v_gla_gate_up=v_gla_gate_up, v_gla_gate_bias=v_gla_gate_bias, v_gla_head_norm=v_gla_head_norm, v_w_branch_fox=v_w_branch_fox, v_w_branch_gla=v_w_branch_gla, v_w_merge_gate=v_w_merge_gate, v_b_merge_gate=v_b_merge_gate, v_w_out=v_w_out, v_ffn2_norm=v_ffn2_norm, v_ffn2_w_gate=v_ffn2_w_gate, v_ffn2_w_up=v_ffn2_w_up, v_ffn2_w_down=v_ffn2_w_down, v_ple_norm=v_ple_norm, v_w_ple_proj=v_w_ple_proj, v_w_ple_gate=v_w_ple_gate, v_final_norm=v_final_norm)
    weights = {n: given[n] for n in TWIN_WEIGHTS}
    shared = {n: given[n] for n in SHARED_INPUTS}
    per_example = {n: given[n] for n in ['x', 'p']}
    grad_fn = _jax.value_and_grad(_loss, argnums=(0, 1))

    def one_microbatch(ex, loss_target):
        ex = dict(ex)
        diff = ex.pop(TWIN_DIFF_INPUT)
        return grad_fn(weights, diff, {**shared, **ex}, loss_target)

    if N_MICROBATCH == 1:
        loss, (grad_w, grad_x) = one_microbatch(per_example, given["loss_target"])
    else:
        def body(carry, xs):
            loss_sum, grad_sum = carry
            l_k, (gw_k, gx_k) = one_microbatch(xs[0], xs[1])
            with _jax.named_scope("update"):
                return (loss_sum + l_k, _jax.tree.map(_jnp.add, grad_sum, gw_k)), gx_k

        init = (_jnp.zeros((), _jnp.float32), _jax.tree.map(_jnp.zeros_like, weights))
        (loss, grad_w), grad_x = _jax.lax.scan(body, init, (per_example, given["loss_target"]))
    with _jax.named_scope("update"):
        delta_w, new_m, new_v = {}, {}, {}
        for n in TWIN_WEIGHTS:
            delta_w[n], new_m[n], new_v[n] = _adamw(weights[n], grad_w[n], given["m_" + n], given["v_" + n])
    return (loss, grad_x, *[grad_w[n] for n in TWIN_WEIGHTS], *[delta_w[n] for n in TWIN_WEIGHTS],
            *[new_m[n] for n in TWIN_WEIGHTS], *[new_v[n] for n in TWIN_WEIGHTS])
```

```python
import functools

import jax
import jax.numpy as jnp
import numpy as np
from jax import lax
from jax.experimental import pallas as pl
from jax.experimental.pallas import tpu as pltpu

F32 = jnp.float32
BF16 = jnp.bfloat16
MESH = pl.DeviceIdType.MESH
AXES = ("x", "y", "c")
N_DEV = 8

VMEM_LIMIT_BYTES = 56 * 1024 * 1024
LANES = 128

EPS = 1e-6
HEAD_DIM = 128
GLA_VAL_DIM = 256
GLA_CHUNK = 64
GLA_GATE_TAU = 16.0
ADAM_LR, ADAM_B1, ADAM_B2, ADAM_EPS, ADAM_WD, ADAM_STEP = 0.001, 0.9, 0.999, 1e-08, 0.01, 10

HIGHEST = lax.Precision.HIGHEST
NN = (((1,), (0,)), ((), ()))
NT = (((1,), (1,)), ((), ()))
TN = (((0,), (0,)), ((), ()))


def _cparams(sem):
    return pltpu.CompilerParams(dimension_semantics=sem, vmem_limit_bytes=VMEM_LIMIT_BYTES)


def _pick(dim, cands):
    for c in cands:
        if dim % c == 0:
            return c
    return dim


def _bf(v):
    return v if v.dtype == BF16 else v.astype(BF16)


def _dot(a, b, dims):
    return lax.dot_general(_bf(a), _bf(b), dims, preferred_element_type=F32)


def _sigmoid(v):
    return 1.0 / (1.0 + jnp.exp(-v))


def _log_sigmoid(v):
    return jnp.minimum(v, 0.0) - jnp.log(1.0 + jnp.exp(-jnp.abs(v)))


def _logical(v):
    return (v.shape[1], v.shape[0] * v.shape[2]) if v.ndim == 3 else v.shape


def _mm(a, b, *, mode, name, out_dtype=F32, add=None, scale=1.0, out_chunked=False):
    la, lb = _logical(a), _logical(b)
    if mode == "nn":
        (m, k), (k2, n) = la, lb
        a_minor, b_minor = "k", "n"
    elif mode == "nt":
        (m, k), (n, k2) = la, lb
        a_minor, b_minor = "k", "k"
    else:
        (k, m), (k2, n) = la, lb
        a_minor, b_minor = "m", "n"
    assert k == k2, (name, a.shape, b.shape)
    forced = {}
    for v, minor in ((a, a_minor), (b, b_minor)):
        if v.ndim == 3:
            assert forced.get(minor, v.shape[2]) == v.shape[2], name
            forced[minor] = v.shape[2]
    if out_chunked:
        assert forced.get("n", n // N_DEV) == n // N_DEV, name
        forced["n"] = n // N_DEV
    tm = forced.get("m") or _pick(m, (1024, 512, 256, 128))
    tn = forced.get("n") or _pick(n, (1408, 1280, 1024, 512, 256, 128))
    tk = forced.get("k") or _pick(k, (512, 640, 256, 128))
    nk = k // tk
    dims = {"nn": NN, "nt": NT, "tn": TN}[mode]
    gi, gj, gk = (lambda i, j, kk: i), (lambda i, j, kk: j), (lambda i, j, kk: kk)

    def spec(v, t_major, t_minor, g_major, g_minor):
        if v is not None and v.ndim == 3:
            return pl.BlockSpec((None, t_major, v.shape[2]), lambda i, j, kk: (g_minor(i, j, kk), g_major(i, j, kk), 0))
        return pl.BlockSpec((t_major, t_minor), lambda i, j, kk: (g_major(i, j, kk), g_minor(i, j, kk)))

    a_spec = spec(a, tk, tm, gk, gi) if mode == "tn" else spec(a, tm, tk, gi, gk)
    b_spec = spec(b, tn, tk, gj, gk) if mode == "nt" else spec(b, tk, tn, gk, gj)
    if out_chunked:
        o_spec = pl.BlockSpec((None, tm, tn), lambda i, j, kk: (j, i, 0))
        out_shape = jax.ShapeDtypeStruct((N_DEV, m, tn), out_dtype)
    else:
        o_spec = pl.BlockSpec((tm, tn), lambda i, j, kk: (i, j))
        out_shape = jax.ShapeDtypeStruct((m, n), out_dtype)
    has_add = add is not None
    assert not (has_add and out_chunked), name

    def body(*refs):
        a_ref, b_ref = refs[0], refs[1]
        add_ref = refs[2] if has_add else None
        o_ref, acc_ref = refs[-2], refs[-1]
        kk = pl.program_id(2)

        @pl.when(kk == 0)
        def _():
            acc_ref[...] = jnp.zeros_like(acc_ref)

        acc_ref[...] += _dot(a_ref[...], b_ref[...], dims)

        @pl.when(kk == nk - 1)
        def _():
            r = acc_ref[...]
            if scale != 1.0:
                r = r * scale
            if has_add:
                r = r + add_ref[...]
            o_ref[...] = r.astype(o_ref.dtype)

    return pl.pallas_call(
        body, name=name,
        grid=(m // tm, n // tn, nk),
        in_specs=[a_spec, b_spec] + ([o_spec] if has_add else []),
        out_specs=o_spec,
        out_shape=out_shape,
        scratch_shapes=[pltpu.VMEM((tm, tn), F32)],
        compiler_params=_cparams(("parallel", "parallel", "arbitrary")),
    )(*([a, b] + ([add] if has_add else [])))


def _rowwise(fn, rows, consts, outs, accs=(), *, name, rc=None):
    rows = [r if isinstance(r, tuple) else (r, r.shape[1], 0) for r in rows]
    m = rows[0][0].shape[0]
    widths = [w for _, w, _ in rows] + [n for n, _ in outs]
    row_bytes = sum(w * r.dtype.itemsize for r, w, _ in rows) + sum(n * jnp.dtype(d).itemsize for n, d in outs)
    tm = 1024
    while tm > 16 and (m % tm or 2 * tm * row_bytes > 24 * 1024 * 1024):
        tm //= 2
    if m % tm:
        tm = m
    if rc is None:
        rc = 16
        while rc * 2 <= tm and rc * 2 * max(widths) <= 32768:
            rc *= 2
    rc = min(rc, tm)
    nr, nc, no = len(rows), len(consts), len(outs)

    def body(*refs):
        in_refs, c_refs = refs[:nr], refs[nr:nr + nc]
        o_refs, a_refs = refs[nr + nc:nr + nc + no], refs[nr + nc + no:]

        @pl.when(pl.program_id(0) == 0)
        def _():
            for r in a_refs:
                r[...] = jnp.zeros_like(r)

        cvals = [c[...] for c in c_refs]

        def chunk(ci, carry):
            sl = pl.ds(pl.multiple_of(ci * rc, rc), rc)
            res = fn(*[r[sl, :] for r in in_refs], *cvals)
            if not isinstance(res, (tuple, list)):
                res = (res,)
            for r, v in zip(o_refs, res[:no]):
                r[sl, :] = v.astype(r.dtype)
            for r, v in zip(a_refs, res[no:]):
                r[...] += v
            return carry

        lax.fori_loop(0, tm // rc, chunk, 0)

    in_specs = [pl.BlockSpec((tm, w), functools.partial(lambda i, cb: (i, cb), cb=cb)) for _, w, cb in rows]
    in_specs += [pl.BlockSpec(c.shape, lambda i: (0, 0)) for c in consts]
    out_specs = [pl.BlockSpec((tm, n), lambda i: (i, 0)) for n, _ in outs]
    out_specs += [pl.BlockSpec((1, n), lambda i: (0, 0)) for n in accs]
    out_shape = [jax.ShapeDtypeStruct((m, n), d) for n, d in outs] + [jax.ShapeDtypeStruct((1, n), F32) for n in accs]
    res = pl.pallas_call(
        body, name=name, grid=(m // tm,),
        in_specs=in_specs, out_specs=out_specs, out_shape=out_shape,
        compiler_params=_cparams(("arbitrary",)),
    )(*[r for r, _, _ in rows], *consts)
    return res


def _colsum(v):
    return jnp.sum(v, axis=0, keepdims=True)


def _rms_parts(xv):
    r = lax.rsqrt(jnp.mean(xv * xv, axis=-1, keepdims=True) + EPS)
    return r, xv * r


def _rms_fwd(xv, g, *, name):
    d = xv.shape[1]

    def fn(xb, gb):
        _, xh = _rms_parts(xb)
        return xh * gb

    return _rowwise(fn, [xv], [g], [(d, BF16)], name=name)[0]


def _rms_bwd(xv, dy, g, add, *, name):
    d = xv.shape[1]

    def fn(xb, dyb, addb, gb):
        r, xh = _rms_parts(xb)
        t = dyb * gb
        dx = r * (t - xh * jnp.mean(t * xh, axis=-1, keepdims=True)) + addb
        return dx, dx, _colsum(dyb * xh)

    return _rowwise(fn, [xv, dy, add], [g], [(d, F32), (d, BF16)], [d], name=name)


def _silu_parts(a):
    sg = _sigmoid(a)
    return a * sg, sg * (1.0 + a * (1.0 - sg))


def _rows(v):
    return v.reshape(v.shape[0] * v.shape[1], v.shape[2])


def _ffn_fwd(h, g, wg, wu, wd, *, tag):
    s, c = h.shape[0], wg.shape[2]
    n = _rms_fwd(h, g, name=f"{tag}_norm")
    a = _mm(n, wg, mode="nn", name=f"{tag}_gate", out_chunked=True)
    b = _mm(n, wu, mode="nn", name=f"{tag}_up", out_chunked=True)
    hm = _rowwise(lambda av, bv: _silu_parts(av)[0] * bv, [_rows(a), _rows(b)], [], [(c, BF16)], name=f"{tag}_act")[0]
    hm = hm.reshape(N_DEV, s, c)
    out = _mm(hm, wd, mode="nn", name=f"{tag}_down", add=h, scale=0.5)
    return out, (n, a, b, hm)


def _ffn_bwd(h, g, wg, wu, wd, saved, dout, dout_bf, *, tag):
    n, a, b, hm = saved
    s, c = h.shape[0], wg.shape[2]
    dhm = _mm(dout_bf, wd, mode="nt", name=f"{tag}_down_dx", scale=0.5, out_chunked=True)
    d_wd = _mm(hm, dout_bf, mode="tn", name=f"{tag}_down_dw", scale=0.5, out_dtype=BF16)

    def act_bwd(av, bv, dv):
        si, dsi = _silu_parts(av)
        return dv * bv * dsi, dv * si

    da, db = _rowwise(act_bwd, [_rows(a), _rows(b), _rows(dhm)], [], [(c, BF16), (c, BF16)], name=f"{tag}_act_bwd")
    da, db = da.reshape(N_DEV, s, c), db.reshape(N_DEV, s, c)
    d_wg = _mm(n, da, mode="tn", name=f"{tag}_gate_dw", out_dtype=BF16, out_chunked=True)
    d_wu = _mm(n, db, mode="tn", name=f"{tag}_up_dw", out_dtype=BF16, out_chunked=True)
    dn = _mm(da, wg, mode="nt", name=f"{tag}_gate_dx")
    dn = _mm(db, wu, mode="nt", name=f"{tag}_up_dx", add=dn)
    dh, dh_bf, dg = _rms_bwd(h, dn, g, dout, name=f"{tag}_norm_bwd")
    return dh, dh_bf, dg, d_wg, d_wu, d_wd


def _cumsum(xv, *, reverse, name):
    h, s = xv.shape
    t = _pick(s, (512, 256, 128))
    nb = s // t

    def blk(j):
        return (0, nb - 1 - j) if reverse else (0, j)

    def body(x_ref, o_ref, carry):
        @pl.when(pl.program_id(0) == 0)
        def _():
            carry[...] = jnp.zeros_like(carry)

        i0 = lax.broadcasted_iota(jnp.int32, (t, t), 0)
        i1 = lax.broadcasted_iota(jnp.int32, (t, t), 1)
        tri = ((i0 >= i1) if reverse else (i0 <= i1)).astype(F32)
        xb = x_ref[...]
        o_ref[...] = jnp.dot(xb, tri, precision=HIGHEST, preferred_element_type=F32) + carry[...]
        carry[...] += jnp.sum(xb, axis=1, keepdims=True)

    return pl.pallas_call(
        body, name=name, grid=(nb,),
        in_specs=[pl.BlockSpec((h, t), blk)], out_specs=pl.BlockSpec((h, t), blk),
        out_shape=jax.ShapeDtypeStruct((h, s), F32),
        scratch_shapes=[pltpu.VMEM((h, 1), F32)],
        compiler_params=_cparams(("arbitrary",)),
    )(xv)


def _fox_tiles(s):
    t = _pick(s, (512, 256, 128))
    return t, t


def _fox_fwd(zf, f_col, f_row, *, heads, name):
    s = zf.shape[0]
    tq, tk = _fox_tiles(s)
    nq, nk = s // tq, s // tk
    scale = HEAD_DIM ** -0.5
    w = heads * HEAD_DIM

    def last_k(i):
        return (i * tq + tq - 1) // tk

    def body(q_ref, k_ref, v_ref, fq_ref, fk_ref, o32_ref, o16_ref, lse_ref, m_sc, l_sc, acc_sc):
        i, j = pl.program_id(1), pl.program_id(2)

        @pl.when(j == 0)
        def _():
            m_sc[...] = jnp.full_like(m_sc, -jnp.inf)
            l_sc[...] = jnp.zeros_like(l_sc)
            acc_sc[...] = jnp.zeros_like(acc_sc)

        @pl.when(j <= last_k(i))
        def _():
            sc = _dot(q_ref[...], k_ref[...], NT) * scale + fq_ref[...] - fk_ref[...]
            qpos = i * tq + lax.broadcasted_iota(jnp.int32, (tq, tk), 0)
            kpos = j * tk + lax.broadcasted_iota(jnp.int32, (tq, tk), 1)
            sc = jnp.where(kpos <= qpos, sc, -jnp.inf)
            m_new = jnp.maximum(m_sc[...], jnp.max(sc, axis=-1, keepdims=True))
            alpha = jnp.exp(m_sc[...] - m_new)
            pr = jnp.exp(sc - m_new)
            l_sc[...] = alpha * l_sc[...] + jnp.sum(pr, axis=-1, keepdims=True)
            acc_sc[...] = alpha * acc_sc[...] + _dot(pr, v_ref[...], NN)
            m_sc[...] = m_new

        @pl.when(j == nk - 1)
        def _():
            o = acc_sc[...] / l_sc[...]
            o32_ref[...] = o
            o16_ref[...] = o.astype(BF16)
            lse_ref[...] = jnp.broadcast_to(m_sc[...] + jnp.log(l_sc[...]), (tq, HEAD_DIM))

    def kv_blk(off):
        return lambda h, i, j: (jnp.minimum(j, last_k(i)), off + h)

    o_spec = pl.BlockSpec((tq, HEAD_DIM), lambda h, i, j: (i, h))
    return pl.pallas_call(
        body, name=name, grid=(heads, nq, nk),
        in_specs=[
            pl.BlockSpec((tq, HEAD_DIM), lambda h, i, j: (i, h)),
            pl.BlockSpec((tk, HEAD_DIM), kv_blk(heads)),
            pl.BlockSpec((tk, HEAD_DIM), kv_blk(2 * heads)),
            pl.BlockSpec((None, tq, 1), lambda h, i, j: (h, i, 0)),
            pl.BlockSpec((None, 1, tk), lambda h, i, j: (h, 0, jnp.minimum(j, last_k(i)))),
        ],
        out_specs=[o_spec, o_spec, o_spec],
        out_shape=[jax.ShapeDtypeStruct((s, w), F32), jax.ShapeDtypeStruct((s, w), BF16),
                   jax.ShapeDtypeStruct((s, w), F32)],
        scratch_shapes=[pltpu.VMEM((tq, 1), F32), pltpu.VMEM((tq, 1), F32), pltpu.VMEM((tq, HEAD_DIM), F32)],
        compiler_params=_cparams(("parallel", "parallel", "arbitrary")),
    )(zf, zf, zf, f_col, f_row)


def _fox_bwd(zf, do, lse, delta, f_col, f_row, *, heads, name):
    s = zf.shape[0]
    tq, tk = _fox_tiles(s)
    nq, nk = s // tq, s // tk
    scale = HEAD_DIM ** -0.5
    w = heads * HEAD_DIM

    def first_q(j):
        return (j * tk) // tq

    def body(q_ref, k_ref, v_ref, do_ref, lse_ref, dl_ref, fq_ref, fk_ref, dq_ref, dk_ref, dv_ref, dfq_ref, dfk_ref):
        j, i = pl.program_id(1), pl.program_id(2)

        @pl.when((j == 0) & (i == 0))
        def _():
            dq_ref[...] = jnp.zeros_like(dq_ref)
            dfq_ref[...] = jnp.zeros_like(dfq_ref)

        @pl.when(i == 0)
        def _():
            dk_ref[...] = jnp.zeros_like(dk_ref)
            dv_ref[...] = jnp.zeros_like(dv_ref)
            dfk_ref[...] = jnp.zeros_like(dfk_ref)

        @pl.when(i >= first_q(j))
        def _():
            q, k, v = q_ref[...], k_ref[...], v_ref[...]
            dob = do_ref[...].astype(BF16)
            sc = _dot(q, k, NT) * scale + fq_ref[...] - fk_ref[...]
            qpos = i * tq + lax.broadcasted_iota(jnp.int32, (tq, tk), 0)
            kpos = j * tk + lax.broadcasted_iota(jnp.int32, (tq, tk), 1)
            sc = jnp.where(kpos <= qpos, sc, -jnp.inf)
            pr = jnp.exp(sc - lse_ref[:, 0:1])
            dv_ref[...] += _dot(pr, dob, TN)
            dp = _dot(dob, v, NT)
            ds = pr * (dp - dl_ref[:, 0:1])
            dsb = ds.astype(BF16)
            dk_ref[...] += _dot(dsb, q, TN) * scale
            rows = pl.ds(pl.multiple_of(i * tq, tq), tq)
            dq_ref[rows, :] += _dot(dsb, k, NN) * scale
            dfq_ref[rows, :] += jnp.broadcast_to(jnp.sum(ds, axis=1, keepdims=True), (tq, HEAD_DIM))
            dfk_ref[...] -= jnp.sum(ds, axis=0, keepdims=True)

    def q_blk(h, j, i):
        return (jnp.maximum(i, first_q(j)), h)

    return pl.pallas_call(
        body, name=name, grid=(heads, nk, nq),
        in_specs=[
            pl.BlockSpec((tq, HEAD_DIM), q_blk),
            pl.BlockSpec((tk, HEAD_DIM), lambda h, j, i: (j, heads + h)),
            pl.BlockSpec((tk, HEAD_DIM), lambda h, j, i: (j, 2 * heads + h)),
            pl.BlockSpec((tq, HEAD_DIM), q_blk),
            pl.BlockSpec((tq, HEAD_DIM), q_blk),
            pl.BlockSpec((tq, HEAD_DIM), q_blk),
            pl.BlockSpec((None, tq, 1), lambda h, j, i: (h, jnp.maximum(i, first_q(j)), 0)),
            pl.BlockSpec((None, 1, tk), lambda h, j, i: (h, 0, j)),
        ],
        out_specs=[
            pl.BlockSpec((s, HEAD_DIM), lambda h, j, i: (0, h)),
            pl.BlockSpec((tk, HEAD_DIM), lambda h, j, i: (j, h)),
            pl.BlockSpec((tk, HEAD_DIM), lambda h, j, i: (j, h)),
            pl.BlockSpec((s, HEAD_DIM), lambda h, j, i: (0, h)),
            pl.BlockSpec((None, 1, tk), lambda h, j, i: (h, 0, j)),
        ],
        out_shape=[jax.ShapeDtypeStruct((s, w), F32), jax.ShapeDtypeStruct((s, w), F32),
                   jax.ShapeDtypeStruct((s, w), F32), jax.ShapeDtypeStruct((s, w), F32),
                   jax.ShapeDtypeStruct((heads, 1, s), F32)],
        compiler_params=_cparams(("parallel", "arbitrary", "arbitrary")),
    )(zf, zf, zf, do, lse, delta, f_col, f_row)


def _gla_rows(s):
    return _pick(s, (256, 128, 64))


def _gla_chunk_terms(la_c, tri):
    a_cum = jnp.dot(tri, la_c, precision=HIGHEST, preferred_element_type=F32)
    a_tot = jnp.sum(la_c, axis=0, keepdims=True)
    return jnp.exp(a_tot - a_cum), jnp.exp(a_tot)


def _gla_fwd(zr, la, *, heads, q_blk, k_blk, name):
    s = zr.shape[0]
    c = GLA_CHUNK
    rows = _gla_rows(s)
    nsteps, ncs = s // rows, rows // c
    scale = HEAD_DIM ** -0.5

    def body(q_ref, k_ref, v_ref, la_ref, o_ref, st_ref, state):
        @pl.when(pl.program_id(1) == 0)
        def _():
            state[...] = jnp.zeros_like(state)

        tri = (lax.broadcasted_iota(jnp.int32, (c, c), 0) >= lax.broadcasted_iota(jnp.int32, (c, c), 1)).astype(F32)
        for t in range(ncs):
            sl = slice(t * c, (t + 1) * c)
            dec, e_tot = _gla_chunk_terms(la_ref[sl, :], tri)
            kd = k_ref[sl, :] * dec
            st_ref[t] = state[...]
            new = state[...] * e_tot + _dot(v_ref[sl, :], kd, TN)
            state[...] = new
            o_ref[sl, :] = _dot(q_ref[sl, :] * scale, new, NT)

    return pl.pallas_call(
        body, name=name, grid=(heads, nsteps),
        in_specs=[
            pl.BlockSpec((rows, HEAD_DIM), lambda h, i: (i, q_blk + h)),
            pl.BlockSpec((rows, HEAD_DIM), lambda h, i: (i, k_blk + h)),
            pl.BlockSpec((rows, GLA_VAL_DIM), lambda h, i: (i, h)),
            pl.BlockSpec((rows, HEAD_DIM), lambda h, i: (i, h)),
        ],
        out_specs=[
            pl.BlockSpec((rows, GLA_VAL_DIM), lambda h, i: (i, h)),
            pl.BlockSpec((None, ncs, GLA_VAL_DIM, HEAD_DIM), lambda h, i: (h, i, 0, 0)),
        ],
        out_shape=[jax.ShapeDtypeStruct((s, heads * GLA_VAL_DIM), F32),
                   jax.ShapeDtypeStruct((heads, s // c, GLA_VAL_DIM, HEAD_DIM), F32)],
        scratch_shapes=[pltpu.VMEM((GLA_VAL_DIM, HEAD_DIM), F32)],
        compiler_params=_cparams(("parallel", "arbitrary")),
    )(zr, zr, zr, la)


def _gla_bwd(zr, la, do, states, *, heads, q_blk, k_blk, name):
    s = zr.shape[0]
    c = GLA_CHUNK
    rows = _gla_rows(s)
    nsteps, ncs = s // rows, rows // c
    scale = HEAD_DIM ** -0.5

    def body(q_ref, k_ref, v_ref, la_ref, do_ref, st_ref, dq_ref, dk_ref, dv_ref, dla_ref, dstate):
        @pl.when(pl.program_id(1) == 0)
        def _():
            dstate[...] = jnp.zeros_like(dstate)

        i0 = lax.broadcasted_iota(jnp.int32, (c, c), 0)
        i1 = lax.broadcasted_iota(jnp.int32, (c, c), 1)
        tri = (i0 >= i1).astype(F32)
        strict = (i0 > i1).astype(F32)
        for t in reversed(range(ncs)):
            sl = slice(t * c, (t + 1) * c)
            dec, e_tot = _gla_chunk_terms(la_ref[sl, :], tri)
            kd = k_ref[sl, :] * dec
            kdb = kd.astype(BF16)
            vb = v_ref[sl, :].astype(BF16)
            dob = do_ref[sl, :].astype(BF16)
            prev = st_ref[t]
            cur = prev * e_tot + _dot(vb, kdb, TN)
            d_cur = dstate[...] + _dot(dob, q_ref[sl, :] * scale, TN)
            d_cur_b = d_cur.astype(BF16)
            dq_ref[sl, :] = _dot(dob, cur, NN) * scale
            dv_ref[sl, :] = _dot(kdb, d_cur_b, NT)
            dkd = _dot(vb, d_cur_b, NN)
            d_tot = e_tot * jnp.sum(d_cur * prev, axis=0, keepdims=True)
            dk_ref[sl, :] = dkd * dec
            dla_ref[sl, :] = d_tot + jnp.dot(strict, dkd * kd, precision=HIGHEST, preferred_element_type=F32)
            dstate[...] = d_cur * e_tot

    def rev(i):
        return nsteps - 1 - i

    kq_spec = pl.BlockSpec((rows, HEAD_DIM), lambda h, i: (rev(i), h))
    v_spec = pl.BlockSpec((rows, GLA_VAL_DIM), lambda h, i: (rev(i), h))
    return pl.pallas_call(
        body, name=name, grid=(heads, nsteps),
        in_specs=[
            pl.BlockSpec((rows, HEAD_DIM), lambda h, i: (rev(i), q_blk + h)),
            pl.BlockSpec((rows, HEAD_DIM), lambda h, i: (rev(i), k_blk + h)),
            v_spec, kq_spec, v_spec,
            pl.BlockSpec((None, ncs, GLA_VAL_DIM, HEAD_DIM), lambda h, i: (h, rev(i), 0, 0)),
        ],
        out_specs=[kq_spec, kq_spec, v_spec, kq_spec],
        out_shape=[jax.ShapeDtypeStruct((s, heads * HEAD_DIM), F32), jax.ShapeDtypeStruct((s, heads * HEAD_DIM), F32),
                   jax.ShapeDtypeStruct((s, heads * GLA_VAL_DIM), F32), jax.ShapeDtypeStruct((s, heads * HEAD_DIM), F32)],
        scratch_shapes=[pltpu.VMEM((GLA_VAL_DIM, HEAD_DIM), F32)],
        compiler_params=_cparams(("parallel", "arbitrary")),
    )(zr, zr, zr, la, do, states)


def _my_place():
    x, y, c = lax.axis_index("x"), lax.axis_index("y"), lax.axis_index("c")
    return x, y, c


def _all_gather(flat, *, name):
    def body(x_ref, out_ref, send_sems, recv_sems, local_sem):
        x, y, c = _my_place()
        me, sibling = (x, y, c), (x, y, 1 - c)
        chips = [(1 - x, y), (x, 1 - y), (1 - x, 1 - y)]

        def blk(px, py, pc):
            return out_ref.at[4 * px + 2 * py + pc]

        def copy(k, block, to, src=None):
            return pltpu.make_async_remote_copy(
                src_ref=blk(*block) if src is None else src, dst_ref=blk(*block),
                send_sem=send_sems.at[k], recv_sem=recv_sems.at[k], device_id=to, device_id_type=MESH)

        mine = pltpu.make_async_copy(x_ref, blk(*me), local_sem)
        mine.start()
        first = [copy(0, me, sibling, src=x_ref)]
        first += [copy(1 + j, me, (*chip, c), src=x_ref) for j, chip in enumerate(chips)]
        for cp in first:
            cp.start()
        passed = [copy(4 + j, (*chip, c), sibling) for j, chip in enumerate(chips)]
        for j, chip in enumerate(chips):
            copy(1 + j, (*chip, c), me).wait_recv()
            passed[j].start()
        copy(0, sibling, me).wait_recv()
        for j, chip in enumerate(chips):
            copy(4 + j, (*chip, 1 - c), me).wait_recv()
        for cp in first + passed:
            cp.wait_send()
        mine.wait()

    return pl.pallas_call(
        body, name=name,
        out_shape=jax.ShapeDtypeStruct((N_DEV,) + flat.shape, flat.dtype),
        in_specs=[pl.BlockSpec(memory_space=pl.ANY)],
        out_specs=pl.BlockSpec(memory_space=pl.ANY),
        scratch_shapes=[pltpu.SemaphoreType.DMA((7,)), pltpu.SemaphoreType.DMA((7,)), pltpu.SemaphoreType.DMA],
    )(flat)


def _exchange(send, *, scatter, name):
    def body(s_ref, r_ref, send_sems, recv_sems, local_sem):
        x, y, c = _my_place()
        me = 4 * x + 2 * y + c
        mine = pltpu.make_async_copy(s_ref.at[me] if scatter else s_ref, r_ref.at[me], local_sem)
        mine.start()
        sends, recvs = [], []
        for k in range(1, N_DEV):
            px, py, pc = x ^ ((k >> 2) & 1), y ^ ((k >> 1) & 1), c ^ (k & 1)
            peer = 4 * px + 2 * py + pc
            src = s_ref.at[peer] if scatter else s_ref
            for dst, out in ((r_ref.at[me], sends), (r_ref.at[peer], recvs)):
                out.append(pltpu.make_async_remote_copy(
                    src_ref=src, dst_ref=dst, send_sem=send_sems.at[k - 1], recv_sem=recv_sems.at[k - 1],
                    device_id=(px, py, pc), device_id_type=MESH))
        for cp in sends:
            cp.start()
        for cp in recvs:
            cp.wait_recv()
        for cp in sends:
            cp.wait_send()
        mine.wait()

    out_shape = send.shape if scatter else (N_DEV,) + send.shape
    return pl.pallas_call(
        body, name=name,
        out_shape=jax.ShapeDtypeStruct(out_shape, send.dtype),
        in_specs=[pl.BlockSpec(memory_space=pl.ANY)],
        out_specs=pl.BlockSpec(memory_space=pl.ANY),
        scratch_shapes=[pltpu.SemaphoreType.DMA((N_DEV - 1,)), pltpu.SemaphoreType.DMA((N_DEV - 1,)),
                        pltpu.SemaphoreType.DMA],
    )(send)


def _sel_tables(dest, ws, wp, tw):
    n_tiles = (int(dest.max()) + tw) // tw
    tbl = np.full((N_DEV, wp), -1, np.int32)
    for j in range(N_DEV):
        tbl[j, :ws] = dest[j * ws:(j + 1) * ws]
    by_tile = [sorted({j for j in range(N_DEV) if ((tbl[j] // tw) == t).any()}) for t in range(n_tiles)]
    by_shard = [sorted({int(t) for t in np.unique(tbl[j, :ws] // tw)}) for j in range(N_DEV)]

    def table(lists):
        width = max(len(v) for v in lists)
        idx = np.array([(v + [v[-1]] * width)[:width] if v else [0] * width for v in lists], np.int32)
        val = np.array([[1] * len(v) + [0] * (width - len(v)) for v in lists], np.int32)
        return idx.reshape(-1), val.reshape(-1), width

    return tbl[:, :, None], table(by_tile), table(by_shard)


def _sel_matrix(d_ref, t, wp, tw):
    cols = t * tw + lax.broadcasted_iota(jnp.int32, (wp, tw), 1)
    return (d_ref[...] == cols).astype(BF16)


def _win_unshard(g, tbl, idx, val, width, *, tw, padded, name):
    _, dm, wp = g.shape
    tm = _pick(dm, (1024, 512, 256, 128))

    def body(idx_ref, val_ref, g_ref, d_ref, o_ref, acc):
        t, s_ = pl.program_id(1), pl.program_id(2)

        @pl.when(s_ == 0)
        def _():
            acc[...] = jnp.zeros_like(acc)

        @pl.when(val_ref[t * width + s_] == 1)
        def _():
            acc[...] += _dot(g_ref[...], _sel_matrix(d_ref, t, wp, tw), NN)

        @pl.when(s_ == width - 1)
        def _():
            o_ref[...] = acc[...].astype(BF16)

    return pl.pallas_call(
        body, name=name,
        grid_spec=pltpu.PrefetchScalarGridSpec(
            num_scalar_prefetch=2, grid=(dm // tm, padded // tw, width),
            in_specs=[pl.BlockSpec((None, tm, wp), lambda i, t, s_, ix, vl: (ix[t * width + s_], i, 0)),
                      pl.BlockSpec((None, wp, 1), lambda i, t, s_, ix, vl: (ix[t * width + s_], 0, 0))],
            out_specs=pl.BlockSpec((tm, tw), lambda i, t, s_, ix, vl: (i, t)),
            scratch_shapes=[pltpu.VMEM((tm, tw), F32)]),
        out_shape=jax.ShapeDtypeStruct((dm, padded), BF16),
        compiler_params=_cparams(("parallel", "parallel", "arbitrary")),
    )(idx, val, g, tbl)


def _win_to_shards(dw, tbl, idx, val, width, *, tw, wp, name):
    dm = dw.shape[0]
    tm = _pick(dm, (1024, 512, 256, 128))

    def body(idx_ref, val_ref, w_ref, d_ref, o_ref, acc):
        j, s_ = pl.program_id(1), pl.program_id(2)

        @pl.when(s_ == 0)
        def _():
            acc[...] = jnp.zeros_like(acc)

        @pl.when(val_ref[j * width + s_] == 1)
        def _():
            acc[...] += _dot(w_ref[...], _sel_matrix(d_ref, idx_ref[j * width + s_], wp, tw), NT)

        @pl.when(s_ == width - 1)
        def _():
            o_ref[...] = acc[...].astype(BF16)

    return pl.pallas_call(
        body, name=name,
        grid_spec=pltpu.PrefetchScalarGridSpec(
            num_scalar_prefetch=2, grid=(dm // tm, N_DEV, width),
            in_specs=[pl.BlockSpec((tm, tw), lambda i, j, s_, ix, vl: (i, ix[j * width + s_])),
                      pl.BlockSpec((None, wp, 1), lambda i, j, s_, ix, vl: (j, 0, 0))],
            out_specs=pl.BlockSpec((None, tm, wp), lambda i, j, s_, ix, vl: (j, i, 0)),
            scratch_shapes=[pltpu.VMEM((tm, wp), F32)]),
        out_shape=jax.ShapeDtypeStruct((N_DEV, dm, wp), BF16),
        compiler_params=_cparams(("parallel", "parallel", "arbitrary")),
    )(idx, val, dw, tbl)


def _adamw(parts, w, m, v, *, name):
    r, cdim = w.shape
    tr = _pick(r, (256, 128, 64, 32, 16, 8))
    rc = min(16, tr)
    c1 = 1.0 - ADAM_B1 ** ADAM_STEP
    c2 = 1.0 - ADAM_B2 ** ADAM_STEP

    def body(p_ref, w_ref, m_ref, v_ref, g_ref, d_ref, mo_ref, vo_ref):
        def chunk(ci, carry):
            sl = pl.ds(pl.multiple_of(ci * rc, rc), rc)
            g = p_ref[0, sl, :].astype(F32)
            for i in range(1, N_DEV):
                g = g + p_ref[i, sl, :].astype(F32)
            mn = ADAM_B1 * m_ref[sl, :] + (1.0 - ADAM_B1) * g
            vn = ADAM_B2 * v_ref[sl, :] + (1.0 - ADAM_B2) * jnp.square(g)
            m_hat = mn / c1
            v_hat = vn / c2
            g_ref[sl, :] = g
            d_ref[sl, :] = -ADAM_LR * (m_hat / (jnp.sqrt(v_hat) + ADAM_EPS) + ADAM_WD * w_ref[sl, :])
            mo_ref[sl, :] = mn
            vo_ref[sl, :] = vn
            return carry

        lax.fori_loop(0, tr // rc, chunk, 0)

    spec = pl.BlockSpec((tr, cdim), lambda i: (i, 0))
    return pl.pallas_call(
        body, name=name, grid=(r // tr,),
        in_specs=[pl.BlockSpec((N_DEV, tr, cdim), lambda i: (0, i, 0)), spec, spec, spec],
        out_specs=[spec] * 4,
        out_shape=[jax.ShapeDtypeStruct((r, cdim), F32)] * 4,
        compiler_params=_cparams(("parallel",)),
    )(parts, w, m, v)


def _pad_to(v, n):
    return v if v.shape[0] == n else jnp.concatenate([v, jnp.zeros((n - v.shape[0],), v.dtype)])


def _pad_cols(v, n):
    return v if v.shape[-1] == n else jnp.concatenate([v, jnp.zeros(v.shape[:-1] + (n - v.shape[-1],), v.dtype)], axis=-1)


def _pack(vs, cols, row_mult, dtype):
    offs, o = [], 0
    for v in vs:
        offs.append(o)
        o += v.size
    rows = -(-o // cols)
    rows = -(-rows // row_mult) * row_mult
    flat = jnp.concatenate([v.reshape(-1).astype(dtype) for v in vs])
    return _pad_to(flat, rows * cols).reshape(rows, cols), offs


SHARDED = ("ffn1_w_gate", "ffn1_w_up", "ffn1_w_down", "w_in", "gla_gate_up", "w_branch_fox", "w_branch_gla",
           "w_merge_gate", "w_out", "ffn2_w_gate", "ffn2_w_up", "ffn2_w_down", "w_ple_proj", "w_ple_gate")
ROW_SHARDED = ("ffn1_w_down", "w_out", "ffn2_w_down", "w_ple_gate")
REPLICATED = ("ffn1_norm", "mix_norm", "fox_forget_bias", "gla_gate_bias", "gla_head_norm", "b_merge_gate",
              "ffn2_norm", "ple_norm", "final_norm")
WEIGHTS = ("ffn1_norm", "ffn1_w_gate", "ffn1_w_up", "ffn1_w_down", "mix_norm", "w_in", "fox_forget_bias",
           "gla_gate_up", "gla_gate_bias", "gla_head_norm", "w_branch_fox", "w_branch_gla", "w_merge_gate",
           "b_merge_gate", "w_out", "ffn2_norm", "ffn2_w_gate", "ffn2_w_up", "ffn2_w_down", "ple_norm",
           "w_ple_proj", "w_ple_gate", "final_norm")


def kernel(x, p, ffn1_norm, ffn1_w_gate, ffn1_w_up, ffn1_w_down, mix_norm, w_in, fox_forget_bias, gla_gate_up, gla_gate_bias, gla_head_norm, w_branch_fox, w_branch_gla, w_merge_gate, b_merge_gate, w_out, ffn2_norm, ffn2_w_gate, ffn2_w_up, ffn2_w_down, ple_norm, w_ple_proj, w_ple_gate, final_norm, loss_target, m_ffn1_norm, m_ffn1_w_gate, m_ffn1_w_up, m_ffn1_w_down, m_mix_norm, m_w_in, m_fox_forget_bias, m_gla_gate_up, m_gla_gate_bias, m_gla_head_norm, m_w_branch_fox, m_w_branch_gla, m_w_merge_gate, m_b_merge_gate, m_w_out, m_ffn2_norm, m_ffn2_w_gate, m_ffn2_w_up, m_ffn2_w_down, m_ple_norm, m_w_ple_proj, m_w_ple_gate, m_final_norm, v_ffn1_norm, v_ffn1_w_gate, v_ffn1_w_up, v_ffn1_w_down, v_mix_norm, v_w_in, v_fox_forget_bias, v_gla_gate_up, v_gla_gate_bias, v_gla_head_norm, v_w_branch_fox, v_w_branch_gla, v_w_merge_gate, v_b_merge_gate, v_w_out, v_ffn2_norm, v_ffn2_w_gate, v_ffn2_w_up, v_ffn2_w_down, v_ple_norm, v_w_ple_proj, v_w_ple_gate, v_final_norm):
    args = dict(locals())
    wts = {n: args[n] for n in WEIGHTS}
    mom_m = {n: args["m_" + n] for n in WEIGHTS}
    mom_v = {n: args["v_" + n] for n in WEIGHTS}

    xs, ps, tgt = x[0], p[0, 0], loss_target[0]
    s, d = xs.shape
    fox_w = w_branch_fox.shape[1]
    gla_vw = w_branch_gla.shape[1]
    fox_heads = fox_w // HEAD_DIM
    gla_heads = gla_vw // GLA_VAL_DIM
    gla_kw = gla_heads * HEAD_DIM
    rank = gla_gate_up.shape[1]

    c_fl = 3 * fox_w
    o_gr, o_gq, o_gk = gla_vw, 2 * gla_vw, 2 * gla_vw + gla_kw
    o_fl = o_gk + gla_kw
    o_gd = o_fl + LANES
    rest_w = o_gd + LANES
    padded = c_fl + rest_w
    seg = [(c_fl, 0), (fox_heads, c_fl + o_fl), (gla_kw, c_fl + o_gq), (gla_kw, c_fl + o_gk), (gla_vw, c_fl),
           (gla_vw, c_fl + o_gr), (rank, c_fl + o_gd)]
    dest = np.concatenate([np.arange(w_, dtype=np.int32) + o_ for w_, o_ in seg])
    ws = w_in.shape[2]
    wp = -(-ws // LANES) * LANES
    tw = 256 if padded % 256 == 0 else LANES
    tbl, (t_idx, t_val, t_width), (s_idx, s_val, s_width) = _sel_tables(dest, ws, wp, tw)
    tbl = jnp.asarray(tbl)

    full = {}
    for n in SHARDED:
        sh = wts[n][0].astype(BF16)
        if n == "w_in":
            sh = _pad_cols(sh, wp)
        full[n] = _all_gather(sh, name=f"gather_{n}")
    for n in ROW_SHARDED:
        full[n] = full[n].reshape(-1, full[n].shape[2])
    win = _win_unshard(full["w_in"], tbl, jnp.asarray(t_idx), jnp.asarray(t_val), t_width, tw=tw, padded=padded,
                       name="in_proj_unshard")
    win_fox, win_rest = win[:, :c_fl], win[:, c_fl:]
    gup = full["gla_gate_up"].transpose(1, 0, 2).reshape(rank, gla_kw)
    gup = jnp.concatenate([gup, jnp.zeros((LANES - rank, gla_kw), BF16)], axis=0)
    fbias = _pad_cols(fox_forget_bias, LANES)
    bmg_f, bmg_g = b_merge_gate[:, :d], b_merge_gate[:, d:]
    ghn = jnp.tile(gla_head_norm, (1, gla_heads))

    h1, ffn1_saved = _ffn_fwd(xs, ffn1_norm, full["ffn1_w_gate"], full["ffn1_w_up"], full["ffn1_w_down"], tag="ffn1")
    u = _rms_fwd(h1, mix_norm, name="mix_norm")
    zf = _mm(u, win_fox, mode="nn", name="in_proj_fox", out_dtype=BF16)
    zr = _mm(u, win_rest, mode="nn", name="in_proj_rest")
    gz = _mm(u, full["w_merge_gate"], mode="nn", name="merge_gate")

    log_f = _rowwise(lambda fl, b: _log_sigmoid(fl + b), [(zr, LANES, o_fl // LANES)], [fbias], [(LANES, F32)],
                     name="forget_gate")[0]
    f_cum = _cumsum(log_f[:, :fox_heads].T, reverse=False, name="forget_cumsum")
    f_col, f_row = f_cum[:, :, None], f_cum[:, None, :]
    y_fox, y_fox_bf, lse = _fox_fwd(zf, f_col, f_row, heads=fox_heads, name="fox_fwd")

    def decay_fn(gd, gupv, gb):
        return _log_sigmoid(_dot(gd, gupv, NN) + gb) * (1.0 / GLA_GATE_TAU)

    la = _rowwise(decay_fn, [(zr, LANES, o_gd // LANES)], [gup, gla_gate_bias], [(gla_kw, F32)], name="gla_decay", rc=128)[0]
    q_blk, k_blk = o_gq // HEAD_DIM, o_gk // HEAD_DIM
    o_gla, states = _gla_fwd(zr, la, heads=gla_heads, q_blk=q_blk, k_blk=k_blk, name="gla_fwd")

    def gla_out_fn(o, gr, g):
        outs = []
        for hh in range(gla_heads):
            sl = slice(hh * GLA_VAL_DIM, (hh + 1) * GLA_VAL_DIM)
            _, oh = _rms_parts(o[:, sl])
            outs.append(oh * g[:, sl] * _silu_parts(gr[:, sl])[0])
        return jnp.concatenate(outs, axis=1)

    y_gla = _rowwise(gla_out_fn, [o_gla, (zr, gla_vw, o_gr // gla_vw)], [ghn], [(gla_vw, BF16)], name="gla_out")[0]
    br_f = _mm(y_fox_bf, full["w_branch_fox"], mode="nn", name="branch_fox")
    br_g = _mm(y_gla, full["w_branch_gla"], mode="nn", name="branch_gla")

    def merge_fn(zf_, zg_, bf_, bg_, b1, b2):
        return _sigmoid(zf_ + b1) * bf_ + _sigmoid(zg_ + b2) * bg_

    merged = _rowwise(merge_fn, [(gz, d, 0), (gz, d, 1), br_f, br_g], [bmg_f, bmg_g], [(d, BF16)], name="merge")[0]
    h2 = _mm(merged, full["w_out"], mode="nn", name="out_proj", add=h1)
    h3, ffn2_saved = _ffn_fwd(h2, ffn2_norm, full["ffn2_w_gate"], full["ffn2_w_up"], full["ffn2_w_down"], tag="ffn2")
    n3 = _rms_fwd(h3, ple_norm, name="ple_norm")
    gl = _mm(n3, full["w_ple_gate"], mode="nn", name="ple_gate")
    pe = _mm(ps, full["w_ple_proj"], mode="nn", name="ple_proj")

    def head_fn(h3b, glb, peb, tb, gfin):
        pg = _sigmoid(glb)
        h4 = h3b + pg * peb
        r, xh = _rms_parts(h4)
        err = xh * gfin - tb
        dy = err * (1.0 / d)
        t = dy * gfin
        dh4 = r * (t - xh * jnp.mean(t * xh, axis=-1, keepdims=True))
        return dh4, dh4 * pg, dh4 * peb * pg * (1.0 - pg), _colsum(err * err), _colsum(dy * xh)

    dh4, dpe, dgl, loss_cols, d_final = _rowwise(
        head_fn, [h3, gl, pe, tgt], [final_norm.reshape(1, d)], [(d, F32), (d, BF16), (d, BF16)], [d, d], name="loss_head")
    loss = lax.psum(0.5 * jnp.sum(loss_cols) / d, AXES)

    grads = {"final_norm": d_final.reshape(d)}
    grads["w_ple_proj"] = _mm(ps, dpe, mode="tn", name="ple_proj_dw", out_dtype=BF16, out_chunked=True)
    grads["w_ple_gate"] = _mm(n3, dgl, mode="tn", name="ple_gate_dw", out_dtype=BF16)
    dn3 = _mm(dgl, full["w_ple_gate"], mode="nt", name="ple_gate_dx")
    dh3, dh3_bf, grads["ple_norm"] = _rms_bwd(h3, dn3, ple_norm, dh4, name="ple_norm_bwd")
    dh2, dh2_bf, grads["ffn2_norm"], grads["ffn2_w_gate"], grads["ffn2_w_up"], grads["ffn2_w_down"] = _ffn_bwd(
        h2, ffn2_norm, full["ffn2_w_gate"], full["ffn2_w_up"], full["ffn2_w_down"], ffn2_saved, dh3, dh3_bf, tag="ffn2")

    dmerged = _mm(dh2_bf, full["w_out"], mode="nt", name="out_proj_dx")
    grads["w_out"] = _mm(merged, dh2_bf, mode="tn", name="out_proj_dw", out_dtype=BF16)

    def merge_bwd_fn(zf_, zg_, bf_, bg_, dm, b1, b2):
        sf, sg = _sigmoid(zf_ + b1), _sigmoid(zg_ + b2)
        dz = jnp.concatenate([dm * bf_ * sf * (1.0 - sf), dm * bg_ * sg * (1.0 - sg)], axis=1)
        return dm * sf, dm * sg, dz, _colsum(dz)

    dbr_f, dbr_g, dgz, grads["b_merge_gate"] = _rowwise(
        merge_bwd_fn, [(gz, d, 0), (gz, d, 1), br_f, br_g, dmerged], [bmg_f, bmg_g],
        [(d, BF16), (d, BF16), (2 * d, BF16)], [2 * d], name="merge_bwd")
    grads["w_merge_gate"] = _mm(u, dgz, mode="tn", name="merge_gate_dw", out_dtype=BF16, out_chunked=True)
    grads["w_branch_fox"] = _mm(y_fox_bf, dbr_f, mode="tn", name="branch_fox_dw", out_dtype=BF16, out_chunked=True)
    grads["w_branch_gla"] = _mm(y_gla, dbr_g, mode="tn", name="branch_gla_dw", out_dtype=BF16, out_chunked=True)
    dy_fox = _mm(dbr_f, full["w_branch_fox"], mode="nt", name="branch_fox_dx")
    dy_gla = _mm(dbr_g, full["w_branch_gla"], mode="nt", name="branch_gla_dx")

    def gla_out_bwd_fn(o, gr, dy, g):
        dos, dgrs, dgs = [], [], []
        for hh in range(gla_heads):
            sl = slice(hh * GLA_VAL_DIM, (hh + 1) * GLA_VAL_DIM)
            r, oh = _rms_parts(o[:, sl])
            si, dsi = _silu_parts(gr[:, sl])
            don = dy[:, sl] * si
            dgrs.append(dy[:, sl] * oh * g[:, sl] * dsi)
            t = don * g[:, sl]
            dos.append(r * (t - oh * jnp.mean(t * oh, axis=-1, keepdims=True)))
            dgs.append(_colsum(don * oh))
        return jnp.concatenate(dos, axis=1), jnp.concatenate(dgrs, axis=1), jnp.concatenate(dgs, axis=1)

    do_gla, dgr, d_ghn = _rowwise(gla_out_bwd_fn, [o_gla, (zr, gla_vw, o_gr // gla_vw), dy_gla], [ghn],
                                  [(gla_vw, F32), (gla_vw, F32)], [gla_vw], name="gla_out_bwd")
    grads["gla_head_norm"] = d_ghn.reshape(gla_heads, GLA_VAL_DIM).sum(axis=0, keepdims=True)
    dgq, dgk, dgv, dla = _gla_bwd(zr, la, do_gla, states, heads=gla_heads, q_blk=q_blk, k_blk=k_blk, name="gla_bwd")

    def decay_bwd_fn(dl, gd, gupv, gb):
        pre = _dot(gd, gupv, NN) + gb
        dpre = dl * (1.0 / GLA_GATE_TAU) * _sigmoid(-pre)
        return _dot(dpre, gupv, NT), dpre, _colsum(dpre)

    dgd, dpre_bf, grads["gla_gate_bias"] = _rowwise(
        decay_bwd_fn, [dla, (zr, LANES, o_gd // LANES)], [gup, gla_gate_bias], [(LANES, F32), (gla_kw, BF16)], [gla_kw],
        name="gla_decay_bwd", rc=128)
    d_gup = _mm(zr[:, o_gd:o_gd + LANES], dpre_bf, mode="tn", name="gla_gate_up_dw")[:rank]
    grads["gla_gate_up"] = d_gup.reshape(rank, N_DEV, gla_kw // N_DEV).transpose(1, 0, 2).astype(BF16)

    def delta_fn(dyv, ov):
        outs = []
        for hh in range(fox_heads):
            sl = slice(hh * HEAD_DIM, (hh + 1) * HEAD_DIM)
            outs.append(jnp.broadcast_to(jnp.sum(dyv[:, sl] * ov[:, sl], axis=-1, keepdims=True), (dyv.shape[0], HEAD_DIM)))
        return jnp.concatenate(outs, axis=1)

    delta = _rowwise(delta_fn, [dy_fox, y_fox], [], [(fox_w, F32)], name="fox_delta")[0]
    dfq, dfk, dfv, d_fcol, d_frow = _fox_bwd(zf, dy_fox, lse, delta, f_col, f_row, heads=fox_heads, name="fox_bwd")
    d_fcum = d_fcol[:, ::HEAD_DIM].T + d_frow.reshape(fox_heads, s)
    d_logf = _cumsum(d_fcum, reverse=True, name="forget_cumsum_bwd")
    d_logf = _pad_cols(d_logf.T, LANES)

    def forget_bwd_fn(dl, fl, b):
        dfl_ = dl * _sigmoid(-(fl + b))
        return dfl_, _colsum(dfl_)

    dfl, d_fbias = _rowwise(forget_bwd_fn, [d_logf, (zr, LANES, o_fl // LANES)], [fbias], [(LANES, F32)], [LANES],
                            name="forget_gate_bwd")
    grads["fox_forget_bias"] = d_fbias[:, :fox_heads]

    dz = jnp.concatenate([dfq, dfk, dfv, dgv, dgr, dgq, dgk, dfl, dgd], axis=1).astype(BF16)
    dwin = _mm(u, dz, mode="tn", name="in_proj_dw", out_dtype=BF16)
    grads["w_in"] = _win_to_shards(dwin, tbl, jnp.asarray(s_idx), jnp.asarray(s_val), s_width, tw=tw, wp=wp,
                                   name="in_proj_dw_shards")
    du = _mm(dgz, full["w_merge_gate"], mode="nt", name="merge_gate_dx")
    du = _mm(dz, win, mode="nt", name="in_proj_dx", add=du)
    dh1, dh1_bf, grads["mix_norm"] = _rms_bwd(h1, du, mix_norm, dh2, name="mix_norm_bwd")
    dx, _, grads["ffn1_norm"], grads["ffn1_w_gate"], grads["ffn1_w_up"], grads["ffn1_w_down"] = _ffn_bwd(
        xs, ffn1_norm, full["ffn1_w_gate"], full["ffn1_w_up"], full["ffn1_w_down"], ffn1_saved, dh1, dh1_bf, tag="ffn1")

    outs = {}
    for n in SHARDED:
        shape = wts[n].shape[1:]
        if n == "w_in":
            shape = (shape[0], wp)
        parts = _exchange(grads[n].reshape((N_DEV,) + shape), scatter=True, name=f"exchange_{n}")
        state = [_pad_cols(t_[n][0], shape[1]) for t_ in (wts, mom_m, mom_v)]
        res4 = _adamw(parts, *state, name=f"adamw_{n}")
        for kind, r_ in zip(("grad", "delta", "new_m", "new_v"), res4):
            outs[f"{kind}_{n}"] = r_[:, :wts[n].shape[2]][None]

    send_small, small_offs = _pack([grads[n] for n in REPLICATED], LANES, 8, F32)
    recv_small = _exchange(send_small, scatter=False, name="exchange_replicated")
    w_sm, _ = _pack([wts[n] for n in REPLICATED], LANES, 8, F32)
    m_sm, _ = _pack([mom_m[n] for n in REPLICATED], LANES, 8, F32)
    v_sm, _ = _pack([mom_v[n] for n in REPLICATED], LANES, 8, F32)
    small = _adamw(recv_small, w_sm, m_sm, v_sm, name="adamw_replicated")
    for kind, buf in zip(("grad", "delta", "new_m", "new_v"), small):
        fs = buf.reshape(-1)
        for n, o in zip(REPLICATED, small_offs):
            outs[f"{kind}_{n}"] = fs[o:o + wts[n].size].reshape(wts[n].shape)

    res = [loss, dx[None]]
    for kind in ("grad", "delta", "new_m", "new_v"):
        res += [outs[f"{kind}_{n}"] for n in WEIGHTS]
    return tuple(res)
```

```python
import functools

import jax
import jax.numpy as jnp
import numpy as np
from jax import lax
from jax.experimental import pallas as pl
from jax.experimental.pallas import tpu as pltpu

F32 = jnp.float32
BF16 = jnp.bfloat16
MESH = pl.DeviceIdType.MESH
AXES = ("x", "y", "c")
N_DEV = 8

VMEM_LIMIT_BYTES = 56 * 1024 * 1024
MM_BLOCK_BUDGET_BYTES = 40 * 1024 * 1024
LANES = 128

EPS = 1e-6
HEAD_DIM = 128
GLA_VAL_DIM = 256
GLA_CHUNK = 64
GLA_GATE_TAU = 16.0
ADAM_LR, ADAM_B1, ADAM_B2, ADAM_EPS, ADAM_WD, ADAM_STEP = 0.001, 0.9, 0.999, 1e-08, 0.01, 10

HIGHEST = lax.Precision.HIGHEST
NN = (((1,), (0,)), ((), ()))
NT = (((1,), (1,)), ((), ()))
TN = (((0,), (0,)), ((), ()))


def _cparams(sem):
    return pltpu.CompilerParams(dimension_semantics=sem, vmem_limit_bytes=VMEM_LIMIT_BYTES)


def _pick(dim, cands):
    for c in cands:
        if dim % c == 0:
            return c
    return dim


def _bf(v):
    return v if v.dtype == BF16 else v.astype(BF16)


def _dot(a, b, dims):
    return lax.dot_general(_bf(a), _bf(b), dims, preferred_element_type=F32)


def _sigmoid(v):
    return 1.0 / (1.0 + jnp.exp(-v))


def _log_sigmoid(v):
    return jnp.minimum(v, 0.0) - jnp.log(1.0 + jnp.exp(-jnp.abs(v)))


def _logical(v):
    return (v.shape[1], v.shape[0] * v.shape[2]) if v.ndim == 3 else v.shape


def _mm(a, b, *, mode, name, out_dtype=F32, add=None, scale=1.0, out_chunked=False, comm=()):
    la, lb = _logical(a), _logical(b)
    if mode == "nn":
        (m, k), (k2, n) = la, lb
        a_minor, b_minor = "k", "n"
    elif mode == "nt":
        (m, k), (n, k2) = la, lb
        a_minor, b_minor = "k", "k"
    else:
        (k, m), (k2, n) = la, lb
        a_minor, b_minor = "m", "n"
    assert k == k2, (name, a.shape, b.shape)
    forced = {}
    for v, minor in ((a, a_minor), (b, b_minor)):
        if v.ndim == 3:
            assert forced.get(minor, v.shape[2]) == v.shape[2], name
            forced[minor] = v.shape[2]
    if out_chunked:
        assert forced.get("n", n // N_DEV) == n // N_DEV, name
        forced["n"] = n // N_DEV
    tm = forced.get("m") or _pick(m, (1024, 512, 256, 128))
    tn = forced.get("n") or _pick(n, (1408, 1280, 1024, 512, 256, 128))
    tk = forced.get("k")
    if not tk:
        def blocks_bytes(t):
            io = tm * t * a.dtype.itemsize + t * tn * b.dtype.itemsize
            return 2 * (io + tm * tn * (jnp.dtype(out_dtype).itemsize + (4 if add is not None else 0))) + tm * tn * 4
        tk = k if blocks_bytes(k) <= MM_BLOCK_BUDGET_BYTES else _pick(k, (512, 640, 256, 128))
    nk = k // tk
    dims = {"nn": NN, "nt": NT, "tn": TN}[mode]
    gi, gj, gk = (lambda i, j, kk: i), (lambda i, j, kk: j), (lambda i, j, kk: kk)

    def spec(v, t_major, t_minor, g_major, g_minor):
        if v is not None and v.ndim == 3:
            return pl.BlockSpec((None, t_major, v.shape[2]), lambda i, j, kk: (g_minor(i, j, kk), g_major(i, j, kk), 0))
        return pl.BlockSpec((t_major, t_minor), lambda i, j, kk: (g_major(i, j, kk), g_minor(i, j, kk)))

    a_spec = spec(a, tk, tm, gk, gi) if mode == "tn" else spec(a, tm, tk, gi, gk)
    b_spec = spec(b, tn, tk, gj, gk) if mode == "nt" else spec(b, tk, tn, gk, gj)
    if out_chunked:
        o_spec = pl.BlockSpec((None, tm, tn), lambda i, j, kk: (j, i, 0))
        out_shape = jax.ShapeDtypeStruct((N_DEV, m, tn), out_dtype)
    else:
        o_spec = pl.BlockSpec((tm, tn), lambda i, j, kk: (i, j))
        out_shape = jax.ShapeDtypeStruct((m, n), out_dtype)
    has_add = add is not None
    assert not (has_add and out_chunked), name

    n_in = 3 if has_add else 2
    n_comm = len(comm)
    grid = (m // tm, n // tn, nk)

    def body(*refs):
        a_ref, b_ref = refs[0], refs[1]
        add_ref = refs[2] if has_add else None
        o_ref = refs[n_in + n_comm]
        scratch = refs[n_in + 2 * n_comm + 1:]
        acc_ref = scratch[0] if nk > 1 else None
        sems = scratch[1 if nk > 1 else 0:]
        i, j, kk = pl.program_id(0), pl.program_id(1), pl.program_id(2)
        tasks = [_comm_ops(kind, refs[n_in + t], refs[n_in + n_comm + 1 + t], *sems[3 * t:3 * t + 3])
                 for t, (kind, _) in enumerate(comm)]

        if tasks:
            @pl.when((i == 0) & (j == 0) & (kk == 0))
            def _():
                for start, _ in tasks:
                    start()

        def finish(r):
            if scale != 1.0:
                r = r * scale
            if has_add:
                r = r + add_ref[...]
            o_ref[...] = r.astype(o_ref.dtype)

        if nk == 1:
            finish(_dot(a_ref[...], b_ref[...], dims))
        else:
            @pl.when(kk == 0)
            def _():
                acc_ref[...] = jnp.zeros_like(acc_ref)

            acc_ref[...] += _dot(a_ref[...], b_ref[...], dims)

            @pl.when(kk == nk - 1)
            def _():
                finish(acc_ref[...])

        if tasks:
            @pl.when((i == grid[0] - 1) & (j == grid[1] - 1) & (kk == nk - 1))
            def _():
                for _, done in tasks:
                    done()

    any_spec = pl.BlockSpec(memory_space=pl.ANY)
    scratch_shapes = [pltpu.VMEM((tm, tn), F32)] if nk > 1 else []
    for _ in comm:
        scratch_shapes += [pltpu.SemaphoreType.DMA((N_DEV - 1,)), pltpu.SemaphoreType.DMA((N_DEV - 1,)),
                           pltpu.SemaphoreType.DMA]
    res = pl.pallas_call(
        body, name=name, grid=grid,
        in_specs=[a_spec, b_spec] + ([o_spec] if has_add else []) + [any_spec] * n_comm,
        out_specs=[o_spec] + [any_spec] * n_comm,
        out_shape=[out_shape] + [_comm_out_shape(kind, v) for kind, v in comm],
        scratch_shapes=scratch_shapes,
        compiler_params=_cparams(("arbitrary",) * 3 if comm else ("parallel", "parallel", "arbitrary")),
    )(*([a, b] + ([add] if has_add else []) + [v for _, v in comm]))
    return (res[0], res[1:]) if comm else res[0]


def _rowwise(fn, rows, consts, outs, accs=(), *, name, rc=None):
    rows = [r if isinstance(r, tuple) else (r, r.shape[1], 0) for r in rows]
    m = rows[0][0].shape[0]
    widths = [w for _, w, _ in rows] + [n for n, _ in outs]
    row_bytes = sum(w * r.dtype.itemsize for r, w, _ in rows) + sum(n * jnp.dtype(d).itemsize for n, d in outs)
    tm = 1024
    while tm > 16 and (m % tm or 2 * tm * row_bytes > 24 * 1024 * 1024):
        tm //= 2
    if m % tm:
        tm = m
    if rc is None:
        rc = 16
        while rc * 2 <= tm and rc * 2 * max(widths) <= 32768:
            rc *= 2
    rc = min(rc, tm)
    nr, nc, no = len(rows), len(consts), len(outs)

    def body(*refs):
        in_refs, c_refs = refs[:nr], refs[nr:nr + nc]
        o_refs, a_refs = refs[nr + nc:nr + nc + no], refs[nr + nc + no:]

        @pl.when(pl.program_id(0) == 0)
        def _():
            for r in a_refs:
                r[...] = jnp.zeros_like(r)

        cvals = [c[...] for c in c_refs]

        def chunk(ci, carry):
            sl = pl.ds(pl.multiple_of(ci * rc, rc), rc)
            res = fn(*[r[sl, :] for r in in_refs], *cvals)
            if not isinstance(res, (tuple, list)):
                res = (res,)
            for r, v in zip(o_refs, res[:no]):
                r[sl, :] = v.astype(r.dtype)
            for r, v in zip(a_refs, res[no:]):
                r[...] += v
            return carry

        lax.fori_loop(0, tm // rc, chunk, 0)

    in_specs = [pl.BlockSpec((tm, w), functools.partial(lambda i, cb: (i, cb), cb=cb)) for _, w, cb in rows]
    in_specs += [pl.BlockSpec(c.shape, lambda i: (0, 0)) for c in consts]
    out_specs = [pl.BlockSpec((tm, n), lambda i: (i, 0)) for n, _ in outs]
    out_specs += [pl.BlockSpec((1, n), lambda i: (0, 0)) for n in accs]
    out_shape = [jax.ShapeDtypeStruct((m, n), d) for n, d in outs] + [jax.ShapeDtypeStruct((1, n), F32) for n in accs]
    res = pl.pallas_call(
        body, name=name, grid=(m // tm,),
        in_specs=in_specs, out_specs=out_specs, out_shape=out_shape,
        compiler_params=_cparams(("arbitrary",)),
    )(*[r for r, _, _ in rows], *consts)
    return res


def _colsum(v):
    return jnp.sum(v, axis=0, keepdims=True)


def _rms_parts(xv):
    r = lax.rsqrt(jnp.mean(xv * xv, axis=-1, keepdims=True) + EPS)
    return r, xv * r


def _rms_fwd(xv, g, *, name):
    d = xv.shape[1]

    def fn(xb, gb):
        _, xh = _rms_parts(xb)
        return xh * gb

    return _rowwise(fn, [xv], [g], [(d, BF16)], name=name)[0]


def _rms_bwd(xv, dy, g, add, *, name):
    d = xv.shape[1]

    def fn(xb, dyb, addb, gb):
        r, xh = _rms_parts(xb)
        t = dyb * gb
        dx = r * (t - xh * jnp.mean(t * xh, axis=-1, keepdims=True)) + addb
        return dx, dx, _colsum(dyb * xh)

    return _rowwise(fn, [xv, dy, add], [g], [(d, F32), (d, BF16)], [d], name=name)


def _silu_parts(a):
    sg = _sigmoid(a)
    return a * sg, sg * (1.0 + a * (1.0 - sg))


def _rows(v):
    return v.reshape(v.shape[0] * v.shape[1], v.shape[2])


def _ffn_fwd(h, g, ov, *, tag):
    s = h.shape[0]
    n = _rms_fwd(h, g, name=f"{tag}_norm")
    a = ov.mm(n, ov.weight(f"{tag}_w_gate"), mode="nn", name=f"{tag}_gate", out_chunked=True)
    b = ov.mm(n, ov.weight(f"{tag}_w_up"), mode="nn", name=f"{tag}_up", out_chunked=True)
    c = a.shape[2]
    hm = _rowwise(lambda av, bv: _silu_parts(av)[0] * bv, [_rows(a), _rows(b)], [], [(c, BF16)], name=f"{tag}_act")[0]
    hm = hm.reshape(N_DEV, s, c)
    out = ov.mm(hm, ov.weight(f"{tag}_w_down"), mode="nn", name=f"{tag}_down", add=h, scale=0.5)
    return out, (n, a, b, hm)


def _ffn_bwd(h, g, ov, saved, dout, dout_bf, *, tag):
    n, a, b, hm = saved
    wg, wu, wd = ov.weight(f"{tag}_w_gate"), ov.weight(f"{tag}_w_up"), ov.weight(f"{tag}_w_down")
    s, c = h.shape[0], wg.shape[2]
    d_wd = ov.mm(hm, dout_bf, mode="tn", name=f"{tag}_down_dw", scale=0.5, out_dtype=BF16)
    ov.grad(f"{tag}_w_down", d_wd)
    dhm = ov.mm(dout_bf, wd, mode="nt", name=f"{tag}_down_dx", scale=0.5, out_chunked=True)

    def act_bwd(av, bv, dv):
        si, dsi = _silu_parts(av)
        return dv * bv * dsi, dv * si

    da, db = _rowwise(act_bwd, [_rows(a), _rows(b), _rows(dhm)], [], [(c, BF16), (c, BF16)], name=f"{tag}_act_bwd")
    da, db = da.reshape(N_DEV, s, c), db.reshape(N_DEV, s, c)
    ov.grad(f"{tag}_w_gate", ov.mm(n, da, mode="tn", name=f"{tag}_gate_dw", out_dtype=BF16, out_chunked=True))
    ov.grad(f"{tag}_w_up", ov.mm(n, db, mode="tn", name=f"{tag}_up_dw", out_dtype=BF16, out_chunked=True))
    dn = ov.mm(da, wg, mode="nt", name=f"{tag}_gate_dx")
    dn = ov.mm(db, wu, mode="nt", name=f"{tag}_up_dx", add=dn)
    dh, dh_bf, dg = _rms_bwd(h, dn, g, dout, name=f"{tag}_norm_bwd")
    return dh, dh_bf, dg


def _cumsum(xv, *, reverse, name):
    h, s = xv.shape
    t = _pick(s, (512, 256, 128))
    nb = s // t

    def blk(j):
        return (0, nb - 1 - j) if reverse else (0, j)

    def body(x_ref, o_ref, carry):
        @pl.when(pl.program_id(0) == 0)
        def _():
            carry[...] = jnp.zeros_like(carry)

        i0 = lax.broadcasted_iota(jnp.int32, (t, t), 0)
        i1 = lax.broadcasted_iota(jnp.int32, (t, t), 1)
        tri = ((i0 >= i1) if reverse else (i0 <= i1)).astype(F32)
        xb = x_ref[...]
        o_ref[...] = jnp.dot(xb, tri, precision=HIGHEST, preferred_element_type=F32) + carry[...]
        carry[...] += jnp.sum(xb, axis=1, keepdims=True)

    return pl.pallas_call(
        body, name=name, grid=(nb,),
        in_specs=[pl.BlockSpec((h, t), blk)], out_specs=pl.BlockSpec((h, t), blk),
        out_shape=jax.ShapeDtypeStruct((h, s), F32),
        scratch_shapes=[pltpu.VMEM((h, 1), F32)],
        compiler_params=_cparams(("arbitrary",)),
    )(xv)


def _fox_tiles(s):
    t = _pick(s, (512, 256, 128))
    return t, t


def _fox_fwd(zf, f_col, f_row, *, heads, name):
    s = zf.shape[0]
    tq, tk = _fox_tiles(s)
    nq, nk = s // tq, s // tk
    scale = HEAD_DIM ** -0.5
    w = heads * HEAD_DIM

    def last_k(i):
        return (i * tq + tq - 1) // tk

    def body(q_ref, k_ref, v_ref, fq_ref, fk_ref, o32_ref, o16_ref, lse_ref, m_sc, l_sc, acc_sc):
        i, j = pl.program_id(1), pl.program_id(2)

        @pl.when(j == 0)
        def _():
            m_sc[...] = jnp.full_like(m_sc, -jnp.inf)
            l_sc[...] = jnp.zeros_like(l_sc)
            acc_sc[...] = jnp.zeros_like(acc_sc)

        @pl.when(j <= last_k(i))
        def _():
            sc = _dot(q_ref[...], k_ref[...], NT) * scale + fq_ref[...] - fk_ref[...]
            qpos = i * tq + lax.broadcasted_iota(jnp.int32, (tq, tk), 0)
            kpos = j * tk + lax.broadcasted_iota(jnp.int32, (tq, tk), 1)
            sc = jnp.where(kpos <= qpos, sc, -jnp.inf)
            m_new = jnp.maximum(m_sc[...], jnp.max(sc, axis=-1, keepdims=True))
            alpha = jnp.exp(m_sc[...] - m_new)
            pr = jnp.exp(sc - m_new)
            l_sc[...] = alpha * l_sc[...] + jnp.sum(pr, axis=-1, keepdims=True)
            acc_sc[...] = alpha * acc_sc[...] + _dot(pr, v_ref[...], NN)
            m_sc[...] = m_new

        @pl.when(j == nk - 1)
        def _():
            o = acc_sc[...] / l_sc[...]
            o32_ref[...] = o
            o16_ref[...] = o.astype(BF16)
            lse_ref[...] = jnp.broadcast_to(m_sc[...] + jnp.log(l_sc[...]), (tq, HEAD_DIM))

    def kv_blk(off):
        return lambda h, i, j: (jnp.minimum(j, last_k(i)), off + h)

    o_spec = pl.BlockSpec((tq, HEAD_DIM), lambda h, i, j: (i, h))
    return pl.pallas_call(
        body, name=name, grid=(heads, nq, nk),
        in_specs=[
            pl.BlockSpec((tq, HEAD_DIM), lambda h, i, j: (i, h)),
            pl.BlockSpec((tk, HEAD_DIM), kv_blk(heads)),
            pl.BlockSpec((tk, HEAD_DIM), kv_blk(2 * heads)),
            pl.BlockSpec((None, tq, 1), lambda h, i, j: (h, i, 0)),
            pl.BlockSpec((None, 1, tk), lambda h, i, j: (h, 0, jnp.minimum(j, last_k(i)))),
        ],
        out_specs=[o_spec, o_spec, o_spec],
        out_shape=[jax.ShapeDtypeStruct((s, w), F32), jax.ShapeDtypeStruct((s, w), BF16),
                   jax.ShapeDtypeStruct((s, w), F32)],
        scratch_shapes=[pltpu.VMEM((tq, 1), F32), pltpu.VMEM((tq, 1), F32), pltpu.VMEM((tq, HEAD_DIM), F32)],
        compiler_params=_cparams(("parallel", "parallel", "arbitrary")),
    )(zf, zf, zf, f_col, f_row)


def _fox_bwd(zf, do, lse, delta, f_col, f_row, *, heads, name):
    s = zf.shape[0]
    tq, tk = _fox_tiles(s)
    nq, nk = s // tq, s // tk
    scale = HEAD_DIM ** -0.5
    w = heads * HEAD_DIM

    def first_q(j):
        return (j * tk) // tq

    def body(q_ref, k_ref, v_ref, do_ref, lse_ref, dl_ref, fq_ref, fk_ref, dq_ref, dk_ref, dv_ref, dfq_ref, dfk_ref):
        j, i = pl.program_id(1), pl.program_id(2)

        @pl.when((j == 0) & (i == 0))
        def _():
            dq_ref[...] = jnp.zeros_like(dq_ref)
            dfq_ref[...] = jnp.zeros_like(dfq_ref)

        @pl.when(i == 0)
        def _():
            dk_ref[...] = jnp.zeros_like(dk_ref)
            dv_ref[...] = jnp.zeros_like(dv_ref)
            dfk_ref[...] = jnp.zeros_like(dfk_ref)

        @pl.when(i >= first_q(j))
        def _():
            q, k, v = q_ref[...], k_ref[...], v_ref[...]
            dob = do_ref[...].astype(BF16)
            sc = _dot(q, k, NT) * scale + fq_ref[...] - fk_ref[...]
            qpos = i * tq + lax.broadcasted_iota(jnp.int32, (tq, tk), 0)
            kpos = j * tk + lax.broadcasted_iota(jnp.int32, (tq, tk), 1)
            sc = jnp.where(kpos <= qpos, sc, -jnp.inf)
            pr = jnp.exp(sc - lse_ref[:, 0:1])
            dv_ref[...] += _dot(pr, dob, TN)
            dp = _dot(dob, v, NT)
            ds = pr * (dp - dl_ref[:, 0:1])
            dsb = ds.astype(BF16)
            dk_ref[...] += _dot(dsb, q, TN) * scale
            rows = pl.ds(pl.multiple_of(i * tq, tq), tq)
            dq_ref[rows, :] += _dot(dsb, k, NN) * scale
            dfq_ref[rows, :] += jnp.broadcast_to(jnp.sum(ds, axis=1, keepdims=True), (tq, HEAD_DIM))
            dfk_ref[...] -= jnp.sum(ds, axis=0, keepdims=True)

    def q_blk(h, j, i):
        return (jnp.maximum(i, first_q(j)), h)

    return pl.pallas_call(
        body, name=name, grid=(heads, nk, nq),
        in_specs=[
            pl.BlockSpec((tq, HEAD_DIM), q_blk),
            pl.BlockSpec((tk, HEAD_DIM), lambda h, j, i: (j, heads + h)),
            pl.BlockSpec((tk, HEAD_DIM), lambda h, j, i: (j, 2 * heads + h)),
            pl.BlockSpec((tq, HEAD_DIM), q_blk),
            pl.BlockSpec((tq, HEAD_DIM), q_blk),
            pl.BlockSpec((tq, HEAD_DIM), q_blk),
            pl.BlockSpec((None, tq, 1), lambda h, j, i: (h, jnp.maximum(i, first_q(j)), 0)),
            pl.BlockSpec((None, 1, tk), lambda h, j, i: (h, 0, j)),
        ],
        out_specs=[
            pl.BlockSpec((s, HEAD_DIM), lambda h, j, i: (0, h)),
            pl.BlockSpec((tk, HEAD_DIM), lambda h, j, i: (j, h)),
            pl.BlockSpec((tk, HEAD_DIM), lambda h, j, i: (j, h)),
            pl.BlockSpec((s, HEAD_DIM), lambda h, j, i: (0, h)),
            pl.BlockSpec((None, 1, tk), lambda h, j, i: (h, 0, j)),
        ],
        out_shape=[jax.ShapeDtypeStruct((s, w), F32), jax.ShapeDtypeStruct((s, w), F32),
                   jax.ShapeDtypeStruct((s, w), F32), jax.ShapeDtypeStruct((s, w), F32),
                   jax.ShapeDtypeStruct((heads, 1, s), F32)],
        compiler_params=_cparams(("parallel", "arbitrary", "arbitrary")),
    )(zf, zf, zf, do, lse, delta, f_col, f_row)


def _gla_rows(s):
    return _pick(s, (256, 128, 64))


def _gla_chunk_terms(la_c, tri):
    a_cum = jnp.dot(tri, la_c, precision=HIGHEST, preferred_element_type=F32)
    a_tot = jnp.sum(la_c, axis=0, keepdims=True)
    return jnp.exp(a_tot - a_cum), jnp.exp(a_tot)


def _gla_fwd(zr, la, *, heads, q_blk, k_blk, name):
    s = zr.shape[0]
    c = GLA_CHUNK
    rows = _gla_rows(s)
    nsteps, ncs = s // rows, rows // c
    scale = HEAD_DIM ** -0.5

    def body(q_ref, k_ref, v_ref, la_ref, o_ref, st_ref, state):
        @pl.when(pl.program_id(1) == 0)
        def _():
            state[...] = jnp.zeros_like(state)

        tri = (lax.broadcasted_iota(jnp.int32, (c, c), 0) >= lax.broadcasted_iota(jnp.int32, (c, c), 1)).astype(F32)
        for t in range(ncs):
            sl = slice(t * c, (t + 1) * c)
            dec, e_tot = _gla_chunk_terms(la_ref[sl, :], tri)
            kd = k_ref[sl, :] * dec
            st_ref[t] = state[...]
            new = state[...] * e_tot + _dot(v_ref[sl, :], kd, TN)
            state[...] = new
            o_ref[sl, :] = _dot(q_ref[sl, :] * scale, new, NT)

    return pl.pallas_call(
        body, name=name, grid=(heads, nsteps),
        in_specs=[
            pl.BlockSpec((rows, HEAD_DIM), lambda h, i: (i, q_blk + h)),
            pl.BlockSpec((rows, HEAD_DIM), lambda h, i: (i, k_blk + h)),
            pl.BlockSpec((rows, GLA_VAL_DIM), lambda h, i: (i, h)),
            pl.BlockSpec((rows, HEAD_DIM), lambda h, i: (i, h)),
        ],
        out_specs=[
            pl.BlockSpec((rows, GLA_VAL_DIM), lambda h, i: (i, h)),
            pl.BlockSpec((None, ncs, GLA_VAL_DIM, HEAD_DIM), lambda h, i: (h, i, 0, 0)),
        ],
        out_shape=[jax.ShapeDtypeStruct((s, heads * GLA_VAL_DIM), F32),
                   jax.ShapeDtypeStruct((heads, s // c, GLA_VAL_DIM, HEAD_DIM), F32)],
        scratch_shapes=[pltpu.VMEM((GLA_VAL_DIM, HEAD_DIM), F32)],
        compiler_params=_cparams(("parallel", "arbitrary")),
    )(zr, zr, zr, la)


def _gla_bwd(zr, la, do, states, *, heads, q_blk, k_blk, name):
    s = zr.shape[0]
    c = GLA_CHUNK
    rows = _gla_rows(s)
    nsteps, ncs = s // rows, rows // c
    scale = HEAD_DIM ** -0.5

    def body(q_ref, k_ref, v_ref, la_ref, do_ref, st_ref, dq_ref, dk_ref, dv_ref, dla_ref, dstate):
        @pl.when(pl.program_id(1) == 0)
        def _():
            dstate[...] = jnp.zeros_like(dstate)

        i0 = lax.broadcasted_iota(jnp.int32, (c, c), 0)
        i1 = lax.broadcasted_iota(jnp.int32, (c, c), 1)
        tri = (i0 >= i1).astype(F32)
        strict = (i0 > i1).astype(F32)
        for t in reversed(range(ncs)):
            sl = slice(t * c, (t + 1) * c)
            dec, e_tot = _gla_chunk_terms(la_ref[sl, :], tri)
            kd = k_ref[sl, :] * dec
            kdb = kd.astype(BF16)
            vb = v_ref[sl, :].astype(BF16)
            dob = do_ref[sl, :].astype(BF16)
            prev = st_ref[t]
            cur = prev * e_tot + _dot(vb, kdb, TN)
            d_cur = dstate[...] + _dot(dob, q_ref[sl, :] * scale, TN)
            d_cur_b = d_cur.astype(BF16)
            dq_ref[sl, :] = _dot(dob, cur, NN) * scale
            dv_ref[sl, :] = _dot(kdb, d_cur_b, NT)
            dkd = _dot(vb, d_cur_b, NN)
            d_tot = e_tot * jnp.sum(d_cur * prev, axis=0, keepdims=True)
            dk_ref[sl, :] = dkd * dec
            dla_ref[sl, :] = d_tot + jnp.dot(strict, dkd * kd, precision=HIGHEST, preferred_element_type=F32)
            dstate[...] = d_cur * e_tot

    def rev(i):
        return nsteps - 1 - i

    kq_spec = pl.BlockSpec((rows, HEAD_DIM), lambda h, i: (rev(i), h))
    v_spec = pl.BlockSpec((rows, GLA_VAL_DIM), lambda h, i: (rev(i), h))
    return pl.pallas_call(
        body, name=name, grid=(heads, nsteps),
        in_specs=[
            pl.BlockSpec((rows, HEAD_DIM), lambda h, i: (rev(i), q_blk + h)),
            pl.BlockSpec((rows, HEAD_DIM), lambda h, i: (rev(i), k_blk + h)),
            v_spec, kq_spec, v_spec,
            pl.BlockSpec((None, ncs, GLA_VAL_DIM, HEAD_DIM), lambda h, i: (h, rev(i), 0, 0)),
        ],
        out_specs=[kq_spec, kq_spec, v_spec, kq_spec],
        out_shape=[jax.ShapeDtypeStruct((s, heads * HEAD_DIM), F32), jax.ShapeDtypeStruct((s, heads * HEAD_DIM), F32),
                   jax.ShapeDtypeStruct((s, heads * GLA_VAL_DIM), F32), jax.ShapeDtypeStruct((s, heads * HEAD_DIM), F32)],
        scratch_shapes=[pltpu.VMEM((GLA_VAL_DIM, HEAD_DIM), F32)],
        compiler_params=_cparams(("parallel", "arbitrary")),
    )(zr, zr, zr, la, do, states)


def _my_place():
    x, y, c = lax.axis_index("x"), lax.axis_index("y"), lax.axis_index("c")
    return x, y, c


def _gather_ops(x_ref, out_ref, send_sems, recv_sems, local_sem):
    def plan():
        x, y, c = _my_place()
        me, sibling = (x, y, c), (x, y, 1 - c)
        chips = [(1 - x, y), (x, 1 - y), (1 - x, 1 - y)]

        def blk(px, py, pc):
            return out_ref.at[4 * px + 2 * py + pc]

        def copy(k, block, to, src=None):
            return pltpu.make_async_remote_copy(
                src_ref=blk(*block) if src is None else src, dst_ref=blk(*block),
                send_sem=send_sems.at[k], recv_sem=recv_sems.at[k], device_id=to, device_id_type=MESH)

        mine = pltpu.make_async_copy(x_ref, blk(*me), local_sem)
        first = [copy(0, me, sibling, src=x_ref)]
        first += [copy(1 + j, me, (*chip, c), src=x_ref) for j, chip in enumerate(chips)]
        passed = [copy(4 + j, (*chip, c), sibling) for j, chip in enumerate(chips)]
        landed = [copy(1 + j, (*chip, c), me) for j, chip in enumerate(chips)]
        from_sibling = [copy(0, sibling, me)] + [copy(4 + j, (*chip, 1 - c), me) for j, chip in enumerate(chips)]
        return mine, first, passed, landed, from_sibling

    def start():
        mine, first, _, _, _ = plan()
        mine.start()
        for cp in first:
            cp.start()

    def finish():
        mine, first, passed, landed, from_sibling = plan()
        for cp, fwd in zip(landed, passed):
            cp.wait_recv()
            fwd.start()
        for cp in from_sibling:
            cp.wait_recv()
        for cp in first + passed:
            cp.wait_send()
        mine.wait()

    return start, finish


def _exchange_ops(scatter, s_ref, r_ref, send_sems, recv_sems, local_sem):
    def plan():
        x, y, c = _my_place()
        me = 4 * x + 2 * y + c
        mine = pltpu.make_async_copy(s_ref.at[me] if scatter else s_ref, r_ref.at[me], local_sem)
        sends, recvs = [], []
        for k in range(1, N_DEV):
            px, py, pc = x ^ ((k >> 2) & 1), y ^ ((k >> 1) & 1), c ^ (k & 1)
            peer = 4 * px + 2 * py + pc
            src = s_ref.at[peer] if scatter else s_ref
            for dst, out in ((r_ref.at[me], sends), (r_ref.at[peer], recvs)):
                out.append(pltpu.make_async_remote_copy(
                    src_ref=src, dst_ref=dst, send_sem=send_sems.at[k - 1], recv_sem=recv_sems.at[k - 1],
                    device_id=(px, py, pc), device_id_type=MESH))
        return mine, sends, recvs

    def start():
        mine, sends, _ = plan()
        mine.start()
        for cp in sends:
            cp.start()

    def finish():
        mine, sends, recvs = plan()
        for cp in recvs:
            cp.wait_recv()
        for cp in sends:
            cp.wait_send()
        mine.wait()

    return start, finish


def _comm_ops(kind, src_ref, dst_ref, send_sems, recv_sems, local_sem):
    if kind == "gather":
        return _gather_ops(src_ref, dst_ref, send_sems, recv_sems, local_sem)
    return _exchange_ops(kind == "scatter", src_ref, dst_ref, send_sems, recv_sems, local_sem)


def _comm_out_shape(kind, v):
    return jax.ShapeDtypeStruct(v.shape if kind == "scatter" else (N_DEV,) + v.shape, v.dtype)


def _comm(kind, v, *, name):
    def body(s_ref, r_ref, send_sems, recv_sems, local_sem):
        start, finish = _comm_ops(kind, s_ref, r_ref, send_sems, recv_sems, local_sem)
        start()
        finish()

    return pl.pallas_call(
        body, name=name,
        out_shape=_comm_out_shape(kind, v),
        in_specs=[pl.BlockSpec(memory_space=pl.ANY)],
        out_specs=pl.BlockSpec(memory_space=pl.ANY),
        scratch_shapes=[pltpu.SemaphoreType.DMA((N_DEV - 1,)), pltpu.SemaphoreType.DMA((N_DEV - 1,)),
                        pltpu.SemaphoreType.DMA],
    )(v)


class _Overlap:
    US_PER_MB = {"gather": 52.0, "scatter": 97.0, "bcast": 97.0}
    MM_FLOPS_PER_US = 6.0e8

    def __init__(self):
        self.queue, self.results = [], {}

    def add(self, key, kind, v):
        self.queue.append((key, kind, v))

    def _cost(self, kind, v):
        blocks = N_DEV if kind == "scatter" else 1
        return v.size * v.dtype.itemsize / blocks / 2 ** 20 * self.US_PER_MB[kind]

    def mm(self, a, b, *, mode, **kw):
        la, lb = _logical(a), _logical(b)
        budget = 2.0 * la[0] * la[1] * (lb[0] if mode == "nt" else lb[1]) / self.MM_FLOPS_PER_US
        taken, cum = [], 0.0
        while self.queue:
            cost = self._cost(*self.queue[0][1:])
            if taken and cum + cost > budget:
                break
            taken.append(self.queue.pop(0))
            cum += cost
        if not taken:
            return _mm(a, b, mode=mode, **kw)
        out, res = _mm(a, b, mode=mode, comm=[(kind, v) for _, kind, v in taken], **kw)
        for (key, _, _), r in zip(taken, res):
            self.results[key] = r
        return out

    def get(self, key):
        if key not in self.results:
            key, kind, v = self.queue.pop([k for k, _, _ in self.queue].index(key))
            self.results[key] = _comm(kind, v, name=f"{kind}_{key}")
        return self.results[key]

    def weight(self, n):
        g = self.get(n)
        return g.reshape(-1, g.shape[2]) if n in ROW_SHARDED else g

    def grad(self, n, g):
        self.add("d_" + n, "scatter", g if g.ndim == 3 else g.reshape(N_DEV, g.shape[0] // N_DEV, g.shape[1]))


def _sel_tables(dest, ws, wp, tw):
    n_tiles = (int(dest.max()) + tw) // tw
    tbl = np.full((N_DEV, wp), -1, np.int32)
    for j in range(N_DEV):
        tbl[j, :ws] = dest[j * ws:(j + 1) * ws]
    by_tile = [sorted({j for j in range(N_DEV) if ((tbl[j] // tw) == t).any()}) for t in range(n_tiles)]
    by_shard = [sorted({int(t) for t in np.unique(tbl[j, :ws] // tw)}) for j in range(N_DEV)]

    def table(lists):
        width = max(len(v) for v in lists)
        idx = np.array([(v + [v[-1]] * width)[:width] if v else [0] * width for v in lists], np.int32)
        val = np.array([[1] * len(v) + [0] * (width - len(v)) for v in lists], np.int32)
        return idx.reshape(-1), val.reshape(-1), width

    return tbl[:, :, None], table(by_tile), table(by_shard)


def _sel_matrix(d_ref, t, wp, tw):
    cols = t * tw + lax.broadcasted_iota(jnp.int32, (wp, tw), 1)
    return (d_ref[...] == cols).astype(BF16)


def _win_unshard(g, tbl, idx, val, width, *, tw, padded, name):
    _, dm, wp = g.shape
    tm = _pick(dm, (1024, 512, 256, 128))

    def body(idx_ref, val_ref, g_ref, d_ref, o_ref, acc):
        t, s_ = pl.program_id(1), pl.program_id(2)

        @pl.when(s_ == 0)
        def _():
            acc[...] = jnp.zeros_like(acc)

        @pl.when(val_ref[t * width + s_] == 1)
        def _():
            acc[...] += _dot(g_ref[...], _sel_matrix(d_ref, t, wp, tw), NN)

        @pl.when(s_ == width - 1)
        def _():
            o_ref[...] = acc[...].astype(BF16)

    return pl.pallas_call(
        body, name=name,
        grid_spec=pltpu.PrefetchScalarGridSpec(
            num_scalar_prefetch=2, grid=(dm // tm, padded // tw, width),
            in_specs=[pl.BlockSpec((None, tm, wp), lambda i, t, s_, ix, vl: (ix[t * width + s_], i, 0)),
                      pl.BlockSpec((None, wp, 1), lambda i, t, s_, ix, vl: (ix[t * width + s_], 0, 0))],
            out_specs=pl.BlockSpec((tm, tw), lambda i, t, s_, ix, vl: (i, t)),
            scratch_shapes=[pltpu.VMEM((tm, tw), F32)]),
        out_shape=jax.ShapeDtypeStruct((dm, padded), BF16),
        compiler_params=_cparams(("parallel", "parallel", "arbitrary")),
    )(idx, val, g, tbl)


def _win_to_shards(dw, tbl, idx, val, width, *, tw, wp, name):
    dm = dw.shape[0]
    tm = _pick(dm, (1024, 512, 256, 128))

    def body(idx_ref, val_ref, w_ref, d_ref, o_ref, acc):
        j, s_ = pl.program_id(1), pl.program_id(2)

        @pl.when(s_ == 0)
        def _():
            acc[...] = jnp.zeros_like(acc)

        @pl.when(val_ref[j * width + s_] == 1)
        def _():
            acc[...] += _dot(w_ref[...], _sel_matrix(d_ref, idx_ref[j * width + s_], wp, tw), NT)

        @pl.when(s_ == width - 1)
        def _():
            o_ref[...] = acc[...].astype(BF16)

    return pl.pallas_call(
        body, name=name,
        grid_spec=pltpu.PrefetchScalarGridSpec(
            num_scalar_prefetch=2, grid=(dm // tm, N_DEV, width),
            in_specs=[pl.BlockSpec((tm, tw), lambda i, j, s_, ix, vl: (i, ix[j * width + s_])),
                      pl.BlockSpec((None, wp, 1), lambda i, j, s_, ix, vl: (j, 0, 0))],
            out_specs=pl.BlockSpec((None, tm, wp), lambda i, j, s_, ix, vl: (j, i, 0)),
            scratch_shapes=[pltpu.VMEM((tm, wp), F32)]),
        out_shape=jax.ShapeDtypeStruct((N_DEV, dm, wp), BF16),
        compiler_params=_cparams(("parallel", "parallel", "arbitrary")),
    )(idx, val, dw, tbl)


def _adamw(parts, w, m, v, *, name):
    r, cdim = w.shape
    tr = _pick(r, (256, 128, 64, 32, 16, 8))
    rc = min(16, tr)
    c1 = 1.0 - ADAM_B1 ** ADAM_STEP
    c2 = 1.0 - ADAM_B2 ** ADAM_STEP

    def body(p_ref, w_ref, m_ref, v_ref, g_ref, d_ref, mo_ref, vo_ref):
        def chunk(ci, carry):
            sl = pl.ds(pl.multiple_of(ci * rc, rc), rc)
            g = p_ref[0, sl, :].astype(F32)
            for i in range(1, N_DEV):
                g = g + p_ref[i, sl, :].astype(F32)
            mn = ADAM_B1 * m_ref[sl, :] + (1.0 - ADAM_B1) * g
            vn = ADAM_B2 * v_ref[sl, :] + (1.0 - ADAM_B2) * jnp.square(g)
            m_hat = mn / c1
            v_hat = vn / c2
            g_ref[sl, :] = g
            d_ref[sl, :] = -ADAM_LR * (m_hat / (jnp.sqrt(v_hat) + ADAM_EPS) + ADAM_WD * w_ref[sl, :])
            mo_ref[sl, :] = mn
            vo_ref[sl, :] = vn
            return carry

        lax.fori_loop(0, tr // rc, chunk, 0)

    spec = pl.BlockSpec((tr, cdim), lambda i: (i, 0))
    return pl.pallas_call(
        body, name=name, grid=(r // tr,),
        in_specs=[pl.BlockSpec((N_DEV, tr, cdim), lambda i: (0, i, 0)), spec, spec, spec],
        out_specs=[spec] * 4,
        out_shape=[jax.ShapeDtypeStruct((r, cdim), F32)] * 4,
        compiler_params=_cparams(("parallel",)),
    )(parts, w, m, v)


def _pad_to(v, n):
    return v if v.shape[0] == n else jnp.concatenate([v, jnp.zeros((n - v.shape[0],), v.dtype)])


def _pad_cols(v, n):
    return v if v.shape[-1] == n else jnp.concatenate([v, jnp.zeros(v.shape[:-1] + (n - v.shape[-1],), v.dtype)], axis=-1)


def _pack(vs, cols, row_mult, dtype):
    offs, o = [], 0
    for v in vs:
        offs.append(o)
        o += v.size
    rows = -(-o // cols)
    rows = -(-rows // row_mult) * row_mult
    flat = jnp.concatenate([v.reshape(-1).astype(dtype) for v in vs])
    return _pad_to(flat, rows * cols).reshape(rows, cols), offs


SHARDED = ("ffn1_w_gate", "ffn1_w_up", "ffn1_w_down", "w_in", "w_merge_gate", "gla_gate_up", "w_branch_fox",
           "w_branch_gla", "w_out", "ffn2_w_gate", "ffn2_w_up", "ffn2_w_down", "w_ple_gate", "w_ple_proj")
ROW_SHARDED = ("ffn1_w_down", "w_out", "ffn2_w_down", "w_ple_gate")
REPLICATED = ("ffn1_norm", "mix_norm", "fox_forget_bias", "gla_gate_bias", "gla_head_norm", "b_merge_gate",
              "ffn2_norm", "ple_norm", "final_norm")
WEIGHTS = ("ffn1_norm", "ffn1_w_gate", "ffn1_w_up", "ffn1_w_down", "mix_norm", "w_in", "fox_forget_bias",
           "gla_gate_up", "gla_gate_bias", "gla_head_norm", "w_branch_fox", "w_branch_gla", "w_merge_gate",
           "b_merge_gate", "w_out", "ffn2_norm", "ffn2_w_gate", "ffn2_w_up", "ffn2_w_down", "ple_norm",
           "w_ple_proj", "w_ple_gate", "final_norm")


def kernel(x, p, ffn1_norm, ffn1_w_gate, ffn1_w_up, ffn1_w_down, mix_norm, w_in, fox_forget_bias, gla_gate_up, gla_gate_bias, gla_head_norm, w_branch_fox, w_branch_gla, w_merge_gate, b_merge_gate, w_out, ffn2_norm, ffn2_w_gate, ffn2_w_up, ffn2_w_down, ple_norm, w_ple_proj, w_ple_gate, final_norm, loss_target, m_ffn1_norm, m_ffn1_w_gate, m_ffn1_w_up, m_ffn1_w_down, m_mix_norm, m_w_in, m_fox_forget_bias, m_gla_gate_up, m_gla_gate_bias, m_gla_head_norm, m_w_branch_fox, m_w_branch_gla, m_w_merge_gate, m_b_merge_gate, m_w_out, m_ffn2_norm, m_ffn2_w_gate, m_ffn2_w_up, m_ffn2_w_down, m_ple_norm, m_w_ple_proj, m_w_ple_gate, m_final_norm, v_ffn1_norm, v_ffn1_w_gate, v_ffn1_w_up, v_ffn1_w_down, v_mix_norm, v_w_in, v_fox_forget_bias, v_gla_gate_up, v_gla_gate_bias, v_gla_head_norm, v_w_branch_fox, v_w_branch_gla, v_w_merge_gate, v_b_merge_gate, v_w_out, v_ffn2_norm, v_ffn2_w_gate, v_ffn2_w_up, v_ffn2_w_down, v_ple_norm, v_w_ple_proj, v_w_ple_gate, v_final_norm):
    args = dict(locals())
    wts = {n: args[n] for n in WEIGHTS}
    mom_m = {n: args["m_" + n] for n in WEIGHTS}
    mom_v = {n: args["v_" + n] for n in WEIGHTS}

    xs, ps, tgt = x[0], p[0, 0], loss_target[0]
    s, d = xs.shape
    fox_w = w_branch_fox.shape[1]
    gla_vw = w_branch_gla.shape[1]
    fox_heads = fox_w // HEAD_DIM
    gla_heads = gla_vw // GLA_VAL_DIM
    gla_kw = gla_heads * HEAD_DIM
    rank = gla_gate_up.shape[1]

    c_fl = 3 * fox_w
    o_gr, o_gq, o_gk = gla_vw, 2 * gla_vw, 2 * gla_vw + gla_kw
    o_fl = o_gk + gla_kw
    o_gd = o_fl + LANES
    rest_w = o_gd + LANES
    padded = c_fl + rest_w
    seg = [(c_fl, 0), (fox_heads, c_fl + o_fl), (gla_kw, c_fl + o_gq), (gla_kw, c_fl + o_gk), (gla_vw, c_fl),
           (gla_vw, c_fl + o_gr), (rank, c_fl + o_gd)]
    dest = np.concatenate([np.arange(w_, dtype=np.int32) + o_ for w_, o_ in seg])
    ws = w_in.shape[2]
    wp = -(-ws // LANES) * LANES
    tw = 256 if padded % 256 == 0 else LANES
    tbl, (t_idx, t_val, t_width), (s_idx, s_val, s_width) = _sel_tables(dest, ws, wp, tw)
    tbl = jnp.asarray(tbl)

    ov = _Overlap()
    for n in SHARDED:
        sh = wts[n][0].astype(BF16)
        ov.add(n, "gather", _pad_cols(sh, wp) if n == "w_in" else sh)
    fbias = _pad_cols(fox_forget_bias, LANES)
    bmg_f, bmg_g = b_merge_gate[:, :d], b_merge_gate[:, d:]
    ghn = jnp.tile(gla_head_norm, (1, gla_heads))

    h1, ffn1_saved = _ffn_fwd(xs, ffn1_norm, ov, tag="ffn1")
    u = _rms_fwd(h1, mix_norm, name="mix_norm")
    win = _win_unshard(ov.weight("w_in"), tbl, jnp.asarray(t_idx), jnp.asarray(t_val), t_width, tw=tw, padded=padded,
                       name="in_proj_unshard")
    win_fox, win_rest = win[:, :c_fl], win[:, c_fl:]
    zf = ov.mm(u, win_fox, mode="nn", name="in_proj_fox", out_dtype=BF16)
    zr = ov.mm(u, win_rest, mode="nn", name="in_proj_rest")
    gz = ov.mm(u, ov.weight("w_merge_gate"), mode="nn", name="merge_gate")
    gup = ov.weight("gla_gate_up").transpose(1, 0, 2).reshape(rank, gla_kw)
    gup = jnp.concatenate([gup, jnp.zeros((LANES - rank, gla_kw), BF16)], axis=0)

    log_f = _rowwise(lambda fl, b: _log_sigmoid(fl + b), [(zr, LANES, o_fl // LANES)], [fbias], [(LANES, F32)],
                     name="forget_gate")[0]
    f_cum = _cumsum(log_f[:, :fox_heads].T, reverse=False, name="forget_cumsum")
    f_col, f_row = f_cum[:, :, None], f_cum[:, None, :]
    y_fox, y_fox_bf, lse = _fox_fwd(zf, f_col, f_row, heads=fox_heads, name="fox_fwd")

    def decay_fn(gd, gupv, gb):
        return _log_sigmoid(_dot(gd, gupv, NN) + gb) * (1.0 / GLA_GATE_TAU)

    la = _rowwise(decay_fn, [(zr, LANES, o_gd // LANES)], [gup, gla_gate_bias], [(gla_kw, F32)], name="gla_decay", rc=128)[0]
    q_blk, k_blk = o_gq // HEAD_DIM, o_gk // HEAD_DIM
    o_gla, states = _gla_fwd(zr, la, heads=gla_heads, q_blk=q_blk, k_blk=k_blk, name="gla_fwd")

    def gla_out_fn(o, gr, g):
        outs = []
        for hh in range(gla_heads):
            sl = slice(hh * GLA_VAL_DIM, (hh + 1) * GLA_VAL_DIM)
            _, oh = _rms_parts(o[:, sl])
            outs.append(oh * g[:, sl] * _silu_parts(gr[:, sl])[0])
        return jnp.concatenate(outs, axis=1)

    y_gla = _rowwise(gla_out_fn, [o_gla, (zr, gla_vw, o_gr // gla_vw)], [ghn], [(gla_vw, BF16)], name="gla_out")[0]
    br_f = ov.mm(y_fox_bf, ov.weight("w_branch_fox"), mode="nn", name="branch_fox")
    br_g = ov.mm(y_gla, ov.weight("w_branch_gla"), mode="nn", name="branch_gla")

    def merge_fn(zf_, zg_, bf_, bg_, b1, b2):
        return _sigmoid(zf_ + b1) * bf_ + _sigmoid(zg_ + b2) * bg_

    merged = _rowwise(merge_fn, [(gz, d, 0), (gz, d, 1), br_f, br_g], [bmg_f, bmg_g], [(d, BF16)], name="merge")[0]
    h2 = ov.mm(merged, ov.weight("w_out"), mode="nn", name="out_proj", add=h1)
    h3, ffn2_saved = _ffn_fwd(h2, ffn2_norm, ov, tag="ffn2")
    n3 = _rms_fwd(h3, ple_norm, name="ple_norm")
    gl = ov.mm(n3, ov.weight("w_ple_gate"), mode="nn", name="ple_gate")
    pe = ov.mm(ps, ov.weight("w_ple_proj"), mode="nn", name="ple_proj")

    def head_fn(h3b, glb, peb, tb, gfin):
        pg = _sigmoid(glb)
        h4 = h3b + pg * peb
        r, xh = _rms_parts(h4)
        err = xh * gfin - tb
        dy = err * (1.0 / d)
        t = dy * gfin
        dh4 = r * (t - xh * jnp.mean(t * xh, axis=-1, keepdims=True))
        return dh4, dh4 * pg, dh4 * peb * pg * (1.0 - pg), _colsum(err * err), _colsum(dy * xh)

    dh4, dpe, dgl, loss_cols, d_final = _rowwise(
        head_fn, [h3, gl, pe, tgt], [final_norm.reshape(1, d)], [(d, F32), (d, BF16), (d, BF16)], [d, d], name="loss_head")
    loss = lax.psum(0.5 * jnp.sum(loss_cols) / d, AXES)

    grads = {"final_norm": d_final.reshape(d)}
    ov.grad("w_ple_proj", ov.mm(ps, dpe, mode="tn", name="ple_proj_dw", out_dtype=BF16, out_chunked=True))
    ov.grad("w_ple_gate", ov.mm(n3, dgl, mode="tn", name="ple_gate_dw", out_dtype=BF16))
    dn3 = ov.mm(dgl, ov.weight("w_ple_gate"), mode="nt", name="ple_gate_dx")
    dh3, dh3_bf, grads["ple_norm"] = _rms_bwd(h3, dn3, ple_norm, dh4, name="ple_norm_bwd")
    dh2, dh2_bf, grads["ffn2_norm"] = _ffn_bwd(h2, ffn2_norm, ov, ffn2_saved, dh3, dh3_bf, tag="ffn2")

    ov.grad("w_out", ov.mm(merged, dh2_bf, mode="tn", name="out_proj_dw", out_dtype=BF16))
    dmerged = ov.mm(dh2_bf, ov.weight("w_out"), mode="nt", name="out_proj_dx")

    def merge_bwd_fn(zf_, zg_, bf_, bg_, dm, b1, b2):
        sf, sg = _sigmoid(zf_ + b1), _sigmoid(zg_ + b2)
        dz = jnp.concatenate([dm * bf_ * sf * (1.0 - sf), dm * bg_ * sg * (1.0 - sg)], axis=1)
        return dm * sf, dm * sg, dz, _colsum(dz)

    dbr_f, dbr_g, dgz, grads["b_merge_gate"] = _rowwise(
        merge_bwd_fn, [(gz, d, 0), (gz, d, 1), br_f, br_g, dmerged], [bmg_f, bmg_g],
        [(d, BF16), (d, BF16), (2 * d, BF16)], [2 * d], name="merge_bwd")
    ov.grad("w_merge_gate", ov.mm(u, dgz, mode="tn", name="merge_gate_dw", out_dtype=BF16, out_chunked=True))
    ov.grad("w_branch_fox", ov.mm(y_fox_bf, dbr_f, mode="tn", name="branch_fox_dw", out_dtype=BF16, out_chunked=True))
    ov.grad("w_branch_gla", ov.mm(y_gla, dbr_g, mode="tn", name="branch_gla_dw", out_dtype=BF16, out_chunked=True))
    dy_fox = ov.mm(dbr_f, ov.weight("w_branch_fox"), mode="nt", name="branch_fox_dx")
    dy_gla = ov.mm(dbr_g, ov.weight("w_branch_gla"), mode="nt", name="branch_gla_dx")

    def gla_out_bwd_fn(o, gr, dy, g):
        dos, dgrs, dgs = [], [], []
        for hh in range(gla_heads):
            sl = slice(hh * GLA_VAL_DIM, (hh + 1) * GLA_VAL_DIM)
            r, oh = _rms_parts(o[:, sl])
            si, dsi = _silu_parts(gr[:, sl])
            don = dy[:, sl] * si
            dgrs.append(dy[:, sl] * oh * g[:, sl] * dsi)
            t = don * g[:, sl]
            dos.append(r * (t - oh * jnp.mean(t * oh, axis=-1, keepdims=True)))
            dgs.append(_colsum(don * oh))
        return jnp.concatenate(dos, axis=1), jnp.concatenate(dgrs, axis=1), jnp.concatenate(dgs, axis=1)

    do_gla, dgr, d_ghn = _rowwise(gla_out_bwd_fn, [o_gla, (zr, gla_vw, o_gr // gla_vw), dy_gla], [ghn],
                                  [(gla_vw, F32), (gla_vw, F32)], [gla_vw], name="gla_out_bwd")
    grads["gla_head_norm"] = d_ghn.reshape(gla_heads, GLA_VAL_DIM).sum(axis=0, keepdims=True)
    dgq, dgk, dgv, dla = _gla_bwd(zr, la, do_gla, states, heads=gla_heads, q_blk=q_blk, k_blk=k_blk, name="gla_bwd")

    def decay_bwd_fn(dl, gd, gupv, gb):
        pre = _dot(gd, gupv, NN) + gb
        dpre = dl * (1.0 / GLA_GATE_TAU) * _sigmoid(-pre)
        return _dot(dpre, gupv, NT), dpre, _colsum(dpre)

    dgd, dpre_bf, grads["gla_gate_bias"] = _rowwise(
        decay_bwd_fn, [dla, (zr, LANES, o_gd // LANES)], [gup, gla_gate_bias], [(LANES, F32), (gla_kw, BF16)], [gla_kw],
        name="gla_decay_bwd", rc=128)
    d_gup = ov.mm(zr[:, o_gd:o_gd + LANES], dpre_bf, mode="tn", name="gla_gate_up_dw")[:rank]
    ov.grad("gla_gate_up", d_gup.reshape(rank, N_DEV, gla_kw // N_DEV).transpose(1, 0, 2).astype(BF16))

    def delta_fn(dyv, yv):
        outs = []
        for hh in range(fox_heads):
            sl = slice(hh * HEAD_DIM, (hh + 1) * HEAD_DIM)
            outs.append(jnp.broadcast_to(jnp.sum(dyv[:, sl] * yv[:, sl], axis=-1, keepdims=True), (dyv.shape[0], HEAD_DIM)))
        return jnp.concatenate(outs, axis=1)

    delta = _rowwise(delta_fn, [dy_fox, y_fox], [], [(fox_w, F32)], name="fox_delta")[0]
    dfq, dfk, dfv, d_fcol, d_frow = _fox_bwd(zf, dy_fox, lse, delta, f_col, f_row, heads=fox_heads, name="fox_bwd")
    d_fcum = d_fcol[:, ::HEAD_DIM].T + d_frow.reshape(fox_heads, s)
    d_logf = _cumsum(d_fcum, reverse=True, name="forget_cumsum_bwd")
    d_logf = _pad_cols(d_logf.T, LANES)

    def forget_bwd_fn(dl, fl, b):
        dfl_ = dl * _sigmoid(-(fl + b))
        return dfl_, _colsum(dfl_)

    dfl, d_fbias = _rowwise(forget_bwd_fn, [d_logf, (zr, LANES, o_fl // LANES)], [fbias], [(LANES, F32)], [LANES],
                            name="forget_gate_bwd")
    grads["fox_forget_bias"] = d_fbias[:, :fox_heads]

    dz = jnp.concatenate([dfq, dfk, dfv, dgv, dgr, dgq, dgk, dfl, dgd], axis=1).astype(BF16)
    dwin = ov.mm(u, dz, mode="tn", name="in_proj_dw", out_dtype=BF16)
    ov.grad("w_in", _win_to_shards(dwin, tbl, jnp.asarray(s_idx), jnp.asarray(s_val), s_width, tw=tw, wp=wp,
                                   name="in_proj_dw_shards"))
    du = ov.mm(dgz, ov.weight("w_merge_gate"), mode="nt", name="merge_gate_dx")
    du = ov.mm(dz, win, mode="nt", name="in_proj_dx", add=du)
    dh1, dh1_bf, grads["mix_norm"] = _rms_bwd(h1, du, mix_norm, dh2, name="mix_norm_bwd")
    dx, _, grads["ffn1_norm"] = _ffn_bwd(xs, ffn1_norm, ov, ffn1_saved, dh1, dh1_bf, tag="ffn1")

    outs = {}
    for n in SHARDED:
        parts = ov.get("d_" + n)
        state = [_pad_cols(t_[n][0], parts.shape[2]) for t_ in (wts, mom_m, mom_v)]
        res4 = _adamw(parts, *state, name=f"adamw_{n}")
        for kind, r_ in zip(("grad", "delta", "new_m", "new_v"), res4):
            outs[f"{kind}_{n}"] = r_[:, :wts[n].shape[2]][None]

    send_small, small_offs = _pack([grads[n] for n in REPLICATED], LANES, 8, F32)
    recv_small = _comm("bcast", send_small, name="exchange_replicated")
    w_sm, _ = _pack([wts[n] for n in REPLICATED], LANES, 8, F32)
    m_sm, _ = _pack([mom_m[n] for n in REPLICATED], LANES, 8, F32)
    v_sm, _ = _pack([mom_v[n] for n in REPLICATED], LANES, 8, F32)
    small = _adamw(recv_small, w_sm, m_sm, v_sm, name="adamw_replicated")
    for kind, buf in zip(("grad", "delta", "new_m", "new_v"), small):
        fs = buf.reshape(-1)
        for n, o in zip(REPLICATED, small_offs):
            outs[f"{kind}_{n}"] = fs[o:o + wts[n].size].reshape(wts[n].shape)

    res = [loss, dx[None]]
    for kind in ("grad", "delta", "new_m", "new_v"):
        res += [outs[f"{kind}_{n}"] for n in WEIGHTS]
    return tuple(res)
```

```python
import functools

import jax
import jax.numpy as jnp
import numpy as np
from jax import lax
from jax.experimental import pallas as pl
from jax.experimental.pallas import tpu as pltpu

F32 = jnp.float32
BF16 = jnp.bfloat16
MESH = pl.DeviceIdType.MESH
AXES = ("x", "y", "c")
N_DEV = 8

VMEM_LIMIT_BYTES = 56 * 1024 * 1024
MM_BLOCK_BUDGET_BYTES = 40 * 1024 * 1024
LANES = 128

EPS = 1e-6
HEAD_DIM = 128
GLA_VAL_DIM = 256
GLA_CHUNK = 64
GLA_GATE_TAU = 16.0
ADAM_LR, ADAM_B1, ADAM_B2, ADAM_EPS, ADAM_WD, ADAM_STEP = 0.001, 0.9, 0.999, 1e-08, 0.01, 10

ATTN_FWD_MATMULS = 7.0
ATTN_BWD_MATMULS = 7.0

HIGHEST = lax.Precision.HIGHEST
NN = (((1,), (0,)), ((), ()))
NT = (((1,), (1,)), ((), ()))
TN = (((0,), (0,)), ((), ()))


def _cparams(sem):
    return pltpu.CompilerParams(dimension_semantics=sem, vmem_limit_bytes=VMEM_LIMIT_BYTES)


def _pick(dim, cands):
    for c in cands:
        if dim % c == 0:
            return c
    return dim


def _bf(v):
    return v if v.dtype == BF16 else v.astype(BF16)


def _dot(a, b, dims):
    return lax.dot_general(_bf(a), _bf(b), dims, preferred_element_type=F32)


def _sigmoid(v):
    return 1.0 / (1.0 + jnp.exp(-v))


def _log_sigmoid(v):
    return jnp.minimum(v, 0.0) - jnp.log(1.0 + jnp.exp(-jnp.abs(v)))


def _logical(v):
    return (v.shape[1], v.shape[0] * v.shape[2]) if v.ndim == 3 else v.shape


def _mm(a, b, *, mode, name, out_dtype=F32, add=None, scale=1.0, out_chunked=False, comm=()):
    la, lb = _logical(a), _logical(b)
    if mode == "nn":
        (m, k), (k2, n) = la, lb
        a_minor, b_minor = "k", "n"
    elif mode == "nt":
        (m, k), (n, k2) = la, lb
        a_minor, b_minor = "k", "k"
    else:
        (k, m), (k2, n) = la, lb
        a_minor, b_minor = "m", "n"
    assert k == k2, (name, a.shape, b.shape)
    forced = {}
    for v, minor in ((a, a_minor), (b, b_minor)):
        if v.ndim == 3:
            assert forced.get(minor, v.shape[2]) == v.shape[2], name
            forced[minor] = v.shape[2]
    if out_chunked:
        assert forced.get("n", n // N_DEV) == n // N_DEV, name
        forced["n"] = n // N_DEV
    tm = forced.get("m") or _pick(m, (1024, 512, 256, 128))
    tn = forced.get("n") or _pick(n, (1408, 1280, 1024, 512, 256, 128))
    tk = forced.get("k")
    if not tk:
        def blocks_bytes(t):
            io = tm * t * a.dtype.itemsize + t * tn * b.dtype.itemsize
            return 2 * (io + tm * tn * (jnp.dtype(out_dtype).itemsize + (4 if add is not None else 0))) + tm * tn * 4
        tk = k if blocks_bytes(k) <= MM_BLOCK_BUDGET_BYTES else _pick(k, (512, 640, 256, 128))
    nk = k // tk
    dims = {"nn": NN, "nt": NT, "tn": TN}[mode]
    gi, gj, gk = (lambda i, j, kk: i), (lambda i, j, kk: j), (lambda i, j, kk: kk)

    def spec(v, t_major, t_minor, g_major, g_minor):
        if v is not None and v.ndim == 3:
            return pl.BlockSpec((None, t_major, v.shape[2]), lambda i, j, kk: (g_minor(i, j, kk), g_major(i, j, kk), 0))
        return pl.BlockSpec((t_major, t_minor), lambda i, j, kk: (g_major(i, j, kk), g_minor(i, j, kk)))

    a_spec = spec(a, tk, tm, gk, gi) if mode == "tn" else spec(a, tm, tk, gi, gk)
    b_spec = spec(b, tn, tk, gj, gk) if mode == "nt" else spec(b, tk, tn, gk, gj)
    if out_chunked:
        o_spec = pl.BlockSpec((None, tm, tn), lambda i, j, kk: (j, i, 0))
        out_shape = jax.ShapeDtypeStruct((N_DEV, m, tn), out_dtype)
    else:
        o_spec = pl.BlockSpec((tm, tn), lambda i, j, kk: (i, j))
        out_shape = jax.ShapeDtypeStruct((m, n), out_dtype)
    has_add = add is not None
    assert not (has_add and out_chunked), name

    def body(*refs):
        a_ref, b_ref = refs[0], refs[1]
        add_ref = refs[2] if has_add else None
        o_ref = refs[3 if has_add else 2]
        kk = pl.program_id(2)

        def finish(r):
            if scale != 1.0:
                r = r * scale
            if has_add:
                r = r + add_ref[...]
            o_ref[...] = r.astype(o_ref.dtype)

        if nk == 1:
            finish(_dot(a_ref[...], b_ref[...], dims))
        else:
            acc_ref = refs[-1]

            @pl.when(kk == 0)
            def _():
                acc_ref[...] = jnp.zeros_like(acc_ref)

            acc_ref[...] += _dot(a_ref[...], b_ref[...], dims)

            @pl.when(kk == nk - 1)
            def _():
                finish(acc_ref[...])

    res, carried = _call(
        body, name=name, grid=(m // tm, n // tn, nk),
        in_specs=[a_spec, b_spec] + ([o_spec] if has_add else []), out_specs=[o_spec], out_shape=[out_shape],
        scratch_shapes=[pltpu.VMEM((tm, tn), F32)] if nk > 1 else [],
        semantics=("parallel", "parallel", "arbitrary"), operands=[a, b] + ([add] if has_add else []), comm=comm)
    return (res[0], carried) if comm else res[0]


def _call(body, *, name, grid, in_specs, out_specs, out_shape, scratch_shapes, semantics, operands, comm=()):
    n_in, n_out, n_scr, n = len(in_specs), len(out_specs), len(scratch_shapes), len(comm)
    if not comm:
        res = pl.pallas_call(body, name=name, grid=grid, in_specs=in_specs, out_specs=out_specs, out_shape=out_shape,
                             scratch_shapes=scratch_shapes, compiler_params=_cparams(semantics))(*operands)
        return res, []

    def carrying(*refs):
        ins, c_in = refs[:n_in], refs[n_in:n_in + n]
        outs, c_out = refs[n_in + n:n_in + n + n_out], refs[n_in + n + n_out:n_in + 2 * n + n_out]
        scratch, sems = refs[n_in + 2 * n + n_out:][:n_scr], refs[n_in + 2 * n + n_out + n_scr:]
        tasks = [_comm_ops(kind, c_in[t], c_out[t], *sems[3 * t:3 * t + 3]) for t, (kind, _) in enumerate(comm)]
        ids = [pl.program_id(ax) for ax in range(len(grid))]

        @pl.when(functools.reduce(lambda p, q: p & q, [i == 0 for i in ids]))
        def _():
            for start, _ in tasks:
                start()

        body(*ins, *outs, *scratch)

        @pl.when(functools.reduce(lambda p, q: p & q, [i == g - 1 for i, g in zip(ids, grid)]))
        def _():
            for _, finish in tasks:
                finish()

    any_spec = pl.BlockSpec(memory_space=pl.ANY)
    res = pl.pallas_call(
        carrying, name=name, grid=grid,
        in_specs=list(in_specs) + [any_spec] * n, out_specs=list(out_specs) + [any_spec] * n,
        out_shape=list(out_shape) + [_comm_out_shape(kind, v) for kind, v in comm],
        scratch_shapes=list(scratch_shapes) + _comm_scratch(n),
        compiler_params=_cparams(("arbitrary",) * len(grid)),
    )(*operands, *[v for _, v in comm])
    return res[:n_out], res[n_out:]


def _rowwise(fn, rows, consts, outs, accs=(), *, name, rc=None):
    rows = [r if isinstance(r, tuple) else (r, r.shape[1], 0) for r in rows]
    m = rows[0][0].shape[0]
    widths = [w for _, w, _ in rows] + [n for n, _ in outs]
    row_bytes = sum(w * r.dtype.itemsize for r, w, _ in rows) + sum(n * jnp.dtype(d).itemsize for n, d in outs)
    tm = 1024
    while tm > 16 and (m % tm or 2 * tm * row_bytes > 24 * 1024 * 1024):
        tm //= 2
    if m % tm:
        tm = m
    if rc is None:
        rc = 16
        while rc * 2 <= tm and rc * 2 * max(widths) <= 32768:
            rc *= 2
    rc = min(rc, tm)
    nr, nc, no = len(rows), len(consts), len(outs)

    def body(*refs):
        in_refs, c_refs = refs[:nr], refs[nr:nr + nc]
        o_refs, a_refs = refs[nr + nc:nr + nc + no], refs[nr + nc + no:]

        @pl.when(pl.program_id(0) == 0)
        def _():
            for r in a_refs:
                r[...] = jnp.zeros_like(r)

        cvals = [c[...] for c in c_refs]

        def chunk(ci, carry):
            sl = pl.ds(pl.multiple_of(ci * rc, rc), rc)
            res = fn(*[r[sl, :] for r in in_refs], *cvals)
            if not isinstance(res, (tuple, list)):
                res = (res,)
            for r, v in zip(o_refs, res[:no]):
                r[sl, :] = v.astype(r.dtype)
            for r, v in zip(a_refs, res[no:]):
                r[...] += v
            return carry

        lax.fori_loop(0, tm // rc, chunk, 0)

    in_specs = [pl.BlockSpec((tm, w), functools.partial(lambda i, cb: (i, cb), cb=cb)) for _, w, cb in rows]
    in_specs += [pl.BlockSpec(c.shape, lambda i: (0, 0)) for c in consts]
    out_specs = [pl.BlockSpec((tm, n), lambda i: (i, 0)) for n, _ in outs]
    out_specs += [pl.BlockSpec((1, n), lambda i: (0, 0)) for n in accs]
    out_shape = [jax.ShapeDtypeStruct((m, n), d) for n, d in outs] + [jax.ShapeDtypeStruct((1, n), F32) for n in accs]
    res = pl.pallas_call(
        body, name=name, grid=(m // tm,),
        in_specs=in_specs, out_specs=out_specs, out_shape=out_shape,
        compiler_params=_cparams(("arbitrary",)),
    )(*[r for r, _, _ in rows], *consts)
    return res


def _colsum(v):
    return jnp.sum(v, axis=0, keepdims=True)


def _rms_parts(xv):
    r = lax.rsqrt(jnp.mean(xv * xv, axis=-1, keepdims=True) + EPS)
    return r, xv * r


def _rms_fwd(xv, g, *, name):
    d = xv.shape[1]

    def fn(xb, gb):
        _, xh = _rms_parts(xb)
        return xh * gb

    return _rowwise(fn, [xv], [g], [(d, BF16)], name=name)[0]


def _rms_bwd(xv, dy, g, add, *, name):
    d = xv.shape[1]

    def fn(xb, dyb, addb, gb):
        r, xh = _rms_parts(xb)
        t = dyb * gb
        dx = r * (t - xh * jnp.mean(t * xh, axis=-1, keepdims=True)) + addb
        return dx, dx, _colsum(dyb * xh)

    return _rowwise(fn, [xv, dy, add], [g], [(d, F32), (d, BF16)], [d], name=name)


def _silu_parts(a):
    sg = _sigmoid(a)
    return a * sg, sg * (1.0 + a * (1.0 - sg))


def _rows(v):
    return v.reshape(v.shape[0] * v.shape[1], v.shape[2])


def _ffn_fwd(h, g, ov, *, tag):
    s = h.shape[0]
    n = _rms_fwd(h, g, name=f"{tag}_norm")
    a = ov.mm(n, ov.weight(f"{tag}_w_gate"), mode="nn", name=f"{tag}_gate", out_chunked=True)
    b = ov.mm(n, ov.weight(f"{tag}_w_up"), mode="nn", name=f"{tag}_up", out_chunked=True)
    c = a.shape[2]
    hm = _rowwise(lambda av, bv: _silu_parts(av)[0] * bv, [_rows(a), _rows(b)], [], [(c, BF16)], name=f"{tag}_act")[0]
    hm = hm.reshape(N_DEV, s, c)
    out = ov.mm(hm, ov.weight(f"{tag}_w_down"), mode="nn", name=f"{tag}_down", add=h, scale=0.5)
    return out, (n, a, b, hm)


def _ffn_bwd(h, g, ov, saved, dout, dout_bf, *, tag):
    n, a, b, hm = saved
    wg, wu, wd = ov.weight(f"{tag}_w_gate"), ov.weight(f"{tag}_w_up"), ov.weight(f"{tag}_w_down")
    s, c = h.shape[0], wg.shape[2]
    d_wd = ov.mm(hm, dout_bf, mode="tn", name=f"{tag}_down_dw", scale=0.5, out_dtype=BF16)
    ov.grad(f"{tag}_w_down", d_wd)
    dhm = ov.mm(dout_bf, wd, mode="nt", name=f"{tag}_down_dx", scale=0.5, out_chunked=True)

    def act_bwd(av, bv, dv):
        si, dsi = _silu_parts(av)
        return dv * bv * dsi, dv * si

    da, db = _rowwise(act_bwd, [_rows(a), _rows(b), _rows(dhm)], [], [(c, BF16), (c, BF16)], name=f"{tag}_act_bwd")
    da, db = da.reshape(N_DEV, s, c), db.reshape(N_DEV, s, c)
    ov.grad(f"{tag}_w_gate", ov.mm(n, da, mode="tn", name=f"{tag}_gate_dw", out_dtype=BF16, out_chunked=True))
    ov.grad(f"{tag}_w_up", ov.mm(n, db, mode="tn", name=f"{tag}_up_dw", out_dtype=BF16, out_chunked=True))
    dn = ov.mm(da, wg, mode="nt", name=f"{tag}_gate_dx")
    dn = ov.mm(db, wu, mode="nt", name=f"{tag}_up_dx", add=dn)
    dh, dh_bf, dg = _rms_bwd(h, dn, g, dout, name=f"{tag}_norm_bwd")
    return dh, dh_bf, dg


def _cumsum(xv, *, reverse, name):
    h, s = xv.shape
    t = _pick(s, (512, 256, 128))
    nb = s // t

    def blk(j):
        return (0, nb - 1 - j) if reverse else (0, j)

    def body(x_ref, o_ref, carry):
        @pl.when(pl.program_id(0) == 0)
        def _():
            carry[...] = jnp.zeros_like(carry)

        i0 = lax.broadcasted_iota(jnp.int32, (t, t), 0)
        i1 = lax.broadcasted_iota(jnp.int32, (t, t), 1)
        tri = ((i0 >= i1) if reverse else (i0 <= i1)).astype(F32)
        xb = x_ref[...]
        o_ref[...] = jnp.dot(xb, tri, precision=HIGHEST, preferred_element_type=F32) + carry[...]
        carry[...] += jnp.sum(xb, axis=1, keepdims=True)

    return pl.pallas_call(
        body, name=name, grid=(nb,),
        in_specs=[pl.BlockSpec((h, t), blk)], out_specs=pl.BlockSpec((h, t), blk),
        out_shape=jax.ShapeDtypeStruct((h, s), F32),
        scratch_shapes=[pltpu.VMEM((h, 1), F32)],
        compiler_params=_cparams(("arbitrary",)),
    )(xv)


def _fox_tiles(s):
    t = _pick(s, (512, 256, 128))
    return t, t


def _causal(sc):
    t = sc.shape[0]
    keep = lax.broadcasted_iota(jnp.int32, (t, t), 1) <= lax.broadcasted_iota(jnp.int32, (t, t), 0)
    return jnp.where(keep, sc, -jnp.inf)


def _fox_fwd(zf, f_col, f_row, *, heads, name, comm=()):
    s = zf.shape[0]
    tq, tk = _fox_tiles(s)
    assert tq == tk
    nq, nk = s // tq, s // tk
    scale = HEAD_DIM ** -0.5
    w = heads * HEAD_DIM

    def last_k(i):
        return i

    def body(q_ref, k_ref, v_ref, fq_ref, fk_ref, o32_ref, o16_ref, lse_ref, m_sc, l_sc, acc_sc):
        i, j = pl.program_id(1), pl.program_id(2)

        @pl.when(j == 0)
        def _():
            m_sc[...] = jnp.full_like(m_sc, -jnp.inf)
            l_sc[...] = jnp.zeros_like(l_sc)
            acc_sc[...] = jnp.zeros_like(acc_sc)

        def step(diagonal):
            sc = _dot(q_ref[...], k_ref[...], NT) * scale + fq_ref[...] - fk_ref[...]
            if diagonal:
                sc = _causal(sc)
            m_new = jnp.maximum(m_sc[...], jnp.max(sc, axis=-1, keepdims=True))
            alpha = jnp.exp(m_sc[...] - m_new)
            pr = jnp.exp(sc - m_new)
            l_sc[...] = alpha * l_sc[...] + jnp.sum(pr, axis=-1, keepdims=True)
            acc_sc[...] = alpha * acc_sc[...] + _dot(pr, v_ref[...], NN)
            m_sc[...] = m_new

        @pl.when(j < i)
        def _():
            step(False)

        @pl.when(j == i)
        def _():
            step(True)

        @pl.when(j == nk - 1)
        def _():
            o = acc_sc[...] / l_sc[...]
            o32_ref[...] = o
            o16_ref[...] = o.astype(BF16)
            lse_ref[...] = jnp.broadcast_to(m_sc[...] + jnp.log(l_sc[...]), (tq, HEAD_DIM))

    def kv_blk(off):
        return lambda h, i, j: (jnp.minimum(j, last_k(i)), off + h)

    o_spec = pl.BlockSpec((tq, HEAD_DIM), lambda h, i, j: (i, h))
    return _call(
        body, name=name, grid=(heads, nq, nk),
        in_specs=[
            pl.BlockSpec((tq, HEAD_DIM), lambda h, i, j: (i, h)),
            pl.BlockSpec((tk, HEAD_DIM), kv_blk(heads)),
            pl.BlockSpec((tk, HEAD_DIM), kv_blk(2 * heads)),
            pl.BlockSpec((None, tq, 1), lambda h, i, j: (h, i, 0)),
            pl.BlockSpec((None, 1, tk), lambda h, i, j: (h, 0, jnp.minimum(j, last_k(i)))),
        ],
        out_specs=[o_spec, o_spec, o_spec],
        out_shape=[jax.ShapeDtypeStruct((s, w), F32), jax.ShapeDtypeStruct((s, w), BF16),
                   jax.ShapeDtypeStruct((s, w), F32)],
        scratch_shapes=[pltpu.VMEM((tq, 1), F32), pltpu.VMEM((tq, 1), F32), pltpu.VMEM((tq, HEAD_DIM), F32)],
        semantics=("parallel", "parallel", "arbitrary"), operands=[zf, zf, zf, f_col, f_row], comm=comm)


def _fox_bwd(zf, do, lse, delta, f_col, f_row, *, heads, name, comm=()):
    s = zf.shape[0]
    tq, tk = _fox_tiles(s)
    assert tq == tk
    nq, nk = s // tq, s // tk
    scale = HEAD_DIM ** -0.5
    w = heads * HEAD_DIM

    def first_q(j):
        return j

    def body(q_ref, k_ref, v_ref, do_ref, lse_ref, dl_ref, fq_ref, fk_ref, dq_ref, dk_ref, dv_ref, dfq_ref, dfk_ref):
        j, i = pl.program_id(1), pl.program_id(2)

        @pl.when((j == 0) & (i == 0))
        def _():
            dq_ref[...] = jnp.zeros_like(dq_ref)
            dfq_ref[...] = jnp.zeros_like(dfq_ref)

        @pl.when(i == 0)
        def _():
            dk_ref[...] = jnp.zeros_like(dk_ref)
            dv_ref[...] = jnp.zeros_like(dv_ref)
            dfk_ref[...] = jnp.zeros_like(dfk_ref)

        def step(diagonal):
            q, k, v = q_ref[...], k_ref[...], v_ref[...]
            dob = do_ref[...].astype(BF16)
            sc = _dot(q, k, NT) * scale + fq_ref[...] - fk_ref[...]
            if diagonal:
                sc = _causal(sc)
            pr = jnp.exp(sc - lse_ref[:, 0:1])
            dv_ref[...] += _dot(pr, dob, TN)
            dp = _dot(dob, v, NT)
            ds = pr * (dp - dl_ref[:, 0:1])
            dsb = ds.astype(BF16)
            dk_ref[...] += _dot(dsb, q, TN) * scale
            rows = pl.ds(pl.multiple_of(i * tq, tq), tq)
            dq_ref[rows, :] += _dot(dsb, k, NN) * scale
            dfq_ref[rows, :] += jnp.broadcast_to(jnp.sum(ds, axis=1, keepdims=True), (tq, HEAD_DIM))
            dfk_ref[...] -= jnp.sum(ds, axis=0, keepdims=True)

        @pl.when(i > j)
        def _():
            step(False)

        @pl.when(i == j)
        def _():
            step(True)

    def q_blk(h, j, i):
        return (jnp.maximum(i, first_q(j)), h)

    return _call(
        body, name=name, grid=(heads, nk, nq),
        in_specs=[
            pl.BlockSpec((tq, HEAD_DIM), q_blk),
            pl.BlockSpec((tk, HEAD_DIM), lambda h, j, i: (j, heads + h)),
            pl.BlockSpec((tk, HEAD_DIM), lambda h, j, i: (j, 2 * heads + h)),
            pl.BlockSpec((tq, HEAD_DIM), q_blk),
            pl.BlockSpec((tq, HEAD_DIM), q_blk),
            pl.BlockSpec((tq, HEAD_DIM), q_blk),
            pl.BlockSpec((None, tq, 1), lambda h, j, i: (h, jnp.maximum(i, first_q(j)), 0)),
            pl.BlockSpec((None, 1, tk), lambda h, j, i: (h, 0, j)),
        ],
        out_specs=[
            pl.BlockSpec((s, HEAD_DIM), lambda h, j, i: (0, h)),
            pl.BlockSpec((tk, HEAD_DIM), lambda h, j, i: (j, h)),
            pl.BlockSpec((tk, HEAD_DIM), lambda h, j, i: (j, h)),
            pl.BlockSpec((s, HEAD_DIM), lambda h, j, i: (0, h)),
            pl.BlockSpec((None, 1, tk), lambda h, j, i: (h, 0, j)),
        ],
        out_shape=[jax.ShapeDtypeStruct((s, w), F32), jax.ShapeDtypeStruct((s, w), F32),
                   jax.ShapeDtypeStruct((s, w), F32), jax.ShapeDtypeStruct((s, w), F32),
                   jax.ShapeDtypeStruct((heads, 1, s), F32)],
        scratch_shapes=[], semantics=("parallel", "arbitrary", "arbitrary"),
        operands=[zf, zf, zf, do, lse, delta, f_col, f_row], comm=comm)


def _gla_rows(s):
    return _pick(s, (256, 128, 64))


def _gla_chunk_terms(la_c, tri):
    a_cum = jnp.dot(tri, la_c, precision=HIGHEST, preferred_element_type=F32)
    a_tot = jnp.sum(la_c, axis=0, keepdims=True)
    return jnp.exp(a_tot - a_cum), jnp.exp(a_tot)


def _gla_fwd(zr, la, *, heads, q_blk, k_blk, name):
    s = zr.shape[0]
    c = GLA_CHUNK
    rows = _gla_rows(s)
    nsteps, ncs = s // rows, rows // c
    scale = HEAD_DIM ** -0.5

    def body(q_ref, k_ref, v_ref, la_ref, o_ref, st_ref, state):
        @pl.when(pl.program_id(1) == 0)
        def _():
            state[...] = jnp.zeros_like(state)

        tri = (lax.broadcasted_iota(jnp.int32, (c, c), 0) >= lax.broadcasted_iota(jnp.int32, (c, c), 1)).astype(F32)
        for t in range(ncs):
            sl = slice(t * c, (t + 1) * c)
            dec, e_tot = _gla_chunk_terms(la_ref[sl, :], tri)
            kd = k_ref[sl, :] * dec
            st_ref[t] = state[...]
            new = state[...] * e_tot + _dot(v_ref[sl, :], kd, TN)
            state[...] = new
            o_ref[sl, :] = _dot(q_ref[sl, :] * scale, new, NT)

    return pl.pallas_call(
        body, name=name, grid=(heads, nsteps),
        in_specs=[
            pl.BlockSpec((rows, HEAD_DIM), lambda h, i: (i, q_blk + h)),
            pl.BlockSpec((rows, HEAD_DIM), lambda h, i: (i, k_blk + h)),
            pl.BlockSpec((rows, GLA_VAL_DIM), lambda h, i: (i, h)),
            pl.BlockSpec((rows, HEAD_DIM), lambda h, i: (i, h)),
        ],
        out_specs=[
            pl.BlockSpec((rows, GLA_VAL_DIM), lambda h, i: (i, h)),
            pl.BlockSpec((None, ncs, GLA_VAL_DIM, HEAD_DIM), lambda h, i: (h, i, 0, 0)),
        ],
        out_shape=[jax.ShapeDtypeStruct((s, heads * GLA_VAL_DIM), F32),
                   jax.ShapeDtypeStruct((heads, s // c, GLA_VAL_DIM, HEAD_DIM), F32)],
        scratch_shapes=[pltpu.VMEM((GLA_VAL_DIM, HEAD_DIM), F32)],
        compiler_params=_cparams(("parallel", "arbitrary")),
    )(zr, zr, zr, la)


def _gla_bwd(zr, la, do, states, *, heads, q_blk, k_blk, name):
    s = zr.shape[0]
    c = GLA_CHUNK
    rows = _gla_rows(s)
    nsteps, ncs = s // rows, rows // c
    scale = HEAD_DIM ** -0.5

    def body(q_ref, k_ref, v_ref, la_ref, do_ref, st_ref, dq_ref, dk_ref, dv_ref, dla_ref, dstate):
        @pl.when(pl.program_id(1) == 0)
        def _():
            dstate[...] = jnp.zeros_like(dstate)

        i0 = lax.broadcasted_iota(jnp.int32, (c, c), 0)
        i1 = lax.broadcasted_iota(jnp.int32, (c, c), 1)
        tri = (i0 >= i1).astype(F32)
        strict = (i0 > i1).astype(F32)
        for t in reversed(range(ncs)):
            sl = slice(t * c, (t + 1) * c)
            dec, e_tot = _gla_chunk_terms(la_ref[sl, :], tri)
            kd = k_ref[sl, :] * dec
            kdb = kd.astype(BF16)
            vb = v_ref[sl, :].astype(BF16)
            dob = do_ref[sl, :].astype(BF16)
            prev = st_ref[t]
            cur = prev * e_tot + _dot(vb, kdb, TN)
            d_cur = dstate[...] + _dot(dob, q_ref[sl, :] * scale, TN)
            d_cur_b = d_cur.astype(BF16)
            dq_ref[sl, :] = _dot(dob, cur, NN) * scale
            dv_ref[sl, :] = _dot(kdb, d_cur_b, NT)
            dkd = _dot(vb, d_cur_b, NN)
            d_tot = e_tot * jnp.sum(d_cur * prev, axis=0, keepdims=True)
            dk_ref[sl, :] = dkd * dec
            dla_ref[sl, :] = d_tot + jnp.dot(strict, dkd * kd, precision=HIGHEST, preferred_element_type=F32)
            dstate[...] = d_cur * e_tot

    def rev(i):
        return nsteps - 1 - i

    kq_spec = pl.BlockSpec((rows, HEAD_DIM), lambda h, i: (rev(i), h))
    v_spec = pl.BlockSpec((rows, GLA_VAL_DIM), lambda h, i: (rev(i), h))
    return pl.pallas_call(
        body, name=name, grid=(heads, nsteps),
        in_specs=[
            pl.BlockSpec((rows, HEAD_DIM), lambda h, i: (rev(i), q_blk + h)),
            pl.BlockSpec((rows, HEAD_DIM), lambda h, i: (rev(i), k_blk + h)),
            v_spec, kq_spec, v_spec,
            pl.BlockSpec((None, ncs, GLA_VAL_DIM, HEAD_DIM), lambda h, i: (h, rev(i), 0, 0)),
        ],
        out_specs=[kq_spec, kq_spec, v_spec, kq_spec],
        out_shape=[jax.ShapeDtypeStruct((s, heads * HEAD_DIM), F32), jax.ShapeDtypeStruct((s, heads * HEAD_DIM), F32),
                   jax.ShapeDtypeStruct((s, heads * GLA_VAL_DIM), F32), jax.ShapeDtypeStruct((s, heads * HEAD_DIM), F32)],
        scratch_shapes=[pltpu.VMEM((GLA_VAL_DIM, HEAD_DIM), F32)],
        compiler_params=_cparams(("parallel", "arbitrary")),
    )(zr, zr, zr, la, do, states)


def _my_place():
    x, y, c = lax.axis_index("x"), lax.axis_index("y"), lax.axis_index("c")
    return x, y, c


def _gather_ops(x_ref, out_ref, send_sems, recv_sems, local_sem):
    def plan():
        x, y, c = _my_place()
        me, sibling = (x, y, c), (x, y, 1 - c)
        chips = [(1 - x, y), (x, 1 - y), (1 - x, 1 - y)]

        def blk(px, py, pc):
            return out_ref.at[4 * px + 2 * py + pc]

        def copy(k, block, to, src=None):
            return pltpu.make_async_remote_copy(
                src_ref=blk(*block) if src is None else src, dst_ref=blk(*block),
                send_sem=send_sems.at[k], recv_sem=recv_sems.at[k], device_id=to, device_id_type=MESH)

        mine = pltpu.make_async_copy(x_ref, blk(*me), local_sem)
        first = [copy(0, me, sibling, src=x_ref)]
        first += [copy(1 + j, me, (*chip, c), src=x_ref) for j, chip in enumerate(chips)]
        passed = [copy(4 + j, (*chip, c), sibling) for j, chip in enumerate(chips)]
        landed = [copy(1 + j, (*chip, c), me) for j, chip in enumerate(chips)]
        from_sibling = [copy(0, sibling, me)] + [copy(4 + j, (*chip, 1 - c), me) for j, chip in enumerate(chips)]
        return mine, first, passed, landed, from_sibling

    def start():
        mine, first, _, _, _ = plan()
        mine.start()
        for cp in first:
            cp.start()

    def finish():
        mine, first, passed, landed, from_sibling = plan()
        for cp, fwd in zip(landed, passed):
            cp.wait_recv()
            fwd.start()
        for cp in from_sibling:
            cp.wait_recv()
        for cp in first + passed:
            cp.wait_send()
        mine.wait()

    return start, finish


def _exchange_ops(scatter, s_ref, r_ref, send_sems, recv_sems, local_sem):
    def plan():
        x, y, c = _my_place()
        me = 4 * x + 2 * y + c
        mine = pltpu.make_async_copy(s_ref.at[me] if scatter else s_ref, r_ref.at[me], local_sem)
        sends, recvs = [], []
        for k in range(1, N_DEV):
            px, py, pc = x ^ ((k >> 2) & 1), y ^ ((k >> 1) & 1), c ^ (k & 1)
            peer = 4 * px + 2 * py + pc
            src = s_ref.at[peer] if scatter else s_ref
            for dst, out in ((r_ref.at[me], sends), (r_ref.at[peer], recvs)):
                out.append(pltpu.make_async_remote_copy(
                    src_ref=src, dst_ref=dst, send_sem=send_sems.at[k - 1], recv_sem=recv_sems.at[k - 1],
                    device_id=(px, py, pc), device_id_type=MESH))
        return mine, sends, recvs

    def start():
        mine, sends, _ = plan()
        mine.start()
        for cp in sends:
            cp.start()

    def finish():
        mine, sends, recvs = plan()
        for cp in recvs:
            cp.wait_recv()
        for cp in sends:
            cp.wait_send()
        mine.wait()

    return start, finish


def _pair_ops(s_ref, r_ref, send_sems, recv_sems):
    def plan():
        x, y, c = _my_place()
        return [pltpu.make_async_remote_copy(
            src_ref=s_ref.at[2 * q + 1 - c], dst_ref=r_ref.at[q], send_sem=send_sems.at[q], recv_sem=recv_sems.at[q],
            device_id=(x, y, 1 - c), device_id_type=MESH) for q in range(N_DEV // 2)]

    def start():
        for cp in plan():
            cp.start()

    def finish():
        copies = plan()
        for cp in copies:
            cp.wait_recv()
        for cp in copies:
            cp.wait_send()

    return start, finish


def _chips_ops(p_ref, r_ref, send_sems, recv_sems, local_sem):
    def plan():
        x, y, c = _my_place()
        chip = 2 * x + y
        mine = pltpu.make_async_copy(p_ref.at[chip], r_ref.at[chip], local_sem)
        sends, recvs = [], []
        for k in range(1, N_DEV // 2):
            px, py = x ^ (k >> 1), y ^ (k & 1)
            peer = 2 * px + py
            for dst, out in ((r_ref.at[chip], sends), (r_ref.at[peer], recvs)):
                out.append(pltpu.make_async_remote_copy(
                    src_ref=p_ref.at[peer], dst_ref=dst, send_sem=send_sems.at[k - 1], recv_sem=recv_sems.at[k - 1],
                    device_id=(px, py, c), device_id_type=MESH))
        return mine, sends, recvs

    def start():
        mine, sends, _ = plan()
        mine.start()
        for cp in sends:
            cp.start()

    def finish():
        mine, sends, recvs = plan()
        for cp in recvs:
            cp.wait_recv()
        for cp in sends:
            cp.wait_send()
        mine.wait()

    return start, finish


def _comm_ops(kind, src_ref, dst_ref, send_sems, recv_sems, local_sem):
    if kind == "gather":
        return _gather_ops(src_ref, dst_ref, send_sems, recv_sems, local_sem)
    if kind == "pair":
        return _pair_ops(src_ref, dst_ref, send_sems, recv_sems)
    if kind == "chips":
        return _chips_ops(src_ref, dst_ref, send_sems, recv_sems, local_sem)
    return _exchange_ops(False, src_ref, dst_ref, send_sems, recv_sems, local_sem)


def _comm_out_shape(kind, v):
    shape = {"pair": (N_DEV // 2,) + v.shape[1:], "chips": v.shape}.get(kind, (N_DEV,) + v.shape)
    return jax.ShapeDtypeStruct(shape, v.dtype)


def _comm_scratch(n_tasks):
    return [pltpu.SemaphoreType.DMA((N_DEV - 1,)), pltpu.SemaphoreType.DMA((N_DEV - 1,)), pltpu.SemaphoreType.DMA] * n_tasks


def _pair_sum(s, r, *, name):
    _, rows, cdim = s.shape
    tr = _pick(rows, (256, 128, 64, 32, 16, 8))
    rc = min(16, tr)

    def body(s_ref, r_ref, o_ref):
        c = lax.axis_index("c")

        def chunk(ci, carry):
            sl = pl.ds(pl.multiple_of(ci * rc, rc), rc)
            o_ref[sl, :] = (s_ref[c, sl, :].astype(F32) + r_ref[sl, :].astype(F32)).astype(o_ref.dtype)
            return carry

        lax.fori_loop(0, tr // rc, chunk, 0)

    spec = pl.BlockSpec((None, tr, cdim), lambda q, i: (q, i, 0))
    return pl.pallas_call(
        body, name=name, grid=(N_DEV // 2, rows // tr),
        in_specs=[pl.BlockSpec((None, 2, tr, cdim), lambda q, i: (q, 0, i, 0)), spec],
        out_specs=spec,
        out_shape=jax.ShapeDtypeStruct(r.shape, s.dtype),
        compiler_params=_cparams(("parallel", "parallel")),
    )(s.reshape(N_DEV // 2, 2, rows, cdim), r)


def _comm(kind, v, *, name):
    def body(s_ref, r_ref, send_sems, recv_sems, local_sem):
        start, finish = _comm_ops(kind, s_ref, r_ref, send_sems, recv_sems, local_sem)
        start()
        finish()

    return pl.pallas_call(
        body, name=name,
        out_shape=_comm_out_shape(kind, v),
        in_specs=[pl.BlockSpec(memory_space=pl.ANY)],
        out_specs=pl.BlockSpec(memory_space=pl.ANY),
        scratch_shapes=_comm_scratch(1),
    )(v)


class _Overlap:
    US_PER_MB = {"gather": 52.0, "pair": 1.0, "chips": 13.0, "bcast": 97.0}
    MM_FLOPS_PER_US = 6.0e8

    def __init__(self):
        self.queue, self.results, self.then = [], {}, {}

    def add(self, key, kind, v):
        self.queue.append((key, kind, v))

    def _cost(self, kind, v):
        return v.size * v.dtype.itemsize / 2 ** 20 * self.US_PER_MB[kind]

    def take(self, budget_us):
        taken, cum = [], 0.0
        while self.queue:
            cost = self._cost(*self.queue[0][1:])
            if taken and cum + cost > budget_us:
                break
            taken.append(self.queue.pop(0))
            cum += cost
        return taken

    def put(self, taken, res):
        for (key, _, _), r in zip(taken, res):
            self.results[key] = r
            if key in self.then:
                self.then.pop(key)(r)

    def carry(self, budget_us, fn):
        taken = self.take(budget_us)
        out, res = fn([(kind, v) for _, kind, v in taken])
        self.put(taken, res)
        return out

    def mm(self, a, b, *, mode, **kw):
        la, lb = _logical(a), _logical(b)
        budget = 2.0 * la[0] * la[1] * (lb[0] if mode == "nt" else lb[1]) / self.MM_FLOPS_PER_US

        def fn(comm):
            return _mm(a, b, mode=mode, comm=comm, **kw) if comm else (_mm(a, b, mode=mode, **kw), [])

        return self.carry(budget, fn)

    def get(self, key):
        while key not in self.results:
            keys = [k for k, _, _ in self.queue]
            task = self.queue.pop(keys.index(key if key in keys else "pair_" + key))
            self.put([task], [_comm(task[1], task[2], name=f"alone_{task[0]}")])
        return self.results[key]

    def weight(self, n):
        g = self.get(n)
        return g.reshape(-1, g.shape[2]) if n in ROW_SHARDED else g

    def grad(self, n, g):
        g = g if g.ndim == 3 else g.reshape(N_DEV, g.shape[0] // N_DEV, g.shape[1])
        self.add("pair_d_" + n, "pair", g)
        self.then["pair_d_" + n] = lambda r: self.add("d_" + n, "chips", _pair_sum(g, r, name=f"pair_sum_{n}"))


def _sel_tables(dest, ws, wp, tw):
    n_tiles = (int(dest.max()) + tw) // tw
    tbl = np.full((N_DEV, wp), -1, np.int32)
    for j in range(N_DEV):
        tbl[j, :ws] = dest[j * ws:(j + 1) * ws]
    by_tile = [sorted({j for j in range(N_DEV) if ((tbl[j] // tw) == t).any()}) for t in range(n_tiles)]
    by_shard = [sorted({int(t) for t in np.unique(tbl[j, :ws] // tw)}) for j in range(N_DEV)]

    def table(lists):
        width = max(len(v) for v in lists)
        idx = np.array([(v + [v[-1]] * width)[:width] if v else [0] * width for v in lists], np.int32)
        val = np.array([[1] * len(v) + [0] * (width - len(v)) for v in lists], np.int32)
        return idx.reshape(-1), val.reshape(-1), width

    return tbl[:, :, None], table(by_tile), table(by_shard)


def _sel_matrix(d_ref, t, wp, tw):
    cols = t * tw + lax.broadcasted_iota(jnp.int32, (wp, tw), 1)
    return (d_ref[...] == cols).astype(BF16)


def _win_unshard(g, tbl, idx, val, width, *, tw, padded, name):
    _, dm, wp = g.shape
    tm = _pick(dm, (1024, 512, 256, 128))

    def body(idx_ref, val_ref, g_ref, d_ref, o_ref, acc):
        t, s_ = pl.program_id(1), pl.program_id(2)

        @pl.when(s_ == 0)
        def _():
            acc[...] = jnp.zeros_like(acc)

        @pl.when(val_ref[t * width + s_] == 1)
        def _():
            acc[...] += _dot(g_ref[...], _sel_matrix(d_ref, t, wp, tw), NN)

        @pl.when(s_ == width - 1)
        def _():
            o_ref[...] = acc[...].astype(BF16)

    return pl.pallas_call(
        body, name=name,
        grid_spec=pltpu.PrefetchScalarGridSpec(
            num_scalar_prefetch=2, grid=(dm // tm, padded // tw, width),
            in_specs=[pl.BlockSpec((None, tm, wp), lambda i, t, s_, ix, vl: (ix[t * width + s_], i, 0)),
                      pl.BlockSpec((None, wp, 1), lambda i, t, s_, ix, vl: (ix[t * width + s_], 0, 0))],
            out_specs=pl.BlockSpec((tm, tw), lambda i, t, s_, ix, vl: (i, t)),
            scratch_shapes=[pltpu.VMEM((tm, tw), F32)]),
        out_shape=jax.ShapeDtypeStruct((dm, padded), BF16),
        compiler_params=_cparams(("parallel", "parallel", "arbitrary")),
    )(idx, val, g, tbl)


def _win_to_shards(dw, tbl, idx, val, width, *, tw, wp, name):
    dm = dw.shape[0]
    tm = _pick(dm, (1024, 512, 256, 128))

    def body(idx_ref, val_ref, w_ref, d_ref, o_ref, acc):
        j, s_ = pl.program_id(1), pl.program_id(2)

        @pl.when(s_ == 0)
        def _():
            acc[...] = jnp.zeros_like(acc)

        @pl.when(val_ref[j * width + s_] == 1)
        def _():
            acc[...] += _dot(w_ref[...], _sel_matrix(d_ref, idx_ref[j * width + s_], wp, tw), NT)

        @pl.when(s_ == width - 1)
        def _():
            o_ref[...] = acc[...].astype(BF16)

    return pl.pallas_call(
        body, name=name,
        grid_spec=pltpu.PrefetchScalarGridSpec(
            num_scalar_prefetch=2, grid=(dm // tm, N_DEV, width),
            in_specs=[pl.BlockSpec((tm, tw), lambda i, j, s_, ix, vl: (i, ix[j * width + s_])),
                      pl.BlockSpec((None, wp, 1), lambda i, j, s_, ix, vl: (j, 0, 0))],
            out_specs=pl.BlockSpec((None, tm, wp), lambda i, j, s_, ix, vl: (j, i, 0)),
            scratch_shapes=[pltpu.VMEM((tm, wp), F32)]),
        out_shape=jax.ShapeDtypeStruct((N_DEV, dm, wp), BF16),
        compiler_params=_cparams(("parallel", "parallel", "arbitrary")),
    )(idx, val, dw, tbl)


def _adamw(parts, w, m, v, *, name):
    r, cdim = w.shape
    n_parts = parts.shape[0]
    tr = _pick(r, (256, 128, 64, 32, 16, 8))
    rc = min(16, tr)
    c1 = 1.0 - ADAM_B1 ** ADAM_STEP
    c2 = 1.0 - ADAM_B2 ** ADAM_STEP

    def body(p_ref, w_ref, m_ref, v_ref, g_ref, d_ref, mo_ref, vo_ref):
        def chunk(ci, carry):
            sl = pl.ds(pl.multiple_of(ci * rc, rc), rc)
            g = p_ref[0, sl, :].astype(F32)
            for i in range(1, n_parts):
                g = g + p_ref[i, sl, :].astype(F32)
            mn = ADAM_B1 * m_ref[sl, :] + (1.0 - ADAM_B1) * g
            vn = ADAM_B2 * v_ref[sl, :] + (1.0 - ADAM_B2) * jnp.square(g)
            m_hat = mn / c1
            v_hat = vn / c2
            g_ref[sl, :] = g
            d_ref[sl, :] = -ADAM_LR * (m_hat / (jnp.sqrt(v_hat) + ADAM_EPS) + ADAM_WD * w_ref[sl, :])
            mo_ref[sl, :] = mn
            vo_ref[sl, :] = vn
            return carry

        lax.fori_loop(0, tr // rc, chunk, 0)

    spec = pl.BlockSpec((tr, cdim), lambda i: (i, 0))
    return pl.pallas_call(
        body, name=name, grid=(r // tr,),
        in_specs=[pl.BlockSpec((n_parts, tr, cdim), lambda i: (0, i, 0)), spec, spec, spec],
        out_specs=[spec] * 4,
        out_shape=[jax.ShapeDtypeStruct((r, cdim), F32)] * 4,
        compiler_params=_cparams(("parallel",)),
    )(parts, w, m, v)


def _pad_to(v, n):
    return v if v.shape[0] == n else jnp.concatenate([v, jnp.zeros((n - v.shape[0],), v.dtype)])


def _pad_cols(v, n):
    return v if v.shape[-1] == n else jnp.concatenate([v, jnp.zeros(v.shape[:-1] + (n - v.shape[-1],), v.dtype)], axis=-1)


def _pack(vs, cols, row_mult, dtype):
    offs, o = [], 0
    for v in vs:
        offs.append(o)
        o += v.size
    rows = -(-o // cols)
    rows = -(-rows // row_mult) * row_mult
    flat = jnp.concatenate([v.reshape(-1).astype(dtype) for v in vs])
    return _pad_to(flat, rows * cols).reshape(rows, cols), offs


SHARDED = ("ffn1_w_gate", "ffn1_w_up", "ffn1_w_down", "w_in", "w_merge_gate", "gla_gate_up", "w_branch_fox",
           "w_branch_gla", "w_out", "ffn2_w_gate", "ffn2_w_up", "ffn2_w_down", "w_ple_gate", "w_ple_proj")
ROW_SHARDED = ("ffn1_w_down", "w_out", "ffn2_w_down", "w_ple_gate")
REPLICATED = ("ffn1_norm", "mix_norm", "fox_forget_bias", "gla_gate_bias", "gla_head_norm", "b_merge_gate",
              "ffn2_norm", "ple_norm", "final_norm")
WEIGHTS = ("ffn1_norm", "ffn1_w_gate", "ffn1_w_up", "ffn1_w_down", "mix_norm", "w_in", "fox_forget_bias",
           "gla_gate_up", "gla_gate_bias", "gla_head_norm", "w_branch_fox", "w_branch_gla", "w_merge_gate",
           "b_merge_gate", "w_out", "ffn2_norm", "ffn2_w_gate", "ffn2_w_up", "ffn2_w_down", "ple_norm",
           "w_ple_proj", "w_ple_gate", "final_norm")


def kernel(x, p, ffn1_norm, ffn1_w_gate, ffn1_w_up, ffn1_w_down, mix_norm, w_in, fox_forget_bias, gla_gate_up, gla_gate_bias, gla_head_norm, w_branch_fox, w_branch_gla, w_merge_gate, b_merge_gate, w_out, ffn2_norm, ffn2_w_gate, ffn2_w_up, ffn2_w_down, ple_norm, w_ple_proj, w_ple_gate, final_norm, loss_target, m_ffn1_norm, m_ffn1_w_gate, m_ffn1_w_up, m_ffn1_w_down, m_mix_norm, m_w_in, m_fox_forget_bias, m_gla_gate_up, m_gla_gate_bias, m_gla_head_norm, m_w_branch_fox, m_w_branch_gla, m_w_merge_gate, m_b_merge_gate, m_w_out, m_ffn2_norm, m_ffn2_w_gate, m_ffn2_w_up, m_ffn2_w_down, m_ple_norm, m_w_ple_proj, m_w_ple_gate, m_final_norm, v_ffn1_norm, v_ffn1_w_gate, v_ffn1_w_up, v_ffn1_w_down, v_mix_norm, v_w_in, v_fox_forget_bias, v_gla_gate_up, v_gla_gate_bias, v_gla_head_norm, v_w_branch_fox, v_w_branch_gla, v_w_merge_gate, v_b_merge_gate, v_w_out, v_ffn2_norm, v_ffn2_w_gate, v_ffn2_w_up, v_ffn2_w_down, v_ple_norm, v_w_ple_proj, v_w_ple_gate, v_final_norm):
    args = dict(locals())
    wts = {n: args[n] for n in WEIGHTS}
    mom_m = {n: args["m_" + n] for n in WEIGHTS}
    mom_v = {n: args["v_" + n] for n in WEIGHTS}

    xs, ps, tgt = x[0], p[0, 0], loss_target[0]
    s, d = xs.shape
    fox_w = w_branch_fox.shape[1]
    gla_vw = w_branch_gla.shape[1]
    fox_heads = fox_w // HEAD_DIM
    gla_heads = gla_vw // GLA_VAL_DIM
    gla_kw = gla_heads * HEAD_DIM
    rank = gla_gate_up.shape[1]

    c_fl = 3 * fox_w
    o_gr, o_gq, o_gk = gla_vw, 2 * gla_vw, 2 * gla_vw + gla_kw
    o_fl = o_gk + gla_kw
    o_gd = o_fl + LANES
    rest_w = o_gd + LANES
    padded = c_fl + rest_w
    seg = [(c_fl, 0), (fox_heads, c_fl + o_fl), (gla_kw, c_fl + o_gq), (gla_kw, c_fl + o_gk), (gla_vw, c_fl),
           (gla_vw, c_fl + o_gr), (rank, c_fl + o_gd)]
    dest = np.concatenate([np.arange(w_, dtype=np.int32) + o_ for w_, o_ in seg])
    ws = w_in.shape[2]
    wp = -(-ws // LANES) * LANES
    tw = 256 if padded % 256 == 0 else LANES
    tbl, (t_idx, t_val, t_width), (s_idx, s_val, s_width) = _sel_tables(dest, ws, wp, tw)
    tbl = jnp.asarray(tbl)

    ov = _Overlap()
    for n in SHARDED:
        sh = wts[n][0].astype(BF16)
        ov.add(n, "gather", _pad_cols(sh, wp) if n == "w_in" else sh)
    fbias = _pad_cols(fox_forget_bias, LANES)
    bmg_f, bmg_g = b_merge_gate[:, :d], b_merge_gate[:, d:]
    ghn = jnp.tile(gla_head_norm, (1, gla_heads))

    h1, ffn1_saved = _ffn_fwd(xs, ffn1_norm, ov, tag="ffn1")
    u = _rms_fwd(h1, mix_norm, name="mix_norm")
    win = _win_unshard(ov.weight("w_in"), tbl, jnp.asarray(t_idx), jnp.asarray(t_val), t_width, tw=tw, padded=padded,
                       name="in_proj_unshard")
    win_fox, win_rest = win[:, :c_fl], win[:, c_fl:]
    zf = ov.mm(u, win_fox, mode="nn", name="in_proj_fox", out_dtype=BF16)
    zr = ov.mm(u, win_rest, mode="nn", name="in_proj_rest")
    gz = ov.mm(u, ov.weight("w_merge_gate"), mode="nn", name="merge_gate")
    gup = ov.weight("gla_gate_up").transpose(1, 0, 2).reshape(rank, gla_kw)
    gup = jnp.concatenate([gup, jnp.zeros((LANES - rank, gla_kw), BF16)], axis=0)

    log_f = _rowwise(lambda fl, b: _log_sigmoid(fl + b), [(zr, LANES, o_fl // LANES)], [fbias], [(LANES, F32)],
                     name="forget_gate")[0]
    f_cum = _cumsum(log_f[:, :fox_heads].T, reverse=False, name="forget_cumsum")
    f_col, f_row = f_cum[:, :, None], f_cum[:, None, :]
    attn_us = 2.0 * s * s * HEAD_DIM * fox_heads / _Overlap.MM_FLOPS_PER_US
    y_fox, y_fox_bf, lse = ov.carry(
        ATTN_FWD_MATMULS * attn_us, lambda comm: _fox_fwd(zf, f_col, f_row, heads=fox_heads, name="fox_fwd", comm=comm))

    def decay_fn(gd, gupv, gb):
        return _log_sigmoid(_dot(gd, gupv, NN) + gb) * (1.0 / GLA_GATE_TAU)

    la = _rowwise(decay_fn, [(zr, LANES, o_gd // LANES)], [gup, gla_gate_bias], [(gla_kw, F32)], name="gla_decay", rc=128)[0]
    q_blk, k_blk = o_gq // HEAD_DIM, o_gk // HEAD_DIM
    o_gla, states = _gla_fwd(zr, la, heads=gla_heads, q_blk=q_blk, k_blk=k_blk, name="gla_fwd")

    def gla_out_fn(o, gr, g):
        outs = []
        for hh in range(gla_heads):
            sl = slice(hh * GLA_VAL_DIM, (hh + 1) * GLA_VAL_DIM)
            _, oh = _rms_parts(o[:, sl])
            outs.append(oh * g[:, sl] * _silu_parts(gr[:, sl])[0])
        return jnp.concatenate(outs, axis=1)

    y_gla = _rowwise(gla_out_fn, [o_gla, (zr, gla_vw, o_gr // gla_vw)], [ghn], [(gla_vw, BF16)], name="gla_out")[0]
    br_f = ov.mm(y_fox_bf, ov.weight("w_branch_fox"), mode="nn", name="branch_fox")
    br_g = ov.mm(y_gla, ov.weight("w_branch_gla"), mode="nn", name="branch_gla")

    def merge_fn(zf_, zg_, bf_, bg_, b1, b2):
        return _sigmoid(zf_ + b1) * bf_ + _sigmoid(zg_ + b2) * bg_

    merged = _rowwise(merge_fn, [(gz, d, 0), (gz, d, 1), br_f, br_g], [bmg_f, bmg_g], [(d, BF16)], name="merge")[0]
    h2 = ov.mm(merged, ov.weight("w_out"), mode="nn", name="out_proj", add=h1)
    h3, ffn2_saved = _ffn_fwd(h2, ffn2_norm, ov, tag="ffn2")
    n3 = _rms_fwd(h3, ple_norm, name="ple_norm")
    gl = ov.mm(n3, ov.weight("w_ple_gate"), mode="nn", name="ple_gate")
    pe = ov.mm(ps, ov.weight("w_ple_proj"), mode="nn", name="ple_proj")

    def head_fn(h3b, glb, peb, tb, gfin):
        pg = _sigmoid(glb)
        h4 = h3b + pg * peb
        r, xh = _rms_parts(h4)
        err = xh * gfin - tb
        dy = err * (1.0 / d)
        t = dy * gfin
        dh4 = r * (t - xh * jnp.mean(t * xh, axis=-1, keepdims=True))
        return dh4, dh4 * pg, dh4 * peb * pg * (1.0 - pg), _colsum(err * err), _colsum(dy * xh)

    dh4, dpe, dgl, loss_cols, d_final = _rowwise(
        head_fn, [h3, gl, pe, tgt], [final_norm.reshape(1, d)], [(d, F32), (d, BF16), (d, BF16)], [d, d], name="loss_head")
    loss = lax.psum(0.5 * jnp.sum(loss_cols) / d, AXES)

    grads = {"final_norm": d_final.reshape(d)}
    ov.grad("w_ple_proj", ov.mm(ps, dpe, mode="tn", name="ple_proj_dw", out_dtype=BF16, out_chunked=True))
    ov.grad("w_ple_gate", ov.mm(n3, dgl, mode="tn", name="ple_gate_dw", out_dtype=BF16))
    dn3 = ov.mm(dgl, ov.weight("w_ple_gate"), mode="nt", name="ple_gate_dx")
    dh3, dh3_bf, grads["ple_norm"] = _rms_bwd(h3, dn3, ple_norm, dh4, name="ple_norm_bwd")
    dh2, dh2_bf, grads["ffn2_norm"] = _ffn_bwd(h2, ffn2_norm, ov, ffn2_saved, dh3, dh3_bf, tag="ffn2")

    ov.grad("w_out", ov.mm(merged, dh2_bf, mode="tn", name="out_proj_dw", out_dtype=BF16))
    dmerged = ov.mm(dh2_bf, ov.weight("w_out"), mode="nt", name="out_proj_dx")

    def merge_bwd_fn(zf_, zg_, bf_, bg_, dm, b1, b2):
        sf, sg = _sigmoid(zf_ + b1), _sigmoid(zg_ + b2)
        dz = jnp.concatenate([dm * bf_ * sf * (1.0 - sf), dm * bg_ * sg * (1.0 - sg)], axis=1)
        return dm * sf, dm * sg, dz, _colsum(dz)

    dbr_f, dbr_g, dgz, grads["b_merge_gate"] = _rowwise(
        merge_bwd_fn, [(gz, d, 0), (gz, d, 1), br_f, br_g, dmerged], [bmg_f, bmg_g],
        [(d, BF16), (d, BF16), (2 * d, BF16)], [2 * d], name="merge_bwd")
    ov.grad("w_merge_gate", ov.mm(u, dgz, mode="tn", name="merge_gate_dw", out_dtype=BF16, out_chunked=True))
    ov.grad("w_branch_fox", ov.mm(y_fox_bf, dbr_f, mode="tn", name="branch_fox_dw", out_dtype=BF16, out_chunked=True))
    ov.grad("w_branch_gla", ov.mm(y_gla, dbr_g, mode="tn", name="branch_gla_dw", out_dtype=BF16, out_chunked=True))
    dy_fox = ov.mm(dbr_f, ov.weight("w_branch_fox"), mode="nt", name="branch_fox_dx")
    dy_gla = ov.mm(dbr_g, ov.weight("w_branch_gla"), mode="nt", name="branch_gla_dx")

    def gla_out_bwd_fn(o, gr, dy, g):
        dos, dgrs, dgs = [], [], []
        for hh in range(gla_heads):
            sl = slice(hh * GLA_VAL_DIM, (hh + 1) * GLA_VAL_DIM)
            r, oh = _rms_parts(o[:, sl])
            si, dsi = _silu_parts(gr[:, sl])
            don = dy[:, sl] * si
            dgrs.append(dy[:, sl] * oh * g[:, sl] * dsi)
            t = don * g[:, sl]
            dos.append(r * (t - oh * jnp.mean(t * oh, axis=-1, keepdims=True)))
            dgs.append(_colsum(don * oh))
        return jnp.concatenate(dos, axis=1), jnp.concatenate(dgrs, axis=1), jnp.concatenate(dgs, axis=1)

    do_gla, dgr, d_ghn = _rowwise(gla_out_bwd_fn, [o_gla, (zr, gla_vw, o_gr // gla_vw), dy_gla], [ghn],
                                  [(gla_vw, F32), (gla_vw, F32)], [gla_vw], name="gla_out_bwd")
    grads["gla_head_norm"] = d_ghn.reshape(gla_heads, GLA_VAL_DIM).sum(axis=0, keepdims=True)
    dgq, dgk, dgv, dla = _gla_bwd(zr, la, do_gla, states, heads=gla_heads, q_blk=q_blk, k_blk=k_blk, name="gla_bwd")

    def decay_bwd_fn(dl, gd, gupv, gb):
        pre = _dot(gd, gupv, NN) + gb
        dpre = dl * (1.0 / GLA_GATE_TAU) * _sigmoid(-pre)
        return _dot(dpre, gupv, NT), dpre, _colsum(dpre)

    dgd, dpre_bf, grads["gla_gate_bias"] = _rowwise(
        decay_bwd_fn, [dla, (zr, LANES, o_gd // LANES)], [gup, gla_gate_bias], [(LANES, F32), (gla_kw, BF16)], [gla_kw],
        name="gla_decay_bwd", rc=128)
    d_gup = ov.mm(zr[:, o_gd:o_gd + LANES], dpre_bf, mode="tn", name="gla_gate_up_dw")[:rank]
    ov.grad("gla_gate_up", d_gup.reshape(rank, N_DEV, gla_kw // N_DEV).transpose(1, 0, 2).astype(BF16))

    def delta_fn(dyv, yv):
        outs = []
        for hh in range(fox_heads):
            sl = slice(hh * HEAD_DIM, (hh + 1) * HEAD_DIM)
            outs.append(jnp.broadcast_to(jnp.sum(dyv[:, sl] * yv[:, sl], axis=-1, keepdims=True), (dyv.shape[0], HEAD_DIM)))
        return jnp.concatenate(outs, axis=1)

    delta = _rowwise(delta_fn, [dy_fox, y_fox], [], [(fox_w, F32)], name="fox_delta")[0]
    dfq, dfk, dfv, d_fcol, d_frow = ov.carry(
        ATTN_BWD_MATMULS * attn_us,
        lambda comm: _fox_bwd(zf, dy_fox, lse, delta, f_col, f_row, heads=fox_heads, name="fox_bwd", comm=comm))
    d_fcum = d_fcol[:, ::HEAD_DIM].T + d_frow.reshape(fox_heads, s)
    d_logf = _cumsum(d_fcum, reverse=True, name="forget_cumsum_bwd")
    d_logf = _pad_cols(d_logf.T, LANES)

    def forget_bwd_fn(dl, fl, b):
        dfl_ = dl * _sigmoid(-(fl + b))
        return dfl_, _colsum(dfl_)

    dfl, d_fbias = _rowwise(forget_bwd_fn, [d_logf, (zr, LANES, o_fl // LANES)], [fbias], [(LANES, F32)], [LANES],
                            name="forget_gate_bwd")
    grads["fox_forget_bias"] = d_fbias[:, :fox_heads]

    dz = jnp.concatenate([dfq, dfk, dfv, dgv, dgr, dgq, dgk, dfl, dgd], axis=1).astype(BF16)
    dwin = ov.mm(u, dz, mode="tn", name="in_proj_dw", out_dtype=BF16)
    ov.grad("w_in", _win_to_shards(dwin, tbl, jnp.asarray(s_idx), jnp.asarray(s_val), s_width, tw=tw, wp=wp,
                                   name="in_proj_dw_shards"))
    du = ov.mm(dgz, ov.weight("w_merge_gate"), mode="nt", name="merge_gate_dx")
    du = ov.mm(dz, win, mode="nt", name="in_proj_dx", add=du)
    dh1, dh1_bf, grads["mix_norm"] = _rms_bwd(h1, du, mix_norm, dh2, name="mix_norm_bwd")
    dx, _, grads["ffn1_norm"] = _ffn_bwd(xs, ffn1_norm, ov, ffn1_saved, dh1, dh1_bf, tag="ffn1")

    outs = {}
    for n in SHARDED:
        parts = ov.get("d_" + n)
        state = [_pad_cols(t_[n][0], parts.shape[2]) for t_ in (wts, mom_m, mom_v)]
        res4 = _adamw(parts, *state, name=f"adamw_{n}")
        for kind, r_ in zip(("grad", "delta", "new_m", "new_v"), res4):
            outs[f"{kind}_{n}"] = r_[:, :wts[n].shape[2]][None]

    send_small, small_offs = _pack([grads[n] for n in REPLICATED], LANES, 8, F32)
    recv_small = _comm("bcast", send_small, name="exchange_replicated")
    w_sm, _ = _pack([wts[n] for n in REPLICATED], LANES, 8, F32)
    m_sm, _ = _pack([mom_m[n] for n in REPLICATED], LANES, 8, F32)
    v_sm, _ = _pack([mom_v[n] for n in REPLICATED], LANES, 8, F32)
    small = _adamw(recv_small, w_sm, m_sm, v_sm, name="adamw_replicated")
    for kind, buf in zip(("grad", "delta", "new_m", "new_v"), small):
        fs = buf.reshape(-1)
        for n, o in zip(REPLICATED, small_offs):
            outs[f"{kind}_{n}"] = fs[o:o + wts[n].size].reshape(wts[n].shape)

    res = [loss, dx[None]]
    for kind in ("grad", "delta", "new_m", "new_v"):
        res += [outs[f"{kind}_{n}"] for n in WEIGHTS]
    return tuple(res)
```

```python
import functools

import jax
import jax.numpy as jnp
import numpy as np
from jax import lax
from jax.experimental import pallas as pl
from jax.experimental.pallas import tpu as pltpu

F32 = jnp.float32
BF16 = jnp.bfloat16
MESH = pl.DeviceIdType.MESH
AXES = ("x", "y", "c")
N_DEV = 8

VMEM_LIMIT_BYTES = 56 * 1024 * 1024
MM_BLOCK_BUDGET_BYTES = 40 * 1024 * 1024
LANES = 128

EPS = 1e-6
HEAD_DIM = 128
GLA_VAL_DIM = 256
GLA_CHUNK = 64
GLA_GATE_TAU = 16.0
ADAM_LR, ADAM_B1, ADAM_B2, ADAM_EPS, ADAM_WD, ADAM_STEP = 0.001, 0.9, 0.999, 1e-08, 0.01, 10

ATTN_FWD_MATMULS = 7.0
ATTN_BWD_MATMULS = 7.0

HIGHEST = lax.Precision.HIGHEST
NN = (((1,), (0,)), ((), ()))
NT = (((1,), (1,)), ((), ()))
TN = (((0,), (0,)), ((), ()))


def _cparams(sem):
    return pltpu.CompilerParams(dimension_semantics=sem, vmem_limit_bytes=VMEM_LIMIT_BYTES)


def _pick(dim, cands):
    for c in cands:
        if dim % c == 0:
            return c
    return dim


def _bf(v):
    return v if v.dtype == BF16 else v.astype(BF16)


def _dot(a, b, dims):
    return lax.dot_general(_bf(a), _bf(b), dims, preferred_element_type=F32)


def _sigmoid(v):
    return 1.0 / (1.0 + jnp.exp(-v))


def _log_sigmoid(v):
    return jnp.minimum(v, 0.0) - jnp.log(1.0 + jnp.exp(-jnp.abs(v)))


def _logical(v):
    return (v.shape[1], v.shape[0] * v.shape[2]) if v.ndim == 3 else v.shape


def _mm(a, b, *, mode, name, out_dtype=F32, add=None, scale=1.0, out_chunked=False, comm=()):
    la, lb = _logical(a), _logical(b)
    if mode == "nn":
        (m, k), (k2, n) = la, lb
        a_minor, b_minor = "k", "n"
    elif mode == "nt":
        (m, k), (n, k2) = la, lb
        a_minor, b_minor = "k", "k"
    else:
        (k, m), (k2, n) = la, lb
        a_minor, b_minor = "m", "n"
    assert k == k2, (name, a.shape, b.shape)
    forced = {}
    for v, minor in ((a, a_minor), (b, b_minor)):
        if v.ndim == 3:
            assert forced.get(minor, v.shape[2]) == v.shape[2], name
            forced[minor] = v.shape[2]
    if out_chunked:
        assert forced.get("n", n // N_DEV) == n // N_DEV, name
        forced["n"] = n // N_DEV
    tm = forced.get("m") or _pick(m, (1024, 512, 256, 128))
    tn = forced.get("n") or _pick(n, (1408, 1280, 1024, 512, 256, 128))
    tk = forced.get("k")
    if not tk:
        def blocks_bytes(t):
            io = tm * t * a.dtype.itemsize + t * tn * b.dtype.itemsize
            return 2 * (io + tm * tn * (jnp.dtype(out_dtype).itemsize + (4 if add is not None else 0))) + tm * tn * 4
        tk = k if blocks_bytes(k) <= MM_BLOCK_BUDGET_BYTES else _pick(k, (512, 640, 256, 128))
    nk = k // tk
    dims = {"nn": NN, "nt": NT, "tn": TN}[mode]
    gi, gj, gk = (lambda i, j, kk: i), (lambda i, j, kk: j), (lambda i, j, kk: kk)

    def spec(v, t_major, t_minor, g_major, g_minor):
        if v is not None and v.ndim == 3:
            return pl.BlockSpec((None, t_major, v.shape[2]), lambda i, j, kk: (g_minor(i, j, kk), g_major(i, j, kk), 0))
        return pl.BlockSpec((t_major, t_minor), lambda i, j, kk: (g_major(i, j, kk), g_minor(i, j, kk)))

    a_spec = spec(a, tk, tm, gk, gi) if mode == "tn" else spec(a, tm, tk, gi, gk)
    b_spec = spec(b, tn, tk, gj, gk) if mode == "nt" else spec(b, tk, tn, gk, gj)
    if out_chunked:
        o_spec = pl.BlockSpec((None, tm, tn), lambda i, j, kk: (j, i, 0))
        out_shape = jax.ShapeDtypeStruct((N_DEV, m, tn), out_dtype)
    else:
        o_spec = pl.BlockSpec((tm, tn), lambda i, j, kk: (i, j))
        out_shape = jax.ShapeDtypeStruct((m, n), out_dtype)
    has_add = add is not None
    assert not (has_add and out_chunked), name

    def body(*refs):
        a_ref, b_ref = refs[0], refs[1]
        add_ref = refs[2] if has_add else None
        o_ref = refs[3 if has_add else 2]
        kk = pl.program_id(2)

        def finish(r):
            if scale != 1.0:
                r = r * scale
            if has_add:
                r = r + add_ref[...]
            o_ref[...] = r.astype(o_ref.dtype)

        if nk == 1:
            finish(_dot(a_ref[...], b_ref[...], dims))
        else:
            acc_ref = refs[-1]

            @pl.when(kk == 0)
            def _():
                acc_ref[...] = jnp.zeros_like(acc_ref)

            acc_ref[...] += _dot(a_ref[...], b_ref[...], dims)

            @pl.when(kk == nk - 1)
            def _():
                finish(acc_ref[...])

    res, carried = _call(
        body, name=name, grid=(m // tm, n // tn, nk),
        in_specs=[a_spec, b_spec] + ([o_spec] if has_add else []), out_specs=[o_spec], out_shape=[out_shape],
        scratch_shapes=[pltpu.VMEM((tm, tn), F32)] if nk > 1 else [],
        semantics=("parallel", "parallel", "arbitrary"), operands=[a, b] + ([add] if has_add else []), comm=comm)
    return (res[0], carried) if comm else res[0]


def _call(body, *, name, grid, in_specs, out_specs, out_shape, scratch_shapes, semantics, operands, comm=()):
    n_in, n_out, n_scr, n = len(in_specs), len(out_specs), len(scratch_shapes), len(comm)
    if not comm:
        res = pl.pallas_call(body, name=name, grid=grid, in_specs=in_specs, out_specs=out_specs, out_shape=out_shape,
                             scratch_shapes=scratch_shapes, compiler_params=_cparams(semantics))(*operands)
        return res, []

    def carrying(*refs):
        ins, c_in = refs[:n_in], refs[n_in:n_in + n]
        outs, c_out = refs[n_in + n:n_in + n + n_out], refs[n_in + n + n_out:n_in + 2 * n + n_out]
        scratch, sems = refs[n_in + 2 * n + n_out:][:n_scr], refs[n_in + 2 * n + n_out + n_scr:]
        tasks = [_comm_ops(kind, c_in[t], c_out[t], *sems[3 * t:3 * t + 3]) for t, (kind, _) in enumerate(comm)]
        ids = [pl.program_id(ax) for ax in range(len(grid))]

        @pl.when(functools.reduce(lambda p, q: p & q, [i == 0 for i in ids]))
        def _():
            for start, _ in tasks:
                start()

        body(*ins, *outs, *scratch)

        @pl.when(functools.reduce(lambda p, q: p & q, [i == g - 1 for i, g in zip(ids, grid)]))
        def _():
            for _, finish in tasks:
                finish()

    any_spec = pl.BlockSpec(memory_space=pl.ANY)
    res = pl.pallas_call(
        carrying, name=name, grid=grid,
        in_specs=list(in_specs) + [any_spec] * n, out_specs=list(out_specs) + [any_spec] * n,
        out_shape=list(out_shape) + [_comm_out_shape(kind, v) for kind, v in comm],
        scratch_shapes=list(scratch_shapes) + _comm_scratch(n),
        compiler_params=_cparams(("arbitrary",) * len(grid)),
    )(*operands, *[v for _, v in comm])
    return res[:n_out], res[n_out:]


def _rowwise(fn, rows, consts, outs, accs=(), *, name, rc=None):
    rows = [r if isinstance(r, tuple) else (r, r.shape[1], 0) for r in rows]
    m = rows[0][0].shape[0]
    widths = [w for _, w, _ in rows] + [n for n, _ in outs]
    row_bytes = sum(w * r.dtype.itemsize for r, w, _ in rows) + sum(n * jnp.dtype(d).itemsize for n, d in outs)
    tm = 1024
    while tm > 16 and (m % tm or 2 * tm * row_bytes > 24 * 1024 * 1024):
        tm //= 2
    if m % tm:
        tm = m
    if rc is None:
        rc = 16
        while rc * 2 <= tm and rc * 2 * max(widths) <= 32768:
            rc *= 2
    rc = min(rc, tm)
    nr, nc, no = len(rows), len(consts), len(outs)

    def body(*refs):
        in_refs, c_refs = refs[:nr], refs[nr:nr + nc]
        o_refs, a_refs = refs[nr + nc:nr + nc + no], refs[nr + nc + no:]

        @pl.when(pl.program_id(0) == 0)
        def _():
            for r in a_refs:
                r[...] = jnp.zeros_like(r)

        cvals = [c[...] for c in c_refs]

        def chunk(ci, carry):
            sl = pl.ds(pl.multiple_of(ci * rc, rc), rc)
            res = fn(*[r[sl, :] for r in in_refs], *cvals)
            if not isinstance(res, (tuple, list)):
                res = (res,)
            for r, v in zip(o_refs, res[:no]):
                r[sl, :] = v.astype(r.dtype)
            for r, v in zip(a_refs, res[no:]):
                r[...] += v
            return carry

        lax.fori_loop(0, tm // rc, chunk, 0)

    in_specs = [pl.BlockSpec((tm, w), functools.partial(lambda i, cb: (i, cb), cb=cb)) for _, w, cb in rows]
    in_specs += [pl.BlockSpec(c.shape, lambda i: (0, 0)) for c in consts]
    out_specs = [pl.BlockSpec((tm, n), lambda i: (i, 0)) for n, _ in outs]
    out_specs += [pl.BlockSpec((1, n), lambda i: (0, 0)) for n in accs]
    out_shape = [jax.ShapeDtypeStruct((m, n), d) for n, d in outs] + [jax.ShapeDtypeStruct((1, n), F32) for n in accs]
    res = pl.pallas_call(
        body, name=name, grid=(m // tm,),
        in_specs=in_specs, out_specs=out_specs, out_shape=out_shape,
        compiler_params=_cparams(("arbitrary",)),
    )(*[r for r, _, _ in rows], *consts)
    return res


def _colsum(v):
    return jnp.sum(v, axis=0, keepdims=True)


def _rms_parts(xv):
    r = lax.rsqrt(jnp.mean(xv * xv, axis=-1, keepdims=True) + EPS)
    return r, xv * r


def _rms_fwd(xv, g, *, name):
    d = xv.shape[1]

    def fn(xb, gb):
        _, xh = _rms_parts(xb)
        return xh * gb

    return _rowwise(fn, [xv], [g], [(d, BF16)], name=name)[0]


def _rms_bwd(xv, dy, g, add, *, name):
    d = xv.shape[1]

    def fn(xb, dyb, addb, gb):
        r, xh = _rms_parts(xb)
        t = dyb * gb
        dx = r * (t - xh * jnp.mean(t * xh, axis=-1, keepdims=True)) + addb
        return dx, dx, _colsum(dyb * xh)

    return _rowwise(fn, [xv, dy, add], [g], [(d, F32), (d, BF16)], [d], name=name)


def _silu_parts(a):
    sg = _sigmoid(a)
    return a * sg, sg * (1.0 + a * (1.0 - sg))


def _rows(v):
    return v.reshape(v.shape[0] * v.shape[1], v.shape[2])


def _ffn_fwd(h, g, ov, *, tag):
    s = h.shape[0]
    n = _rms_fwd(h, g, name=f"{tag}_norm")
    a = ov.mm(n, ov.weight(f"{tag}_w_gate"), mode="nn", name=f"{tag}_gate", out_chunked=True)
    b = ov.mm(n, ov.weight(f"{tag}_w_up"), mode="nn", name=f"{tag}_up", out_chunked=True)
    c = a.shape[2]
    hm = _rowwise(lambda av, bv: _silu_parts(av)[0] * bv, [_rows(a), _rows(b)], [], [(c, BF16)], name=f"{tag}_act")[0]
    hm = hm.reshape(N_DEV, s, c)
    out = ov.mm(hm, ov.weight(f"{tag}_w_down"), mode="nn", name=f"{tag}_down", add=h, scale=0.5)
    return out, (n, a, b, hm)


def _ffn_bwd(h, g, ov, saved, dout, dout_bf, *, tag):
    n, a, b, hm = saved
    wg, wu, wd = ov.weight(f"{tag}_w_gate"), ov.weight(f"{tag}_w_up"), ov.weight(f"{tag}_w_down")
    s, c = h.shape[0], wg.shape[2]
    d_wd = ov.mm(hm, dout_bf, mode="tn", name=f"{tag}_down_dw", scale=0.5, out_dtype=BF16)
    ov.grad(f"{tag}_w_down", d_wd)
    dhm = ov.mm(dout_bf, wd, mode="nt", name=f"{tag}_down_dx", scale=0.5, out_chunked=True)

    def act_bwd(av, bv, dv):
        si, dsi = _silu_parts(av)
        return dv * bv * dsi, dv * si

    da, db = _rowwise(act_bwd, [_rows(a), _rows(b), _rows(dhm)], [], [(c, BF16), (c, BF16)], name=f"{tag}_act_bwd")
    da, db = da.reshape(N_DEV, s, c), db.reshape(N_DEV, s, c)
    ov.grad(f"{tag}_w_gate", ov.mm(n, da, mode="tn", name=f"{tag}_gate_dw", out_dtype=BF16, out_chunked=True))
    ov.grad(f"{tag}_w_up", ov.mm(n, db, mode="tn", name=f"{tag}_up_dw", out_dtype=BF16, out_chunked=True))
    dn = ov.mm(da, wg, mode="nt", name=f"{tag}_gate_dx")
    dn = ov.mm(db, wu, mode="nt", name=f"{tag}_up_dx", add=dn)
    dh, dh_bf, dg = _rms_bwd(h, dn, g, dout, name=f"{tag}_norm_bwd")
    return dh, dh_bf, dg


def _cumsum(xv, *, reverse, name):
    h, s = xv.shape
    t = _pick(s, (512, 256, 128))
    nb = s // t

    def blk(j):
        return (0, nb - 1 - j) if reverse else (0, j)

    def body(x_ref, o_ref, carry):
        @pl.when(pl.program_id(0) == 0)
        def _():
            carry[...] = jnp.zeros_like(carry)

        i0 = lax.broadcasted_iota(jnp.int32, (t, t), 0)
        i1 = lax.broadcasted_iota(jnp.int32, (t, t), 1)
        tri = ((i0 >= i1) if reverse else (i0 <= i1)).astype(F32)
        xb = x_ref[...]
        o_ref[...] = jnp.dot(xb, tri, precision=HIGHEST, preferred_element_type=F32) + carry[...]
        carry[...] += jnp.sum(xb, axis=1, keepdims=True)

    return pl.pallas_call(
        body, name=name, grid=(nb,),
        in_specs=[pl.BlockSpec((h, t), blk)], out_specs=pl.BlockSpec((h, t), blk),
        out_shape=jax.ShapeDtypeStruct((h, s), F32),
        scratch_shapes=[pltpu.VMEM((h, 1), F32)],
        compiler_params=_cparams(("arbitrary",)),
    )(xv)


def _fox_tiles(s):
    t = _pick(s, (512, 256, 128))
    return t, t


def _causal(sc):
    t = sc.shape[0]
    keep = lax.broadcasted_iota(jnp.int32, (t, t), 1) <= lax.broadcasted_iota(jnp.int32, (t, t), 0)
    return jnp.where(keep, sc, -jnp.inf)


ATTN_HEADS_PER_STEP = 2


def _head_cols(hh):
    return slice(hh * HEAD_DIM, (hh + 1) * HEAD_DIM)


def _across(rowstat, width):
    return jnp.tile(rowstat, (1, width // HEAD_DIM))


def _fox_fwd(zf, f_rep, f_row, *, heads, name, comm=()):
    s = zf.shape[0]
    tq, tk = _fox_tiles(s)
    assert tq == tk
    nq, nk = s // tq, s // tk
    hp = ATTN_HEADS_PER_STEP if heads % ATTN_HEADS_PER_STEP == 0 else 1
    wb = hp * HEAD_DIM
    scale = HEAD_DIM ** -0.5
    w = heads * HEAD_DIM

    def body(q_ref, k_ref, v_ref, fq_ref, fk_ref, o32_ref, o16_ref, lse_ref, m_sc, l_sc, acc_sc):
        i, j = pl.program_id(1), pl.program_id(2)

        @pl.when(j == 0)
        def _():
            m_sc[...] = jnp.full_like(m_sc, -jnp.inf)
            l_sc[...] = jnp.zeros_like(l_sc)
            acc_sc[...] = jnp.zeros_like(acc_sc)

        def step(diagonal):
            for hh in range(hp):
                cols = _head_cols(hh)
                sc = _dot(q_ref[:, cols], k_ref[:, cols], NT) * scale + _across(fq_ref[:, cols], tk) - fk_ref[hh]
                if diagonal:
                    sc = _causal(sc)
                m_old = m_sc[hh]
                m_new = jnp.maximum(m_old, jnp.max(sc, axis=-1, keepdims=True))
                alpha = jnp.exp(m_old - m_new)
                pr = jnp.exp(sc - _across(m_new, tk))
                l_sc[hh] = alpha * l_sc[hh] + jnp.sum(pr, axis=-1, keepdims=True)
                acc_sc[hh] = alpha * acc_sc[hh] + _dot(pr, v_ref[:, cols], NN)
                m_sc[hh] = m_new

        @pl.when(j < i)
        def _():
            step(False)

        @pl.when(j == i)
        def _():
            step(True)

        @pl.when(j == nk - 1)
        def _():
            for hh in range(hp):
                cols = _head_cols(hh)
                o = acc_sc[hh] / l_sc[hh]
                o32_ref[:, cols] = o
                o16_ref[:, cols] = o.astype(BF16)
                lse_ref[:, cols] = m_sc[hh] + jnp.log(l_sc[hh])

    def kv_blk(off):
        return lambda h, i, j: (jnp.minimum(j, i), off + h)

    o_spec = pl.BlockSpec((tq, wb), lambda h, i, j: (i, h))
    stat = pltpu.VMEM((hp, tq, HEAD_DIM), F32)
    return _call(
        body, name=name, grid=(heads // hp, nq, nk),
        in_specs=[
            o_spec,
            pl.BlockSpec((tk, wb), kv_blk(heads // hp)),
            pl.BlockSpec((tk, wb), kv_blk(2 * heads // hp)),
            o_spec,
            pl.BlockSpec((hp, 1, tk), lambda h, i, j: (h, 0, jnp.minimum(j, i))),
        ],
        out_specs=[o_spec, o_spec, o_spec],
        out_shape=[jax.ShapeDtypeStruct((s, w), F32), jax.ShapeDtypeStruct((s, w), BF16),
                   jax.ShapeDtypeStruct((s, w), F32)],
        scratch_shapes=[stat, stat, stat],
        semantics=("parallel", "parallel", "arbitrary"), operands=[zf, zf, zf, f_rep, f_row], comm=comm)


def _fox_bwd(zf, do, lse, delta, f_rep, f_row, *, heads, name, comm=()):
    s = zf.shape[0]
    tq, tk = _fox_tiles(s)
    assert tq == tk
    nq, nk = s // tq, s // tk
    hp = ATTN_HEADS_PER_STEP if heads % ATTN_HEADS_PER_STEP == 0 else 1
    wb = hp * HEAD_DIM
    scale = HEAD_DIM ** -0.5
    w = heads * HEAD_DIM

    def body(q_ref, k_ref, v_ref, do_ref, lse_ref, dl_ref, fq_ref, fk_ref, dq_ref, dk_ref, dv_ref, dfq_ref, dfk_ref):
        j, i = pl.program_id(1), pl.program_id(2)

        @pl.when((j == 0) & (i == 0))
        def _():
            dq_ref[...] = jnp.zeros_like(dq_ref)
            dfq_ref[...] = jnp.zeros_like(dfq_ref)

        @pl.when(i == 0)
        def _():
            dk_ref[...] = jnp.zeros_like(dk_ref)
            dv_ref[...] = jnp.zeros_like(dv_ref)
            dfk_ref[...] = jnp.zeros_like(dfk_ref)

        def step(diagonal):
            rows = pl.ds(pl.multiple_of(i * tq, tq), tq)
            for hh in range(hp):
                cols = _head_cols(hh)
                q, k, v = q_ref[:, cols], k_ref[:, cols], v_ref[:, cols]
                dob = do_ref[:, cols].astype(BF16)
                sc = _dot(q, k, NT) * scale + _across(fq_ref[:, cols], tk) - fk_ref[hh]
                if diagonal:
                    sc = _causal(sc)
                pr = jnp.exp(sc - _across(lse_ref[:, cols], tk))
                dv_ref[:, cols] += _dot(pr, dob, TN)
                dp = _dot(dob, v, NT)
                ds = pr * (dp - _across(dl_ref[:, cols], tk))
                dsb = ds.astype(BF16)
                dk_ref[:, cols] += _dot(dsb, q, TN) * scale
                dq_ref[rows, cols] += _dot(dsb, k, NN) * scale
                dfq_ref[rows, cols] += jnp.broadcast_to(jnp.sum(ds, axis=1, keepdims=True), (tq, HEAD_DIM))
                dfk_ref[hh] -= jnp.sum(ds, axis=0, keepdims=True)

        @pl.when(i > j)
        def _():
            step(False)

        @pl.when(i == j)
        def _():
            step(True)

    q_spec = pl.BlockSpec((tq, wb), lambda h, j, i: (jnp.maximum(i, j), h))
    k_spec = pl.BlockSpec((tk, wb), lambda h, j, i: (j, h))
    whole = pl.BlockSpec((s, wb), lambda h, j, i: (0, h))
    row_spec = pl.BlockSpec((hp, 1, tk), lambda h, j, i: (h, 0, j))
    return _call(
        body, name=name, grid=(heads // hp, nk, nq),
        in_specs=[
            q_spec,
            pl.BlockSpec((tk, wb), lambda h, j, i: (j, heads // hp + h)),
            pl.BlockSpec((tk, wb), lambda h, j, i: (j, 2 * heads // hp + h)),
            q_spec, q_spec, q_spec, q_spec, row_spec,
        ],
        out_specs=[whole, k_spec, k_spec, whole, row_spec],
        out_shape=[jax.ShapeDtypeStruct((s, w), F32), jax.ShapeDtypeStruct((s, w), F32),
                   jax.ShapeDtypeStruct((s, w), F32), jax.ShapeDtypeStruct((s, w), F32),
                   jax.ShapeDtypeStruct((heads, 1, s), F32)],
        scratch_shapes=[], semantics=("parallel", "arbitrary", "arbitrary"),
        operands=[zf, zf, zf, do, lse, delta, f_rep, f_row], comm=comm)


def _gla_rows(s):
    return _pick(s, (256, 128, 64))


def _gla_chunk_terms(la_c, tri):
    a_cum = jnp.dot(tri, la_c, precision=HIGHEST, preferred_element_type=F32)
    a_tot = jnp.sum(la_c, axis=0, keepdims=True)
    return jnp.exp(a_tot - a_cum), jnp.exp(a_tot)


def _gla_fwd(zr, la, *, heads, q_blk, k_blk, name):
    s = zr.shape[0]
    c = GLA_CHUNK
    rows = _gla_rows(s)
    nsteps, ncs = s // rows, rows // c
    scale = HEAD_DIM ** -0.5

    def body(q_ref, k_ref, v_ref, la_ref, o_ref, st_ref, state):
        @pl.when(pl.program_id(1) == 0)
        def _():
            state[...] = jnp.zeros_like(state)

        tri = (lax.broadcasted_iota(jnp.int32, (c, c), 0) >= lax.broadcasted_iota(jnp.int32, (c, c), 1)).astype(F32)
        for t in range(ncs):
            sl = slice(t * c, (t + 1) * c)
            dec, e_tot = _gla_chunk_terms(la_ref[sl, :], tri)
            kd = k_ref[sl, :] * dec
            st_ref[t] = state[...]
            new = state[...] * e_tot + _dot(v_ref[sl, :], kd, TN)
            state[...] = new
            o_ref[sl, :] = _dot(q_ref[sl, :] * scale, new, NT)

    return pl.pallas_call(
        body, name=name, grid=(heads, nsteps),
        in_specs=[
            pl.BlockSpec((rows, HEAD_DIM), lambda h, i: (i, q_blk + h)),
            pl.BlockSpec((rows, HEAD_DIM), lambda h, i: (i, k_blk + h)),
            pl.BlockSpec((rows, GLA_VAL_DIM), lambda h, i: (i, h)),
            pl.BlockSpec((rows, HEAD_DIM), lambda h, i: (i, h)),
        ],
        out_specs=[
            pl.BlockSpec((rows, GLA_VAL_DIM), lambda h, i: (i, h)),
            pl.BlockSpec((None, ncs, GLA_VAL_DIM, HEAD_DIM), lambda h, i: (h, i, 0, 0)),
        ],
        out_shape=[jax.ShapeDtypeStruct((s, heads * GLA_VAL_DIM), F32),
                   jax.ShapeDtypeStruct((heads, s // c, GLA_VAL_DIM, HEAD_DIM), F32)],
        scratch_shapes=[pltpu.VMEM((GLA_VAL_DIM, HEAD_DIM), F32)],
        compiler_params=_cparams(("parallel", "arbitrary")),
    )(zr, zr, zr, la)


def _gla_bwd(zr, la, do, states, *, heads, q_blk, k_blk, name):
    s = zr.shape[0]
    c = GLA_CHUNK
    rows = _gla_rows(s)
    nsteps, ncs = s // rows, rows // c
    scale = HEAD_DIM ** -0.5

    def body(q_ref, k_ref, v_ref, la_ref, do_ref, st_ref, dq_ref, dk_ref, dv_ref, dla_ref, dstate):
        @pl.when(pl.program_id(1) == 0)
        def _():
            dstate[...] = jnp.zeros_like(dstate)

        i0 = lax.broadcasted_iota(jnp.int32, (c, c), 0)
        i1 = lax.broadcasted_iota(jnp.int32, (c, c), 1)
        tri = (i0 >= i1).astype(F32)
        strict = (i0 > i1).astype(F32)
        for t in reversed(range(ncs)):
            sl = slice(t * c, (t + 1) * c)
            dec, e_tot = _gla_chunk_terms(la_ref[sl, :], tri)
            kd = k_ref[sl, :] * dec
            kdb = kd.astype(BF16)
            vb = v_ref[sl, :].astype(BF16)
            dob = do_ref[sl, :].astype(BF16)
            prev = st_ref[t]
            cur = prev * e_tot + _dot(vb, kdb, TN)
            d_cur = dstate[...] + _dot(dob, q_ref[sl, :] * scale, TN)
            d_cur_b = d_cur.astype(BF16)
            dq_ref[sl, :] = _dot(dob, cur, NN) * scale
            dv_ref[sl, :] = _dot(kdb, d_cur_b, NT)
            dkd = _dot(vb, d_cur_b, NN)
            d_tot = e_tot * jnp.sum(d_cur * prev, axis=0, keepdims=True)
            dk_ref[sl, :] = dkd * dec
            dla_ref[sl, :] = d_tot + jnp.dot(strict, dkd * kd, precision=HIGHEST, preferred_element_type=F32)
            dstate[...] = d_cur * e_tot

    def rev(i):
        return nsteps - 1 - i

    kq_spec = pl.BlockSpec((rows, HEAD_DIM), lambda h, i: (rev(i), h))
    v_spec = pl.BlockSpec((rows, GLA_VAL_DIM), lambda h, i: (rev(i), h))
    return pl.pallas_call(
        body, name=name, grid=(heads, nsteps),
        in_specs=[
            pl.BlockSpec((rows, HEAD_DIM), lambda h, i: (rev(i), q_blk + h)),
            pl.BlockSpec((rows, HEAD_DIM), lambda h, i: (rev(i), k_blk + h)),
            v_spec, kq_spec, v_spec,
            pl.BlockSpec((None, ncs, GLA_VAL_DIM, HEAD_DIM), lambda h, i: (h, rev(i), 0, 0)),
        ],
        out_specs=[kq_spec, kq_spec, v_spec, kq_spec],
        out_shape=[jax.ShapeDtypeStruct((s, heads * HEAD_DIM), F32), jax.ShapeDtypeStruct((s, heads * HEAD_DIM), F32),
                   jax.ShapeDtypeStruct((s, heads * GLA_VAL_DIM), F32), jax.ShapeDtypeStruct((s, heads * HEAD_DIM), F32)],
        scratch_shapes=[pltpu.VMEM((GLA_VAL_DIM, HEAD_DIM), F32)],
        compiler_params=_cparams(("parallel", "arbitrary")),
    )(zr, zr, zr, la, do, states)


def _my_place():
    x, y, c = lax.axis_index("x"), lax.axis_index("y"), lax.axis_index("c")
    return x, y, c


def _gather_ops(x_ref, out_ref, send_sems, recv_sems, local_sem):
    def plan():
        x, y, c = _my_place()
        me, sibling = (x, y, c), (x, y, 1 - c)
        chips = [(1 - x, y), (x, 1 - y), (1 - x, 1 - y)]

        def blk(px, py, pc):
            return out_ref.at[4 * px + 2 * py + pc]

        def copy(k, block, to, src=None):
            return pltpu.make_async_remote_copy(
                src_ref=blk(*block) if src is None else src, dst_ref=blk(*block),
                send_sem=send_sems.at[k], recv_sem=recv_sems.at[k], device_id=to, device_id_type=MESH)

        mine = pltpu.make_async_copy(x_ref, blk(*me), local_sem)
        first = [copy(0, me, sibling, src=x_ref)]
        first += [copy(1 + j, me, (*chip, c), src=x_ref) for j, chip in enumerate(chips)]
        passed = [copy(4 + j, (*chip, c), sibling) for j, chip in enumerate(chips)]
        landed = [copy(1 + j, (*chip, c), me) for j, chip in enumerate(chips)]
        from_sibling = [copy(0, sibling, me)] + [copy(4 + j, (*chip, 1 - c), me) for j, chip in enumerate(chips)]
        return mine, first, passed, landed, from_sibling

    def start():
        mine, first, _, _, _ = plan()
        mine.start()
        for cp in first:
            cp.start()

    def finish():
        mine, first, passed, landed, from_sibling = plan()
        for cp, fwd in zip(landed, passed):
            cp.wait_recv()
            fwd.start()
        for cp in from_sibling:
            cp.wait_recv()
        for cp in first + passed:
            cp.wait_send()
        mine.wait()

    return start, finish


def _exchange_ops(scatter, s_ref, r_ref, send_sems, recv_sems, local_sem):
    def plan():
        x, y, c = _my_place()
        me = 4 * x + 2 * y + c
        mine = pltpu.make_async_copy(s_ref.at[me] if scatter else s_ref, r_ref.at[me], local_sem)
        sends, recvs = [], []
        for k in range(1, N_DEV):
            px, py, pc = x ^ ((k >> 2) & 1), y ^ ((k >> 1) & 1), c ^ (k & 1)
            peer = 4 * px + 2 * py + pc
            src = s_ref.at[peer] if scatter else s_ref
            for dst, out in ((r_ref.at[me], sends), (r_ref.at[peer], recvs)):
                out.append(pltpu.make_async_remote_copy(
                    src_ref=src, dst_ref=dst, send_sem=send_sems.at[k - 1], recv_sem=recv_sems.at[k - 1],
                    device_id=(px, py, pc), device_id_type=MESH))
        return mine, sends, recvs

    def start():
        mine, sends, _ = plan()
        mine.start()
        for cp in sends:
            cp.start()

    def finish():
        mine, sends, recvs = plan()
        for cp in recvs:
            cp.wait_recv()
        for cp in sends:
            cp.wait_send()
        mine.wait()

    return start, finish


def _pair_ops(s_ref, r_ref, send_sems, recv_sems):
    def plan():
        x, y, c = _my_place()
        return [pltpu.make_async_remote_copy(
            src_ref=s_ref.at[2 * q + 1 - c], dst_ref=r_ref.at[q], send_sem=send_sems.at[q], recv_sem=recv_sems.at[q],
            device_id=(x, y, 1 - c), device_id_type=MESH) for q in range(N_DEV // 2)]

    def start():
        for cp in plan():
            cp.start()

    def finish():
        copies = plan()
        for cp in copies:
            cp.wait_recv()
        for cp in copies:
            cp.wait_send()

    return start, finish


def _chips_ops(p_ref, r_ref, send_sems, recv_sems, local_sem):
    def plan():
        x, y, c = _my_place()
        chip = 2 * x + y
        mine = pltpu.make_async_copy(p_ref.at[chip], r_ref.at[chip], local_sem)
        sends, recvs = [], []
        for k in range(1, N_DEV // 2):
            px, py = x ^ (k >> 1), y ^ (k & 1)
            peer = 2 * px + py
            for dst, out in ((r_ref.at[chip], sends), (r_ref.at[peer], recvs)):
                out.append(pltpu.make_async_remote_copy(
                    src_ref=p_ref.at[peer], dst_ref=dst, send_sem=send_sems.at[k - 1], recv_sem=recv_sems.at[k - 1],
                    device_id=(px, py, c), device_id_type=MESH))
        return mine, sends, recvs

    def start():
        mine, sends, _ = plan()
        mine.start()
        for cp in sends:
            cp.start()

    def finish():
        mine, sends, recvs = plan()
        for cp in recvs:
            cp.wait_recv()
        for cp in sends:
            cp.wait_send()
        mine.wait()

    return start, finish


def _comm_ops(kind, src_ref, dst_ref, send_sems, recv_sems, local_sem):
    if kind == "gather":
        return _gather_ops(src_ref, dst_ref, send_sems, recv_sems, local_sem)
    if kind == "pair":
        return _pair_ops(src_ref, dst_ref, send_sems, recv_sems)
    if kind == "chips":
        return _chips_ops(src_ref, dst_ref, send_sems, recv_sems, local_sem)
    return _exchange_ops(False, src_ref, dst_ref, send_sems, recv_sems, local_sem)


def _comm_out_shape(kind, v):
    shape = {"pair": (N_DEV // 2,) + v.shape[1:], "chips": v.shape}.get(kind, (N_DEV,) + v.shape)
    return jax.ShapeDtypeStruct(shape, v.dtype)


def _comm_scratch(n_tasks):
    return [pltpu.SemaphoreType.DMA((N_DEV - 1,)), pltpu.SemaphoreType.DMA((N_DEV - 1,)), pltpu.SemaphoreType.DMA] * n_tasks


def _pair_sum(s, r, *, name):
    _, rows, cdim = s.shape
    tr = _pick(rows, (256, 128, 64, 32, 16, 8))
    rc = min(16, tr)

    def body(s_ref, r_ref, o_ref):
        c = lax.axis_index("c")

        def chunk(ci, carry):
            sl = pl.ds(pl.multiple_of(ci * rc, rc), rc)
            o_ref[sl, :] = (s_ref[c, sl, :].astype(F32) + r_ref[sl, :].astype(F32)).astype(o_ref.dtype)
            return carry

        lax.fori_loop(0, tr // rc, chunk, 0)

    spec = pl.BlockSpec((None, tr, cdim), lambda q, i: (q, i, 0))
    return pl.pallas_call(
        body, name=name, grid=(N_DEV // 2, rows // tr),
        in_specs=[pl.BlockSpec((None, 2, tr, cdim), lambda q, i: (q, 0, i, 0)), spec],
        out_specs=spec,
        out_shape=jax.ShapeDtypeStruct(r.shape, s.dtype),
        compiler_params=_cparams(("parallel", "parallel")),
    )(s.reshape(N_DEV // 2, 2, rows, cdim), r)


def _comm(kind, v, *, name):
    def body(s_ref, r_ref, send_sems, recv_sems, local_sem):
        start, finish = _comm_ops(kind, s_ref, r_ref, send_sems, recv_sems, local_sem)
        start()
        finish()

    return pl.pallas_call(
        body, name=name,
        out_shape=_comm_out_shape(kind, v),
        in_specs=[pl.BlockSpec(memory_space=pl.ANY)],
        out_specs=pl.BlockSpec(memory_space=pl.ANY),
        scratch_shapes=_comm_scratch(1),
    )(v)


class _Overlap:
    US_PER_MB = {"gather": 52.0, "pair": 1.0, "chips": 13.0, "bcast": 97.0}
    MM_FLOPS_PER_US = 6.0e8

    def __init__(self):
        self.queue, self.results, self.then = [], {}, {}

    def add(self, key, kind, v):
        self.queue.append((key, kind, v))

    def _cost(self, kind, v):
        return v.size * v.dtype.itemsize / 2 ** 20 * self.US_PER_MB[kind]

    def take(self, budget_us):
        taken, cum = [], 0.0
        while self.queue:
            cost = self._cost(*self.queue[0][1:])
            if taken and cum + cost > budget_us:
                break
            taken.append(self.queue.pop(0))
            cum += cost
        return taken

    def put(self, taken, res):
        for (key, _, _), r in zip(taken, res):
            self.results[key] = r
            if key in self.then:
                self.then.pop(key)(r)

    def carry(self, budget_us, fn):
        taken = self.take(budget_us)
        out, res = fn([(kind, v) for _, kind, v in taken])
        self.put(taken, res)
        return out

    def mm(self, a, b, *, mode, **kw):
        la, lb = _logical(a), _logical(b)
        budget = 2.0 * la[0] * la[1] * (lb[0] if mode == "nt" else lb[1]) / self.MM_FLOPS_PER_US

        def fn(comm):
            return _mm(a, b, mode=mode, comm=comm, **kw) if comm else (_mm(a, b, mode=mode, **kw), [])

        return self.carry(budget, fn)

    def get(self, key):
        while key not in self.results:
            keys = [k for k, _, _ in self.queue]
            task = self.queue.pop(keys.index(key if key in keys else "pair_" + key))
            self.put([task], [_comm(task[1], task[2], name=f"alone_{task[0]}")])
        return self.results[key]

    def weight(self, n):
        g = self.get(n)
        return g.reshape(-1, g.shape[2]) if n in ROW_SHARDED else g

    def grad(self, n, g):
        g = g if g.ndim == 3 else g.reshape(N_DEV, g.shape[0] // N_DEV, g.shape[1])
        self.add("pair_d_" + n, "pair", g)
        self.then["pair_d_" + n] = lambda r: self.add("d_" + n, "chips", _pair_sum(g, r, name=f"pair_sum_{n}"))


def _sel_tables(dest, ws, wp, tw):
    n_tiles = (int(dest.max()) + tw) // tw
    tbl = np.full((N_DEV, wp), -1, np.int32)
    for j in range(N_DEV):
        tbl[j, :ws] = dest[j * ws:(j + 1) * ws]
    by_tile = [sorted({j for j in range(N_DEV) if ((tbl[j] // tw) == t).any()}) for t in range(n_tiles)]
    by_shard = [sorted({int(t) for t in np.unique(tbl[j, :ws] // tw)}) for j in range(N_DEV)]

    def table(lists):
        width = max(len(v) for v in lists)
        idx = np.array([(v + [v[-1]] * width)[:width] if v else [0] * width for v in lists], np.int32)
        val = np.array([[1] * len(v) + [0] * (width - len(v)) for v in lists], np.int32)
        return idx.reshape(-1), val.reshape(-1), width

    return tbl[:, :, None], table(by_tile), table(by_shard)


def _sel_matrix(d_ref, t, wp, tw):
    cols = t * tw + lax.broadcasted_iota(jnp.int32, (wp, tw), 1)
    return (d_ref[...] == cols).astype(BF16)


def _win_unshard(g, tbl, idx, val, width, *, tw, padded, name):
    _, dm, wp = g.shape
    tm = _pick(dm, (1024, 512, 256, 128))

    def body(idx_ref, val_ref, g_ref, d_ref, o_ref, acc):
        t, s_ = pl.program_id(1), pl.program_id(2)

        @pl.when(s_ == 0)
        def _():
            acc[...] = jnp.zeros_like(acc)

        @pl.when(val_ref[t * width + s_] == 1)
        def _():
            acc[...] += _dot(g_ref[...], _sel_matrix(d_ref, t, wp, tw), NN)

        @pl.when(s_ == width - 1)
        def _():
            o_ref[...] = acc[...].astype(BF16)

    return pl.pallas_call(
        body, name=name,
        grid_spec=pltpu.PrefetchScalarGridSpec(
            num_scalar_prefetch=2, grid=(dm // tm, padded // tw, width),
            in_specs=[pl.BlockSpec((None, tm, wp), lambda i, t, s_, ix, vl: (ix[t * width + s_], i, 0)),
                      pl.BlockSpec((None, wp, 1), lambda i, t, s_, ix, vl: (ix[t * width + s_], 0, 0))],
            out_specs=pl.BlockSpec((tm, tw), lambda i, t, s_, ix, vl: (i, t)),
            scratch_shapes=[pltpu.VMEM((tm, tw), F32)]),
        out_shape=jax.ShapeDtypeStruct((dm, padded), BF16),
        compiler_params=_cparams(("parallel", "parallel", "arbitrary")),
    )(idx, val, g, tbl)


def _win_to_shards(dw, tbl, idx, val, width, *, tw, wp, name):
    dm = dw.shape[0]
    tm = _pick(dm, (1024, 512, 256, 128))

    def body(idx_ref, val_ref, w_ref, d_ref, o_ref, acc):
        j, s_ = pl.program_id(1), pl.program_id(2)

        @pl.when(s_ == 0)
        def _():
            acc[...] = jnp.zeros_like(acc)

        @pl.when(val_ref[j * width + s_] == 1)
        def _():
            acc[...] += _dot(w_ref[...], _sel_matrix(d_ref, idx_ref[j * width + s_], wp, tw), NT)

        @pl.when(s_ == width - 1)
        def _():
            o_ref[...] = acc[...].astype(BF16)

    return pl.pallas_call(
        body, name=name,
        grid_spec=pltpu.PrefetchScalarGridSpec(
            num_scalar_prefetch=2, grid=(dm // tm, N_DEV, width),
            in_specs=[pl.BlockSpec((tm, tw), lambda i, j, s_, ix, vl: (i, ix[j * width + s_])),
                      pl.BlockSpec((None, wp, 1), lambda i, j, s_, ix, vl: (j, 0, 0))],
            out_specs=pl.BlockSpec((None, tm, wp), lambda i, j, s_, ix, vl: (j, i, 0)),
            scratch_shapes=[pltpu.VMEM((tm, wp), F32)]),
        out_shape=jax.ShapeDtypeStruct((N_DEV, dm, wp), BF16),
        compiler_params=_cparams(("parallel", "parallel", "arbitrary")),
    )(idx, val, dw, tbl)


def _adamw(parts, w, m, v, *, name):
    r, cdim = w.shape
    n_parts = parts.shape[0]
    tr = _pick(r, (256, 128, 64, 32, 16, 8))
    rc = min(16, tr)
    c1 = 1.0 - ADAM_B1 ** ADAM_STEP
    c2 = 1.0 - ADAM_B2 ** ADAM_STEP

    def body(p_ref, w_ref, m_ref, v_ref, g_ref, d_ref, mo_ref, vo_ref):
        def chunk(ci, carry):
            sl = pl.ds(pl.multiple_of(ci * rc, rc), rc)
            g = p_ref[0, sl, :].astype(F32)
            for i in range(1, n_parts):
                g = g + p_ref[i, sl, :].astype(F32)
            mn = ADAM_B1 * m_ref[sl, :] + (1.0 - ADAM_B1) * g
            vn = ADAM_B2 * v_ref[sl, :] + (1.0 - ADAM_B2) * jnp.square(g)
            m_hat = mn / c1
            v_hat = vn / c2
            g_ref[sl, :] = g
            d_ref[sl, :] = -ADAM_LR * (m_hat / (jnp.sqrt(v_hat) + ADAM_EPS) + ADAM_WD * w_ref[sl, :])
            mo_ref[sl, :] = mn
            vo_ref[sl, :] = vn
            return carry

        lax.fori_loop(0, tr // rc, chunk, 0)

    spec = pl.BlockSpec((tr, cdim), lambda i: (i, 0))
    return pl.pallas_call(
        body, name=name, grid=(r // tr,),
        in_specs=[pl.BlockSpec((n_parts, tr, cdim), lambda i: (0, i, 0)), spec, spec, spec],
        out_specs=[spec] * 4,
        out_shape=[jax.ShapeDtypeStruct((r, cdim), F32)] * 4,
        compiler_params=_cparams(("parallel",)),
    )(parts, w, m, v)


def _pad_to(v, n):
    return v if v.shape[0] == n else jnp.concatenate([v, jnp.zeros((n - v.shape[0],), v.dtype)])


def _pad_cols(v, n):
    return v if v.shape[-1] == n else jnp.concatenate([v, jnp.zeros(v.shape[:-1] + (n - v.shape[-1],), v.dtype)], axis=-1)


def _pack(vs, cols, row_mult, dtype):
    offs, o = [], 0
    for v in vs:
        offs.append(o)
        o += v.size
    rows = -(-o // cols)
    rows = -(-rows // row_mult) * row_mult
    flat = jnp.concatenate([v.reshape(-1).astype(dtype) for v in vs])
    return _pad_to(flat, rows * cols).reshape(rows, cols), offs


SHARDED = ("ffn1_w_gate", "ffn1_w_up", "ffn1_w_down", "w_in", "w_merge_gate", "gla_gate_up", "w_branch_fox",
           "w_branch_gla", "w_out", "ffn2_w_gate", "ffn2_w_up", "ffn2_w_down", "w_ple_gate", "w_ple_proj")
ROW_SHARDED = ("ffn1_w_down", "w_out", "ffn2_w_down", "w_ple_gate")
REPLICATED = ("ffn1_norm", "mix_norm", "fox_forget_bias", "gla_gate_bias", "gla_head_norm", "b_merge_gate",
              "ffn2_norm", "ple_norm", "final_norm")
WEIGHTS = ("ffn1_norm", "ffn1_w_gate", "ffn1_w_up", "ffn1_w_down", "mix_norm", "w_in", "fox_forget_bias",
           "gla_gate_up", "gla_gate_bias", "gla_head_norm", "w_branch_fox", "w_branch_gla", "w_merge_gate",
           "b_merge_gate", "w_out", "ffn2_norm", "ffn2_w_gate", "ffn2_w_up", "ffn2_w_down", "ple_norm",
           "w_ple_proj", "w_ple_gate", "final_norm")


def kernel(x, p, ffn1_norm, ffn1_w_gate, ffn1_w_up, ffn1_w_down, mix_norm, w_in, fox_forget_bias, gla_gate_up, gla_gate_bias, gla_head_norm, w_branch_fox, w_branch_gla, w_merge_gate, b_merge_gate, w_out, ffn2_norm, ffn2_w_gate, ffn2_w_up, ffn2_w_down, ple_norm, w_ple_proj, w_ple_gate, final_norm, loss_target, m_ffn1_norm, m_ffn1_w_gate, m_ffn1_w_up, m_ffn1_w_down, m_mix_norm, m_w_in, m_fox_forget_bias, m_gla_gate_up, m_gla_gate_bias, m_gla_head_norm, m_w_branch_fox, m_w_branch_gla, m_w_merge_gate, m_b_merge_gate, m_w_out, m_ffn2_norm, m_ffn2_w_gate, m_ffn2_w_up, m_ffn2_w_down, m_ple_norm, m_w_ple_proj, m_w_ple_gate, m_final_norm, v_ffn1_norm, v_ffn1_w_gate, v_ffn1_w_up, v_ffn1_w_down, v_mix_norm, v_w_in, v_fox_forget_bias, v_gla_gate_up, v_gla_gate_bias, v_gla_head_norm, v_w_branch_fox, v_w_branch_gla, v_w_merge_gate, v_b_merge_gate, v_w_out, v_ffn2_norm, v_ffn2_w_gate, v_ffn2_w_up, v_ffn2_w_down, v_ple_norm, v_w_ple_proj, v_w_ple_gate, v_final_norm):
    args = dict(locals())
    wts = {n: args[n] for n in WEIGHTS}
    mom_m = {n: args["m_" + n] for n in WEIGHTS}
    mom_v = {n: args["v_" + n] for n in WEIGHTS}

    xs, ps, tgt = x[0], p[0, 0], loss_target[0]
    s, d = xs.shape
    fox_w = w_branch_fox.shape[1]
    gla_vw = w_branch_gla.shape[1]
    fox_heads = fox_w // HEAD_DIM
    gla_heads = gla_vw // GLA_VAL_DIM
    gla_kw = gla_heads * HEAD_DIM
    rank = gla_gate_up.shape[1]

    c_fl = 3 * fox_w
    o_gr, o_gq, o_gk = gla_vw, 2 * gla_vw, 2 * gla_vw + gla_kw
    o_fl = o_gk + gla_kw
    o_gd = o_fl + LANES
    rest_w = o_gd + LANES
    padded = c_fl + rest_w
    seg = [(c_fl, 0), (fox_heads, c_fl + o_fl), (gla_kw, c_fl + o_gq), (gla_kw, c_fl + o_gk), (gla_vw, c_fl),
           (gla_vw, c_fl + o_gr), (rank, c_fl + o_gd)]
    dest = np.concatenate([np.arange(w_, dtype=np.int32) + o_ for w_, o_ in seg])
    ws = w_in.shape[2]
    wp = -(-ws // LANES) * LANES
    tw = 256 if padded % 256 == 0 else LANES
    tbl, (t_idx, t_val, t_width), (s_idx, s_val, s_width) = _sel_tables(dest, ws, wp, tw)
    tbl = jnp.asarray(tbl)

    ov = _Overlap()
    for n in SHARDED:
        sh = wts[n][0].astype(BF16)
        ov.add(n, "gather", _pad_cols(sh, wp) if n == "w_in" else sh)
    fbias = _pad_cols(fox_forget_bias, LANES)
    bmg_f, bmg_g = b_merge_gate[:, :d], b_merge_gate[:, d:]
    ghn = jnp.tile(gla_head_norm, (1, gla_heads))

    h1, ffn1_saved = _ffn_fwd(xs, ffn1_norm, ov, tag="ffn1")
    u = _rms_fwd(h1, mix_norm, name="mix_norm")
    win = _win_unshard(ov.weight("w_in"), tbl, jnp.asarray(t_idx), jnp.asarray(t_val), t_width, tw=tw, padded=padded,
                       name="in_proj_unshard")
    win_fox, win_rest = win[:, :c_fl], win[:, c_fl:]
    zf = ov.mm(u, win_fox, mode="nn", name="in_proj_fox", out_dtype=BF16)
    zr = ov.mm(u, win_rest, mode="nn", name="in_proj_rest")
    gz = ov.mm(u, ov.weight("w_merge_gate"), mode="nn", name="merge_gate")
    gup = ov.weight("gla_gate_up").transpose(1, 0, 2).reshape(rank, gla_kw)
    gup = jnp.concatenate([gup, jnp.zeros((LANES - rank, gla_kw), BF16)], axis=0)

    log_f = _rowwise(lambda fl, b: _log_sigmoid(fl + b), [(zr, LANES, o_fl // LANES)], [fbias], [(LANES, F32)],
                     name="forget_gate")[0]
    f_cum = _cumsum(log_f[:, :fox_heads].T, reverse=False, name="forget_cumsum")
    f_rep = jnp.broadcast_to(f_cum.T[:, :, None], (s, fox_heads, HEAD_DIM)).reshape(s, fox_w)
    f_row = f_cum[:, None, :]
    attn_us = 2.0 * s * s * HEAD_DIM * fox_heads / _Overlap.MM_FLOPS_PER_US
    y_fox, y_fox_bf, lse = ov.carry(
        ATTN_FWD_MATMULS * attn_us, lambda comm: _fox_fwd(zf, f_rep, f_row, heads=fox_heads, name="fox_fwd", comm=comm))

    def decay_fn(gd, gupv, gb):
        return _log_sigmoid(_dot(gd, gupv, NN) + gb) * (1.0 / GLA_GATE_TAU)

    la = _rowwise(decay_fn, [(zr, LANES, o_gd // LANES)], [gup, gla_gate_bias], [(gla_kw, F32)], name="gla_decay", rc=128)[0]
    q_blk, k_blk = o_gq // HEAD_DIM, o_gk // HEAD_DIM
    o_gla, states = _gla_fwd(zr, la, heads=gla_heads, q_blk=q_blk, k_blk=k_blk, name="gla_fwd")

    def gla_out_fn(o, gr, g):
        outs = []
        for hh in range(gla_heads):
            sl = slice(hh * GLA_VAL_DIM, (hh + 1) * GLA_VAL_DIM)
            _, oh = _rms_parts(o[:, sl])
            outs.append(oh * g[:, sl] * _silu_parts(gr[:, sl])[0])
        return jnp.concatenate(outs, axis=1)

    y_gla = _rowwise(gla_out_fn, [o_gla, (zr, gla_vw, o_gr // gla_vw)], [ghn], [(gla_vw, BF16)], name="gla_out")[0]
    br_f = ov.mm(y_fox_bf, ov.weight("w_branch_fox"), mode="nn", name="branch_fox")
    br_g = ov.mm(y_gla, ov.weight("w_branch_gla"), mode="nn", name="branch_gla")

    def merge_fn(zf_, zg_, bf_, bg_, b1, b2):
        return _sigmoid(zf_ + b1) * bf_ + _sigmoid(zg_ + b2) * bg_

    merged = _rowwise(merge_fn, [(gz, d, 0), (gz, d, 1), br_f, br_g], [bmg_f, bmg_g], [(d, BF16)], name="merge")[0]
    h2 = ov.mm(merged, ov.weight("w_out"), mode="nn", name="out_proj", add=h1)
    h3, ffn2_saved = _ffn_fwd(h2, ffn2_norm, ov, tag="ffn2")
    n3 = _rms_fwd(h3, ple_norm, name="ple_norm")
    gl = ov.mm(n3, ov.weight("w_ple_gate"), mode="nn", name="ple_gate")
    pe = ov.mm(ps, ov.weight("w_ple_proj"), mode="nn", name="ple_proj")

    def head_fn(h3b, glb, peb, tb, gfin):
        pg = _sigmoid(glb)
        h4 = h3b + pg * peb
        r, xh = _rms_parts(h4)
        err = xh * gfin - tb
        dy = err * (1.0 / d)
        t = dy * gfin
        dh4 = r * (t - xh * jnp.mean(t * xh, axis=-1, keepdims=True))
        return dh4, dh4 * pg, dh4 * peb * pg * (1.0 - pg), _colsum(err * err), _colsum(dy * xh)

    dh4, dpe, dgl, loss_cols, d_final = _rowwise(
        head_fn, [h3, gl, pe, tgt], [final_norm.reshape(1, d)], [(d, F32), (d, BF16), (d, BF16)], [d, d], name="loss_head")
    loss = lax.psum(0.5 * jnp.sum(loss_cols) / d, AXES)

    grads = {"final_norm": d_final.reshape(d)}
    ov.grad("w_ple_proj", ov.mm(ps, dpe, mode="tn", name="ple_proj_dw", out_dtype=BF16, out_chunked=True))
    ov.grad("w_ple_gate", ov.mm(n3, dgl, mode="tn", name="ple_gate_dw", out_dtype=BF16))
    dn3 = ov.mm(dgl, ov.weight("w_ple_gate"), mode="nt", name="ple_gate_dx")
    dh3, dh3_bf, grads["ple_norm"] = _rms_bwd(h3, dn3, ple_norm, dh4, name="ple_norm_bwd")
    dh2, dh2_bf, grads["ffn2_norm"] = _ffn_bwd(h2, ffn2_norm, ov, ffn2_saved, dh3, dh3_bf, tag="ffn2")

    ov.grad("w_out", ov.mm(merged, dh2_bf, mode="tn", name="out_proj_dw", out_dtype=BF16))
    dmerged = ov.mm(dh2_bf, ov.weight("w_out"), mode="nt", name="out_proj_dx")

    def merge_bwd_fn(zf_, zg_, bf_, bg_, dm, b1, b2):
        sf, sg = _sigmoid(zf_ + b1), _sigmoid(zg_ + b2)
        dz = jnp.concatenate([dm * bf_ * sf * (1.0 - sf), dm * bg_ * sg * (1.0 - sg)], axis=1)
        return dm * sf, dm * sg, dz, _colsum(dz)

    dbr_f, dbr_g, dgz, grads["b_merge_gate"] = _rowwise(
        merge_bwd_fn, [(gz, d, 0), (gz, d, 1), br_f, br_g, dmerged], [bmg_f, bmg_g],
        [(d, BF16), (d, BF16), (2 * d, BF16)], [2 * d], name="merge_bwd")
    ov.grad("w_merge_gate", ov.mm(u, dgz, mode="tn", name="merge_gate_dw", out_dtype=BF16, out_chunked=True))
    ov.grad("w_branch_fox", ov.mm(y_fox_bf, dbr_f, mode="tn", name="branch_fox_dw", out_dtype=BF16, out_chunked=True))
    ov.grad("w_branch_gla", ov.mm(y_gla, dbr_g, mode="tn", name="branch_gla_dw", out_dtype=BF16, out_chunked=True))
    dy_fox = ov.mm(dbr_f, ov.weight("w_branch_fox"), mode="nt", name="branch_fox_dx")
    dy_gla = ov.mm(dbr_g, ov.weight("w_branch_gla"), mode="nt", name="branch_gla_dx")

    def gla_out_bwd_fn(o, gr, dy, g):
        dos, dgrs, dgs = [], [], []
        for hh in range(gla_heads):
            sl = slice(hh * GLA_VAL_DIM, (hh + 1) * GLA_VAL_DIM)
            r, oh = _rms_parts(o[:, sl])
            si, dsi = _silu_parts(gr[:, sl])
            don = dy[:, sl] * si
            dgrs.append(dy[:, sl] * oh * g[:, sl] * dsi)
            t = don * g[:, sl]
            dos.append(r * (t - oh * jnp.mean(t * oh, axis=-1, keepdims=True)))
            dgs.append(_colsum(don * oh))
        return jnp.concatenate(dos, axis=1), jnp.concatenate(dgrs, axis=1), jnp.concatenate(dgs, axis=1)

    do_gla, dgr, d_ghn = _rowwise(gla_out_bwd_fn, [o_gla, (zr, gla_vw, o_gr // gla_vw), dy_gla], [ghn],
                                  [(gla_vw, F32), (gla_vw, F32)], [gla_vw], name="gla_out_bwd")
    grads["gla_head_norm"] = d_ghn.reshape(gla_heads, GLA_VAL_DIM).sum(axis=0, keepdims=True)
    dgq, dgk, dgv, dla = _gla_bwd(zr, la, do_gla, states, heads=gla_heads, q_blk=q_blk, k_blk=k_blk, name="gla_bwd")

    def decay_bwd_fn(dl, gd, gupv, gb):
        pre = _dot(gd, gupv, NN) + gb
        dpre = dl * (1.0 / GLA_GATE_TAU) * _sigmoid(-pre)
        return _dot(dpre, gupv, NT), dpre, _colsum(dpre)

    dgd, dpre_bf, grads["gla_gate_bias"] = _rowwise(
        decay_bwd_fn, [dla, (zr, LANES, o_gd // LANES)], [gup, gla_gate_bias], [(LANES, F32), (gla_kw, BF16)], [gla_kw],
        name="gla_decay_bwd", rc=128)
    d_gup = ov.mm(zr[:, o_gd:o_gd + LANES], dpre_bf, mode="tn", name="gla_gate_up_dw")[:rank]
    ov.grad("gla_gate_up", d_gup.reshape(rank, N_DEV, gla_kw // N_DEV).transpose(1, 0, 2).astype(BF16))

    def delta_fn(dyv, yv):
        outs = []
        for hh in range(fox_heads):
            sl = slice(hh * HEAD_DIM, (hh + 1) * HEAD_DIM)
            outs.append(jnp.broadcast_to(jnp.sum(dyv[:, sl] * yv[:, sl], axis=-1, keepdims=True), (dyv.shape[0], HEAD_DIM)))
        return jnp.concatenate(outs, axis=1)

    delta = _rowwise(delta_fn, [dy_fox, y_fox], [], [(fox_w, F32)], name="fox_delta")[0]
    dfq, dfk, dfv, d_fcol, d_frow = ov.carry(
        ATTN_BWD_MATMULS * attn_us,
        lambda comm: _fox_bwd(zf, dy_fox, lse, delta, f_rep, f_row, heads=fox_heads, name="fox_bwd", comm=comm))
    d_fcum = d_fcol[:, ::HEAD_DIM].T + d_frow.reshape(fox_heads, s)
    d_logf = _cumsum(d_fcum, reverse=True, name="forget_cumsum_bwd")
    d_logf = _pad_cols(d_logf.T, LANES)

    def forget_bwd_fn(dl, fl, b):
        dfl_ = dl * _sigmoid(-(fl + b))
        return dfl_, _colsum(dfl_)

    dfl, d_fbias = _rowwise(forget_bwd_fn, [d_logf, (zr, LANES, o_fl // LANES)], [fbias], [(LANES, F32)], [LANES],
                            name="forget_gate_bwd")
    grads["fox_forget_bias"] = d_fbias[:, :fox_heads]

    dz = jnp.concatenate([dfq, dfk, dfv, dgv, dgr, dgq, dgk, dfl, dgd], axis=1).astype(BF16)
    dwin = ov.mm(u, dz, mode="tn", name="in_proj_dw", out_dtype=BF16)
    ov.grad("w_in", _win_to_shards(dwin, tbl, jnp.asarray(s_idx), jnp.asarray(s_val), s_width, tw=tw, wp=wp,
                                   name="in_proj_dw_shards"))
    du = ov.mm(dgz, ov.weight("w_merge_gate"), mode="nt", name="merge_gate_dx")
    du = ov.mm(dz, win, mode="nt", name="in_proj_dx", add=du)
    dh1, dh1_bf, grads["mix_norm"] = _rms_bwd(h1, du, mix_norm, dh2, name="mix_norm_bwd")
    dx, _, grads["ffn1_norm"] = _ffn_bwd(xs, ffn1_norm, ov, ffn1_saved, dh1, dh1_bf, tag="ffn1")

    outs = {}
    for n in SHARDED:
        parts = ov.get("d_" + n)
        state = [_pad_cols(t_[n][0], parts.shape[2]) for t_ in (wts, mom_m, mom_v)]
        res4 = _adamw(parts, *state, name=f"adamw_{n}")
        for kind, r_ in zip(("grad", "delta", "new_m", "new_v"), res4):
            outs[f"{kind}_{n}"] = r_[:, :wts[n].shape[2]][None]

    send_small, small_offs = _pack([grads[n] for n in REPLICATED], LANES, 8, F32)
    recv_small = _comm("bcast", send_small, name="exchange_replicated")
    w_sm, _ = _pack([wts[n] for n in REPLICATED], LANES, 8, F32)
    m_sm, _ = _pack([mom_m[n] for n in REPLICATED], LANES, 8, F32)
    v_sm, _ = _pack([mom_v[n] for n in REPLICATED], LANES, 8, F32)
    small = _adamw(recv_small, w_sm, m_sm, v_sm, name="adamw_replicated")
    for kind, buf in zip(("grad", "delta", "new_m", "new_v"), small):
        fs = buf.reshape(-1)
        for n, o in zip(REPLICATED, small_offs):
            outs[f"{kind}_{n}"] = fs[o:o + wts[n].size].reshape(wts[n].shape)

    res = [loss, dx[None]]
    for kind in ("grad", "delta", "new_m", "new_v"):
        res += [outs[f"{kind}_{n}"] for n in WEIGHTS]
    return tuple(res)
```

```python
import functools

import jax
import jax.numpy as jnp
import numpy as np
from jax import lax
from jax.experimental import pallas as pl
from jax.experimental.pallas import tpu as pltpu

F32 = jnp.float32
BF16 = jnp.bfloat16
MESH = pl.DeviceIdType.MESH
AXES = ("x", "y", "c")
N_DEV = 8

VMEM_LIMIT_BYTES = 56 * 1024 * 1024
MM_BLOCK_BUDGET_BYTES = 40 * 1024 * 1024
LANES = 128

EPS = 1e-6
HEAD_DIM = 128
GLA_VAL_DIM = 256
GLA_CHUNK = 64
GLA_GATE_TAU = 16.0
ADAM_LR, ADAM_B1, ADAM_B2, ADAM_EPS, ADAM_WD, ADAM_STEP = 0.001, 0.9, 0.999, 1e-08, 0.01, 10

ATTN_FWD_MATMULS = 7.0
ATTN_BWD_MATMULS = 7.0

HIGHEST = lax.Precision.HIGHEST
NN = (((1,), (0,)), ((), ()))
NT = (((1,), (1,)), ((), ()))
TN = (((0,), (0,)), ((), ()))


def _cparams(sem):
    return pltpu.CompilerParams(dimension_semantics=sem, vmem_limit_bytes=VMEM_LIMIT_BYTES)


def _pick(dim, cands):
    for c in cands:
        if dim % c == 0:
            return c
    return dim


def _bf(v):
    return v if v.dtype == BF16 else v.astype(BF16)


def _dot(a, b, dims):
    return lax.dot_general(_bf(a), _bf(b), dims, preferred_element_type=F32)


def _sigmoid(v):
    return 1.0 / (1.0 + jnp.exp(-v))


def _log_sigmoid(v):
    return jnp.minimum(v, 0.0) - jnp.log(1.0 + jnp.exp(-jnp.abs(v)))


def _logical(v):
    return (v.shape[1], v.shape[0] * v.shape[2]) if v.ndim == 3 else v.shape


def _mm(a, b, *, mode, name, out_dtype=F32, add=None, scale=1.0, out_chunked=False, comm=()):
    la, lb = _logical(a), _logical(b)
    if mode == "nn":
        (m, k), (k2, n) = la, lb
        a_minor, b_minor = "k", "n"
    elif mode == "nt":
        (m, k), (n, k2) = la, lb
        a_minor, b_minor = "k", "k"
    else:
        (k, m), (k2, n) = la, lb
        a_minor, b_minor = "m", "n"
    assert k == k2, (name, a.shape, b.shape)
    forced = {}
    for v, minor in ((a, a_minor), (b, b_minor)):
        if v.ndim == 3:
            assert forced.get(minor, v.shape[2]) == v.shape[2], name
            forced[minor] = v.shape[2]
    if out_chunked:
        assert forced.get("n", n // N_DEV) == n // N_DEV, name
        forced["n"] = n // N_DEV
    tm = forced.get("m") or _pick(m, (1024, 512, 256, 128))
    tn = forced.get("n") or _pick(n, (1408, 1280, 1024, 512, 256, 128))

    def blocks_bytes(tm_, tn_, t):
        io = tm_ * t * a.dtype.itemsize + t * tn_ * b.dtype.itemsize
        return 2 * (io + tm_ * tn_ * (jnp.dtype(out_dtype).itemsize + (4 if add is not None else 0))) + tm_ * tn_ * 4

    kc = forced.get("k")
    a_k_minor, b_k_minor = mode != "tn", mode == "nt"
    if kc:
        aligned = all(v.ndim == 3 or kc % (LANES if minor else 16) == 0 for v, minor in ((a, a_k_minor), (b, b_k_minor)))
        fits = [(tm_, tn_) for tm_, tn_ in ((tm, tn), (512, tn), (512, 512))
                if m % tm_ == 0 and n % tn_ == 0 and forced.get("m", tm_) == tm_ and forced.get("n", tn_) == tn_
                and blocks_bytes(tm_, tn_, k) <= MM_BLOCK_BUDGET_BYTES]
        if aligned and fits:
            (tm, tn), tk = fits[0], k
        else:
            tk, kc = kc, None
    else:
        tk = k if blocks_bytes(tm, tn, k) <= MM_BLOCK_BUDGET_BYTES else _pick(k, (512, 640, 256, 128))
    nk = k // tk
    dims = {"nn": NN, "nt": NT, "tn": TN}[mode]
    gi, gj, gk = (lambda i, j, kk: i), (lambda i, j, kk: j), (lambda i, j, kk: kk)

    def spec(v, t_major, t_minor, g_major, g_minor, all_chunks=False):
        if v.ndim == 3:
            if all_chunks:
                return pl.BlockSpec((N_DEV, t_major, v.shape[2]), lambda i, j, kk: (0, g_major(i, j, kk), 0))
            return pl.BlockSpec((None, t_major, v.shape[2]), lambda i, j, kk: (g_minor(i, j, kk), g_major(i, j, kk), 0))
        return pl.BlockSpec((t_major, t_minor), lambda i, j, kk: (g_major(i, j, kk), g_minor(i, j, kk)))

    a_spec = spec(a, tk, tm, gk, gi) if mode == "tn" else spec(a, tm, tk, gi, gk, bool(kc))
    b_spec = spec(b, tn, tk, gj, gk, bool(kc)) if mode == "nt" else spec(b, tk, tn, gk, gj)

    def k_chunk(ref, minor, c_):
        if len(ref.shape) == 3:
            return ref[c_]
        return ref[:, c_ * kc:(c_ + 1) * kc] if minor else ref[c_ * kc:(c_ + 1) * kc, :]
    if out_chunked:
        o_spec = pl.BlockSpec((None, tm, tn), lambda i, j, kk: (j, i, 0))
        out_shape = jax.ShapeDtypeStruct((N_DEV, m, tn), out_dtype)
    else:
        o_spec = pl.BlockSpec((tm, tn), lambda i, j, kk: (i, j))
        out_shape = jax.ShapeDtypeStruct((m, n), out_dtype)
    has_add = add is not None
    assert not (has_add and out_chunked), name

    def body(*refs):
        a_ref, b_ref = refs[0], refs[1]
        add_ref = refs[2] if has_add else None
        o_ref = refs[3 if has_add else 2]
        kk = pl.program_id(2)

        def finish(r):
            if scale != 1.0:
                r = r * scale
            if has_add:
                r = r + add_ref[...]
            o_ref[...] = r.astype(o_ref.dtype)

        if kc:
            r = _dot(k_chunk(a_ref, a_k_minor, 0), k_chunk(b_ref, b_k_minor, 0), dims)
            for c_ in range(1, k // kc):
                r = r + _dot(k_chunk(a_ref, a_k_minor, c_), k_chunk(b_ref, b_k_minor, c_), dims)
            finish(r)
        elif nk == 1:
            finish(_dot(a_ref[...], b_ref[...], dims))
        else:
            acc_ref = refs[-1]

            @pl.when(kk == 0)
            def _():
                acc_ref[...] = jnp.zeros_like(acc_ref)

            acc_ref[...] += _dot(a_ref[...], b_ref[...], dims)

            @pl.when(kk == nk - 1)
            def _():
                finish(acc_ref[...])

    res, carried = _call(
        body, name=name, grid=(m // tm, n // tn, nk),
        in_specs=[a_spec, b_spec] + ([o_spec] if has_add else []), out_specs=[o_spec], out_shape=[out_shape],
        scratch_shapes=[pltpu.VMEM((tm, tn), F32)] if nk > 1 else [],
        semantics=("parallel", "parallel", "arbitrary"), operands=[a, b] + ([add] if has_add else []), comm=comm)
    return (res[0], carried) if comm else res[0]


def _call(body, *, name, grid, in_specs, out_specs, out_shape, scratch_shapes, semantics, operands, comm=()):
    n_in, n_out, n_scr, n = len(in_specs), len(out_specs), len(scratch_shapes), len(comm)
    if not comm:
        res = pl.pallas_call(body, name=name, grid=grid, in_specs=in_specs, out_specs=out_specs, out_shape=out_shape,
                             scratch_shapes=scratch_shapes, compiler_params=_cparams(semantics))(*operands)
        return res, []

    def carrying(*refs):
        ins, c_in = refs[:n_in], refs[n_in:n_in + n]
        outs, c_out = refs[n_in + n:n_in + n + n_out], refs[n_in + n + n_out:n_in + 2 * n + n_out]
        scratch, sems = refs[n_in + 2 * n + n_out:][:n_scr], refs[n_in + 2 * n + n_out + n_scr:]
        tasks = [_comm_ops(kind, c_in[t], c_out[t], *sems[3 * t:3 * t + 3]) for t, (kind, _) in enumerate(comm)]
        ids = [pl.program_id(ax) for ax in range(len(grid))]

        @pl.when(functools.reduce(lambda p, q: p & q, [i == 0 for i in ids]))
        def _():
            for start, _ in tasks:
                start()

        body(*ins, *outs, *scratch)

        @pl.when(functools.reduce(lambda p, q: p & q, [i == g - 1 for i, g in zip(ids, grid)]))
        def _():
            for _, finish in tasks:
                finish()

    any_spec = pl.BlockSpec(memory_space=pl.ANY)
    res = pl.pallas_call(
        carrying, name=name, grid=grid,
        in_specs=list(in_specs) + [any_spec] * n, out_specs=list(out_specs) + [any_spec] * n,
        out_shape=list(out_shape) + [_comm_out_shape(kind, v) for kind, v in comm],
        scratch_shapes=list(scratch_shapes) + _comm_scratch(n),
        compiler_params=_cparams(("arbitrary",) * len(grid)),
    )(*operands, *[v for _, v in comm])
    return res[:n_out], res[n_out:]


def _rowwise(fn, rows, consts, outs, accs=(), *, name, rc=None):
    rows = [r if isinstance(r, tuple) else (r, r.shape[1], 0) for r in rows]
    m = rows[0][0].shape[0]
    widths = [w for _, w, _ in rows] + [n for n, _ in outs]
    row_bytes = sum(w * r.dtype.itemsize for r, w, _ in rows) + sum(n * jnp.dtype(d).itemsize for n, d in outs)
    tm = 1024
    while tm > 16 and (m % tm or 2 * tm * row_bytes > 24 * 1024 * 1024):
        tm //= 2
    if m % tm:
        tm = m
    if rc is None:
        rc = 16
        while rc * 2 <= tm and rc * 2 * max(widths) <= 32768:
            rc *= 2
    rc = min(rc, tm)
    nr, nc, no = len(rows), len(consts), len(outs)

    def body(*refs):
        in_refs, c_refs = refs[:nr], refs[nr:nr + nc]
        o_refs, a_refs = refs[nr + nc:nr + nc + no], refs[nr + nc + no:]

        @pl.when(pl.program_id(0) == 0)
        def _():
            for r in a_refs:
                r[...] = jnp.zeros_like(r)

        cvals = [c[...] for c in c_refs]

        def chunk(ci, carry):
            sl = pl.ds(pl.multiple_of(ci * rc, rc), rc)
            res = fn(*[r[sl, :] for r in in_refs], *cvals)
            if not isinstance(res, (tuple, list)):
                res = (res,)
            for r, v in zip(o_refs, res[:no]):
                r[sl, :] = v.astype(r.dtype)
            for r, v in zip(a_refs, res[no:]):
                r[...] += v
            return carry

        lax.fori_loop(0, tm // rc, chunk, 0)

    in_specs = [pl.BlockSpec((tm, w), functools.partial(lambda i, cb: (i, cb), cb=cb)) for _, w, cb in rows]
    in_specs += [pl.BlockSpec(c.shape, lambda i: (0, 0)) for c in consts]
    out_specs = [pl.BlockSpec((tm, n), lambda i: (i, 0)) for n, _ in outs]
    out_specs += [pl.BlockSpec((1, n), lambda i: (0, 0)) for n in accs]
    out_shape = [jax.ShapeDtypeStruct((m, n), d) for n, d in outs] + [jax.ShapeDtypeStruct((1, n), F32) for n in accs]
    res = pl.pallas_call(
        body, name=name, grid=(m // tm,),
        in_specs=in_specs, out_specs=out_specs, out_shape=out_shape,
        compiler_params=_cparams(("arbitrary",)),
    )(*[r for r, _, _ in rows], *consts)
    return res


def _colsum(v):
    return jnp.sum(v, axis=0, keepdims=True)


def _rms_parts(xv):
    r = lax.rsqrt(jnp.mean(xv * xv, axis=-1, keepdims=True) + EPS)
    return r, xv * r


def _rms_fwd(xv, g, *, name):
    d = xv.shape[1]

    def fn(xb, gb):
        _, xh = _rms_parts(xb)
        return xh * gb

    return _rowwise(fn, [xv], [g], [(d, BF16)], name=name)[0]


def _rms_bwd(xv, dy, g, add, *, name):
    d = xv.shape[1]

    def fn(xb, dyb, addb, gb):
        r, xh = _rms_parts(xb)
        t = dyb * gb
        dx = r * (t - xh * jnp.mean(t * xh, axis=-1, keepdims=True)) + addb
        return dx, dx, _colsum(dyb * xh)

    return _rowwise(fn, [xv, dy, add], [g], [(d, F32), (d, BF16)], [d], name=name)


def _silu_parts(a):
    sg = _sigmoid(a)
    return a * sg, sg * (1.0 + a * (1.0 - sg))


def _rows(v):
    return v.reshape(v.shape[0] * v.shape[1], v.shape[2])


def _ffn_fwd(h, g, ov, *, tag):
    s = h.shape[0]
    n = _rms_fwd(h, g, name=f"{tag}_norm")
    a = ov.mm(n, ov.weight(f"{tag}_w_gate"), mode="nn", name=f"{tag}_gate", out_chunked=True)
    b = ov.mm(n, ov.weight(f"{tag}_w_up"), mode="nn", name=f"{tag}_up", out_chunked=True)
    c = a.shape[2]
    hm = _rowwise(lambda av, bv: _silu_parts(av)[0] * bv, [_rows(a), _rows(b)], [], [(c, BF16)], name=f"{tag}_act")[0]
    hm = hm.reshape(N_DEV, s, c)
    out = ov.mm(hm, ov.weight(f"{tag}_w_down"), mode="nn", name=f"{tag}_down", add=h, scale=0.5)
    return out, (n, a, b, hm)


def _ffn_bwd(h, g, ov, saved, dout, dout_bf, *, tag):
    n, a, b, hm = saved
    wg, wu, wd = ov.weight(f"{tag}_w_gate"), ov.weight(f"{tag}_w_up"), ov.weight(f"{tag}_w_down")
    s, c = h.shape[0], wg.shape[2]
    d_wd = ov.mm(hm, dout_bf, mode="tn", name=f"{tag}_down_dw", scale=0.5, out_dtype=BF16)
    ov.grad(f"{tag}_w_down", d_wd)
    dhm = ov.mm(dout_bf, wd, mode="nt", name=f"{tag}_down_dx", scale=0.5, out_chunked=True)

    def act_bwd(av, bv, dv):
        si, dsi = _silu_parts(av)
        return dv * bv * dsi, dv * si

    da, db = _rowwise(act_bwd, [_rows(a), _rows(b), _rows(dhm)], [], [(c, BF16), (c, BF16)], name=f"{tag}_act_bwd")
    da, db = da.reshape(N_DEV, s, c), db.reshape(N_DEV, s, c)
    ov.grad(f"{tag}_w_gate", ov.mm(n, da, mode="tn", name=f"{tag}_gate_dw", out_dtype=BF16, out_chunked=True))
    ov.grad(f"{tag}_w_up", ov.mm(n, db, mode="tn", name=f"{tag}_up_dw", out_dtype=BF16, out_chunked=True))
    dn = ov.mm(da, wg, mode="nt", name=f"{tag}_gate_dx")
    dn = ov.mm(db, wu, mode="nt", name=f"{tag}_up_dx", add=dn)
    dh, dh_bf, dg = _rms_bwd(h, dn, g, dout, name=f"{tag}_norm_bwd")
    return dh, dh_bf, dg


def _cumsum(xv, *, reverse, name):
    h, s = xv.shape
    t = _pick(s, (512, 256, 128))
    nb = s // t

    def blk(j):
        return (0, nb - 1 - j) if reverse else (0, j)

    def body(x_ref, o_ref, carry):
        @pl.when(pl.program_id(0) == 0)
        def _():
            carry[...] = jnp.zeros_like(carry)

        i0 = lax.broadcasted_iota(jnp.int32, (t, t), 0)
        i1 = lax.broadcasted_iota(jnp.int32, (t, t), 1)
        tri = ((i0 >= i1) if reverse else (i0 <= i1)).astype(F32)
        xb = x_ref[...]
        o_ref[...] = jnp.dot(xb, tri, precision=HIGHEST, preferred_element_type=F32) + carry[...]
        carry[...] += jnp.sum(xb, axis=1, keepdims=True)

    return pl.pallas_call(
        body, name=name, grid=(nb,),
        in_specs=[pl.BlockSpec((h, t), blk)], out_specs=pl.BlockSpec((h, t), blk),
        out_shape=jax.ShapeDtypeStruct((h, s), F32),
        scratch_shapes=[pltpu.VMEM((h, 1), F32)],
        compiler_params=_cparams(("arbitrary",)),
    )(xv)


def _fox_tiles(s):
    t = _pick(s, (512, 256, 128))
    return t, t


def _causal(sc):
    t = sc.shape[0]
    keep = lax.broadcasted_iota(jnp.int32, (t, t), 1) <= lax.broadcasted_iota(jnp.int32, (t, t), 0)
    return jnp.where(keep, sc, -jnp.inf)


ATTN_HEADS_PER_STEP = 2


def _head_cols(hh):
    return slice(hh * HEAD_DIM, (hh + 1) * HEAD_DIM)


def _across(rowstat, width):
    return jnp.tile(rowstat, (1, width // HEAD_DIM))


def _fox_fwd(zf, f_rep, f_row, *, heads, name, comm=()):
    s = zf.shape[0]
    tq, tk = _fox_tiles(s)
    assert tq == tk
    nq, nk = s // tq, s // tk
    hp = ATTN_HEADS_PER_STEP if heads % ATTN_HEADS_PER_STEP == 0 else 1
    wb = hp * HEAD_DIM
    scale = HEAD_DIM ** -0.5
    w = heads * HEAD_DIM

    def body(q_ref, k_ref, v_ref, fq_ref, fk_ref, o32_ref, o16_ref, lse_ref, m_sc, l_sc, acc_sc):
        i, j = pl.program_id(1), pl.program_id(2)

        @pl.when(j == 0)
        def _():
            m_sc[...] = jnp.full_like(m_sc, -jnp.inf)
            l_sc[...] = jnp.zeros_like(l_sc)
            acc_sc[...] = jnp.zeros_like(acc_sc)

        def step(diagonal):
            for hh in range(hp):
                cols = _head_cols(hh)
                sc = _dot(q_ref[:, cols], k_ref[:, cols], NT) * scale + _across(fq_ref[:, cols], tk) - fk_ref[hh]
                if diagonal:
                    sc = _causal(sc)
                m_old = m_sc[hh]
                m_new = jnp.maximum(m_old, jnp.max(sc, axis=-1, keepdims=True))
                alpha = jnp.exp(m_old - m_new)
                pr = jnp.exp(sc - _across(m_new, tk))
                l_sc[hh] = alpha * l_sc[hh] + jnp.sum(pr, axis=-1, keepdims=True)
                acc_sc[hh] = alpha * acc_sc[hh] + _dot(pr, v_ref[:, cols], NN)
                m_sc[hh] = m_new

        @pl.when(j < i)
        def _():
            step(False)

        @pl.when(j == i)
        def _():
            step(True)

        @pl.when(j == nk - 1)
        def _():
            for hh in range(hp):
                cols = _head_cols(hh)
                o = acc_sc[hh] / l_sc[hh]
                o32_ref[:, cols] = o
                o16_ref[:, cols] = o.astype(BF16)
                lse_ref[:, cols] = m_sc[hh] + jnp.log(l_sc[hh])

    def kv_blk(off):
        return lambda h, i, j: (jnp.minimum(j, i), off + h)

    o_spec = pl.BlockSpec((tq, wb), lambda h, i, j: (i, h))
    stat = pltpu.VMEM((hp, tq, HEAD_DIM), F32)
    return _call(
        body, name=name, grid=(heads // hp, nq, nk),
        in_specs=[
            o_spec,
            pl.BlockSpec((tk, wb), kv_blk(heads // hp)),
            pl.BlockSpec((tk, wb), kv_blk(2 * heads // hp)),
            o_spec,
            pl.BlockSpec((hp, 1, tk), lambda h, i, j: (h, 0, jnp.minimum(j, i))),
        ],
        out_specs=[o_spec, o_spec, o_spec],
        out_shape=[jax.ShapeDtypeStruct((s, w), F32), jax.ShapeDtypeStruct((s, w), BF16),
                   jax.ShapeDtypeStruct((s, w), F32)],
        scratch_shapes=[stat, stat, stat],
        semantics=("parallel", "parallel", "arbitrary"), operands=[zf, zf, zf, f_rep, f_row], comm=comm)


def _fox_bwd(zf, do, lse, delta, f_rep, f_row, *, heads, name, comm=()):
    s = zf.shape[0]
    tq, tk = _fox_tiles(s)
    assert tq == tk
    nq, nk = s // tq, s // tk
    hp = ATTN_HEADS_PER_STEP if heads % ATTN_HEADS_PER_STEP == 0 else 1
    wb = hp * HEAD_DIM
    scale = HEAD_DIM ** -0.5
    w = heads * HEAD_DIM

    def body(q_ref, k_ref, v_ref, do_ref, lse_ref, dl_ref, fq_ref, fk_ref, dq_ref, dk_ref, dv_ref, dfq_ref, dfk_ref):
        j, i = pl.program_id(1), pl.program_id(2)

        @pl.when((j == 0) & (i == 0))
        def _():
            dq_ref[...] = jnp.zeros_like(dq_ref)
            dfq_ref[...] = jnp.zeros_like(dfq_ref)

        @pl.when(i == 0)
        def _():
            dk_ref[...] = jnp.zeros_like(dk_ref)
            dv_ref[...] = jnp.zeros_like(dv_ref)
            dfk_ref[...] = jnp.zeros_like(dfk_ref)

        def step(diagonal):
            rows = pl.ds(pl.multiple_of(i * tq, tq), tq)
            for hh in range(hp):
                cols = _head_cols(hh)
                q, k, v = q_ref[:, cols], k_ref[:, cols], v_ref[:, cols]
                dob = do_ref[:, cols].astype(BF16)
                sc = _dot(q, k, NT) * scale + _across(fq_ref[:, cols], tk) - fk_ref[hh]
                if diagonal:
                    sc = _causal(sc)
                pr = jnp.exp(sc - _across(lse_ref[:, cols], tk))
                dv_ref[:, cols] += _dot(pr, dob, TN)
                dp = _dot(dob, v, NT)
                ds = pr * (dp - _across(dl_ref[:, cols], tk))
                dsb = ds.astype(BF16)
                dk_ref[:, cols] += _dot(dsb, q, TN) * scale
                dq_ref[rows, cols] += _dot(dsb, k, NN) * scale
                dfq_ref[rows, cols] += jnp.broadcast_to(jnp.sum(ds, axis=1, keepdims=True), (tq, HEAD_DIM))
                dfk_ref[hh] -= jnp.sum(ds, axis=0, keepdims=True)

        @pl.when(i > j)
        def _():
            step(False)

        @pl.when(i == j)
        def _():
            step(True)

    q_spec = pl.BlockSpec((tq, wb), lambda h, j, i: (jnp.maximum(i, j), h))
    k_spec = pl.BlockSpec((tk, wb), lambda h, j, i: (j, h))
    whole = pl.BlockSpec((s, wb), lambda h, j, i: (0, h))
    row_spec = pl.BlockSpec((hp, 1, tk), lambda h, j, i: (h, 0, j))
    return _call(
        body, name=name, grid=(heads // hp, nk, nq),
        in_specs=[
            q_spec,
            pl.BlockSpec((tk, wb), lambda h, j, i: (j, heads // hp + h)),
            pl.BlockSpec((tk, wb), lambda h, j, i: (j, 2 * heads // hp + h)),
            q_spec, q_spec, q_spec, q_spec, row_spec,
        ],
        out_specs=[whole, k_spec, k_spec, whole, row_spec],
        out_shape=[jax.ShapeDtypeStruct((s, w), F32), jax.ShapeDtypeStruct((s, w), F32),
                   jax.ShapeDtypeStruct((s, w), F32), jax.ShapeDtypeStruct((s, w), F32),
                   jax.ShapeDtypeStruct((heads, 1, s), F32)],
        scratch_shapes=[], semantics=("parallel", "arbitrary", "arbitrary"),
        operands=[zf, zf, zf, do, lse, delta, f_rep, f_row], comm=comm)


def _gla_rows(s):
    return _pick(s, (256, 128, 64))


def _gla_chunk_terms(la_c, tri):
    a_cum = jnp.dot(tri, la_c, precision=HIGHEST, preferred_element_type=F32)
    a_tot = jnp.sum(la_c, axis=0, keepdims=True)
    return jnp.exp(a_tot - a_cum), jnp.exp(a_tot)


def _gla_fwd(zr, la, *, heads, q_blk, k_blk, name):
    s = zr.shape[0]
    c = GLA_CHUNK
    rows = _gla_rows(s)
    nsteps, ncs = s // rows, rows // c
    scale = HEAD_DIM ** -0.5

    def body(q_ref, k_ref, v_ref, la_ref, o_ref, st_ref, state):
        @pl.when(pl.program_id(1) == 0)
        def _():
            state[...] = jnp.zeros_like(state)

        tri = (lax.broadcasted_iota(jnp.int32, (c, c), 0) >= lax.broadcasted_iota(jnp.int32, (c, c), 1)).astype(F32)
        for t in range(ncs):
            sl = slice(t * c, (t + 1) * c)
            dec, e_tot = _gla_chunk_terms(la_ref[sl, :], tri)
            kd = k_ref[sl, :] * dec
            st_ref[t] = state[...]
            new = state[...] * e_tot + _dot(v_ref[sl, :], kd, TN)
            state[...] = new
            o_ref[sl, :] = _dot(q_ref[sl, :] * scale, new, NT)

    return pl.pallas_call(
        body, name=name, grid=(heads, nsteps),
        in_specs=[
            pl.BlockSpec((rows, HEAD_DIM), lambda h, i: (i, q_blk + h)),
            pl.BlockSpec((rows, HEAD_DIM), lambda h, i: (i, k_blk + h)),
            pl.BlockSpec((rows, GLA_VAL_DIM), lambda h, i: (i, h)),
            pl.BlockSpec((rows, HEAD_DIM), lambda h, i: (i, h)),
        ],
        out_specs=[
            pl.BlockSpec((rows, GLA_VAL_DIM), lambda h, i: (i, h)),
            pl.BlockSpec((None, ncs, GLA_VAL_DIM, HEAD_DIM), lambda h, i: (h, i, 0, 0)),
        ],
        out_shape=[jax.ShapeDtypeStruct((s, heads * GLA_VAL_DIM), F32),
                   jax.ShapeDtypeStruct((heads, s // c, GLA_VAL_DIM, HEAD_DIM), F32)],
        scratch_shapes=[pltpu.VMEM((GLA_VAL_DIM, HEAD_DIM), F32)],
        compiler_params=_cparams(("parallel", "arbitrary")),
    )(zr, zr, zr, la)


def _gla_bwd(zr, la, do, states, *, heads, q_blk, k_blk, name):
    s = zr.shape[0]
    c = GLA_CHUNK
    rows = _gla_rows(s)
    nsteps, ncs = s // rows, rows // c
    scale = HEAD_DIM ** -0.5

    def body(q_ref, k_ref, v_ref, la_ref, do_ref, st_ref, dq_ref, dk_ref, dv_ref, dla_ref, dstate):
        @pl.when(pl.program_id(1) == 0)
        def _():
            dstate[...] = jnp.zeros_like(dstate)

        i0 = lax.broadcasted_iota(jnp.int32, (c, c), 0)
        i1 = lax.broadcasted_iota(jnp.int32, (c, c), 1)
        tri = (i0 >= i1).astype(F32)
        strict = (i0 > i1).astype(F32)
        for t in reversed(range(ncs)):
            sl = slice(t * c, (t + 1) * c)
            dec, e_tot = _gla_chunk_terms(la_ref[sl, :], tri)
            kd = k_ref[sl, :] * dec
            kdb = kd.astype(BF16)
            vb = v_ref[sl, :].astype(BF16)
            dob = do_ref[sl, :].astype(BF16)
            prev = st_ref[t]
            cur = prev * e_tot + _dot(vb, kdb, TN)
            d_cur = dstate[...] + _dot(dob, q_ref[sl, :] * scale, TN)
            d_cur_b = d_cur.astype(BF16)
            dq_ref[sl, :] = _dot(dob, cur, NN) * scale
            dv_ref[sl, :] = _dot(kdb, d_cur_b, NT)
            dkd = _dot(vb, d_cur_b, NN)
            d_tot = e_tot * jnp.sum(d_cur * prev, axis=0, keepdims=True)
            dk_ref[sl, :] = dkd * dec
            dla_ref[sl, :] = d_tot + jnp.dot(strict, dkd * kd, precision=HIGHEST, preferred_element_type=F32)
            dstate[...] = d_cur * e_tot

    def rev(i):
        return nsteps - 1 - i

    kq_spec = pl.BlockSpec((rows, HEAD_DIM), lambda h, i: (rev(i), h))
    v_spec = pl.BlockSpec((rows, GLA_VAL_DIM), lambda h, i: (rev(i), h))
    return pl.pallas_call(
        body, name=name, grid=(heads, nsteps),
        in_specs=[
            pl.BlockSpec((rows, HEAD_DIM), lambda h, i: (rev(i), q_blk + h)),
            pl.BlockSpec((rows, HEAD_DIM), lambda h, i: (rev(i), k_blk + h)),
            v_spec, kq_spec, v_spec,
            pl.BlockSpec((None, ncs, GLA_VAL_DIM, HEAD_DIM), lambda h, i: (h, rev(i), 0, 0)),
        ],
        out_specs=[kq_spec, kq_spec, v_spec, kq_spec],
        out_shape=[jax.ShapeDtypeStruct((s, heads * HEAD_DIM), F32), jax.ShapeDtypeStruct((s, heads * HEAD_DIM), F32),
                   jax.ShapeDtypeStruct((s, heads * GLA_VAL_DIM), F32), jax.ShapeDtypeStruct((s, heads * HEAD_DIM), F32)],
        scratch_shapes=[pltpu.VMEM((GLA_VAL_DIM, HEAD_DIM), F32)],
        compiler_params=_cparams(("parallel", "arbitrary")),
    )(zr, zr, zr, la, do, states)


def _my_place():
    x, y, c = lax.axis_index("x"), lax.axis_index("y"), lax.axis_index("c")
    return x, y, c


def _gather_ops(x_ref, out_ref, send_sems, recv_sems, local_sem):
    def plan():
        x, y, c = _my_place()
        me, sibling = (x, y, c), (x, y, 1 - c)
        chips = [(1 - x, y), (x, 1 - y), (1 - x, 1 - y)]

        def blk(px, py, pc):
            return out_ref.at[4 * px + 2 * py + pc]

        def copy(k, block, to, src=None):
            return pltpu.make_async_remote_copy(
                src_ref=blk(*block) if src is None else src, dst_ref=blk(*block),
                send_sem=send_sems.at[k], recv_sem=recv_sems.at[k], device_id=to, device_id_type=MESH)

        mine = pltpu.make_async_copy(x_ref, blk(*me), local_sem)
        first = [copy(0, me, sibling, src=x_ref)]
        first += [copy(1 + j, me, (*chip, c), src=x_ref) for j, chip in enumerate(chips)]
        passed = [copy(4 + j, (*chip, c), sibling) for j, chip in enumerate(chips)]
        landed = [copy(1 + j, (*chip, c), me) for j, chip in enumerate(chips)]
        from_sibling = [copy(0, sibling, me)] + [copy(4 + j, (*chip, 1 - c), me) for j, chip in enumerate(chips)]
        return mine, first, passed, landed, from_sibling

    def start():
        mine, first, _, _, _ = plan()
        mine.start()
        for cp in first:
            cp.start()

    def finish():
        mine, first, passed, landed, from_sibling = plan()
        for cp, fwd in zip(landed, passed):
            cp.wait_recv()
            fwd.start()
        for cp in from_sibling:
            cp.wait_recv()
        for cp in first + passed:
            cp.wait_send()
        mine.wait()

    return start, finish


def _exchange_ops(scatter, s_ref, r_ref, send_sems, recv_sems, local_sem):
    def plan():
        x, y, c = _my_place()
        me = 4 * x + 2 * y + c
        mine = pltpu.make_async_copy(s_ref.at[me] if scatter else s_ref, r_ref.at[me], local_sem)
        sends, recvs = [], []
        for k in range(1, N_DEV):
            px, py, pc = x ^ ((k >> 2) & 1), y ^ ((k >> 1) & 1), c ^ (k & 1)
            peer = 4 * px + 2 * py + pc
            src = s_ref.at[peer] if scatter else s_ref
            for dst, out in ((r_ref.at[me], sends), (r_ref.at[peer], recvs)):
                out.append(pltpu.make_async_remote_copy(
                    src_ref=src, dst_ref=dst, send_sem=send_sems.at[k - 1], recv_sem=recv_sems.at[k - 1],
                    device_id=(px, py, pc), device_id_type=MESH))
        return mine, sends, recvs

    def start():
        mine, sends, _ = plan()
        mine.start()
        for cp in sends:
            cp.start()

    def finish():
        mine, sends, recvs = plan()
        for cp in recvs:
            cp.wait_recv()
        for cp in sends:
            cp.wait_send()
        mine.wait()

    return start, finish


def _pair_ops(s_ref, r_ref, send_sems, recv_sems):
    def plan():
        x, y, c = _my_place()
        return [pltpu.make_async_remote_copy(
            src_ref=s_ref.at[2 * q + 1 - c], dst_ref=r_ref.at[q], send_sem=send_sems.at[q], recv_sem=recv_sems.at[q],
            device_id=(x, y, 1 - c), device_id_type=MESH) for q in range(N_DEV // 2)]

    def start():
        for cp in plan():
            cp.start()

    def finish():
        copies = plan()
        for cp in copies:
            cp.wait_recv()
        for cp in copies:
            cp.wait_send()

    return start, finish


def _chips_ops(p_ref, r_ref, send_sems, recv_sems, local_sem):
    def plan():
        x, y, c = _my_place()
        chip = 2 * x + y
        mine = pltpu.make_async_copy(p_ref.at[chip], r_ref.at[chip], local_sem)
        sends, recvs = [], []
        for k in range(1, N_DEV // 2):
            px, py = x ^ (k >> 1), y ^ (k & 1)
            peer = 2 * px + py
            for dst, out in ((r_ref.at[chip], sends), (r_ref.at[peer], recvs)):
                out.append(pltpu.make_async_remote_copy(
                    src_ref=p_ref.at[peer], dst_ref=dst, send_sem=send_sems.at[k - 1], recv_sem=recv_sems.at[k - 1],
                    device_id=(px, py, c), device_id_type=MESH))
        return mine, sends, recvs

    def start():
        mine, sends, _ = plan()
        mine.start()
        for cp in sends:
            cp.start()

    def finish():
        mine, sends, recvs = plan()
        for cp in recvs:
            cp.wait_recv()
        for cp in sends:
            cp.wait_send()
        mine.wait()

    return start, finish


def _comm_ops(kind, src_ref, dst_ref, send_sems, recv_sems, local_sem):
    if kind == "gather":
        return _gather_ops(src_ref, dst_ref, send_sems, recv_sems, local_sem)
    if kind == "pair":
        return _pair_ops(src_ref, dst_ref, send_sems, recv_sems)
    if kind == "chips":
        return _chips_ops(src_ref, dst_ref, send_sems, recv_sems, local_sem)
    return _exchange_ops(False, src_ref, dst_ref, send_sems, recv_sems, local_sem)


def _comm_out_shape(kind, v):
    shape = {"pair": (N_DEV // 2,) + v.shape[1:], "chips": v.shape}.get(kind, (N_DEV,) + v.shape)
    return jax.ShapeDtypeStruct(shape, v.dtype)


def _comm_scratch(n_tasks):
    return [pltpu.SemaphoreType.DMA((N_DEV - 1,)), pltpu.SemaphoreType.DMA((N_DEV - 1,)), pltpu.SemaphoreType.DMA] * n_tasks


def _pair_sum(s, r, *, name):
    _, rows, cdim = s.shape
    tr = _pick(rows, (1024, 704, 512, 256, 128, 64, 32, 16, 8))
    rc = min(16, tr)

    def body(c_ref, s_ref, r_ref, o_ref):
        def chunk(ci, carry):
            sl = pl.ds(pl.multiple_of(ci * rc, rc), rc)
            o_ref[sl, :] = (s_ref[sl, :].astype(F32) + r_ref[sl, :].astype(F32)).astype(o_ref.dtype)
            return carry

        lax.fori_loop(0, tr // rc, chunk, 0)

    spec = pl.BlockSpec((None, tr, cdim), lambda q, i, c_ref: (q, i, 0))
    return pl.pallas_call(
        body, name=name,
        grid_spec=pltpu.PrefetchScalarGridSpec(
            num_scalar_prefetch=1, grid=(N_DEV // 2, rows // tr),
            in_specs=[pl.BlockSpec((None, None, tr, cdim), lambda q, i, c_ref: (q, c_ref[0], i, 0)), spec],
            out_specs=spec),
        out_shape=jax.ShapeDtypeStruct(r.shape, s.dtype),
        compiler_params=_cparams(("parallel", "parallel")),
    )(lax.axis_index("c").astype(jnp.int32).reshape(1), s.reshape(N_DEV // 2, 2, rows, cdim), r)


def _comm(kind, v, *, name):
    def body(s_ref, r_ref, send_sems, recv_sems, local_sem):
        start, finish = _comm_ops(kind, s_ref, r_ref, send_sems, recv_sems, local_sem)
        start()
        finish()

    return pl.pallas_call(
        body, name=name,
        out_shape=_comm_out_shape(kind, v),
        in_specs=[pl.BlockSpec(memory_space=pl.ANY)],
        out_specs=pl.BlockSpec(memory_space=pl.ANY),
        scratch_shapes=_comm_scratch(1),
    )(v)


class _Overlap:
    US_PER_MB = {"gather": 52.0, "pair": 1.0, "chips": 13.0, "bcast": 97.0}
    MM_FLOPS_PER_US = 6.0e8

    def __init__(self):
        self.queue, self.results, self.then = [], {}, {}

    def add(self, key, kind, v):
        self.queue.append((key, kind, v))

    def _cost(self, kind, v):
        return v.size * v.dtype.itemsize / 2 ** 20 * self.US_PER_MB[kind]

    def take(self, budget_us):
        taken, cum = [], 0.0
        while self.queue:
            cost = self._cost(*self.queue[0][1:])
            if taken and cum + cost > 1.25 * budget_us:
                break
            taken.append(self.queue.pop(0))
            cum += cost
        return taken

    def put(self, taken, res):
        for (key, _, _), r in zip(taken, res):
            self.results[key] = r
            if key in self.then:
                self.then.pop(key)(r)

    def carry(self, budget_us, fn):
        taken = self.take(budget_us)
        out, res = fn([(kind, v) for _, kind, v in taken])
        self.put(taken, res)
        return out

    def mm(self, a, b, *, mode, **kw):
        la, lb = _logical(a), _logical(b)
        budget = 2.0 * la[0] * la[1] * (lb[0] if mode == "nt" else lb[1]) / self.MM_FLOPS_PER_US

        def fn(comm):
            return _mm(a, b, mode=mode, comm=comm, **kw) if comm else (_mm(a, b, mode=mode, **kw), [])

        return self.carry(budget, fn)

    def get(self, key):
        while key not in self.results:
            keys = [k for k, _, _ in self.queue]
            task = self.queue.pop(keys.index(key if key in keys else "pair_" + key))
            self.put([task], [_comm(task[1], task[2], name=f"alone_{task[0]}")])
        return self.results[key]

    def weight(self, n):
        g = self.get(n)
        return g.reshape(-1, g.shape[2]) if n in ROW_SHARDED else g

    def grad(self, n, g):
        g = g if g.ndim == 3 else g.reshape(N_DEV, g.shape[0] // N_DEV, g.shape[1])
        self.add("pair_d_" + n, "pair", g)
        self.then["pair_d_" + n] = lambda r: self.add("d_" + n, "chips", _pair_sum(g, r, name=f"pair_sum_{n}"))


def _sel_tables(dest, ws, wp, tw):
    n_tiles = (int(dest.max()) + tw) // tw
    tbl = np.full((N_DEV, wp), -1, np.int32)
    for j in range(N_DEV):
        tbl[j, :ws] = dest[j * ws:(j + 1) * ws]
    by_tile = [sorted({j for j in range(N_DEV) if ((tbl[j] // tw) == t).any()}) for t in range(n_tiles)]
    by_shard = [sorted({int(t) for t in np.unique(tbl[j, :ws] // tw)}) for j in range(N_DEV)]

    def table(lists):
        width = max(len(v) for v in lists)
        idx = np.array([(v + [v[-1]] * width)[:width] if v else [0] * width for v in lists], np.int32)
        val = np.array([[1] * len(v) + [0] * (width - len(v)) for v in lists], np.int32)
        return idx.reshape(-1), val.reshape(-1), width

    return tbl[:, :, None], table(by_tile), table(by_shard)


def _sel_matrix(d_ref, t, wp, tw):
    cols = t * tw + lax.broadcasted_iota(jnp.int32, (wp, tw), 1)
    return (d_ref[...] == cols).astype(BF16)


def _win_unshard(g, tbl, idx, val, width, *, tw, padded, name):
    _, dm, wp = g.shape
    tm = _pick(dm, (1024, 512, 256, 128))

    def body(idx_ref, val_ref, g_ref, d_ref, o_ref, acc):
        t, s_ = pl.program_id(1), pl.program_id(2)

        @pl.when(s_ == 0)
        def _():
            acc[...] = jnp.zeros_like(acc)

        @pl.when(val_ref[t * width + s_] == 1)
        def _():
            acc[...] += _dot(g_ref[...], _sel_matrix(d_ref, t, wp, tw), NN)

        @pl.when(s_ == width - 1)
        def _():
            o_ref[...] = acc[...].astype(BF16)

    return pl.pallas_call(
        body, name=name,
        grid_spec=pltpu.PrefetchScalarGridSpec(
            num_scalar_prefetch=2, grid=(dm // tm, padded // tw, width),
            in_specs=[pl.BlockSpec((None, tm, wp), lambda i, t, s_, ix, vl: (ix[t * width + s_], i, 0)),
                      pl.BlockSpec((None, wp, 1), lambda i, t, s_, ix, vl: (ix[t * width + s_], 0, 0))],
            out_specs=pl.BlockSpec((tm, tw), lambda i, t, s_, ix, vl: (i, t)),
            scratch_shapes=[pltpu.VMEM((tm, tw), F32)]),
        out_shape=jax.ShapeDtypeStruct((dm, padded), BF16),
        compiler_params=_cparams(("parallel", "parallel", "arbitrary")),
    )(idx, val, g, tbl)


def _win_to_shards(dw, tbl, idx, val, width, *, tw, wp, name):
    dm = dw.shape[0]
    tm = _pick(dm, (1024, 512, 256, 128))

    def body(idx_ref, val_ref, w_ref, d_ref, o_ref, acc):
        j, s_ = pl.program_id(1), pl.program_id(2)

        @pl.when(s_ == 0)
        def _():
            acc[...] = jnp.zeros_like(acc)

        @pl.when(val_ref[j * width + s_] == 1)
        def _():
            acc[...] += _dot(w_ref[...], _sel_matrix(d_ref, idx_ref[j * width + s_], wp, tw), NT)

        @pl.when(s_ == width - 1)
        def _():
            o_ref[...] = acc[...].astype(BF16)

    return pl.pallas_call(
        body, name=name,
        grid_spec=pltpu.PrefetchScalarGridSpec(
            num_scalar_prefetch=2, grid=(dm // tm, N_DEV, width),
            in_specs=[pl.BlockSpec((tm, tw), lambda i, j, s_, ix, vl: (i, ix[j * width + s_])),
                      pl.BlockSpec((None, wp, 1), lambda i, j, s_, ix, vl: (j, 0, 0))],
            out_specs=pl.BlockSpec((None, tm, wp), lambda i, j, s_, ix, vl: (j, i, 0)),
            scratch_shapes=[pltpu.VMEM((tm, wp), F32)]),
        out_shape=jax.ShapeDtypeStruct((N_DEV, dm, wp), BF16),
        compiler_params=_cparams(("parallel", "parallel", "arbitrary")),
    )(idx, val, dw, tbl)


def _adamw(parts, w, m, v, *, name):
    r, cdim = w.shape
    n_parts = parts.shape[0]
    tr = _pick(r, (256, 128, 64, 32, 16, 8))
    rc = min(16, tr)
    c1 = 1.0 - ADAM_B1 ** ADAM_STEP
    c2 = 1.0 - ADAM_B2 ** ADAM_STEP

    def body(p_ref, w_ref, m_ref, v_ref, g_ref, d_ref, mo_ref, vo_ref):
        def chunk(ci, carry):
            sl = pl.ds(pl.multiple_of(ci * rc, rc), rc)
            g = p_ref[0, sl, :].astype(F32)
            for i in range(1, n_parts):
                g = g + p_ref[i, sl, :].astype(F32)
            mn = ADAM_B1 * m_ref[sl, :] + (1.0 - ADAM_B1) * g
            vn = ADAM_B2 * v_ref[sl, :] + (1.0 - ADAM_B2) * jnp.square(g)
            m_hat = mn / c1
            v_hat = vn / c2
            g_ref[sl, :] = g
            d_ref[sl, :] = -ADAM_LR * (m_hat / (jnp.sqrt(v_hat) + ADAM_EPS) + ADAM_WD * w_ref[sl, :])
            mo_ref[sl, :] = mn
            vo_ref[sl, :] = vn
            return carry

        lax.fori_loop(0, tr // rc, chunk, 0)

    spec = pl.BlockSpec((tr, cdim), lambda i: (i, 0))
    return pl.pallas_call(
        body, name=name, grid=(r // tr,),
        in_specs=[pl.BlockSpec((n_parts, tr, cdim), lambda i: (0, i, 0)), spec, spec, spec],
        out_specs=[spec] * 4,
        out_shape=[jax.ShapeDtypeStruct((r, cdim), F32)] * 4,
        compiler_params=_cparams(("parallel",)),
    )(parts, w, m, v)


def _pad_to(v, n):
    return v if v.shape[0] == n else jnp.concatenate([v, jnp.zeros((n - v.shape[0],), v.dtype)])


def _pad_cols(v, n):
    return v if v.shape[-1] == n else jnp.concatenate([v, jnp.zeros(v.shape[:-1] + (n - v.shape[-1],), v.dtype)], axis=-1)


def _pack(vs, cols, row_mult, dtype):
    offs, o = [], 0
    for v in vs:
        offs.append(o)
        o += v.size
    rows = -(-o // cols)
    rows = -(-rows // row_mult) * row_mult
    flat = jnp.concatenate([v.reshape(-1).astype(dtype) for v in vs])
    return _pad_to(flat, rows * cols).reshape(rows, cols), offs


SHARDED = ("ffn1_w_gate", "ffn1_w_up", "ffn1_w_down", "w_in", "w_merge_gate", "gla_gate_up", "w_branch_fox",
           "w_branch_gla", "w_out", "ffn2_w_gate", "ffn2_w_up", "ffn2_w_down", "w_ple_gate", "w_ple_proj")
ROW_SHARDED = ("ffn1_w_down", "w_out", "ffn2_w_down", "w_ple_gate")
REPLICATED = ("ffn1_norm", "mix_norm", "fox_forget_bias", "gla_gate_bias", "gla_head_norm", "b_merge_gate",
              "ffn2_norm", "ple_norm", "final_norm")
WEIGHTS = ("ffn1_norm", "ffn1_w_gate", "ffn1_w_up", "ffn1_w_down", "mix_norm", "w_in", "fox_forget_bias",
           "gla_gate_up", "gla_gate_bias", "gla_head_norm", "w_branch_fox", "w_branch_gla", "w_merge_gate",
           "b_merge_gate", "w_out", "ffn2_norm", "ffn2_w_gate", "ffn2_w_up", "ffn2_w_down", "ple_norm",
           "w_ple_proj", "w_ple_gate", "final_norm")


def kernel(x, p, ffn1_norm, ffn1_w_gate, ffn1_w_up, ffn1_w_down, mix_norm, w_in, fox_forget_bias, gla_gate_up, gla_gate_bias, gla_head_norm, w_branch_fox, w_branch_gla, w_merge_gate, b_merge_gate, w_out, ffn2_norm, ffn2_w_gate, ffn2_w_up, ffn2_w_down, ple_norm, w_ple_proj, w_ple_gate, final_norm, loss_target, m_ffn1_norm, m_ffn1_w_gate, m_ffn1_w_up, m_ffn1_w_down, m_mix_norm, m_w_in, m_fox_forget_bias, m_gla_gate_up, m_gla_gate_bias, m_gla_head_norm, m_w_branch_fox, m_w_branch_gla, m_w_merge_gate, m_b_merge_gate, m_w_out, m_ffn2_norm, m_ffn2_w_gate, m_ffn2_w_up, m_ffn2_w_down, m_ple_norm, m_w_ple_proj, m_w_ple_gate, m_final_norm, v_ffn1_norm, v_ffn1_w_gate, v_ffn1_w_up, v_ffn1_w_down, v_mix_norm, v_w_in, v_fox_forget_bias, v_gla_gate_up, v_gla_gate_bias, v_gla_head_norm, v_w_branch_fox, v_w_branch_gla, v_w_merge_gate, v_b_merge_gate, v_w_out, v_ffn2_norm, v_ffn2_w_gate, v_ffn2_w_up, v_ffn2_w_down, v_ple_norm, v_w_ple_proj, v_w_ple_gate, v_final_norm):
    args = dict(locals())
    wts = {n: args[n] for n in WEIGHTS}
    mom_m = {n: args["m_" + n] for n in WEIGHTS}
    mom_v = {n: args["v_" + n] for n in WEIGHTS}

    xs, ps, tgt = x[0], p[0, 0], loss_target[0]
    s, d = xs.shape
    fox_w = w_branch_fox.shape[1]
    gla_vw = w_branch_gla.shape[1]
    fox_heads = fox_w // HEAD_DIM
    gla_heads = gla_vw // GLA_VAL_DIM
    gla_kw = gla_heads * HEAD_DIM
    rank = gla_gate_up.shape[1]

    c_fl = 3 * fox_w
    o_gr, o_gq, o_gk = gla_vw, 2 * gla_vw, 2 * gla_vw + gla_kw
    o_fl = o_gk + gla_kw
    o_gd = o_fl + LANES
    rest_w = o_gd + LANES
    padded = c_fl + rest_w
    seg = [(c_fl, 0), (fox_heads, c_fl + o_fl), (gla_kw, c_fl + o_gq), (gla_kw, c_fl + o_gk), (gla_vw, c_fl),
           (gla_vw, c_fl + o_gr), (rank, c_fl + o_gd)]
    dest = np.concatenate([np.arange(w_, dtype=np.int32) + o_ for w_, o_ in seg])
    ws = w_in.shape[2]
    wp = -(-ws // LANES) * LANES
    tw = 256 if padded % 256 == 0 else LANES
    tbl, (t_idx, t_val, t_width), (s_idx, s_val, s_width) = _sel_tables(dest, ws, wp, tw)
    tbl = jnp.asarray(tbl)

    ov = _Overlap()
    for n in SHARDED:
        sh = wts[n][0].astype(BF16)
        ov.add(n, "gather", _pad_cols(sh, wp) if n == "w_in" else sh)
    fbias = _pad_cols(fox_forget_bias, LANES)
    bmg_f, bmg_g = b_merge_gate[:, :d], b_merge_gate[:, d:]
    ghn = jnp.tile(gla_head_norm, (1, gla_heads))

    h1, ffn1_saved = _ffn_fwd(xs, ffn1_norm, ov, tag="ffn1")
    u = _rms_fwd(h1, mix_norm, name="mix_norm")
    win = _win_unshard(ov.weight("w_in"), tbl, jnp.asarray(t_idx), jnp.asarray(t_val), t_width, tw=tw, padded=padded,
                       name="in_proj_unshard")
    win_fox, win_rest = win[:, :c_fl], win[:, c_fl:]
    zf = ov.mm(u, win_fox, mode="nn", name="in_proj_fox", out_dtype=BF16)
    zr = ov.mm(u, win_rest, mode="nn", name="in_proj_rest")
    gz = ov.mm(u, ov.weight("w_merge_gate"), mode="nn", name="merge_gate")
    gup = ov.weight("gla_gate_up").transpose(1, 0, 2).reshape(rank, gla_kw)
    gup = jnp.concatenate([gup, jnp.zeros((LANES - rank, gla_kw), BF16)], axis=0)

    log_f = _rowwise(lambda fl, b: _log_sigmoid(fl + b), [(zr, LANES, o_fl // LANES)], [fbias], [(LANES, F32)],
                     name="forget_gate")[0]
    f_cum = _cumsum(log_f[:, :fox_heads].T, reverse=False, name="forget_cumsum")
    f_rep = jnp.broadcast_to(f_cum.T[:, :, None], (s, fox_heads, HEAD_DIM)).reshape(s, fox_w)
    f_row = f_cum[:, None, :]
    attn_us = 2.0 * s * s * HEAD_DIM * fox_heads / _Overlap.MM_FLOPS_PER_US
    y_fox, y_fox_bf, lse = ov.carry(
        ATTN_FWD_MATMULS * attn_us, lambda comm: _fox_fwd(zf, f_rep, f_row, heads=fox_heads, name="fox_fwd", comm=comm))

    def decay_fn(gd, gupv, gb):
        return _log_sigmoid(_dot(gd, gupv, NN) + gb) * (1.0 / GLA_GATE_TAU)

    la = _rowwise(decay_fn, [(zr, LANES, o_gd // LANES)], [gup, gla_gate_bias], [(gla_kw, F32)], name="gla_decay", rc=128)[0]
    q_blk, k_blk = o_gq // HEAD_DIM, o_gk // HEAD_DIM
    o_gla, states = _gla_fwd(zr, la, heads=gla_heads, q_blk=q_blk, k_blk=k_blk, name="gla_fwd")

    def gla_out_fn(o, gr, g):
        outs = []
        for hh in range(gla_heads):
            sl = slice(hh * GLA_VAL_DIM, (hh + 1) * GLA_VAL_DIM)
            _, oh = _rms_parts(o[:, sl])
            outs.append(oh * g[:, sl] * _silu_parts(gr[:, sl])[0])
        return jnp.concatenate(outs, axis=1)

    y_gla = _rowwise(gla_out_fn, [o_gla, (zr, gla_vw, o_gr // gla_vw)], [ghn], [(gla_vw, BF16)], name="gla_out")[0]
    br_f = ov.mm(y_fox_bf, ov.weight("w_branch_fox"), mode="nn", name="branch_fox")
    br_g = ov.mm(y_gla, ov.weight("w_branch_gla"), mode="nn", name="branch_gla")

    def merge_fn(zf_, zg_, bf_, bg_, b1, b2):
        return _sigmoid(zf_ + b1) * bf_ + _sigmoid(zg_ + b2) * bg_

    merged = _rowwise(merge_fn, [(gz, d, 0), (gz, d, 1), br_f, br_g], [bmg_f, bmg_g], [(d, BF16)], name="merge")[0]
    h2 = ov.mm(merged, ov.weight("w_out"), mode="nn", name="out_proj", add=h1)
    h3, ffn2_saved = _ffn_fwd(h2, ffn2_norm, ov, tag="ffn2")
    n3 = _rms_fwd(h3, ple_norm, name="ple_norm")
    gl = ov.mm(n3, ov.weight("w_ple_gate"), mode="nn", name="ple_gate")
    pe = ov.mm(ps, ov.weight("w_ple_proj"), mode="nn", name="ple_proj")

    def head_fn(h3b, glb, peb, tb, gfin):
        pg = _sigmoid(glb)
        h4 = h3b + pg * peb
        r, xh = _rms_parts(h4)
        err = xh * gfin - tb
        dy = err * (1.0 / d)
        t = dy * gfin
        dh4 = r * (t - xh * jnp.mean(t * xh, axis=-1, keepdims=True))
        return dh4, dh4 * pg, dh4 * peb * pg * (1.0 - pg), _colsum(err * err), _colsum(dy * xh)

    dh4, dpe, dgl, loss_cols, d_final = _rowwise(
        head_fn, [h3, gl, pe, tgt], [final_norm.reshape(1, d)], [(d, F32), (d, BF16), (d, BF16)], [d, d], name="loss_head")
    loss = lax.psum(0.5 * jnp.sum(loss_cols) / d, AXES)

    grads = {"final_norm": d_final.reshape(d)}
    ov.grad("w_ple_proj", ov.mm(ps, dpe, mode="tn", name="ple_proj_dw", out_dtype=BF16, out_chunked=True))
    ov.grad("w_ple_gate", ov.mm(n3, dgl, mode="tn", name="ple_gate_dw", out_dtype=BF16))
    dn3 = ov.mm(dgl, ov.weight("w_ple_gate"), mode="nt", name="ple_gate_dx")
    dh3, dh3_bf, grads["ple_norm"] = _rms_bwd(h3, dn3, ple_norm, dh4, name="ple_norm_bwd")
    dh2, dh2_bf, grads["ffn2_norm"] = _ffn_bwd(h2, ffn2_norm, ov, ffn2_saved, dh3, dh3_bf, tag="ffn2")

    ov.grad("w_out", ov.mm(merged, dh2_bf, mode="tn", name="out_proj_dw", out_dtype=BF16))
    dmerged = ov.mm(dh2_bf, ov.weight("w_out"), mode="nt", name="out_proj_dx")

    def merge_bwd_fn(zf_, zg_, bf_, bg_, dm, b1, b2):
        sf, sg = _sigmoid(zf_ + b1), _sigmoid(zg_ + b2)
        dz = jnp.concatenate([dm * bf_ * sf * (1.0 - sf), dm * bg_ * sg * (1.0 - sg)], axis=1)
        return dm * sf, dm * sg, dz, _colsum(dz)

    dbr_f, dbr_g, dgz, grads["b_merge_gate"] = _rowwise(
        merge_bwd_fn, [(gz, d, 0), (gz, d, 1), br_f, br_g, dmerged], [bmg_f, bmg_g],
        [(d, BF16), (d, BF16), (2 * d, BF16)], [2 * d], name="merge_bwd")
    ov.grad("w_merge_gate", ov.mm(u, dgz, mode="tn", name="merge_gate_dw", out_dtype=BF16, out_chunked=True))
    ov.grad("w_branch_fox", ov.mm(y_fox_bf, dbr_f, mode="tn", name="branch_fox_dw", out_dtype=BF16, out_chunked=True))
    ov.grad("w_branch_gla", ov.mm(y_gla, dbr_g, mode="tn", name="branch_gla_dw", out_dtype=BF16, out_chunked=True))
    dy_fox = ov.mm(dbr_f, ov.weight("w_branch_fox"), mode="nt", name="branch_fox_dx")
    dy_gla = ov.mm(dbr_g, ov.weight("w_branch_gla"), mode="nt", name="branch_gla_dx")

    def gla_out_bwd_fn(o, gr, dy, g):
        dos, dgrs, dgs = [], [], []
        for hh in range(gla_heads):
            sl = slice(hh * GLA_VAL_DIM, (hh + 1) * GLA_VAL_DIM)
            r, oh = _rms_parts(o[:, sl])
            si, dsi = _silu_parts(gr[:, sl])
            don = dy[:, sl] * si
            dgrs.append(dy[:, sl] * oh * g[:, sl] * dsi)
            t = don * g[:, sl]
            dos.append(r * (t - oh * jnp.mean(t * oh, axis=-1, keepdims=True)))
            dgs.append(_colsum(don * oh))
        return jnp.concatenate(dos, axis=1), jnp.concatenate(dgrs, axis=1), jnp.concatenate(dgs, axis=1)

    do_gla, dgr, d_ghn = _rowwise(gla_out_bwd_fn, [o_gla, (zr, gla_vw, o_gr // gla_vw), dy_gla], [ghn],
                                  [(gla_vw, F32), (gla_vw, F32)], [gla_vw], name="gla_out_bwd")
    grads["gla_head_norm"] = d_ghn.reshape(gla_heads, GLA_VAL_DIM).sum(axis=0, keepdims=True)
    dgq, dgk, dgv, dla = _gla_bwd(zr, la, do_gla, states, heads=gla_heads, q_blk=q_blk, k_blk=k_blk, name="gla_bwd")

    def decay_bwd_fn(dl, gd, gupv, gb):
        pre = _dot(gd, gupv, NN) + gb
        dpre = dl * (1.0 / GLA_GATE_TAU) * _sigmoid(-pre)
        return _dot(dpre, gupv, NT), dpre, _colsum(dpre)

    dgd, dpre_bf, grads["gla_gate_bias"] = _rowwise(
        decay_bwd_fn, [dla, (zr, LANES, o_gd // LANES)], [gup, gla_gate_bias], [(LANES, F32), (gla_kw, BF16)], [gla_kw],
        name="gla_decay_bwd", rc=128)
    d_gup = ov.mm(zr[:, o_gd:o_gd + LANES], dpre_bf, mode="tn", name="gla_gate_up_dw")[:rank]
    ov.grad("gla_gate_up", d_gup.reshape(rank, N_DEV, gla_kw // N_DEV).transpose(1, 0, 2).astype(BF16))

    def delta_fn(dyv, yv):
        outs = []
        for hh in range(fox_heads):
            sl = slice(hh * HEAD_DIM, (hh + 1) * HEAD_DIM)
            outs.append(jnp.broadcast_to(jnp.sum(dyv[:, sl] * yv[:, sl], axis=-1, keepdims=True), (dyv.shape[0], HEAD_DIM)))
        return jnp.concatenate(outs, axis=1)

    delta = _rowwise(delta_fn, [dy_fox, y_fox], [], [(fox_w, F32)], name="fox_delta")[0]
    dfq, dfk, dfv, d_fcol, d_frow = ov.carry(
        ATTN_BWD_MATMULS * attn_us,
        lambda comm: _fox_bwd(zf, dy_fox, lse, delta, f_rep, f_row, heads=fox_heads, name="fox_bwd", comm=comm))
    d_fcum = d_fcol[:, ::HEAD_DIM].T + d_frow.reshape(fox_heads, s)
    d_logf = _cumsum(d_fcum, reverse=True, name="forget_cumsum_bwd")
    d_logf = _pad_cols(d_logf.T, LANES)

    def forget_bwd_fn(dl, fl, b):
        dfl_ = dl * _sigmoid(-(fl + b))
        return dfl_, _colsum(dfl_)

    dfl, d_fbias = _rowwise(forget_bwd_fn, [d_logf, (zr, LANES, o_fl // LANES)], [fbias], [(LANES, F32)], [LANES],
                            name="forget_gate_bwd")
    grads["fox_forget_bias"] = d_fbias[:, :fox_heads]

    dz = jnp.concatenate([dfq, dfk, dfv, dgv, dgr, dgq, dgk, dfl, dgd], axis=1).astype(BF16)
    dwin = ov.mm(u, dz, mode="tn", name="in_proj_dw", out_dtype=BF16)
    ov.grad("w_in", _win_to_shards(dwin, tbl, jnp.asarray(s_idx), jnp.asarray(s_val), s_width, tw=tw, wp=wp,
                                   name="in_proj_dw_shards"))
    du = ov.mm(dgz, ov.weight("w_merge_gate"), mode="nt", name="merge_gate_dx")
    du = ov.mm(dz, win, mode="nt", name="in_proj_dx", add=du)
    dh1, dh1_bf, grads["mix_norm"] = _rms_bwd(h1, du, mix_norm, dh2, name="mix_norm_bwd")
    dx, _, grads["ffn1_norm"] = _ffn_bwd(xs, ffn1_norm, ov, ffn1_saved, dh1, dh1_bf, tag="ffn1")

    outs = {}
    for n in SHARDED:
        parts = ov.get("d_" + n)
        state = [_pad_cols(t_[n][0], parts.shape[2]) for t_ in (wts, mom_m, mom_v)]
        res4 = _adamw(parts, *state, name=f"adamw_{n}")
        for kind, r_ in zip(("grad", "delta", "new_m", "new_v"), res4):
            outs[f"{kind}_{n}"] = r_[:, :wts[n].shape[2]][None]

    send_small, small_offs = _pack([grads[n] for n in REPLICATED], LANES, 8, F32)
    recv_small = _comm("bcast", send_small, name="exchange_replicated")
    w_sm, _ = _pack([wts[n] for n in REPLICATED], LANES, 8, F32)
    m_sm, _ = _pack([mom_m[n] for n in REPLICATED], LANES, 8, F32)
    v_sm, _ = _pack([mom_v[n] for n in REPLICATED], LANES, 8, F32)
    small = _adamw(recv_small, w_sm, m_sm, v_sm, name="adamw_replicated")
    for kind, buf in zip(("grad", "delta", "new_m", "new_v"), small):
        fs = buf.reshape(-1)
        for n, o in zip(REPLICATED, small_offs):
            outs[f"{kind}_{n}"] = fs[o:o + wts[n].size].reshape(wts[n].shape)

    res = [loss, dx[None]]
    for kind in ("grad", "delta", "new_m", "new_v"):
        res += [outs[f"{kind}_{n}"] for n in WEIGHTS]
    return tuple(res)
```

```python
import functools

import jax
import jax.numpy as jnp
import numpy as np
from jax import lax
from jax.experimental import pallas as pl
from jax.experimental.pallas import tpu as pltpu

F32 = jnp.float32
BF16 = jnp.bfloat16
MESH = pl.DeviceIdType.MESH
AXES = ("x", "y", "c")
N_DEV = 8

VMEM_LIMIT_BYTES = 56 * 1024 * 1024
MM_BLOCK_BUDGET_BYTES = 40 * 1024 * 1024
LANES = 128

EPS = 1e-6
HEAD_DIM = 128
GLA_VAL_DIM = 256
GLA_CHUNK = 64
GLA_GATE_TAU = 16.0
ADAM_LR, ADAM_B1, ADAM_B2, ADAM_EPS, ADAM_WD, ADAM_STEP = 0.001, 0.9, 0.999, 1e-08, 0.01, 10

ATTN_FWD_MATMULS = 7.0
ATTN_BWD_MATMULS = 7.0

HIGHEST = lax.Precision.HIGHEST
NN = (((1,), (0,)), ((), ()))
NT = (((1,), (1,)), ((), ()))
TN = (((0,), (0,)), ((), ()))


def _cparams(sem):
    return pltpu.CompilerParams(dimension_semantics=sem, vmem_limit_bytes=VMEM_LIMIT_BYTES)


def _pick(dim, cands):
    for c in cands:
        if dim % c == 0:
            return c
    return dim


def _bf(v):
    return v if v.dtype == BF16 else v.astype(BF16)


def _dot(a, b, dims):
    return lax.dot_general(_bf(a), _bf(b), dims, preferred_element_type=F32)


def _sigmoid(v):
    return 1.0 / (1.0 + jnp.exp(-v))


def _log_sigmoid(v):
    return jnp.minimum(v, 0.0) - jnp.log(1.0 + jnp.exp(-jnp.abs(v)))


def _logical(v):
    return (v.shape[1], v.shape[0] * v.shape[2]) if v.ndim == 3 else v.shape


def _mm(a, b, *, mode, name, out_dtype=F32, add=None, scale=1.0, out_chunked=False, comm=()):
    la, lb = _logical(a), _logical(b)
    if mode == "nn":
        (m, k), (k2, n) = la, lb
        a_minor, b_minor = "k", "n"
    elif mode == "nt":
        (m, k), (n, k2) = la, lb
        a_minor, b_minor = "k", "k"
    else:
        (k, m), (k2, n) = la, lb
        a_minor, b_minor = "m", "n"
    assert k == k2, (name, a.shape, b.shape)
    forced = {}
    for v, minor in ((a, a_minor), (b, b_minor)):
        if v.ndim == 3:
            assert forced.get(minor, v.shape[2]) == v.shape[2], name
            forced[minor] = v.shape[2]
    if out_chunked:
        assert forced.get("n", n // N_DEV) == n // N_DEV, name
        forced["n"] = n // N_DEV
    tm = forced.get("m") or _pick(m, (1024, 512, 256, 128))
    tn = forced.get("n") or _pick(n, (1408, 1280, 1024, 512, 256, 128))

    def blocks_bytes(tm_, tn_, t):
        io = tm_ * t * a.dtype.itemsize + t * tn_ * b.dtype.itemsize
        return 2 * (io + tm_ * tn_ * (jnp.dtype(out_dtype).itemsize + (4 if add is not None else 0))) + tm_ * tn_ * 4

    kc = forced.get("k")
    a_k_minor, b_k_minor = mode != "tn", mode == "nt"
    if kc:
        aligned = all(v.ndim == 3 or kc % (LANES if minor else 16) == 0 for v, minor in ((a, a_k_minor), (b, b_k_minor)))
        fits = [(tm_, tn_) for tm_, tn_ in ((tm, tn), (512, tn), (512, 512))
                if m % tm_ == 0 and n % tn_ == 0 and forced.get("m", tm_) == tm_ and forced.get("n", tn_) == tn_
                and blocks_bytes(tm_, tn_, k) <= MM_BLOCK_BUDGET_BYTES]
        if aligned and fits:
            (tm, tn), tk = fits[0], k
        else:
            tk, kc = kc, None
    else:
        tk = k if blocks_bytes(tm, tn, k) <= MM_BLOCK_BUDGET_BYTES else _pick(k, (512, 640, 256, 128))
    nk = k // tk
    dims = {"nn": NN, "nt": NT, "tn": TN}[mode]
    gi, gj, gk = (lambda i, j, kk: i), (lambda i, j, kk: j), (lambda i, j, kk: kk)

    def spec(v, t_major, t_minor, g_major, g_minor, all_chunks=False):
        if v.ndim == 3:
            if all_chunks:
                return pl.BlockSpec((N_DEV, t_major, v.shape[2]), lambda i, j, kk: (0, g_major(i, j, kk), 0))
            return pl.BlockSpec((None, t_major, v.shape[2]), lambda i, j, kk: (g_minor(i, j, kk), g_major(i, j, kk), 0))
        return pl.BlockSpec((t_major, t_minor), lambda i, j, kk: (g_major(i, j, kk), g_minor(i, j, kk)))

    a_spec = spec(a, tk, tm, gk, gi) if mode == "tn" else spec(a, tm, tk, gi, gk, bool(kc))
    b_spec = spec(b, tn, tk, gj, gk, bool(kc)) if mode == "nt" else spec(b, tk, tn, gk, gj)

    def k_chunk(ref, minor, c_):
        if len(ref.shape) == 3:
            return ref[c_]
        return ref[:, c_ * kc:(c_ + 1) * kc] if minor else ref[c_ * kc:(c_ + 1) * kc, :]
    if out_chunked:
        o_spec = pl.BlockSpec((None, tm, tn), lambda i, j, kk: (j, i, 0))
        out_shape = jax.ShapeDtypeStruct((N_DEV, m, tn), out_dtype)
    else:
        o_spec = pl.BlockSpec((tm, tn), lambda i, j, kk: (i, j))
        out_shape = jax.ShapeDtypeStruct((m, n), out_dtype)
    has_add = add is not None
    assert not (has_add and out_chunked), name

    def body(*refs):
        a_ref, b_ref = refs[0], refs[1]
        add_ref = refs[2] if has_add else None
        o_ref = refs[3 if has_add else 2]
        kk = pl.program_id(2)

        def finish(r):
            if scale != 1.0:
                r = r * scale
            if has_add:
                r = r + add_ref[...]
            o_ref[...] = r.astype(o_ref.dtype)

        if kc:
            r = _dot(k_chunk(a_ref, a_k_minor, 0), k_chunk(b_ref, b_k_minor, 0), dims)
            for c_ in range(1, k // kc):
                r = r + _dot(k_chunk(a_ref, a_k_minor, c_), k_chunk(b_ref, b_k_minor, c_), dims)
            finish(r)
        elif nk == 1:
            finish(_dot(a_ref[...], b_ref[...], dims))
        else:
            acc_ref = refs[-1]

            @pl.when(kk == 0)
            def _():
                acc_ref[...] = jnp.zeros_like(acc_ref)

            acc_ref[...] += _dot(a_ref[...], b_ref[...], dims)

            @pl.when(kk == nk - 1)
            def _():
                finish(acc_ref[...])

    res, carried = _call(
        body, name=name, grid=(m // tm, n // tn, nk),
        in_specs=[a_spec, b_spec] + ([o_spec] if has_add else []), out_specs=[o_spec], out_shape=[out_shape],
        scratch_shapes=[pltpu.VMEM((tm, tn), F32)] if nk > 1 else [],
        semantics=("parallel", "parallel", "arbitrary"), operands=[a, b] + ([add] if has_add else []), comm=comm)
    return (res[0], carried) if comm else res[0]


def _call(body, *, name, grid, in_specs, out_specs, out_shape, scratch_shapes, semantics, operands, comm=()):
    n_in, n_out, n_scr, n = len(in_specs), len(out_specs), len(scratch_shapes), len(comm)
    if not comm:
        res = pl.pallas_call(body, name=name, grid=grid, in_specs=in_specs, out_specs=out_specs, out_shape=out_shape,
                             scratch_shapes=scratch_shapes, compiler_params=_cparams(semantics))(*operands)
        return res, []

    def carrying(*refs):
        ins, c_in = refs[:n_in], refs[n_in:n_in + n]
        outs, c_out = refs[n_in + n:n_in + n + n_out], refs[n_in + n + n_out:n_in + 2 * n + n_out]
        scratch, sems = refs[n_in + 2 * n + n_out:][:n_scr], refs[n_in + 2 * n + n_out + n_scr:]
        tasks = [_comm_ops(kind, c_in[t], c_out[t], *sems[3 * t:3 * t + 3]) for t, (kind, _) in enumerate(comm)]
        ids = [pl.program_id(ax) for ax in range(len(grid))]

        @pl.when(functools.reduce(lambda p, q: p & q, [i == 0 for i in ids]))
        def _():
            for start, _ in tasks:
                start()

        body(*ins, *outs, *scratch)

        @pl.when(functools.reduce(lambda p, q: p & q, [i == g - 1 for i, g in zip(ids, grid)]))
        def _():
            for _, finish in tasks:
                finish()

    any_spec = pl.BlockSpec(memory_space=pl.ANY)
    res = pl.pallas_call(
        carrying, name=name, grid=grid,
        in_specs=list(in_specs) + [any_spec] * n, out_specs=list(out_specs) + [any_spec] * n,
        out_shape=list(out_shape) + [_comm_out_shape(kind, v) for kind, v in comm],
        scratch_shapes=list(scratch_shapes) + _comm_scratch(n),
        compiler_params=_cparams(("arbitrary",) * len(grid)),
    )(*operands, *[v for _, v in comm])
    return res[:n_out], res[n_out:]


def _rowwise(fn, rows, consts, outs, accs=(), *, name, rc=None):
    rows = [r if isinstance(r, tuple) else (r, r.shape[1], 0) for r in rows]
    m = rows[0][0].shape[0]
    widths = [w for _, w, _ in rows] + [n for n, _ in outs]
    row_bytes = sum(w * r.dtype.itemsize for r, w, _ in rows) + sum(n * jnp.dtype(d).itemsize for n, d in outs)
    tm = 1024
    while tm > 16 and (m % tm or 2 * tm * row_bytes > 24 * 1024 * 1024):
        tm //= 2
    if m % tm:
        tm = m
    if rc is None:
        rc = 16
        while rc * 2 <= tm and rc * 2 * max(widths) <= 32768:
            rc *= 2
    rc = min(rc, tm)
    nr, nc, no = len(rows), len(consts), len(outs)

    def body(*refs):
        in_refs, c_refs = refs[:nr], refs[nr:nr + nc]
        o_refs, a_refs = refs[nr + nc:nr + nc + no], refs[nr + nc + no:]

        @pl.when(pl.program_id(0) == 0)
        def _():
            for r in a_refs:
                r[...] = jnp.zeros_like(r)

        cvals = [c[...] for c in c_refs]

        def chunk(ci, carry):
            sl = pl.ds(pl.multiple_of(ci * rc, rc), rc)
            res = fn(*[r[sl, :] for r in in_refs], *cvals)
            if not isinstance(res, (tuple, list)):
                res = (res,)
            for r, v in zip(o_refs, res[:no]):
                r[sl, :] = v.astype(r.dtype)
            for r, v in zip(a_refs, res[no:]):
                r[...] += v
            return carry

        lax.fori_loop(0, tm // rc, chunk, 0)

    in_specs = [pl.BlockSpec((tm, w), functools.partial(lambda i, cb: (i, cb), cb=cb)) for _, w, cb in rows]
    in_specs += [pl.BlockSpec(c.shape, lambda i: (0, 0)) for c in consts]
    out_specs = [pl.BlockSpec((tm, n), lambda i: (i, 0)) for n, _ in outs]
    out_specs += [pl.BlockSpec((1, n), lambda i: (0, 0)) for n in accs]
    out_shape = [jax.ShapeDtypeStruct((m, n), d) for n, d in outs] + [jax.ShapeDtypeStruct((1, n), F32) for n in accs]
    res = pl.pallas_call(
        body, name=name, grid=(m // tm,),
        in_specs=in_specs, out_specs=out_specs, out_shape=out_shape,
        compiler_params=_cparams(("arbitrary",)),
    )(*[r for r, _, _ in rows], *consts)
    return res


def _colsum(v):
    return jnp.sum(v, axis=0, keepdims=True)


def _rms_parts(xv):
    r = lax.rsqrt(jnp.mean(xv * xv, axis=-1, keepdims=True) + EPS)
    return r, xv * r


def _rms_fwd(xv, g, *, name):
    d = xv.shape[1]

    def fn(xb, gb):
        _, xh = _rms_parts(xb)
        return xh * gb

    return _rowwise(fn, [xv], [g], [(d, BF16)], name=name)[0]


def _rms_bwd(xv, dy, g, add, *, name):
    d = xv.shape[1]

    def fn(xb, dyb, addb, gb):
        r, xh = _rms_parts(xb)
        t = dyb * gb
        dx = r * (t - xh * jnp.mean(t * xh, axis=-1, keepdims=True)) + addb
        return dx, dx, _colsum(dyb * xh)

    return _rowwise(fn, [xv, dy, add], [g], [(d, F32), (d, BF16)], [d], name=name)


def _silu_parts(a):
    sg = _sigmoid(a)
    return a * sg, sg * (1.0 + a * (1.0 - sg))


def _rows(v):
    return v.reshape(v.shape[0] * v.shape[1], v.shape[2])


def _ffn_fwd(h, g, ov, *, tag):
    s = h.shape[0]
    n = _rms_fwd(h, g, name=f"{tag}_norm")
    a = ov.mm(n, ov.weight(f"{tag}_w_gate"), mode="nn", name=f"{tag}_gate", out_dtype=BF16, out_chunked=True)
    b = ov.mm(n, ov.weight(f"{tag}_w_up"), mode="nn", name=f"{tag}_up", out_dtype=BF16, out_chunked=True)
    c = a.shape[2]
    hm = _rowwise(lambda av, bv: _silu_parts(av.astype(F32))[0] * bv.astype(F32), [_rows(a), _rows(b)], [], [(c, BF16)],
                  name=f"{tag}_act")[0]
    hm = hm.reshape(N_DEV, s, c)
    out = ov.mm(hm, ov.weight(f"{tag}_w_down"), mode="nn", name=f"{tag}_down", add=h, scale=0.5)
    return out, (n, a, b, hm)


def _ffn_bwd(h, g, ov, saved, dout, dout_bf, *, tag):
    n, a, b, hm = saved
    wg, wu, wd = ov.weight(f"{tag}_w_gate"), ov.weight(f"{tag}_w_up"), ov.weight(f"{tag}_w_down")
    s, c = h.shape[0], wg.shape[2]
    d_wd = ov.mm(hm, dout_bf, mode="tn", name=f"{tag}_down_dw", scale=0.5, out_dtype=BF16)
    ov.grad(f"{tag}_w_down", d_wd)
    dhm = ov.mm(dout_bf, wd, mode="nt", name=f"{tag}_down_dx", scale=0.5, out_dtype=BF16, out_chunked=True)

    def act_bwd(av, bv, dv):
        av, bv, dv = av.astype(F32), bv.astype(F32), dv.astype(F32)
        si, dsi = _silu_parts(av)
        return dv * bv * dsi, dv * si

    da, db = _rowwise(act_bwd, [_rows(a), _rows(b), _rows(dhm)], [], [(c, BF16), (c, BF16)], name=f"{tag}_act_bwd")
    da, db = da.reshape(N_DEV, s, c), db.reshape(N_DEV, s, c)
    ov.grad(f"{tag}_w_gate", ov.mm(n, da, mode="tn", name=f"{tag}_gate_dw", out_dtype=BF16, out_chunked=True))
    ov.grad(f"{tag}_w_up", ov.mm(n, db, mode="tn", name=f"{tag}_up_dw", out_dtype=BF16, out_chunked=True))
    dn = ov.mm(da, wg, mode="nt", name=f"{tag}_gate_dx")
    dn = ov.mm(db, wu, mode="nt", name=f"{tag}_up_dx", add=dn)
    dh, dh_bf, dg = _rms_bwd(h, dn, g, dout, name=f"{tag}_norm_bwd")
    return dh, dh_bf, dg


def _cumsum(xv, *, reverse, name):
    h, s = xv.shape
    t = _pick(s, (512, 256, 128))
    nb = s // t

    def blk(j):
        return (0, nb - 1 - j) if reverse else (0, j)

    def body(x_ref, o_ref, carry):
        @pl.when(pl.program_id(0) == 0)
        def _():
            carry[...] = jnp.zeros_like(carry)

        i0 = lax.broadcasted_iota(jnp.int32, (t, t), 0)
        i1 = lax.broadcasted_iota(jnp.int32, (t, t), 1)
        tri = ((i0 >= i1) if reverse else (i0 <= i1)).astype(F32)
        xb = x_ref[...]
        o_ref[...] = jnp.dot(xb, tri, precision=HIGHEST, preferred_element_type=F32) + carry[...]
        carry[...] += jnp.sum(xb, axis=1, keepdims=True)

    return pl.pallas_call(
        body, name=name, grid=(nb,),
        in_specs=[pl.BlockSpec((h, t), blk)], out_specs=pl.BlockSpec((h, t), blk),
        out_shape=jax.ShapeDtypeStruct((h, s), F32),
        scratch_shapes=[pltpu.VMEM((h, 1), F32)],
        compiler_params=_cparams(("arbitrary",)),
    )(xv)


def _fox_tiles(s):
    t = _pick(s, (512, 256, 128))
    return t, t


def _causal(sc):
    t = sc.shape[0]
    keep = lax.broadcasted_iota(jnp.int32, (t, t), 1) <= lax.broadcasted_iota(jnp.int32, (t, t), 0)
    return jnp.where(keep, sc, -jnp.inf)


ATTN_HEADS_PER_STEP = 2


def _head_cols(hh):
    return slice(hh * HEAD_DIM, (hh + 1) * HEAD_DIM)


def _across(rowstat, width):
    return jnp.tile(rowstat, (1, width // HEAD_DIM))


def _fox_fwd(zf, f_rep, f_row, *, heads, name, comm=()):
    s = zf.shape[0]
    tq, tk = _fox_tiles(s)
    assert tq == tk
    nq, nk = s // tq, s // tk
    hp = ATTN_HEADS_PER_STEP if heads % ATTN_HEADS_PER_STEP == 0 else 1
    wb = hp * HEAD_DIM
    scale = HEAD_DIM ** -0.5
    w = heads * HEAD_DIM

    def body(q_ref, k_ref, v_ref, fq_ref, fk_ref, o32_ref, o16_ref, lse_ref, m_sc, l_sc, acc_sc):
        i, j = pl.program_id(1), pl.program_id(2)

        @pl.when(j == 0)
        def _():
            m_sc[...] = jnp.full_like(m_sc, -jnp.inf)
            l_sc[...] = jnp.zeros_like(l_sc)
            acc_sc[...] = jnp.zeros_like(acc_sc)

        def step(diagonal):
            for hh in range(hp):
                cols = _head_cols(hh)
                sc = _dot(q_ref[:, cols], k_ref[:, cols], NT) * scale + _across(fq_ref[:, cols], tk) - fk_ref[hh]
                if diagonal:
                    sc = _causal(sc)
                m_old = m_sc[hh]
                m_new = jnp.maximum(m_old, jnp.max(sc, axis=-1, keepdims=True))
                alpha = jnp.exp(m_old - m_new)
                pr = jnp.exp(sc - _across(m_new, tk))
                l_sc[hh] = alpha * l_sc[hh] + jnp.sum(pr, axis=-1, keepdims=True)
                acc_sc[hh] = alpha * acc_sc[hh] + _dot(pr, v_ref[:, cols], NN)
                m_sc[hh] = m_new

        @pl.when(j < i)
        def _():
            step(False)

        @pl.when(j == i)
        def _():
            step(True)

        @pl.when(j == nk - 1)
        def _():
            for hh in range(hp):
                cols = _head_cols(hh)
                o = acc_sc[hh] / l_sc[hh]
                o32_ref[:, cols] = o
                o16_ref[:, cols] = o.astype(BF16)
                lse_ref[:, cols] = m_sc[hh] + jnp.log(l_sc[hh])

    def kv_blk(off):
        return lambda h, i, j: (jnp.minimum(j, i), off + h)

    o_spec = pl.BlockSpec((tq, wb), lambda h, i, j: (i, h))
    stat = pltpu.VMEM((hp, tq, HEAD_DIM), F32)
    return _call(
        body, name=name, grid=(heads // hp, nq, nk),
        in_specs=[
            o_spec,
            pl.BlockSpec((tk, wb), kv_blk(heads // hp)),
            pl.BlockSpec((tk, wb), kv_blk(2 * heads // hp)),
            o_spec,
            pl.BlockSpec((hp, 1, tk), lambda h, i, j: (h, 0, jnp.minimum(j, i))),
        ],
        out_specs=[o_spec, o_spec, o_spec],
        out_shape=[jax.ShapeDtypeStruct((s, w), F32), jax.ShapeDtypeStruct((s, w), BF16),
                   jax.ShapeDtypeStruct((s, w), F32)],
        scratch_shapes=[stat, stat, stat],
        semantics=("parallel", "parallel", "arbitrary"), operands=[zf, zf, zf, f_rep, f_row], comm=comm)


def _fox_bwd(zf, do, lse, delta, f_rep, f_row, *, heads, name, comm=()):
    s = zf.shape[0]
    tq, tk = _fox_tiles(s)
    assert tq == tk
    nq, nk = s // tq, s // tk
    hp = ATTN_HEADS_PER_STEP if heads % ATTN_HEADS_PER_STEP == 0 else 1
    wb = hp * HEAD_DIM
    scale = HEAD_DIM ** -0.5
    w = heads * HEAD_DIM

    def body(q_ref, k_ref, v_ref, do_ref, lse_ref, dl_ref, fq_ref, fk_ref, dq_ref, dk_ref, dv_ref, dfq_ref, dfk_ref):
        j, i = pl.program_id(1), pl.program_id(2)

        @pl.when((j == 0) & (i == 0))
        def _():
            dq_ref[...] = jnp.zeros_like(dq_ref)
            dfq_ref[...] = jnp.zeros_like(dfq_ref)

        @pl.when(i == 0)
        def _():
            dk_ref[...] = jnp.zeros_like(dk_ref)
            dv_ref[...] = jnp.zeros_like(dv_ref)
            dfk_ref[...] = jnp.zeros_like(dfk_ref)

        def step(diagonal):
            rows = pl.ds(pl.multiple_of(i * tq, tq), tq)
            for hh in range(hp):
                cols = _head_cols(hh)
                q, k, v = q_ref[:, cols], k_ref[:, cols], v_ref[:, cols]
                dob = do_ref[:, cols].astype(BF16)
                sc = _dot(q, k, NT) * scale + _across(fq_ref[:, cols], tk) - fk_ref[hh]
                if diagonal:
                    sc = _causal(sc)
                pr = jnp.exp(sc - _across(lse_ref[:, cols], tk))
                dv_ref[:, cols] += _dot(pr, dob, TN)
                dp = _dot(dob, v, NT)
                ds = pr * (dp - _across(dl_ref[:, cols], tk))
                dsb = ds.astype(BF16)
                dk_ref[:, cols] += _dot(dsb, q, TN) * scale
                dq_ref[rows, cols] += _dot(dsb, k, NN) * scale
                dfq_ref[rows, cols] += jnp.broadcast_to(jnp.sum(ds, axis=1, keepdims=True), (tq, HEAD_DIM))
                dfk_ref[hh] -= jnp.sum(ds, axis=0, keepdims=True)

        @pl.when(i > j)
        def _():
            step(False)

        @pl.when(i == j)
        def _():
            step(True)

    q_spec = pl.BlockSpec((tq, wb), lambda h, j, i: (jnp.maximum(i, j), h))
    k_spec = pl.BlockSpec((tk, wb), lambda h, j, i: (j, h))
    whole = pl.BlockSpec((s, wb), lambda h, j, i: (0, h))
    row_spec = pl.BlockSpec((hp, 1, tk), lambda h, j, i: (h, 0, j))
    return _call(
        body, name=name, grid=(heads // hp, nk, nq),
        in_specs=[
            q_spec,
            pl.BlockSpec((tk, wb), lambda h, j, i: (j, heads // hp + h)),
            pl.BlockSpec((tk, wb), lambda h, j, i: (j, 2 * heads // hp + h)),
            q_spec, q_spec, q_spec, q_spec, row_spec,
        ],
        out_specs=[whole, k_spec, k_spec, whole, row_spec],
        out_shape=[jax.ShapeDtypeStruct((s, w), F32), jax.ShapeDtypeStruct((s, w), F32),
                   jax.ShapeDtypeStruct((s, w), F32), jax.ShapeDtypeStruct((s, w), F32),
                   jax.ShapeDtypeStruct((heads, 1, s), F32)],
        scratch_shapes=[], semantics=("parallel", "arbitrary", "arbitrary"),
        operands=[zf, zf, zf, do, lse, delta, f_rep, f_row], comm=comm)


def _gla_rows(s):
    return _pick(s, (256, 128, 64))


def _gla_chunk_terms(la_c, tri):
    a_cum = jnp.dot(tri, la_c, precision=HIGHEST, preferred_element_type=F32)
    a_tot = jnp.sum(la_c, axis=0, keepdims=True)
    return jnp.exp(a_tot - a_cum), jnp.exp(a_tot)


def _gla_fwd(zr, la, *, heads, q_blk, k_blk, name):
    s = zr.shape[0]
    c = GLA_CHUNK
    rows = _gla_rows(s)
    nsteps, ncs = s // rows, rows // c
    scale = HEAD_DIM ** -0.5

    def body(q_ref, k_ref, v_ref, la_ref, o_ref, st_ref, state):
        @pl.when(pl.program_id(1) == 0)
        def _():
            state[...] = jnp.zeros_like(state)

        tri = (lax.broadcasted_iota(jnp.int32, (c, c), 0) >= lax.broadcasted_iota(jnp.int32, (c, c), 1)).astype(F32)
        for t in range(ncs):
            sl = slice(t * c, (t + 1) * c)
            dec, e_tot = _gla_chunk_terms(la_ref[sl, :], tri)
            kd = k_ref[sl, :] * dec
            st_ref[t] = state[...]
            new = state[...] * e_tot + _dot(v_ref[sl, :], kd, TN)
            state[...] = new
            o_ref[sl, :] = _dot(q_ref[sl, :] * scale, new, NT)

    return pl.pallas_call(
        body, name=name, grid=(heads, nsteps),
        in_specs=[
            pl.BlockSpec((rows, HEAD_DIM), lambda h, i: (i, q_blk + h)),
            pl.BlockSpec((rows, HEAD_DIM), lambda h, i: (i, k_blk + h)),
            pl.BlockSpec((rows, GLA_VAL_DIM), lambda h, i: (i, h)),
            pl.BlockSpec((rows, HEAD_DIM), lambda h, i: (i, h)),
        ],
        out_specs=[
            pl.BlockSpec((rows, GLA_VAL_DIM), lambda h, i: (i, h)),
            pl.BlockSpec((None, ncs, GLA_VAL_DIM, HEAD_DIM), lambda h, i: (h, i, 0, 0)),
        ],
        out_shape=[jax.ShapeDtypeStruct((s, heads * GLA_VAL_DIM), F32),
                   jax.ShapeDtypeStruct((heads, s // c, GLA_VAL_DIM, HEAD_DIM), F32)],
        scratch_shapes=[pltpu.VMEM((GLA_VAL_DIM, HEAD_DIM), F32)],
        compiler_params=_cparams(("parallel", "arbitrary")),
    )(zr, zr, zr, la)


def _gla_bwd(zr, la, do, states, *, heads, q_blk, k_blk, name):
    s = zr.shape[0]
    c = GLA_CHUNK
    rows = _gla_rows(s)
    nsteps, ncs = s // rows, rows // c
    scale = HEAD_DIM ** -0.5

    def body(q_ref, k_ref, v_ref, la_ref, do_ref, st_ref, dq_ref, dk_ref, dv_ref, dla_ref, dstate):
        @pl.when(pl.program_id(1) == 0)
        def _():
            dstate[...] = jnp.zeros_like(dstate)

        i0 = lax.broadcasted_iota(jnp.int32, (c, c), 0)
        i1 = lax.broadcasted_iota(jnp.int32, (c, c), 1)
        tri = (i0 >= i1).astype(F32)
        strict = (i0 > i1).astype(F32)
        for t in reversed(range(ncs)):
            sl = slice(t * c, (t + 1) * c)
            dec, e_tot = _gla_chunk_terms(la_ref[sl, :], tri)
            kd = k_ref[sl, :] * dec
            kdb = kd.astype(BF16)
            vb = v_ref[sl, :].astype(BF16)
            dob = do_ref[sl, :].astype(BF16)
            prev = st_ref[t]
            cur = prev * e_tot + _dot(vb, kdb, TN)
            d_cur = dstate[...] + _dot(dob, q_ref[sl, :] * scale, TN)
            d_cur_b = d_cur.astype(BF16)
            dq_ref[sl, :] = _dot(dob, cur, NN) * scale
            dv_ref[sl, :] = _dot(kdb, d_cur_b, NT)
            dkd = _dot(vb, d_cur_b, NN)
            d_tot = e_tot * jnp.sum(d_cur * prev, axis=0, keepdims=True)
            dk_ref[sl, :] = dkd * dec
            dla_ref[sl, :] = d_tot + jnp.dot(strict, dkd * kd, precision=HIGHEST, preferred_element_type=F32)
            dstate[...] = d_cur * e_tot

    def rev(i):
        return nsteps - 1 - i

    kq_spec = pl.BlockSpec((rows, HEAD_DIM), lambda h, i: (rev(i), h))
    v_spec = pl.BlockSpec((rows, GLA_VAL_DIM), lambda h, i: (rev(i), h))
    return pl.pallas_call(
        body, name=name, grid=(heads, nsteps),
        in_specs=[
            pl.BlockSpec((rows, HEAD_DIM), lambda h, i: (rev(i), q_blk + h)),
            pl.BlockSpec((rows, HEAD_DIM), lambda h, i: (rev(i), k_blk + h)),
            v_spec, kq_spec, v_spec,
            pl.BlockSpec((None, ncs, GLA_VAL_DIM, HEAD_DIM), lambda h, i: (h, rev(i), 0, 0)),
        ],
        out_specs=[kq_spec, kq_spec, v_spec, kq_spec],
        out_shape=[jax.ShapeDtypeStruct((s, heads * HEAD_DIM), F32), jax.ShapeDtypeStruct((s, heads * HEAD_DIM), F32),
                   jax.ShapeDtypeStruct((s, heads * GLA_VAL_DIM), F32), jax.ShapeDtypeStruct((s, heads * HEAD_DIM), F32)],
        scratch_shapes=[pltpu.VMEM((GLA_VAL_DIM, HEAD_DIM), F32)],
        compiler_params=_cparams(("parallel", "arbitrary")),
    )(zr, zr, zr, la, do, states)


def _my_place():
    x, y, c = lax.axis_index("x"), lax.axis_index("y"), lax.axis_index("c")
    return x, y, c


def _gather_ops(x_ref, out_ref, send_sems, recv_sems, local_sem):
    def plan():
        x, y, c = _my_place()
        me, sibling = (x, y, c), (x, y, 1 - c)
        chips = [(1 - x, y), (x, 1 - y), (1 - x, 1 - y)]

        def blk(px, py, pc):
            return out_ref.at[4 * px + 2 * py + pc]

        def copy(k, block, to, src=None):
            return pltpu.make_async_remote_copy(
                src_ref=blk(*block) if src is None else src, dst_ref=blk(*block),
                send_sem=send_sems.at[k], recv_sem=recv_sems.at[k], device_id=to, device_id_type=MESH)

        mine = pltpu.make_async_copy(x_ref, blk(*me), local_sem)
        first = [copy(0, me, sibling, src=x_ref)]
        first += [copy(1 + j, me, (*chip, c), src=x_ref) for j, chip in enumerate(chips)]
        passed = [copy(4 + j, (*chip, c), sibling) for j, chip in enumerate(chips)]
        landed = [copy(1 + j, (*chip, c), me) for j, chip in enumerate(chips)]
        from_sibling = [copy(0, sibling, me)] + [copy(4 + j, (*chip, 1 - c), me) for j, chip in enumerate(chips)]
        return mine, first, passed, landed, from_sibling

    def start():
        mine, first, _, _, _ = plan()
        mine.start()
        for cp in first:
            cp.start()

    def finish():
        mine, first, passed, landed, from_sibling = plan()
        for cp, fwd in zip(landed, passed):
            cp.wait_recv()
            fwd.start()
        for cp in from_sibling:
            cp.wait_recv()
        for cp in first + passed:
            cp.wait_send()
        mine.wait()

    return start, finish


def _exchange_ops(scatter, s_ref, r_ref, send_sems, recv_sems, local_sem):
    def plan():
        x, y, c = _my_place()
        me = 4 * x + 2 * y + c
        mine = pltpu.make_async_copy(s_ref.at[me] if scatter else s_ref, r_ref.at[me], local_sem)
        sends, recvs = [], []
        for k in range(1, N_DEV):
            px, py, pc = x ^ ((k >> 2) & 1), y ^ ((k >> 1) & 1), c ^ (k & 1)
            peer = 4 * px + 2 * py + pc
            src = s_ref.at[peer] if scatter else s_ref
            for dst, out in ((r_ref.at[me], sends), (r_ref.at[peer], recvs)):
                out.append(pltpu.make_async_remote_copy(
                    src_ref=src, dst_ref=dst, send_sem=send_sems.at[k - 1], recv_sem=recv_sems.at[k - 1],
                    device_id=(px, py, pc), device_id_type=MESH))
        return mine, sends, recvs

    def start():
        mine, sends, _ = plan()
        mine.start()
        for cp in sends:
            cp.start()

    def finish():
        mine, sends, recvs = plan()
        for cp in recvs:
            cp.wait_recv()
        for cp in sends:
            cp.wait_send()
        mine.wait()

    return start, finish


def _pair_ops(s_ref, r_ref, send_sems, recv_sems):
    def plan():
        x, y, c = _my_place()
        return [pltpu.make_async_remote_copy(
            src_ref=s_ref.at[2 * q + 1 - c], dst_ref=r_ref.at[q], send_sem=send_sems.at[q], recv_sem=recv_sems.at[q],
            device_id=(x, y, 1 - c), device_id_type=MESH) for q in range(N_DEV // 2)]

    def start():
        for cp in plan():
            cp.start()

    def finish():
        copies = plan()
        for cp in copies:
            cp.wait_recv()
        for cp in copies:
            cp.wait_send()

    return start, finish


def _chips_ops(p_ref, r_ref, send_sems, recv_sems, local_sem):
    def plan():
        x, y, c = _my_place()
        chip = 2 * x + y
        mine = pltpu.make_async_copy(p_ref.at[chip], r_ref.at[chip], local_sem)
        sends, recvs = [], []
        for k in range(1, N_DEV // 2):
            px, py = x ^ (k >> 1), y ^ (k & 1)
            peer = 2 * px + py
            for dst, out in ((r_ref.at[chip], sends), (r_ref.at[peer], recvs)):
                out.append(pltpu.make_async_remote_copy(
                    src_ref=p_ref.at[peer], dst_ref=dst, send_sem=send_sems.at[k - 1], recv_sem=recv_sems.at[k - 1],
                    device_id=(px, py, c), device_id_type=MESH))
        return mine, sends, recvs

    def start():
        mine, sends, _ = plan()
        mine.start()
        for cp in sends:
            cp.start()

    def finish():
        mine, sends, recvs = plan()
        for cp in recvs:
            cp.wait_recv()
        for cp in sends:
            cp.wait_send()
        mine.wait()

    return start, finish


def _comm_ops(kind, src_ref, dst_ref, send_sems, recv_sems, local_sem):
    if kind == "gather":
        return _gather_ops(src_ref, dst_ref, send_sems, recv_sems, local_sem)
    if kind == "pair":
        return _pair_ops(src_ref, dst_ref, send_sems, recv_sems)
    if kind == "chips":
        return _chips_ops(src_ref, dst_ref, send_sems, recv_sems, local_sem)
    return _exchange_ops(False, src_ref, dst_ref, send_sems, recv_sems, local_sem)


def _comm_out_shape(kind, v):
    shape = {"pair": (N_DEV // 2,) + v.shape[1:], "chips": v.shape}.get(kind, (N_DEV,) + v.shape)
    return jax.ShapeDtypeStruct(shape, v.dtype)


def _comm_scratch(n_tasks):
    return [pltpu.SemaphoreType.DMA((N_DEV - 1,)), pltpu.SemaphoreType.DMA((N_DEV - 1,)), pltpu.SemaphoreType.DMA] * n_tasks


def _pair_sum(s, r, *, name):
    _, rows, cdim = s.shape
    tr = _pick(rows, (1024, 704, 512, 256, 128, 64, 32, 16, 8))
    rc = min(16, tr)

    def body(c_ref, s_ref, r_ref, o_ref):
        def chunk(ci, carry):
            sl = pl.ds(pl.multiple_of(ci * rc, rc), rc)
            o_ref[sl, :] = (s_ref[sl, :].astype(F32) + r_ref[sl, :].astype(F32)).astype(o_ref.dtype)
            return carry

        lax.fori_loop(0, tr // rc, chunk, 0)

    spec = pl.BlockSpec((None, tr, cdim), lambda q, i, c_ref: (q, i, 0))
    return pl.pallas_call(
        body, name=name,
        grid_spec=pltpu.PrefetchScalarGridSpec(
            num_scalar_prefetch=1, grid=(N_DEV // 2, rows // tr),
            in_specs=[pl.BlockSpec((None, None, tr, cdim), lambda q, i, c_ref: (q, c_ref[0], i, 0)), spec],
            out_specs=spec),
        out_shape=jax.ShapeDtypeStruct(r.shape, s.dtype),
        compiler_params=_cparams(("parallel", "parallel")),
    )(lax.axis_index("c").astype(jnp.int32).reshape(1), s.reshape(N_DEV // 2, 2, rows, cdim), r)


def _comm(kind, v, *, name):
    def body(s_ref, r_ref, send_sems, recv_sems, local_sem):
        start, finish = _comm_ops(kind, s_ref, r_ref, send_sems, recv_sems, local_sem)
        start()
        finish()

    return pl.pallas_call(
        body, name=name,
        out_shape=_comm_out_shape(kind, v),
        in_specs=[pl.BlockSpec(memory_space=pl.ANY)],
        out_specs=pl.BlockSpec(memory_space=pl.ANY),
        scratch_shapes=_comm_scratch(1),
    )(v)


class _Overlap:
    US_PER_MB = {"gather": 52.0, "pair": 1.0, "chips": 13.0, "bcast": 97.0}
    MM_FLOPS_PER_US = 6.0e8

    def __init__(self):
        self.queue, self.results, self.then = [], {}, {}

    def add(self, key, kind, v):
        self.queue.append((key, kind, v))

    def _cost(self, kind, v):
        return v.size * v.dtype.itemsize / 2 ** 20 * self.US_PER_MB[kind]

    def take(self, budget_us):
        taken, cum = [], 0.0
        while self.queue:
            cost = self._cost(*self.queue[0][1:])
            if taken and cum + cost > 1.25 * budget_us:
                break
            taken.append(self.queue.pop(0))
            cum += cost
        return taken

    def put(self, taken, res):
        for (key, _, _), r in zip(taken, res):
            self.results[key] = r
            if key in self.then:
                self.then.pop(key)(r)

    def carry(self, budget_us, fn):
        taken = self.take(budget_us)
        out, res = fn([(kind, v) for _, kind, v in taken])
        self.put(taken, res)
        return out

    def mm(self, a, b, *, mode, **kw):
        la, lb = _logical(a), _logical(b)
        budget = 2.0 * la[0] * la[1] * (lb[0] if mode == "nt" else lb[1]) / self.MM_FLOPS_PER_US

        def fn(comm):
            return _mm(a, b, mode=mode, comm=comm, **kw) if comm else (_mm(a, b, mode=mode, **kw), [])

        return self.carry(budget, fn)

    def get(self, key):
        while key not in self.results:
            keys = [k for k, _, _ in self.queue]
            task = self.queue.pop(keys.index(key if key in keys else "pair_" + key))
            self.put([task], [_comm(task[1], task[2], name=f"alone_{task[0]}")])
        return self.results[key]

    def weight(self, n):
        g = self.get(n)
        return g.reshape(-1, g.shape[2]) if n in ROW_SHARDED else g

    def grad(self, n, g):
        g = g if g.ndim == 3 else g.reshape(N_DEV, g.shape[0] // N_DEV, g.shape[1])
        self.add("pair_d_" + n, "pair", g)
        self.then["pair_d_" + n] = lambda r: self.add("d_" + n, "chips", _pair_sum(g, r, name=f"pair_sum_{n}"))


def _sel_tables(dest, ws, wp, tw):
    n_tiles = (int(dest.max()) + tw) // tw
    tbl = np.full((N_DEV, wp), -1, np.int32)
    for j in range(N_DEV):
        tbl[j, :ws] = dest[j * ws:(j + 1) * ws]
    by_tile = [sorted({j for j in range(N_DEV) if ((tbl[j] // tw) == t).any()}) for t in range(n_tiles)]
    by_shard = [sorted({int(t) for t in np.unique(tbl[j, :ws] // tw)}) for j in range(N_DEV)]

    def table(lists):
        width = max(len(v) for v in lists)
        idx = np.array([(v + [v[-1]] * width)[:width] if v else [0] * width for v in lists], np.int32)
        val = np.array([[1] * len(v) + [0] * (width - len(v)) for v in lists], np.int32)
        return idx.reshape(-1), val.reshape(-1), width

    return tbl[:, :, None], table(by_tile), table(by_shard)


def _sel_matrix(d_ref, t, wp, tw):
    cols = t * tw + lax.broadcasted_iota(jnp.int32, (wp, tw), 1)
    return (d_ref[...] == cols).astype(BF16)


def _win_unshard(g, tbl, idx, val, width, *, tw, padded, name):
    _, dm, wp = g.shape
    tm = _pick(dm, (1024, 512, 256, 128))

    def body(idx_ref, val_ref, g_ref, d_ref, o_ref, acc):
        t, s_ = pl.program_id(1), pl.program_id(2)

        @pl.when(s_ == 0)
        def _():
            acc[...] = jnp.zeros_like(acc)

        @pl.when(val_ref[t * width + s_] == 1)
        def _():
            acc[...] += _dot(g_ref[...], _sel_matrix(d_ref, t, wp, tw), NN)

        @pl.when(s_ == width - 1)
        def _():
            o_ref[...] = acc[...].astype(BF16)

    return pl.pallas_call(
        body, name=name,
        grid_spec=pltpu.PrefetchScalarGridSpec(
            num_scalar_prefetch=2, grid=(dm // tm, padded // tw, width),
            in_specs=[pl.BlockSpec((None, tm, wp), lambda i, t, s_, ix, vl: (ix[t * width + s_], i, 0)),
                      pl.BlockSpec((None, wp, 1), lambda i, t, s_, ix, vl: (ix[t * width + s_], 0, 0))],
            out_specs=pl.BlockSpec((tm, tw), lambda i, t, s_, ix, vl: (i, t)),
            scratch_shapes=[pltpu.VMEM((tm, tw), F32)]),
        out_shape=jax.ShapeDtypeStruct((dm, padded), BF16),
        compiler_params=_cparams(("parallel", "parallel", "arbitrary")),
    )(idx, val, g, tbl)


def _win_to_shards(dw, tbl, idx, val, width, *, tw, wp, name):
    dm = dw.shape[0]
    tm = _pick(dm, (1024, 512, 256, 128))

    def body(idx_ref, val_ref, w_ref, d_ref, o_ref, acc):
        j, s_ = pl.program_id(1), pl.program_id(2)

        @pl.when(s_ == 0)
        def _():
            acc[...] = jnp.zeros_like(acc)

        @pl.when(val_ref[j * width + s_] == 1)
        def _():
            acc[...] += _dot(w_ref[...], _sel_matrix(d_ref, idx_ref[j * width + s_], wp, tw), NT)

        @pl.when(s_ == width - 1)
        def _():
            o_ref[...] = acc[...].astype(BF16)

    return pl.pallas_call(
        body, name=name,
        grid_spec=pltpu.PrefetchScalarGridSpec(
            num_scalar_prefetch=2, grid=(dm // tm, N_DEV, width),
            in_specs=[pl.BlockSpec((tm, tw), lambda i, j, s_, ix, vl: (i, ix[j * width + s_])),
                      pl.BlockSpec((None, wp, 1), lambda i, j, s_, ix, vl: (j, 0, 0))],
            out_specs=pl.BlockSpec((None, tm, wp), lambda i, j, s_, ix, vl: (j, i, 0)),
            scratch_shapes=[pltpu.VMEM((tm, wp), F32)]),
        out_shape=jax.ShapeDtypeStruct((N_DEV, dm, wp), BF16),
        compiler_params=_cparams(("parallel", "parallel", "arbitrary")),
    )(idx, val, dw, tbl)


def _adamw(parts, w, m, v, *, name):
    r, cdim = w.shape
    n_parts = parts.shape[0]
    tr = _pick(r, (256, 128, 64, 32, 16, 8))
    rc = min(16, tr)
    c1 = 1.0 - ADAM_B1 ** ADAM_STEP
    c2 = 1.0 - ADAM_B2 ** ADAM_STEP

    def body(p_ref, w_ref, m_ref, v_ref, g_ref, d_ref, mo_ref, vo_ref):
        def chunk(ci, carry):
            sl = pl.ds(pl.multiple_of(ci * rc, rc), rc)
            g = p_ref[0, sl, :].astype(F32)
            for i in range(1, n_parts):
                g = g + p_ref[i, sl, :].astype(F32)
            mn = ADAM_B1 * m_ref[sl, :] + (1.0 - ADAM_B1) * g
            vn = ADAM_B2 * v_ref[sl, :] + (1.0 - ADAM_B2) * jnp.square(g)
            m_hat = mn / c1
            v_hat = vn / c2
            g_ref[sl, :] = g
            d_ref[sl, :] = -ADAM_LR * (m_hat / (jnp.sqrt(v_hat) + ADAM_EPS) + ADAM_WD * w_ref[sl, :])
            mo_ref[sl, :] = mn
            vo_ref[sl, :] = vn
            return carry

        lax.fori_loop(0, tr // rc, chunk, 0)

    spec = pl.BlockSpec((tr, cdim), lambda i: (i, 0))
    return pl.pallas_call(
        body, name=name, grid=(r // tr,),
        in_specs=[pl.BlockSpec((n_parts, tr, cdim), lambda i: (0, i, 0)), spec, spec, spec],
        out_specs=[spec] * 4,
        out_shape=[jax.ShapeDtypeStruct((r, cdim), F32)] * 4,
        compiler_params=_cparams(("parallel",)),
    )(parts, w, m, v)


def _pad_to(v, n):
    return v if v.shape[0] == n else jnp.concatenate([v, jnp.zeros((n - v.shape[0],), v.dtype)])


def _pad_cols(v, n):
    return v if v.shape[-1] == n else jnp.concatenate([v, jnp.zeros(v.shape[:-1] + (n - v.shape[-1],), v.dtype)], axis=-1)


def _pack(vs, cols, row_mult, dtype):
    offs, o = [], 0
    for v in vs:
        offs.append(o)
        o += v.size
    rows = -(-o // cols)
    rows = -(-rows // row_mult) * row_mult
    flat = jnp.concatenate([v.reshape(-1).astype(dtype) for v in vs])
    return _pad_to(flat, rows * cols).reshape(rows, cols), offs


SHARDED = ("ffn1_w_gate", "ffn1_w_up", "ffn1_w_down", "w_in", "w_merge_gate", "gla_gate_up", "w_branch_fox",
           "w_branch_gla", "w_out", "ffn2_w_gate", "ffn2_w_up", "ffn2_w_down", "w_ple_gate", "w_ple_proj")
ROW_SHARDED = ("ffn1_w_down", "w_out", "ffn2_w_down", "w_ple_gate")
REPLICATED = ("ffn1_norm", "mix_norm", "fox_forget_bias", "gla_gate_bias", "gla_head_norm", "b_merge_gate",
              "ffn2_norm", "ple_norm", "final_norm")
WEIGHTS = ("ffn1_norm", "ffn1_w_gate", "ffn1_w_up", "ffn1_w_down", "mix_norm", "w_in", "fox_forget_bias",
           "gla_gate_up", "gla_gate_bias", "gla_head_norm", "w_branch_fox", "w_branch_gla", "w_merge_gate",
           "b_merge_gate", "w_out", "ffn2_norm", "ffn2_w_gate", "ffn2_w_up", "ffn2_w_down", "ple_norm",
           "w_ple_proj", "w_ple_gate", "final_norm")


def kernel(x, p, ffn1_norm, ffn1_w_gate, ffn1_w_up, ffn1_w_down, mix_norm, w_in, fox_forget_bias, gla_gate_up, gla_gate_bias, gla_head_norm, w_branch_fox, w_branch_gla, w_merge_gate, b_merge_gate, w_out, ffn2_norm, ffn2_w_gate, ffn2_w_up, ffn2_w_down, ple_norm, w_ple_proj, w_ple_gate, final_norm, loss_target, m_ffn1_norm, m_ffn1_w_gate, m_ffn1_w_up, m_ffn1_w_down, m_mix_norm, m_w_in, m_fox_forget_bias, m_gla_gate_up, m_gla_gate_bias, m_gla_head_norm, m_w_branch_fox, m_w_branch_gla, m_w_merge_gate, m_b_merge_gate, m_w_out, m_ffn2_norm, m_ffn2_w_gate, m_ffn2_w_up, m_ffn2_w_down, m_ple_norm, m_w_ple_proj, m_w_ple_gate, m_final_norm, v_ffn1_norm, v_ffn1_w_gate, v_ffn1_w_up, v_ffn1_w_down, v_mix_norm, v_w_in, v_fox_forget_bias, v_gla_gate_up, v_gla_gate_bias, v_gla_head_norm, v_w_branch_fox, v_w_branch_gla, v_w_merge_gate, v_b_merge_gate, v_w_out, v_ffn2_norm, v_ffn2_w_gate, v_ffn2_w_up, v_ffn2_w_down, v_ple_norm, v_w_ple_proj, v_w_ple_gate, v_final_norm):
    args = dict(locals())
    wts = {n: args[n] for n in WEIGHTS}
    mom_m = {n: args["m_" + n] for n in WEIGHTS}
    mom_v = {n: args["v_" + n] for n in WEIGHTS}

    xs, ps, tgt = x[0], p[0, 0], loss_target[0]
    s, d = xs.shape
    fox_w = w_branch_fox.shape[1]
    gla_vw = w_branch_gla.shape[1]
    fox_heads = fox_w // HEAD_DIM
    gla_heads = gla_vw // GLA_VAL_DIM
    gla_kw = gla_heads * HEAD_DIM
    rank = gla_gate_up.shape[1]

    c_fl = 3 * fox_w
    o_gr, o_gq, o_gk = gla_vw, 2 * gla_vw, 2 * gla_vw + gla_kw
    o_fl = o_gk + gla_kw
    o_gd = o_fl + LANES
    rest_w = o_gd + LANES
    padded = c_fl + rest_w
    seg = [(c_fl, 0), (fox_heads, c_fl + o_fl), (gla_kw, c_fl + o_gq), (gla_kw, c_fl + o_gk), (gla_vw, c_fl),
           (gla_vw, c_fl + o_gr), (rank, c_fl + o_gd)]
    dest = np.concatenate([np.arange(w_, dtype=np.int32) + o_ for w_, o_ in seg])
    ws = w_in.shape[2]
    wp = -(-ws // LANES) * LANES
    tw = 256 if padded % 256 == 0 else LANES
    tbl, (t_idx, t_val, t_width), (s_idx, s_val, s_width) = _sel_tables(dest, ws, wp, tw)
    tbl = jnp.asarray(tbl)

    ov = _Overlap()
    for n in SHARDED:
        sh = wts[n][0].astype(BF16)
        ov.add(n, "gather", _pad_cols(sh, wp) if n == "w_in" else sh)
    fbias = _pad_cols(fox_forget_bias, LANES)
    bmg_f, bmg_g = b_merge_gate[:, :d], b_merge_gate[:, d:]
    ghn = jnp.tile(gla_head_norm, (1, gla_heads))

    h1, ffn1_saved = _ffn_fwd(xs, ffn1_norm, ov, tag="ffn1")
    u = _rms_fwd(h1, mix_norm, name="mix_norm")
    win = _win_unshard(ov.weight("w_in"), tbl, jnp.asarray(t_idx), jnp.asarray(t_val), t_width, tw=tw, padded=padded,
                       name="in_proj_unshard")
    win_fox, win_rest = win[:, :c_fl], win[:, c_fl:]
    zf = ov.mm(u, win_fox, mode="nn", name="in_proj_fox", out_dtype=BF16)
    zr = ov.mm(u, win_rest, mode="nn", name="in_proj_rest")
    gz = ov.mm(u, ov.weight("w_merge_gate"), mode="nn", name="merge_gate")
    gup = ov.weight("gla_gate_up").transpose(1, 0, 2).reshape(rank, gla_kw)
    gup = jnp.concatenate([gup, jnp.zeros((LANES - rank, gla_kw), BF16)], axis=0)

    log_f = _rowwise(lambda fl, b: _log_sigmoid(fl + b), [(zr, LANES, o_fl // LANES)], [fbias], [(LANES, F32)],
                     name="forget_gate")[0]
    f_cum = _cumsum(log_f[:, :fox_heads].T, reverse=False, name="forget_cumsum")
    f_rep = jnp.broadcast_to(f_cum.T[:, :, None], (s, fox_heads, HEAD_DIM)).reshape(s, fox_w)
    f_row = f_cum[:, None, :]
    attn_us = 2.0 * s * s * HEAD_DIM * fox_heads / _Overlap.MM_FLOPS_PER_US
    y_fox, y_fox_bf, lse = ov.carry(
        ATTN_FWD_MATMULS * attn_us, lambda comm: _fox_fwd(zf, f_rep, f_row, heads=fox_heads, name="fox_fwd", comm=comm))

    def decay_fn(gd, gupv, gb):
        return _log_sigmoid(_dot(gd, gupv, NN) + gb) * (1.0 / GLA_GATE_TAU)

    la = _rowwise(decay_fn, [(zr, LANES, o_gd // LANES)], [gup, gla_gate_bias], [(gla_kw, F32)], name="gla_decay", rc=128)[0]
    q_blk, k_blk = o_gq // HEAD_DIM, o_gk // HEAD_DIM
    o_gla, states = _gla_fwd(zr, la, heads=gla_heads, q_blk=q_blk, k_blk=k_blk, name="gla_fwd")

    def gla_out_fn(o, gr, g):
        outs = []
        for hh in range(gla_heads):
            sl = slice(hh * GLA_VAL_DIM, (hh + 1) * GLA_VAL_DIM)
            _, oh = _rms_parts(o[:, sl])
            outs.append(oh * g[:, sl] * _silu_parts(gr[:, sl])[0])
        return jnp.concatenate(outs, axis=1)

    y_gla = _rowwise(gla_out_fn, [o_gla, (zr, gla_vw, o_gr // gla_vw)], [ghn], [(gla_vw, BF16)], name="gla_out")[0]
    br_f = ov.mm(y_fox_bf, ov.weight("w_branch_fox"), mode="nn", name="branch_fox")
    br_g = ov.mm(y_gla, ov.weight("w_branch_gla"), mode="nn", name="branch_gla")

    def merge_fn(zf_, zg_, bf_, bg_, b1, b2):
        return _sigmoid(zf_ + b1) * bf_ + _sigmoid(zg_ + b2) * bg_

    merged = _rowwise(merge_fn, [(gz, d, 0), (gz, d, 1), br_f, br_g], [bmg_f, bmg_g], [(d, BF16)], name="merge")[0]
    h2 = ov.mm(merged, ov.weight("w_out"), mode="nn", name="out_proj", add=h1)
    h3, ffn2_saved = _ffn_fwd(h2, ffn2_norm, ov, tag="ffn2")
    n3 = _rms_fwd(h3, ple_norm, name="ple_norm")
    gl = ov.mm(n3, ov.weight("w_ple_gate"), mode="nn", name="ple_gate")
    pe = ov.mm(ps, ov.weight("w_ple_proj"), mode="nn", name="ple_proj")

    def head_fn(h3b, glb, peb, tb, gfin):
        pg = _sigmoid(glb)
        h4 = h3b + pg * peb
        r, xh = _rms_parts(h4)
        err = xh * gfin - tb
        dy = err * (1.0 / d)
        t = dy * gfin
        dh4 = r * (t - xh * jnp.mean(t * xh, axis=-1, keepdims=True))
        return dh4, dh4 * pg, dh4 * peb * pg * (1.0 - pg), _colsum(err * err), _colsum(dy * xh)

    dh4, dpe, dgl, loss_cols, d_final = _rowwise(
        head_fn, [h3, gl, pe, tgt], [final_norm.reshape(1, d)], [(d, F32), (d, BF16), (d, BF16)], [d, d], name="loss_head")
    loss = lax.psum(0.5 * jnp.sum(loss_cols) / d, AXES)

    grads = {"final_norm": d_final.reshape(d)}
    ov.grad("w_ple_proj", ov.mm(ps, dpe, mode="tn", name="ple_proj_dw", out_dtype=BF16, out_chunked=True))
    ov.grad("w_ple_gate", ov.mm(n3, dgl, mode="tn", name="ple_gate_dw", out_dtype=BF16))
    dn3 = ov.mm(dgl, ov.weight("w_ple_gate"), mode="nt", name="ple_gate_dx")
    dh3, dh3_bf, grads["ple_norm"] = _rms_bwd(h3, dn3, ple_norm, dh4, name="ple_norm_bwd")
    dh2, dh2_bf, grads["ffn2_norm"] = _ffn_bwd(h2, ffn2_norm, ov, ffn2_saved, dh3, dh3_bf, tag="ffn2")

    ov.grad("w_out", ov.mm(merged, dh2_bf, mode="tn", name="out_proj_dw", out_dtype=BF16))
    dmerged = ov.mm(dh2_bf, ov.weight("w_out"), mode="nt", name="out_proj_dx")

    def merge_bwd_fn(zf_, zg_, bf_, bg_, dm, b1, b2):
        sf, sg = _sigmoid(zf_ + b1), _sigmoid(zg_ + b2)
        dz = jnp.concatenate([dm * bf_ * sf * (1.0 - sf), dm * bg_ * sg * (1.0 - sg)], axis=1)
        return dm * sf, dm * sg, dz, _colsum(dz)

    dbr_f, dbr_g, dgz, grads["b_merge_gate"] = _rowwise(
        merge_bwd_fn, [(gz, d, 0), (gz, d, 1), br_f, br_g, dmerged], [bmg_f, bmg_g],
        [(d, BF16), (d, BF16), (2 * d, BF16)], [2 * d], name="merge_bwd")
    ov.grad("w_merge_gate", ov.mm(u, dgz, mode="tn", name="merge_gate_dw", out_dtype=BF16, out_chunked=True))
    ov.grad("w_branch_fox", ov.mm(y_fox_bf, dbr_f, mode="tn", name="branch_fox_dw", out_dtype=BF16, out_chunked=True))
    ov.grad("w_branch_gla", ov.mm(y_gla, dbr_g, mode="tn", name="branch_gla_dw", out_dtype=BF16, out_chunked=True))
    dy_fox = ov.mm(dbr_f, ov.weight("w_branch_fox"), mode="nt", name="branch_fox_dx")
    dy_gla = ov.mm(dbr_g, ov.weight("w_branch_gla"), mode="nt", name="branch_gla_dx")

    def gla_out_bwd_fn(o, gr, dy, g):
        dos, dgrs, dgs = [], [], []
        for hh in range(gla_heads):
            sl = slice(hh * GLA_VAL_DIM, (hh + 1) * GLA_VAL_DIM)
            r, oh = _rms_parts(o[:, sl])
            si, dsi = _silu_parts(gr[:, sl])
            don = dy[:, sl] * si
            dgrs.append(dy[:, sl] * oh * g[:, sl] * dsi)
            t = don * g[:, sl]
            dos.append(r * (t - oh * jnp.mean(t * oh, axis=-1, keepdims=True)))
            dgs.append(_colsum(don * oh))
        return jnp.concatenate(dos, axis=1), jnp.concatenate(dgrs, axis=1), jnp.concatenate(dgs, axis=1)

    do_gla, dgr, d_ghn = _rowwise(gla_out_bwd_fn, [o_gla, (zr, gla_vw, o_gr // gla_vw), dy_gla], [ghn],
                                  [(gla_vw, F32), (gla_vw, F32)], [gla_vw], name="gla_out_bwd")
    grads["gla_head_norm"] = d_ghn.reshape(gla_heads, GLA_VAL_DIM).sum(axis=0, keepdims=True)
    dgq, dgk, dgv, dla = _gla_bwd(zr, la, do_gla, states, heads=gla_heads, q_blk=q_blk, k_blk=k_blk, name="gla_bwd")

    def decay_bwd_fn(dl, gd, gupv, gb):
        pre = _dot(gd, gupv, NN) + gb
        dpre = dl * (1.0 / GLA_GATE_TAU) * _sigmoid(-pre)
        return _dot(dpre, gupv, NT), dpre, _colsum(dpre)

    dgd, dpre_bf, grads["gla_gate_bias"] = _rowwise(
        decay_bwd_fn, [dla, (zr, LANES, o_gd // LANES)], [gup, gla_gate_bias], [(LANES, F32), (gla_kw, BF16)], [gla_kw],
        name="gla_decay_bwd", rc=128)
    d_gup = ov.mm(zr[:, o_gd:o_gd + LANES], dpre_bf, mode="tn", name="gla_gate_up_dw")[:rank]
    ov.grad("gla_gate_up", d_gup.reshape(rank, N_DEV, gla_kw // N_DEV).transpose(1, 0, 2).astype(BF16))

    def delta_fn(dyv, yv):
        outs = []
        for hh in range(fox_heads):
            sl = slice(hh * HEAD_DIM, (hh + 1) * HEAD_DIM)
            outs.append(jnp.broadcast_to(jnp.sum(dyv[:, sl] * yv[:, sl], axis=-1, keepdims=True), (dyv.shape[0], HEAD_DIM)))
        return jnp.concatenate(outs, axis=1)

    delta = _rowwise(delta_fn, [dy_fox, y_fox], [], [(fox_w, F32)], name="fox_delta")[0]
    dfq, dfk, dfv, d_fcol, d_frow = ov.carry(
        ATTN_BWD_MATMULS * attn_us,
        lambda comm: _fox_bwd(zf, dy_fox, lse, delta, f_rep, f_row, heads=fox_heads, name="fox_bwd", comm=comm))
    d_fcum = d_fcol[:, ::HEAD_DIM].T + d_frow.reshape(fox_heads, s)
    d_logf = _cumsum(d_fcum, reverse=True, name="forget_cumsum_bwd")
    d_logf = _pad_cols(d_logf.T, LANES)

    def forget_bwd_fn(dl, fl, b):
        dfl_ = dl * _sigmoid(-(fl + b))
        return dfl_, _colsum(dfl_)

    dfl, d_fbias = _rowwise(forget_bwd_fn, [d_logf, (zr, LANES, o_fl // LANES)], [fbias], [(LANES, F32)], [LANES],
                            name="forget_gate_bwd")
    grads["fox_forget_bias"] = d_fbias[:, :fox_heads]

    dz = jnp.concatenate([dfq, dfk, dfv, dgv, dgr, dgq, dgk, dfl, dgd], axis=1).astype(BF16)
    dwin = ov.mm(u, dz, mode="tn", name="in_proj_dw", out_dtype=BF16)
    ov.grad("w_in", _win_to_shards(dwin, tbl, jnp.asarray(s_idx), jnp.asarray(s_val), s_width, tw=tw, wp=wp,
                                   name="in_proj_dw_shards"))
    du = ov.mm(dgz, ov.weight("w_merge_gate"), mode="nt", name="merge_gate_dx")
    du = ov.mm(dz, win, mode="nt", name="in_proj_dx", add=du)
    dh1, dh1_bf, grads["mix_norm"] = _rms_bwd(h1, du, mix_norm, dh2, name="mix_norm_bwd")
    dx, _, grads["ffn1_norm"] = _ffn_bwd(xs, ffn1_norm, ov, ffn1_saved, dh1, dh1_bf, tag="ffn1")

    outs = {}
    for n in SHARDED:
        parts = ov.get("d_" + n)
        state = [_pad_cols(t_[n][0], parts.shape[2]) for t_ in (wts, mom_m, mom_v)]
        res4 = _adamw(parts, *state, name=f"adamw_{n}")
        for kind, r_ in zip(("grad", "delta", "new_m", "new_v"), res4):
            outs[f"{kind}_{n}"] = r_[:, :wts[n].shape[2]][None]

    send_small, small_offs = _pack([grads[n] for n in REPLICATED], LANES, 8, F32)
    recv_small = _comm("bcast", send_small, name="exchange_replicated")
    w_sm, _ = _pack([wts[n] for n in REPLICATED], LANES, 8, F32)
    m_sm, _ = _pack([mom_m[n] for n in REPLICATED], LANES, 8, F32)
    v_sm, _ = _pack([mom_v[n] for n in REPLICATED], LANES, 8, F32)
    small = _adamw(recv_small, w_sm, m_sm, v_sm, name="adamw_replicated")
    for kind, buf in zip(("grad", "delta", "new_m", "new_v"), small):
        fs = buf.reshape(-1)
        for n, o in zip(REPLICATED, small_offs):
            outs[f"{kind}_{n}"] = fs[o:o + wts[n].size].reshape(wts[n].shape)

    res = [loss, dx[None]]
    for kind in ("grad", "delta", "new_m", "new_v"):
        res += [outs[f"{kind}_{n}"] for n in WEIGHTS]
    return tuple(res)
```

```python
import functools

import jax
import jax.numpy as jnp
import numpy as np
from jax import lax
from jax.experimental import pallas as pl
from jax.experimental.pallas import tpu as pltpu

F32 = jnp.float32
BF16 = jnp.bfloat16
MESH = pl.DeviceIdType.MESH
AXES = ("x", "y", "c")
N_DEV = 8

VMEM_LIMIT_BYTES = 56 * 1024 * 1024
MM_BLOCK_BUDGET_BYTES = 40 * 1024 * 1024
LANES = 128

EPS = 1e-6
HEAD_DIM = 128
GLA_VAL_DIM = 256
GLA_CHUNK = 64
GLA_GATE_TAU = 16.0
ADAM_LR, ADAM_B1, ADAM_B2, ADAM_EPS, ADAM_WD, ADAM_STEP = 0.001, 0.9, 0.999, 1e-08, 0.01, 10

ATTN_FWD_MATMULS = 7.0
ATTN_BWD_MATMULS = 7.0

HIGHEST = lax.Precision.HIGHEST
NN = (((1,), (0,)), ((), ()))
NT = (((1,), (1,)), ((), ()))
TN = (((0,), (0,)), ((), ()))


def _cparams(sem):
    return pltpu.CompilerParams(dimension_semantics=sem, vmem_limit_bytes=VMEM_LIMIT_BYTES)


def _pick(dim, cands):
    for c in cands:
        if dim % c == 0:
            return c
    return dim


def _bf(v):
    return v if v.dtype == BF16 else v.astype(BF16)


def _dot(a, b, dims):
    return lax.dot_general(_bf(a), _bf(b), dims, preferred_element_type=F32)


def _sigmoid(v):
    return 1.0 / (1.0 + jnp.exp(-v))


def _log_sigmoid(v):
    return jnp.minimum(v, 0.0) - jnp.log(1.0 + jnp.exp(-jnp.abs(v)))


def _logical(v):
    return (v.shape[1], v.shape[0] * v.shape[2]) if v.ndim == 3 else v.shape


def _mm(a, b, *, mode, name, out_dtype=F32, add=None, scale=1.0, out_chunked=False, comm=()):
    la, lb = _logical(a), _logical(b)
    if mode == "nn":
        (m, k), (k2, n) = la, lb
        a_minor, b_minor = "k", "n"
    elif mode == "nt":
        (m, k), (n, k2) = la, lb
        a_minor, b_minor = "k", "k"
    else:
        (k, m), (k2, n) = la, lb
        a_minor, b_minor = "m", "n"
    assert k == k2, (name, a.shape, b.shape)
    forced = {}
    for v, minor in ((a, a_minor), (b, b_minor)):
        if v.ndim == 3:
            assert forced.get(minor, v.shape[2]) == v.shape[2], name
            forced[minor] = v.shape[2]
    if out_chunked:
        assert forced.get("n", n // N_DEV) == n // N_DEV, name
        forced["n"] = n // N_DEV
    tm = forced.get("m") or _pick(m, (1024, 512, 256, 128))
    tn = forced.get("n") or _pick(n, (1408, 1280, 1024, 512, 256, 128))

    def blocks_bytes(tm_, tn_, t):
        io = tm_ * t * a.dtype.itemsize + t * tn_ * b.dtype.itemsize
        return 2 * (io + tm_ * tn_ * (jnp.dtype(out_dtype).itemsize + (4 if add is not None else 0))) + tm_ * tn_ * 4

    kc = forced.get("k")
    a_k_minor, b_k_minor = mode != "tn", mode == "nt"
    if kc:
        aligned = all(v.ndim == 3 or kc % (LANES if minor else 16) == 0 for v, minor in ((a, a_k_minor), (b, b_k_minor)))
        fits = [(tm_, tn_) for tm_, tn_ in ((tm, tn), (512, tn), (512, 512))
                if m % tm_ == 0 and n % tn_ == 0 and forced.get("m", tm_) == tm_ and forced.get("n", tn_) == tn_
                and blocks_bytes(tm_, tn_, k) <= MM_BLOCK_BUDGET_BYTES]
        if aligned and fits:
            (tm, tn), tk = fits[0], k
        else:
            tk, kc = kc, None
    else:
        tk = k if blocks_bytes(tm, tn, k) <= MM_BLOCK_BUDGET_BYTES else _pick(k, (512, 640, 256, 128))
    nk = k // tk
    dims = {"nn": NN, "nt": NT, "tn": TN}[mode]
    gi, gj, gk = (lambda i, j, kk: i), (lambda i, j, kk: j), (lambda i, j, kk: kk)

    def spec(v, t_major, t_minor, g_major, g_minor, all_chunks=False):
        if v.ndim == 3:
            if all_chunks:
                return pl.BlockSpec((N_DEV, t_major, v.shape[2]), lambda i, j, kk: (0, g_major(i, j, kk), 0))
            return pl.BlockSpec((None, t_major, v.shape[2]), lambda i, j, kk: (g_minor(i, j, kk), g_major(i, j, kk), 0))
        return pl.BlockSpec((t_major, t_minor), lambda i, j, kk: (g_major(i, j, kk), g_minor(i, j, kk)))

    a_spec = spec(a, tk, tm, gk, gi) if mode == "tn" else spec(a, tm, tk, gi, gk, bool(kc))
    b_spec = spec(b, tn, tk, gj, gk, bool(kc)) if mode == "nt" else spec(b, tk, tn, gk, gj)

    def k_chunk(ref, minor, c_):
        if len(ref.shape) == 3:
            return ref[c_]
        return ref[:, c_ * kc:(c_ + 1) * kc] if minor else ref[c_ * kc:(c_ + 1) * kc, :]
    if out_chunked:
        o_spec = pl.BlockSpec((None, tm, tn), lambda i, j, kk: (j, i, 0))
        out_shape = jax.ShapeDtypeStruct((N_DEV, m, tn), out_dtype)
    else:
        o_spec = pl.BlockSpec((tm, tn), lambda i, j, kk: (i, j))
        out_shape = jax.ShapeDtypeStruct((m, n), out_dtype)
    has_add = add is not None
    assert not (has_add and out_chunked), name

    def body(*refs):
        a_ref, b_ref = refs[0], refs[1]
        add_ref = refs[2] if has_add else None
        o_ref = refs[3 if has_add else 2]
        kk = pl.program_id(2)

        def finish(r):
            if scale != 1.0:
                r = r * scale
            if has_add:
                r = r + add_ref[...]
            o_ref[...] = r.astype(o_ref.dtype)

        if kc:
            r = _dot(k_chunk(a_ref, a_k_minor, 0), k_chunk(b_ref, b_k_minor, 0), dims)
            for c_ in range(1, k // kc):
                r = r + _dot(k_chunk(a_ref, a_k_minor, c_), k_chunk(b_ref, b_k_minor, c_), dims)
            finish(r)
        elif nk == 1:
            finish(_dot(a_ref[...], b_ref[...], dims))
        else:
            acc_ref = refs[-1]

            @pl.when(kk == 0)
            def _():
                acc_ref[...] = jnp.zeros_like(acc_ref)

            acc_ref[...] += _dot(a_ref[...], b_ref[...], dims)

            @pl.when(kk == nk - 1)
            def _():
                finish(acc_ref[...])

    res, carried = _call(
        body, name=name, grid=(m // tm, n // tn, nk),
        in_specs=[a_spec, b_spec] + ([o_spec] if has_add else []), out_specs=[o_spec], out_shape=[out_shape],
        scratch_shapes=[pltpu.VMEM((tm, tn), F32)] if nk > 1 else [],
        semantics=("parallel", "parallel", "arbitrary"), operands=[a, b] + ([add] if has_add else []), comm=comm)
    return (res[0], carried) if comm else res[0]


def _call(body, *, name, grid, in_specs, out_specs, out_shape, scratch_shapes, semantics, operands, comm=()):
    n_in, n_out, n_scr, n = len(in_specs), len(out_specs), len(scratch_shapes), len(comm)
    if not comm:
        res = pl.pallas_call(body, name=name, grid=grid, in_specs=in_specs, out_specs=out_specs, out_shape=out_shape,
                             scratch_shapes=scratch_shapes, compiler_params=_cparams(semantics))(*operands)
        return res, []

    def carrying(*refs):
        ins, c_in = refs[:n_in], refs[n_in:n_in + n]
        outs, c_out = refs[n_in + n:n_in + n + n_out], refs[n_in + n + n_out:n_in + 2 * n + n_out]
        scratch, sems = refs[n_in + 2 * n + n_out:][:n_scr], refs[n_in + 2 * n + n_out + n_scr:]
        tasks = [_comm_ops(kind, c_in[t], c_out[t], *sems[3 * t:3 * t + 3]) for t, (kind, _) in enumerate(comm)]
        ids = [pl.program_id(ax) for ax in range(len(grid))]

        @pl.when(functools.reduce(lambda p, q: p & q, [i == 0 for i in ids]))
        def _():
            for start, _ in tasks:
                start()

        body(*ins, *outs, *scratch)

        @pl.when(functools.reduce(lambda p, q: p & q, [i == g - 1 for i, g in zip(ids, grid)]))
        def _():
            for _, finish in tasks:
                finish()

    any_spec = pl.BlockSpec(memory_space=pl.ANY)
    res = pl.pallas_call(
        carrying, name=name, grid=grid,
        in_specs=list(in_specs) + [any_spec] * n, out_specs=list(out_specs) + [any_spec] * n,
        out_shape=list(out_shape) + [_comm_out_shape(kind, v) for kind, v in comm],
        scratch_shapes=list(scratch_shapes) + _comm_scratch(n),
        compiler_params=_cparams(("arbitrary",) * len(grid)),
    )(*operands, *[v for _, v in comm])
    return res[:n_out], res[n_out:]


def _rowwise(fn, rows, consts, outs, accs=(), *, name, rc=None):
    rows = [r if isinstance(r, tuple) else (r, r.shape[1], 0) for r in rows]
    m = rows[0][0].shape[0]
    widths = [w for _, w, _ in rows] + [n for n, _ in outs]
    row_bytes = sum(w * r.dtype.itemsize for r, w, _ in rows) + sum(n * jnp.dtype(d).itemsize for n, d in outs)
    tm = 1024
    while tm > 16 and (m % tm or 2 * tm * row_bytes > 24 * 1024 * 1024):
        tm //= 2
    if m % tm:
        tm = m
    if rc is None:
        rc = 16
        while rc * 2 <= tm and rc * 2 * max(widths) <= 32768:
            rc *= 2
    rc = min(rc, tm)
    nr, nc, no = len(rows), len(consts), len(outs)

    def body(*refs):
        in_refs, c_refs = refs[:nr], refs[nr:nr + nc]
        o_refs, a_refs = refs[nr + nc:nr + nc + no], refs[nr + nc + no:]

        @pl.when(pl.program_id(0) == 0)
        def _():
            for r in a_refs:
                r[...] = jnp.zeros_like(r)

        cvals = [c[...] for c in c_refs]

        def chunk(ci, carry):
            sl = pl.ds(pl.multiple_of(ci * rc, rc), rc)
            res = fn(*[r[sl, :] for r in in_refs], *cvals)
            if not isinstance(res, (tuple, list)):
                res = (res,)
            for r, v in zip(o_refs, res[:no]):
                r[sl, :] = v.astype(r.dtype)
            for r, v in zip(a_refs, res[no:]):
                r[...] += v
            return carry

        lax.fori_loop(0, tm // rc, chunk, 0)

    in_specs = [pl.BlockSpec((tm, w), functools.partial(lambda i, cb: (i, cb), cb=cb)) for _, w, cb in rows]
    in_specs += [pl.BlockSpec(c.shape, lambda i: (0, 0)) for c in consts]
    out_specs = [pl.BlockSpec((tm, n), lambda i: (i, 0)) for n, _ in outs]
    out_specs += [pl.BlockSpec((1, n), lambda i: (0, 0)) for n in accs]
    out_shape = [jax.ShapeDtypeStruct((m, n), d) for n, d in outs] + [jax.ShapeDtypeStruct((1, n), F32) for n in accs]
    res = pl.pallas_call(
        body, name=name, grid=(m // tm,),
        in_specs=in_specs, out_specs=out_specs, out_shape=out_shape,
        compiler_params=_cparams(("arbitrary",)),
    )(*[r for r, _, _ in rows], *consts)
    return res


def _colsum(v):
    return jnp.sum(v, axis=0, keepdims=True)


def _rms_parts(xv):
    r = lax.rsqrt(jnp.mean(xv * xv, axis=-1, keepdims=True) + EPS)
    return r, xv * r


def _rms_fwd(xv, g, *, name):
    d = xv.shape[1]

    def fn(xb, gb):
        _, xh = _rms_parts(xb)
        return xh * gb

    return _rowwise(fn, [xv], [g], [(d, BF16)], name=name)[0]


def _rms_bwd(xv, dy, g, add, *, name):
    d = xv.shape[1]

    def fn(xb, dyb, addb, gb):
        r, xh = _rms_parts(xb)
        t = dyb * gb
        dx = r * (t - xh * jnp.mean(t * xh, axis=-1, keepdims=True)) + addb
        return dx, dx, _colsum(dyb * xh)

    return _rowwise(fn, [xv, dy, add], [g], [(d, F32), (d, BF16)], [d], name=name)


def _silu_parts(a):
    sg = _sigmoid(a)
    return a * sg, sg * (1.0 + a * (1.0 - sg))


def _rows(v):
    return v.reshape(v.shape[0] * v.shape[1], v.shape[2])


def _ffn_fwd(h, g, ov, *, tag):
    s = h.shape[0]
    n = _rms_fwd(h, g, name=f"{tag}_norm")
    a = ov.mm(n, ov.weight(f"{tag}_w_gate"), mode="nn", name=f"{tag}_gate", out_dtype=BF16, out_chunked=True)
    b = ov.mm(n, ov.weight(f"{tag}_w_up"), mode="nn", name=f"{tag}_up", out_dtype=BF16, out_chunked=True)
    c = a.shape[2]
    hm = _rowwise(lambda av, bv: _silu_parts(av.astype(F32))[0] * bv.astype(F32), [_rows(a), _rows(b)], [], [(c, BF16)],
                  name=f"{tag}_act")[0]
    hm = hm.reshape(N_DEV, s, c)
    out = ov.mm(hm, ov.weight(f"{tag}_w_down"), mode="nn", name=f"{tag}_down", add=h, scale=0.5)
    return out, (n, a, b, hm)


def _ffn_bwd(h, g, ov, saved, dout, dout_bf, *, tag):
    n, a, b, hm = saved
    wg, wu, wd = ov.weight(f"{tag}_w_gate"), ov.weight(f"{tag}_w_up"), ov.weight(f"{tag}_w_down")
    s, c = h.shape[0], wg.shape[2]
    d_wd = ov.mm(hm, dout_bf, mode="tn", name=f"{tag}_down_dw", scale=0.5, out_dtype=BF16)
    ov.grad(f"{tag}_w_down", d_wd)
    dhm = ov.mm(dout_bf, wd, mode="nt", name=f"{tag}_down_dx", scale=0.5, out_dtype=BF16, out_chunked=True)

    def act_bwd(av, bv, dv):
        av, bv, dv = av.astype(F32), bv.astype(F32), dv.astype(F32)
        si, dsi = _silu_parts(av)
        return dv * bv * dsi, dv * si

    da, db = _rowwise(act_bwd, [_rows(a), _rows(b), _rows(dhm)], [], [(c, BF16), (c, BF16)], name=f"{tag}_act_bwd")
    da, db = da.reshape(N_DEV, s, c), db.reshape(N_DEV, s, c)
    ov.grad(f"{tag}_w_gate", ov.mm(n, da, mode="tn", name=f"{tag}_gate_dw", out_dtype=BF16, out_chunked=True))
    ov.grad(f"{tag}_w_up", ov.mm(n, db, mode="tn", name=f"{tag}_up_dw", out_dtype=BF16, out_chunked=True))
    dn = ov.mm(da, wg, mode="nt", name=f"{tag}_gate_dx")
    dn = ov.mm(db, wu, mode="nt", name=f"{tag}_up_dx", add=dn)
    dh, dh_bf, dg = _rms_bwd(h, dn, g, dout, name=f"{tag}_norm_bwd")
    return dh, dh_bf, dg


def _cumsum(xv, *, reverse, name):
    h, s = xv.shape
    t = _pick(s, (512, 256, 128))
    nb = s // t

    def blk(j):
        return (0, nb - 1 - j) if reverse else (0, j)

    def body(x_ref, o_ref, carry):
        @pl.when(pl.program_id(0) == 0)
        def _():
            carry[...] = jnp.zeros_like(carry)

        i0 = lax.broadcasted_iota(jnp.int32, (t, t), 0)
        i1 = lax.broadcasted_iota(jnp.int32, (t, t), 1)
        tri = ((i0 >= i1) if reverse else (i0 <= i1)).astype(F32)
        xb = x_ref[...]
        o_ref[...] = jnp.dot(xb, tri, precision=HIGHEST, preferred_element_type=F32) + carry[...]
        carry[...] += jnp.sum(xb, axis=1, keepdims=True)

    return pl.pallas_call(
        body, name=name, grid=(nb,),
        in_specs=[pl.BlockSpec((h, t), blk)], out_specs=pl.BlockSpec((h, t), blk),
        out_shape=jax.ShapeDtypeStruct((h, s), F32),
        scratch_shapes=[pltpu.VMEM((h, 1), F32)],
        compiler_params=_cparams(("arbitrary",)),
    )(xv)


def _fox_tiles(s):
    t = _pick(s, (512, 256, 128))
    return t, t


def _causal(sc):
    t = sc.shape[0]
    keep = lax.broadcasted_iota(jnp.int32, (t, t), 1) <= lax.broadcasted_iota(jnp.int32, (t, t), 0)
    return jnp.where(keep, sc, -jnp.inf)


ATTN_HEADS_PER_STEP = 2


def _head_cols(hh):
    return slice(hh * HEAD_DIM, (hh + 1) * HEAD_DIM)


def _across(rowstat, width):
    return jnp.tile(rowstat, (1, width // HEAD_DIM))


def _fox_fwd(zf, f_rep, f_row, *, heads, name, comm=()):
    s = zf.shape[0]
    tq, tk = _fox_tiles(s)
    assert tq == tk
    nq, nk = s // tq, s // tk
    hp = ATTN_HEADS_PER_STEP if heads % ATTN_HEADS_PER_STEP == 0 else 1
    wb = hp * HEAD_DIM
    scale = HEAD_DIM ** -0.5
    w = heads * HEAD_DIM

    def body(q_ref, k_ref, v_ref, fq_ref, fk_ref, o32_ref, o16_ref, lse_ref, m_sc, l_sc, acc_sc):
        i, j = pl.program_id(1), pl.program_id(2)

        @pl.when(j == 0)
        def _():
            m_sc[...] = jnp.full_like(m_sc, -jnp.inf)
            l_sc[...] = jnp.zeros_like(l_sc)
            acc_sc[...] = jnp.zeros_like(acc_sc)

        def step(diagonal):
            for hh in range(hp):
                cols = _head_cols(hh)
                sc = _dot(q_ref[:, cols], k_ref[:, cols], NT) * scale + _across(fq_ref[:, cols], tk) - fk_ref[hh]
                if diagonal:
                    sc = _causal(sc)
                m_old = m_sc[hh]
                m_new = jnp.maximum(m_old, jnp.max(sc, axis=-1, keepdims=True))
                alpha = jnp.exp(m_old - m_new)
                pr = jnp.exp(sc - _across(m_new, tk))
                l_sc[hh] = alpha * l_sc[hh] + jnp.sum(pr, axis=-1, keepdims=True)
                acc_sc[hh] = alpha * acc_sc[hh] + _dot(pr, v_ref[:, cols], NN)
                m_sc[hh] = m_new

        @pl.when(j < i)
        def _():
            step(False)

        @pl.when(j == i)
        def _():
            step(True)

        @pl.when(j == nk - 1)
        def _():
            for hh in range(hp):
                cols = _head_cols(hh)
                o = acc_sc[hh] / l_sc[hh]
                o32_ref[:, cols] = o
                o16_ref[:, cols] = o.astype(BF16)
                lse_ref[:, cols] = m_sc[hh] + jnp.log(l_sc[hh])

    def kv_blk(off):
        return lambda h, i, j: (jnp.minimum(j, i), off + h)

    o_spec = pl.BlockSpec((tq, wb), lambda h, i, j: (i, h))
    stat = pltpu.VMEM((hp, tq, HEAD_DIM), F32)
    return _call(
        body, name=name, grid=(heads // hp, nq, nk),
        in_specs=[
            o_spec,
            pl.BlockSpec((tk, wb), kv_blk(heads // hp)),
            pl.BlockSpec((tk, wb), kv_blk(2 * heads // hp)),
            o_spec,
            pl.BlockSpec((hp, 1, tk), lambda h, i, j: (h, 0, jnp.minimum(j, i))),
        ],
        out_specs=[o_spec, o_spec, o_spec],
        out_shape=[jax.ShapeDtypeStruct((s, w), F32), jax.ShapeDtypeStruct((s, w), BF16),
                   jax.ShapeDtypeStruct((s, w), F32)],
        scratch_shapes=[stat, stat, stat],
        semantics=("parallel", "parallel", "arbitrary"), operands=[zf, zf, zf, f_rep, f_row], comm=comm)


def _fox_bwd(zf, do, lse, delta, f_rep, f_row, *, heads, name, comm=()):
    s = zf.shape[0]
    tq, tk = _fox_tiles(s)
    assert tq == tk
    nq, nk = s // tq, s // tk
    hp = ATTN_HEADS_PER_STEP if heads % ATTN_HEADS_PER_STEP == 0 else 1
    wb = hp * HEAD_DIM
    scale = HEAD_DIM ** -0.5
    w = heads * HEAD_DIM

    def body(q_ref, k_ref, v_ref, do_ref, lse_ref, dl_ref, fq_ref, fk_ref, dq_ref, dk_ref, dv_ref, dfq_ref, dfk_ref):
        j, i = pl.program_id(1), pl.program_id(2)

        @pl.when((j == 0) & (i == 0))
        def _():
            dq_ref[...] = jnp.zeros_like(dq_ref)
            dfq_ref[...] = jnp.zeros_like(dfq_ref)

        @pl.when(i == 0)
        def _():
            dk_ref[...] = jnp.zeros_like(dk_ref)
            dv_ref[...] = jnp.zeros_like(dv_ref)
            dfk_ref[...] = jnp.zeros_like(dfk_ref)

        def step(diagonal):
            rows = pl.ds(pl.multiple_of(i * tq, tq), tq)
            for hh in range(hp):
                cols = _head_cols(hh)
                q, k, v = q_ref[:, cols], k_ref[:, cols], v_ref[:, cols]
                dob = do_ref[:, cols].astype(BF16)
                sc = _dot(q, k, NT) * scale + _across(fq_ref[:, cols], tk) - fk_ref[hh]
                if diagonal:
                    sc = _causal(sc)
                pr = jnp.exp(sc - _across(lse_ref[:, cols], tk))
                dv_ref[:, cols] += _dot(pr, dob, TN)
                dp = _dot(dob, v, NT)
                ds = pr * (dp - _across(dl_ref[:, cols], tk))
                dsb = ds.astype(BF16)
                dk_ref[:, cols] += _dot(dsb, q, TN) * scale
                dq_ref[rows, cols] += _dot(dsb, k, NN) * scale
                dfq_ref[rows, cols] += jnp.broadcast_to(jnp.sum(ds, axis=1, keepdims=True), (tq, HEAD_DIM))
                dfk_ref[hh] -= jnp.sum(ds, axis=0, keepdims=True)

        @pl.when(i > j)
        def _():
            step(False)

        @pl.when(i == j)
        def _():
            step(True)

    q_spec = pl.BlockSpec((tq, wb), lambda h, j, i: (jnp.maximum(i, j), h))
    k_spec = pl.BlockSpec((tk, wb), lambda h, j, i: (j, h))
    whole = pl.BlockSpec((s, wb), lambda h, j, i: (0, h))
    row_spec = pl.BlockSpec((hp, 1, tk), lambda h, j, i: (h, 0, j))
    return _call(
        body, name=name, grid=(heads // hp, nk, nq),
        in_specs=[
            q_spec,
            pl.BlockSpec((tk, wb), lambda h, j, i: (j, heads // hp + h)),
            pl.BlockSpec((tk, wb), lambda h, j, i: (j, 2 * heads // hp + h)),
            q_spec, q_spec, q_spec, q_spec, row_spec,
        ],
        out_specs=[whole, k_spec, k_spec, whole, row_spec],
        out_shape=[jax.ShapeDtypeStruct((s, w), F32), jax.ShapeDtypeStruct((s, w), F32),
                   jax.ShapeDtypeStruct((s, w), F32), jax.ShapeDtypeStruct((s, w), F32),
                   jax.ShapeDtypeStruct((heads, 1, s), F32)],
        scratch_shapes=[], semantics=("parallel", "arbitrary", "arbitrary"),
        operands=[zf, zf, zf, do, lse, delta, f_rep, f_row], comm=comm)


def _gla_rows(s):
    return _pick(s, (256, 128, 64))


def _gla_chunk_terms(la_c, tri):
    a_cum = jnp.dot(tri, la_c, precision=HIGHEST, preferred_element_type=F32)
    a_tot = jnp.sum(la_c, axis=0, keepdims=True)
    return jnp.exp(a_tot - a_cum), jnp.exp(a_tot)


def _gla_fwd(zr, la, *, heads, q_blk, k_blk, name):
    s = zr.shape[0]
    c = GLA_CHUNK
    rows = _gla_rows(s)
    nsteps, ncs = s // rows, rows // c
    scale = HEAD_DIM ** -0.5

    def body(q_ref, k_ref, v_ref, la_ref, o_ref, st_ref, state):
        @pl.when(pl.program_id(1) == 0)
        def _():
            state[...] = jnp.zeros_like(state)

        tri = (lax.broadcasted_iota(jnp.int32, (c, c), 0) >= lax.broadcasted_iota(jnp.int32, (c, c), 1)).astype(F32)
        for t in range(ncs):
            sl = slice(t * c, (t + 1) * c)
            dec, e_tot = _gla_chunk_terms(la_ref[sl, :], tri)
            kd = k_ref[sl, :] * dec
            st_ref[t] = state[...]
            new = state[...] * e_tot + _dot(v_ref[sl, :], kd, TN)
            state[...] = new
            o_ref[sl, :] = _dot(q_ref[sl, :] * scale, new, NT)

    return pl.pallas_call(
        body, name=name, grid=(heads, nsteps),
        in_specs=[
            pl.BlockSpec((rows, HEAD_DIM), lambda h, i: (i, q_blk + h)),
            pl.BlockSpec((rows, HEAD_DIM), lambda h, i: (i, k_blk + h)),
            pl.BlockSpec((rows, GLA_VAL_DIM), lambda h, i: (i, h)),
            pl.BlockSpec((rows, HEAD_DIM), lambda h, i: (i, h)),
        ],
        out_specs=[
            pl.BlockSpec((rows, GLA_VAL_DIM), lambda h, i: (i, h)),
            pl.BlockSpec((None, ncs, GLA_VAL_DIM, HEAD_DIM), lambda h, i: (h, i, 0, 0)),
        ],
        out_shape=[jax.ShapeDtypeStruct((s, heads * GLA_VAL_DIM), F32),
                   jax.ShapeDtypeStruct((heads, s // c, GLA_VAL_DIM, HEAD_DIM), F32)],
        scratch_shapes=[pltpu.VMEM((GLA_VAL_DIM, HEAD_DIM), F32)],
        compiler_params=_cparams(("parallel", "arbitrary")),
    )(zr, zr, zr, la)


def _gla_bwd(zr, la, do, states, *, heads, q_blk, k_blk, name):
    s = zr.shape[0]
    c = GLA_CHUNK
    rows = _gla_rows(s)
    nsteps, ncs = s // rows, rows // c
    scale = HEAD_DIM ** -0.5

    def body(q_ref, k_ref, v_ref, la_ref, do_ref, st_ref, dq_ref, dk_ref, dv_ref, dla_ref, dstate):
        @pl.when(pl.program_id(1) == 0)
        def _():
            dstate[...] = jnp.zeros_like(dstate)

        i0 = lax.broadcasted_iota(jnp.int32, (c, c), 0)
        i1 = lax.broadcasted_iota(jnp.int32, (c, c), 1)
        tri = (i0 >= i1).astype(F32)
        strict = (i0 > i1).astype(F32)
        for t in reversed(range(ncs)):
            sl = slice(t * c, (t + 1) * c)
            dec, e_tot = _gla_chunk_terms(la_ref[sl, :], tri)
            kd = k_ref[sl, :] * dec
            kdb = kd.astype(BF16)
            vb = v_ref[sl, :].astype(BF16)
            dob = do_ref[sl, :].astype(BF16)
            prev = st_ref[t]
            cur = prev * e_tot + _dot(vb, kdb, TN)
            d_cur = dstate[...] + _dot(dob, q_ref[sl, :] * scale, TN)
            d_cur_b = d_cur.astype(BF16)
            dq_ref[sl, :] = _dot(dob, cur, NN) * scale
            dv_ref[sl, :] = _dot(kdb, d_cur_b, NT)
            dkd = _dot(vb, d_cur_b, NN)
            d_tot = e_tot * jnp.sum(d_cur * prev, axis=0, keepdims=True)
            dk_ref[sl, :] = dkd * dec
            dla_ref[sl, :] = d_tot + jnp.dot(strict, dkd * kd, precision=HIGHEST, preferred_element_type=F32)
            dstate[...] = d_cur * e_tot

    def rev(i):
        return nsteps - 1 - i

    kq_spec = pl.BlockSpec((rows, HEAD_DIM), lambda h, i: (rev(i), h))
    v_spec = pl.BlockSpec((rows, GLA_VAL_DIM), lambda h, i: (rev(i), h))
    return pl.pallas_call(
        body, name=name, grid=(heads, nsteps),
        in_specs=[
            pl.BlockSpec((rows, HEAD_DIM), lambda h, i: (rev(i), q_blk + h)),
            pl.BlockSpec((rows, HEAD_DIM), lambda h, i: (rev(i), k_blk + h)),
            v_spec, kq_spec, v_spec,
            pl.BlockSpec((None, ncs, GLA_VAL_DIM, HEAD_DIM), lambda h, i: (h, rev(i), 0, 0)),
        ],
        out_specs=[kq_spec, kq_spec, v_spec, kq_spec],
        out_shape=[jax.ShapeDtypeStruct((s, heads * HEAD_DIM), F32), jax.ShapeDtypeStruct((s, heads * HEAD_DIM), F32),
                   jax.ShapeDtypeStruct((s, heads * GLA_VAL_DIM), F32), jax.ShapeDtypeStruct((s, heads * HEAD_DIM), F32)],
        scratch_shapes=[pltpu.VMEM((GLA_VAL_DIM, HEAD_DIM), F32)],
        compiler_params=_cparams(("parallel", "arbitrary")),
    )(zr, zr, zr, la, do, states)


def _my_place():
    x, y, c = lax.axis_index("x"), lax.axis_index("y"), lax.axis_index("c")
    return x, y, c


def _gather_ops(x_ref, out_ref, send_sems, recv_sems, local_sem):
    def plan():
        x, y, c = _my_place()
        me, sibling = (x, y, c), (x, y, 1 - c)
        chips = [(1 - x, y), (x, 1 - y), (1 - x, 1 - y)]

        def blk(px, py, pc):
            return out_ref.at[4 * px + 2 * py + pc]

        def copy(k, block, to, src=None):
            return pltpu.make_async_remote_copy(
                src_ref=blk(*block) if src is None else src, dst_ref=blk(*block),
                send_sem=send_sems.at[k], recv_sem=recv_sems.at[k], device_id=to, device_id_type=MESH)

        mine = pltpu.make_async_copy(x_ref, blk(*me), local_sem)
        first = [copy(0, me, sibling, src=x_ref)]
        first += [copy(1 + j, me, (*chip, c), src=x_ref) for j, chip in enumerate(chips)]
        passed = [copy(4 + j, (*chip, c), sibling) for j, chip in enumerate(chips)]
        landed = [copy(1 + j, (*chip, c), me) for j, chip in enumerate(chips)]
        from_sibling = [copy(0, sibling, me)] + [copy(4 + j, (*chip, 1 - c), me) for j, chip in enumerate(chips)]
        return mine, first, passed, landed, from_sibling

    def start():
        mine, first, _, _, _ = plan()
        mine.start()
        for cp in first:
            cp.start()

    def finish():
        mine, first, passed, landed, from_sibling = plan()
        for cp, fwd in zip(landed, passed):
            cp.wait_recv()
            fwd.start()
        for cp in from_sibling:
            cp.wait_recv()
        for cp in first + passed:
            cp.wait_send()
        mine.wait()

    return start, finish


def _exchange_ops(scatter, s_ref, r_ref, send_sems, recv_sems, local_sem):
    def plan():
        x, y, c = _my_place()
        me = 4 * x + 2 * y + c
        mine = pltpu.make_async_copy(s_ref.at[me] if scatter else s_ref, r_ref.at[me], local_sem)
        sends, recvs = [], []
        for k in range(1, N_DEV):
            px, py, pc = x ^ ((k >> 2) & 1), y ^ ((k >> 1) & 1), c ^ (k & 1)
            peer = 4 * px + 2 * py + pc
            src = s_ref.at[peer] if scatter else s_ref
            for dst, out in ((r_ref.at[me], sends), (r_ref.at[peer], recvs)):
                out.append(pltpu.make_async_remote_copy(
                    src_ref=src, dst_ref=dst, send_sem=send_sems.at[k - 1], recv_sem=recv_sems.at[k - 1],
                    device_id=(px, py, pc), device_id_type=MESH))
        return mine, sends, recvs

    def start():
        mine, sends, _ = plan()
        mine.start()
        for cp in sends:
            cp.start()

    def finish():
        mine, sends, recvs = plan()
        for cp in recvs:
            cp.wait_recv()
        for cp in sends:
            cp.wait_send()
        mine.wait()

    return start, finish


def _pair_ops(s_ref, r_ref, send_sems, recv_sems):
    def plan():
        x, y, c = _my_place()
        return [pltpu.make_async_remote_copy(
            src_ref=s_ref.at[2 * q + 1 - c], dst_ref=r_ref.at[q], send_sem=send_sems.at[q], recv_sem=recv_sems.at[q],
            device_id=(x, y, 1 - c), device_id_type=MESH) for q in range(N_DEV // 2)]

    def start():
        for cp in plan():
            cp.start()

    def finish():
        copies = plan()
        for cp in copies:
            cp.wait_recv()
        for cp in copies:
            cp.wait_send()

    return start, finish


def _chips_ops(p_ref, r_ref, send_sems, recv_sems, local_sem):
    def plan():
        x, y, c = _my_place()
        chip = 2 * x + y
        mine = pltpu.make_async_copy(p_ref.at[chip], r_ref.at[chip], local_sem)
        sends, recvs = [], []
        for k in range(1, N_DEV // 2):
            px, py = x ^ (k >> 1), y ^ (k & 1)
            peer = 2 * px + py
            for dst, out in ((r_ref.at[chip], sends), (r_ref.at[peer], recvs)):
                out.append(pltpu.make_async_remote_copy(
                    src_ref=p_ref.at[peer], dst_ref=dst, send_sem=send_sems.at[k - 1], recv_sem=recv_sems.at[k - 1],
                    device_id=(px, py, c), device_id_type=MESH))
        return mine, sends, recvs

    def start():
        mine, sends, _ = plan()
        mine.start()
        for cp in sends:
            cp.start()

    def finish():
        mine, sends, recvs = plan()
        for cp in recvs:
            cp.wait_recv()
        for cp in sends:
            cp.wait_send()
        mine.wait()

    return start, finish


def _comm_ops(kind, src_ref, dst_ref, send_sems, recv_sems, local_sem):
    if kind == "gather":
        return _gather_ops(src_ref, dst_ref, send_sems, recv_sems, local_sem)
    if kind == "pair":
        return _pair_ops(src_ref, dst_ref, send_sems, recv_sems)
    if kind == "chips":
        return _chips_ops(src_ref, dst_ref, send_sems, recv_sems, local_sem)
    return _exchange_ops(False, src_ref, dst_ref, send_sems, recv_sems, local_sem)


def _comm_out_shape(kind, v):
    shape = {"pair": (N_DEV // 2,) + v.shape[1:], "chips": v.shape}.get(kind, (N_DEV,) + v.shape)
    return jax.ShapeDtypeStruct(shape, v.dtype)


def _comm_scratch(n_tasks):
    return [pltpu.SemaphoreType.DMA((N_DEV - 1,)), pltpu.SemaphoreType.DMA((N_DEV - 1,)), pltpu.SemaphoreType.DMA] * n_tasks


def _pair_sum(s, r, *, name):
    _, rows, cdim = s.shape
    tr = _pick(rows, (1024, 704, 512, 256, 128, 64, 32, 16, 8))
    rc = min(16, tr)

    def body(c_ref, s_ref, r_ref, o_ref):
        def chunk(ci, carry):
            sl = pl.ds(pl.multiple_of(ci * rc, rc), rc)
            o_ref[sl, :] = (s_ref[sl, :].astype(F32) + r_ref[sl, :].astype(F32)).astype(o_ref.dtype)
            return carry

        lax.fori_loop(0, tr // rc, chunk, 0)

    spec = pl.BlockSpec((None, tr, cdim), lambda q, i, c_ref: (q, i, 0))
    return pl.pallas_call(
        body, name=name,
        grid_spec=pltpu.PrefetchScalarGridSpec(
            num_scalar_prefetch=1, grid=(N_DEV // 2, rows // tr),
            in_specs=[pl.BlockSpec((None, None, tr, cdim), lambda q, i, c_ref: (q, c_ref[0], i, 0)), spec],
            out_specs=spec),
        out_shape=jax.ShapeDtypeStruct(r.shape, s.dtype),
        compiler_params=_cparams(("parallel", "parallel")),
    )(lax.axis_index("c").astype(jnp.int32).reshape(1), s.reshape(N_DEV // 2, 2, rows, cdim), r)


def _comm(kind, v, *, name):
    def body(s_ref, r_ref, send_sems, recv_sems, local_sem):
        start, finish = _comm_ops(kind, s_ref, r_ref, send_sems, recv_sems, local_sem)
        start()
        finish()

    return pl.pallas_call(
        body, name=name,
        out_shape=_comm_out_shape(kind, v),
        in_specs=[pl.BlockSpec(memory_space=pl.ANY)],
        out_specs=pl.BlockSpec(memory_space=pl.ANY),
        scratch_shapes=_comm_scratch(1),
    )(v)


class _Overlap:
    US_PER_MB = {"gather": 52.0, "pair": 1.0, "chips": 13.0, "bcast": 97.0}
    MM_FLOPS_PER_US = 6.0e8

    def __init__(self):
        self.queue, self.results, self.then = [], {}, {}

    def add(self, key, kind, v):
        self.queue.append((key, kind, v))

    def _cost(self, kind, v):
        return v.size * v.dtype.itemsize / 2 ** 20 * self.US_PER_MB[kind]

    def take(self, budget_us):
        taken, cum = [], 0.0
        while self.queue:
            cost = self._cost(*self.queue[0][1:])
            if taken and cum + cost > 1.25 * budget_us:
                break
            taken.append(self.queue.pop(0))
            cum += cost
        return taken

    def put(self, taken, res):
        for (key, _, _), r in zip(taken, res):
            self.results[key] = r
            if key in self.then:
                self.then.pop(key)(r)

    def carry(self, budget_us, fn):
        taken = self.take(budget_us)
        out, res = fn([(kind, v) for _, kind, v in taken])
        self.put(taken, res)
        return out

    def mm(self, a, b, *, mode, **kw):
        la, lb = _logical(a), _logical(b)
        budget = 2.0 * la[0] * la[1] * (lb[0] if mode == "nt" else lb[1]) / self.MM_FLOPS_PER_US

        def fn(comm):
            return _mm(a, b, mode=mode, comm=comm, **kw) if comm else (_mm(a, b, mode=mode, **kw), [])

        return self.carry(budget, fn)

    def get(self, key):
        while key not in self.results:
            keys = [k for k, _, _ in self.queue]
            task = self.queue.pop(keys.index(key if key in keys else "pair_" + key))
            self.put([task], [_comm(task[1], task[2], name=f"alone_{task[0]}")])
        return self.results[key]

    def weight(self, n):
        g = self.get(n)
        return g.reshape(-1, g.shape[2]) if n in ROW_SHARDED else g

    def grad(self, n, g):
        g = g if g.ndim == 3 else g.reshape(N_DEV, g.shape[0] // N_DEV, g.shape[1])
        self.add("pair_d_" + n, "pair", g)
        self.then["pair_d_" + n] = lambda r: self.add("d_" + n, "chips", _pair_sum(g, r, name=f"pair_sum_{n}"))


def _sel_tables(dest, ws, wp, tw):
    n_tiles = (int(dest.max()) + tw) // tw
    tbl = np.full((N_DEV, wp), -1, np.int32)
    for j in range(N_DEV):
        tbl[j, :ws] = dest[j * ws:(j + 1) * ws]
    by_tile = [sorted({j for j in range(N_DEV) if ((tbl[j] // tw) == t).any()}) for t in range(n_tiles)]
    by_shard = [sorted({int(t) for t in np.unique(tbl[j, :ws] // tw)}) for j in range(N_DEV)]

    def table(lists):
        width = max(len(v) for v in lists)
        idx = np.array([(v + [v[-1]] * width)[:width] if v else [0] * width for v in lists], np.int32)
        val = np.array([[1] * len(v) + [0] * (width - len(v)) for v in lists], np.int32)
        return idx.reshape(-1), val.reshape(-1), width

    return tbl[:, :, None], table(by_tile), table(by_shard)


def _sel_matrix(d_ref, t, wp, tw):
    cols = t * tw + lax.broadcasted_iota(jnp.int32, (wp, tw), 1)
    return (d_ref[...] == cols).astype(BF16)


def _win_unshard(g, tbl, idx, val, width, *, tw, padded, name):
    _, dm, wp = g.shape
    tm = _pick(dm, (1024, 512, 256, 128))

    def body(idx_ref, val_ref, g_ref, d_ref, o_ref, acc):
        t, s_ = pl.program_id(1), pl.program_id(2)

        @pl.when(s_ == 0)
        def _():
            acc[...] = jnp.zeros_like(acc)

        @pl.when(val_ref[t * width + s_] == 1)
        def _():
            acc[...] += _dot(g_ref[...], _sel_matrix(d_ref, t, wp, tw), NN)

        @pl.when(s_ == width - 1)
        def _():
            o_ref[...] = acc[...].astype(BF16)

    return pl.pallas_call(
        body, name=name,
        grid_spec=pltpu.PrefetchScalarGridSpec(
            num_scalar_prefetch=2, grid=(dm // tm, padded // tw, width),
            in_specs=[pl.BlockSpec((None, tm, wp), lambda i, t, s_, ix, vl: (ix[t * width + s_], i, 0)),
                      pl.BlockSpec((None, wp, 1), lambda i, t, s_, ix, vl: (ix[t * width + s_], 0, 0))],
            out_specs=pl.BlockSpec((tm, tw), lambda i, t, s_, ix, vl: (i, t)),
            scratch_shapes=[pltpu.VMEM((tm, tw), F32)]),
        out_shape=jax.ShapeDtypeStruct((dm, padded), BF16),
        compiler_params=_cparams(("parallel", "parallel", "arbitrary")),
    )(idx, val, g, tbl)


def _win_to_shards(dw, tbl, idx, val, width, *, tw, wp, name):
    dm = dw.shape[0]
    tm = _pick(dm, (1024, 512, 256, 128))

    def body(idx_ref, val_ref, w_ref, d_ref, o_ref, acc):
        j, s_ = pl.program_id(1), pl.program_id(2)

        @pl.when(s_ == 0)
        def _():
            acc[...] = jnp.zeros_like(acc)

        @pl.when(val_ref[j * width + s_] == 1)
        def _():
            acc[...] += _dot(w_ref[...], _sel_matrix(d_ref, idx_ref[j * width + s_], wp, tw), NT)

        @pl.when(s_ == width - 1)
        def _():
            o_ref[...] = acc[...].astype(BF16)

    return pl.pallas_call(
        body, name=name,
        grid_spec=pltpu.PrefetchScalarGridSpec(
            num_scalar_prefetch=2, grid=(dm // tm, N_DEV, width),
            in_specs=[pl.BlockSpec((tm, tw), lambda i, j, s_, ix, vl: (i, ix[j * width + s_])),
                      pl.BlockSpec((None, wp, 1), lambda i, j, s_, ix, vl: (j, 0, 0))],
            out_specs=pl.BlockSpec((None, tm, wp), lambda i, j, s_, ix, vl: (j, i, 0)),
            scratch_shapes=[pltpu.VMEM((tm, wp), F32)]),
        out_shape=jax.ShapeDtypeStruct((N_DEV, dm, wp), BF16),
        compiler_params=_cparams(("parallel", "parallel", "arbitrary")),
    )(idx, val, dw, tbl)


def _adamw(parts, w, m, v, *, name):
    _, r, cdim = w.shape
    n_parts, _, cpad = parts.shape
    tr = _pick(r, (256, 128, 64, 32, 16, 8))
    rc = min(16, tr)
    c1 = 1.0 - ADAM_B1 ** ADAM_STEP
    c2 = 1.0 - ADAM_B2 ** ADAM_STEP

    def body(p_ref, w_ref, m_ref, v_ref, g_ref, d_ref, mo_ref, vo_ref):
        def chunk(ci, carry):
            sl = pl.ds(pl.multiple_of(ci * rc, rc), rc)
            g = p_ref[0, sl, :cdim].astype(F32)
            for i in range(1, n_parts):
                g = g + p_ref[i, sl, :cdim].astype(F32)
            mn = ADAM_B1 * m_ref[sl, :] + (1.0 - ADAM_B1) * g
            vn = ADAM_B2 * v_ref[sl, :] + (1.0 - ADAM_B2) * jnp.square(g)
            m_hat = mn / c1
            v_hat = vn / c2
            g_ref[sl, :] = g
            d_ref[sl, :] = -ADAM_LR * (m_hat / (jnp.sqrt(v_hat) + ADAM_EPS) + ADAM_WD * w_ref[sl, :])
            mo_ref[sl, :] = mn
            vo_ref[sl, :] = vn
            return carry

        lax.fori_loop(0, tr // rc, chunk, 0)

    spec = pl.BlockSpec((None, tr, cdim), lambda i: (0, i, 0))
    return pl.pallas_call(
        body, name=name, grid=(r // tr,),
        in_specs=[pl.BlockSpec((n_parts, tr, cpad), lambda i: (0, i, 0)), spec, spec, spec],
        out_specs=[spec] * 4,
        out_shape=[jax.ShapeDtypeStruct((1, r, cdim), F32)] * 4,
        compiler_params=_cparams(("parallel",)),
    )(parts, w, m, v)


def _pad_to(v, n):
    return v if v.shape[0] == n else jnp.concatenate([v, jnp.zeros((n - v.shape[0],), v.dtype)])


def _pad_cols(v, n):
    return v if v.shape[-1] == n else jnp.concatenate([v, jnp.zeros(v.shape[:-1] + (n - v.shape[-1],), v.dtype)], axis=-1)


def _pack(vs, cols, row_mult, dtype):
    offs, o = [], 0
    for v in vs:
        offs.append(o)
        o += v.size
    rows = -(-o // cols)
    rows = -(-rows // row_mult) * row_mult
    flat = jnp.concatenate([v.reshape(-1).astype(dtype) for v in vs])
    return _pad_to(flat, rows * cols).reshape(rows, cols), offs


SHARDED = ("ffn1_w_gate", "ffn1_w_up", "ffn1_w_down", "w_in", "w_merge_gate", "gla_gate_up", "w_branch_fox",
           "w_branch_gla", "w_out", "ffn2_w_gate", "ffn2_w_up", "ffn2_w_down", "w_ple_gate", "w_ple_proj")
ROW_SHARDED = ("ffn1_w_down", "w_out", "ffn2_w_down", "w_ple_gate")
REPLICATED = ("ffn1_norm", "mix_norm", "fox_forget_bias", "gla_gate_bias", "gla_head_norm", "b_merge_gate",
              "ffn2_norm", "ple_norm", "final_norm")
WEIGHTS = ("ffn1_norm", "ffn1_w_gate", "ffn1_w_up", "ffn1_w_down", "mix_norm", "w_in", "fox_forget_bias",
           "gla_gate_up", "gla_gate_bias", "gla_head_norm", "w_branch_fox", "w_branch_gla", "w_merge_gate",
           "b_merge_gate", "w_out", "ffn2_norm", "ffn2_w_gate", "ffn2_w_up", "ffn2_w_down", "ple_norm",
           "w_ple_proj", "w_ple_gate", "final_norm")


def kernel(x, p, ffn1_norm, ffn1_w_gate, ffn1_w_up, ffn1_w_down, mix_norm, w_in, fox_forget_bias, gla_gate_up, gla_gate_bias, gla_head_norm, w_branch_fox, w_branch_gla, w_merge_gate, b_merge_gate, w_out, ffn2_norm, ffn2_w_gate, ffn2_w_up, ffn2_w_down, ple_norm, w_ple_proj, w_ple_gate, final_norm, loss_target, m_ffn1_norm, m_ffn1_w_gate, m_ffn1_w_up, m_ffn1_w_down, m_mix_norm, m_w_in, m_fox_forget_bias, m_gla_gate_up, m_gla_gate_bias, m_gla_head_norm, m_w_branch_fox, m_w_branch_gla, m_w_merge_gate, m_b_merge_gate, m_w_out, m_ffn2_norm, m_ffn2_w_gate, m_ffn2_w_up, m_ffn2_w_down, m_ple_norm, m_w_ple_proj, m_w_ple_gate, m_final_norm, v_ffn1_norm, v_ffn1_w_gate, v_ffn1_w_up, v_ffn1_w_down, v_mix_norm, v_w_in, v_fox_forget_bias, v_gla_gate_up, v_gla_gate_bias, v_gla_head_norm, v_w_branch_fox, v_w_branch_gla, v_w_merge_gate, v_b_merge_gate, v_w_out, v_ffn2_norm, v_ffn2_w_gate, v_ffn2_w_up, v_ffn2_w_down, v_ple_norm, v_w_ple_proj, v_w_ple_gate, v_final_norm):
    args = dict(locals())
    wts = {n: args[n] for n in WEIGHTS}
    mom_m = {n: args["m_" + n] for n in WEIGHTS}
    mom_v = {n: args["v_" + n] for n in WEIGHTS}

    xs, ps, tgt = x[0], p[0, 0], loss_target[0]
    s, d = xs.shape
    fox_w = w_branch_fox.shape[1]
    gla_vw = w_branch_gla.shape[1]
    fox_heads = fox_w // HEAD_DIM
    gla_heads = gla_vw // GLA_VAL_DIM
    gla_kw = gla_heads * HEAD_DIM
    rank = gla_gate_up.shape[1]

    c_fl = 3 * fox_w
    o_gr, o_gq, o_gk = gla_vw, 2 * gla_vw, 2 * gla_vw + gla_kw
    o_fl = o_gk + gla_kw
    o_gd = o_fl + LANES
    rest_w = o_gd + LANES
    padded = c_fl + rest_w
    seg = [(c_fl, 0), (fox_heads, c_fl + o_fl), (gla_kw, c_fl + o_gq), (gla_kw, c_fl + o_gk), (gla_vw, c_fl),
           (gla_vw, c_fl + o_gr), (rank, c_fl + o_gd)]
    dest = np.concatenate([np.arange(w_, dtype=np.int32) + o_ for w_, o_ in seg])
    ws = w_in.shape[2]
    wp = -(-ws // LANES) * LANES
    tw = 256 if padded % 256 == 0 else LANES
    tbl, (t_idx, t_val, t_width), (s_idx, s_val, s_width) = _sel_tables(dest, ws, wp, tw)
    tbl = jnp.asarray(tbl)

    ov = _Overlap()
    for n in SHARDED:
        sh = wts[n][0].astype(BF16)
        ov.add(n, "gather", _pad_cols(sh, wp) if n == "w_in" else sh)
    fbias = _pad_cols(fox_forget_bias, LANES)
    bmg_f, bmg_g = b_merge_gate[:, :d], b_merge_gate[:, d:]
    ghn = jnp.tile(gla_head_norm, (1, gla_heads))

    h1, ffn1_saved = _ffn_fwd(xs, ffn1_norm, ov, tag="ffn1")
    u = _rms_fwd(h1, mix_norm, name="mix_norm")
    win = _win_unshard(ov.weight("w_in"), tbl, jnp.asarray(t_idx), jnp.asarray(t_val), t_width, tw=tw, padded=padded,
                       name="in_proj_unshard")
    win_fox, win_rest = win[:, :c_fl], win[:, c_fl:]
    zf = ov.mm(u, win_fox, mode="nn", name="in_proj_fox", out_dtype=BF16)
    zr = ov.mm(u, win_rest, mode="nn", name="in_proj_rest")
    gz = ov.mm(u, ov.weight("w_merge_gate"), mode="nn", name="merge_gate")
    gup = ov.weight("gla_gate_up").transpose(1, 0, 2).reshape(rank, gla_kw)
    gup = jnp.concatenate([gup, jnp.zeros((LANES - rank, gla_kw), BF16)], axis=0)

    log_f = _rowwise(lambda fl, b: _log_sigmoid(fl + b), [(zr, LANES, o_fl // LANES)], [fbias], [(LANES, F32)],
                     name="forget_gate")[0]
    f_cum = _cumsum(log_f[:, :fox_heads].T, reverse=False, name="forget_cumsum")
    f_rep = jnp.broadcast_to(f_cum.T[:, :, None], (s, fox_heads, HEAD_DIM)).reshape(s, fox_w)
    f_row = f_cum[:, None, :]
    attn_us = 2.0 * s * s * HEAD_DIM * fox_heads / _Overlap.MM_FLOPS_PER_US
    y_fox, y_fox_bf, lse = ov.carry(
        ATTN_FWD_MATMULS * attn_us, lambda comm: _fox_fwd(zf, f_rep, f_row, heads=fox_heads, name="fox_fwd", comm=comm))

    def decay_fn(gd, gupv, gb):
        return _log_sigmoid(_dot(gd, gupv, NN) + gb) * (1.0 / GLA_GATE_TAU)

    la = _rowwise(decay_fn, [(zr, LANES, o_gd // LANES)], [gup, gla_gate_bias], [(gla_kw, F32)], name="gla_decay", rc=128)[0]
    q_blk, k_blk = o_gq // HEAD_DIM, o_gk // HEAD_DIM
    o_gla, states = _gla_fwd(zr, la, heads=gla_heads, q_blk=q_blk, k_blk=k_blk, name="gla_fwd")

    def gla_out_fn(o, gr, g):
        outs = []
        for hh in range(gla_heads):
            sl = slice(hh * GLA_VAL_DIM, (hh + 1) * GLA_VAL_DIM)
            _, oh = _rms_parts(o[:, sl])
            outs.append(oh * g[:, sl] * _silu_parts(gr[:, sl])[0])
        return jnp.concatenate(outs, axis=1)

    y_gla = _rowwise(gla_out_fn, [o_gla, (zr, gla_vw, o_gr // gla_vw)], [ghn], [(gla_vw, BF16)], name="gla_out")[0]
    br_f = ov.mm(y_fox_bf, ov.weight("w_branch_fox"), mode="nn", name="branch_fox")
    br_g = ov.mm(y_gla, ov.weight("w_branch_gla"), mode="nn", name="branch_gla")

    def merge_fn(zf_, zg_, bf_, bg_, b1, b2):
        return _sigmoid(zf_ + b1) * bf_ + _sigmoid(zg_ + b2) * bg_

    merged = _rowwise(merge_fn, [(gz, d, 0), (gz, d, 1), br_f, br_g], [bmg_f, bmg_g], [(d, BF16)], name="merge")[0]
    h2 = ov.mm(merged, ov.weight("w_out"), mode="nn", name="out_proj", add=h1)
    h3, ffn2_saved = _ffn_fwd(h2, ffn2_norm, ov, tag="ffn2")
    n3 = _rms_fwd(h3, ple_norm, name="ple_norm")
    gl = ov.mm(n3, ov.weight("w_ple_gate"), mode="nn", name="ple_gate")
    pe = ov.mm(ps, ov.weight("w_ple_proj"), mode="nn", name="ple_proj")

    def head_fn(h3b, glb, peb, tb, gfin):
        pg = _sigmoid(glb)
        h4 = h3b + pg * peb
        r, xh = _rms_parts(h4)
        err = xh * gfin - tb
        dy = err * (1.0 / d)
        t = dy * gfin
        dh4 = r * (t - xh * jnp.mean(t * xh, axis=-1, keepdims=True))
        return dh4, dh4 * pg, dh4 * peb * pg * (1.0 - pg), _colsum(err * err), _colsum(dy * xh)

    dh4, dpe, dgl, loss_cols, d_final = _rowwise(
        head_fn, [h3, gl, pe, tgt], [final_norm.reshape(1, d)], [(d, F32), (d, BF16), (d, BF16)], [d, d], name="loss_head")
    loss = lax.psum(0.5 * jnp.sum(loss_cols) / d, AXES)

    grads = {"final_norm": d_final.reshape(d)}
    ov.grad("w_ple_proj", ov.mm(ps, dpe, mode="tn", name="ple_proj_dw", out_dtype=BF16, out_chunked=True))
    ov.grad("w_ple_gate", ov.mm(n3, dgl, mode="tn", name="ple_gate_dw", out_dtype=BF16))
    dn3 = ov.mm(dgl, ov.weight("w_ple_gate"), mode="nt", name="ple_gate_dx")
    dh3, dh3_bf, grads["ple_norm"] = _rms_bwd(h3, dn3, ple_norm, dh4, name="ple_norm_bwd")
    dh2, dh2_bf, grads["ffn2_norm"] = _ffn_bwd(h2, ffn2_norm, ov, ffn2_saved, dh3, dh3_bf, tag="ffn2")

    ov.grad("w_out", ov.mm(merged, dh2_bf, mode="tn", name="out_proj_dw", out_dtype=BF16))
    dmerged = ov.mm(dh2_bf, ov.weight("w_out"), mode="nt", name="out_proj_dx")

    def merge_bwd_fn(zf_, zg_, bf_, bg_, dm, b1, b2):
        sf, sg = _sigmoid(zf_ + b1), _sigmoid(zg_ + b2)
        dz = jnp.concatenate([dm * bf_ * sf * (1.0 - sf), dm * bg_ * sg * (1.0 - sg)], axis=1)
        return dm * sf, dm * sg, dz, _colsum(dz)

    dbr_f, dbr_g, dgz, grads["b_merge_gate"] = _rowwise(
        merge_bwd_fn, [(gz, d, 0), (gz, d, 1), br_f, br_g, dmerged], [bmg_f, bmg_g],
        [(d, BF16), (d, BF16), (2 * d, BF16)], [2 * d], name="merge_bwd")
    ov.grad("w_merge_gate", ov.mm(u, dgz, mode="tn", name="merge_gate_dw", out_dtype=BF16, out_chunked=True))
    ov.grad("w_branch_fox", ov.mm(y_fox_bf, dbr_f, mode="tn", name="branch_fox_dw", out_dtype=BF16, out_chunked=True))
    ov.grad("w_branch_gla", ov.mm(y_gla, dbr_g, mode="tn", name="branch_gla_dw", out_dtype=BF16, out_chunked=True))
    dy_fox = ov.mm(dbr_f, ov.weight("w_branch_fox"), mode="nt", name="branch_fox_dx")
    dy_gla = ov.mm(dbr_g, ov.weight("w_branch_gla"), mode="nt", name="branch_gla_dx")

    def gla_out_bwd_fn(o, gr, dy, g):
        dos, dgrs, dgs = [], [], []
        for hh in range(gla_heads):
            sl = slice(hh * GLA_VAL_DIM, (hh + 1) * GLA_VAL_DIM)
            r, oh = _rms_parts(o[:, sl])
            si, dsi = _silu_parts(gr[:, sl])
            don = dy[:, sl] * si
            dgrs.append(dy[:, sl] * oh * g[:, sl] * dsi)
            t = don * g[:, sl]
            dos.append(r * (t - oh * jnp.mean(t * oh, axis=-1, keepdims=True)))
            dgs.append(_colsum(don * oh))
        return jnp.concatenate(dos, axis=1), jnp.concatenate(dgrs, axis=1), jnp.concatenate(dgs, axis=1)

    do_gla, dgr, d_ghn = _rowwise(gla_out_bwd_fn, [o_gla, (zr, gla_vw, o_gr // gla_vw), dy_gla], [ghn],
                                  [(gla_vw, F32), (gla_vw, F32)], [gla_vw], name="gla_out_bwd")
    grads["gla_head_norm"] = d_ghn.reshape(gla_heads, GLA_VAL_DIM).sum(axis=0, keepdims=True)
    dgq, dgk, dgv, dla = _gla_bwd(zr, la, do_gla, states, heads=gla_heads, q_blk=q_blk, k_blk=k_blk, name="gla_bwd")

    def decay_bwd_fn(dl, gd, gupv, gb):
        pre = _dot(gd, gupv, NN) + gb
        dpre = dl * (1.0 / GLA_GATE_TAU) * _sigmoid(-pre)
        return _dot(dpre, gupv, NT), dpre, _colsum(dpre)

    dgd, dpre_bf, grads["gla_gate_bias"] = _rowwise(
        decay_bwd_fn, [dla, (zr, LANES, o_gd // LANES)], [gup, gla_gate_bias], [(LANES, F32), (gla_kw, BF16)], [gla_kw],
        name="gla_decay_bwd", rc=128)
    d_gup = ov.mm(zr[:, o_gd:o_gd + LANES], dpre_bf, mode="tn", name="gla_gate_up_dw")[:rank]
    ov.grad("gla_gate_up", d_gup.reshape(rank, N_DEV, gla_kw // N_DEV).transpose(1, 0, 2).astype(BF16))

    def delta_fn(dyv, yv):
        outs = []
        for hh in range(fox_heads):
            sl = slice(hh * HEAD_DIM, (hh + 1) * HEAD_DIM)
            outs.append(jnp.broadcast_to(jnp.sum(dyv[:, sl] * yv[:, sl], axis=-1, keepdims=True), (dyv.shape[0], HEAD_DIM)))
        return jnp.concatenate(outs, axis=1)

    delta = _rowwise(delta_fn, [dy_fox, y_fox], [], [(fox_w, F32)], name="fox_delta")[0]
    dfq, dfk, dfv, d_fcol, d_frow = ov.carry(
        ATTN_BWD_MATMULS * attn_us,
        lambda comm: _fox_bwd(zf, dy_fox, lse, delta, f_rep, f_row, heads=fox_heads, name="fox_bwd", comm=comm))
    d_fcum = d_fcol[:, ::HEAD_DIM].T + d_frow.reshape(fox_heads, s)
    d_logf = _cumsum(d_fcum, reverse=True, name="forget_cumsum_bwd")
    d_logf = _pad_cols(d_logf.T, LANES)

    def forget_bwd_fn(dl, fl, b):
        dfl_ = dl * _sigmoid(-(fl + b))
        return dfl_, _colsum(dfl_)

    dfl, d_fbias = _rowwise(forget_bwd_fn, [d_logf, (zr, LANES, o_fl // LANES)], [fbias], [(LANES, F32)], [LANES],
                            name="forget_gate_bwd")
    grads["fox_forget_bias"] = d_fbias[:, :fox_heads]

    dz = jnp.concatenate([dfq, dfk, dfv, dgv, dgr, dgq, dgk, dfl, dgd], axis=1).astype(BF16)
    dwin = ov.mm(u, dz, mode="tn", name="in_proj_dw", out_dtype=BF16)
    ov.grad("w_in", _win_to_shards(dwin, tbl, jnp.asarray(s_idx), jnp.asarray(s_val), s_width, tw=tw, wp=wp,
                                   name="in_proj_dw_shards"))
    du = ov.mm(dgz, ov.weight("w_merge_gate"), mode="nt", name="merge_gate_dx")
    du = ov.mm(dz, win, mode="nt", name="in_proj_dx", add=du)
    dh1, dh1_bf, grads["mix_norm"] = _rms_bwd(h1, du, mix_norm, dh2, name="mix_norm_bwd")
    dx, _, grads["ffn1_norm"] = _ffn_bwd(xs, ffn1_norm, ov, ffn1_saved, dh1, dh1_bf, tag="ffn1")

    outs = {}
    for n in SHARDED:
        res4 = _adamw(ov.get("d_" + n), wts[n], mom_m[n], mom_v[n], name=f"adamw_{n}")
        for kind, r_ in zip(("grad", "delta", "new_m", "new_v"), res4):
            outs[f"{kind}_{n}"] = r_

    send_small, small_offs = _pack([grads[n] for n in REPLICATED], LANES, 8, F32)
    recv_small = _comm("bcast", send_small, name="exchange_replicated")
    w_sm, _ = _pack([wts[n] for n in REPLICATED], LANES, 8, F32)
    m_sm, _ = _pack([mom_m[n] for n in REPLICATED], LANES, 8, F32)
    v_sm, _ = _pack([mom_v[n] for n in REPLICATED], LANES, 8, F32)
    small = _adamw(recv_small, w_sm[None], m_sm[None], v_sm[None], name="adamw_replicated")
    for kind, buf in zip(("grad", "delta", "new_m", "new_v"), small):
        fs = buf.reshape(-1)
        for n, o in zip(REPLICATED, small_offs):
            outs[f"{kind}_{n}"] = fs[o:o + wts[n].size].reshape(wts[n].shape)

    res = [loss, dx[None]]
    for kind in ("grad", "delta", "new_m", "new_v"):
        res += [outs[f"{kind}_{n}"] for n in WEIGHTS]
    return tuple(res)
```

```python
import functools

import jax
import jax.numpy as jnp
import numpy as np
from jax import lax
from jax.experimental import pallas as pl
from jax.experimental.pallas import tpu as pltpu

F32 = jnp.float32
BF16 = jnp.bfloat16
MESH = pl.DeviceIdType.MESH
AXES = ("x", "y", "c")
N_DEV = 8

VMEM_LIMIT_BYTES = 56 * 1024 * 1024
MM_BLOCK_BUDGET_BYTES = 40 * 1024 * 1024
LANES = 128

EPS = 1e-6
HEAD_DIM = 128
GLA_VAL_DIM = 256
GLA_CHUNK = 64
GLA_GATE_TAU = 16.0
ADAM_LR, ADAM_B1, ADAM_B2, ADAM_EPS, ADAM_WD, ADAM_STEP = 0.001, 0.9, 0.999, 1e-08, 0.01, 10

ATTN_FWD_MATMULS = 7.0
ATTN_BWD_MATMULS = 7.0

HIGHEST = lax.Precision.HIGHEST
NN = (((1,), (0,)), ((), ()))
NT = (((1,), (1,)), ((), ()))
TN = (((0,), (0,)), ((), ()))


def _cparams(sem):
    return pltpu.CompilerParams(dimension_semantics=sem, vmem_limit_bytes=VMEM_LIMIT_BYTES)


def _pick(dim, cands):
    for c in cands:
        if dim % c == 0:
            return c
    return dim


def _bf(v):
    return v if v.dtype == BF16 else v.astype(BF16)


def _dot(a, b, dims):
    return lax.dot_general(_bf(a), _bf(b), dims, preferred_element_type=F32)


def _sigmoid(v):
    return 1.0 / (1.0 + jnp.exp(-v))


def _log_sigmoid(v):
    return jnp.minimum(v, 0.0) - jnp.log(1.0 + jnp.exp(-jnp.abs(v)))


def _logical(v):
    return (v.shape[1], v.shape[0] * v.shape[2]) if v.ndim == 3 else v.shape


def _mm(a, b, *, mode, name, out_dtype=F32, add=None, scale=1.0, out_chunked=False, comm=()):
    la, lb = _logical(a), _logical(b)
    if mode == "nn":
        (m, k), (k2, n) = la, lb
        a_minor, b_minor = "k", "n"
    elif mode == "nt":
        (m, k), (n, k2) = la, lb
        a_minor, b_minor = "k", "k"
    else:
        (k, m), (k2, n) = la, lb
        a_minor, b_minor = "m", "n"
    assert k == k2, (name, a.shape, b.shape)
    forced = {}
    for v, minor in ((a, a_minor), (b, b_minor)):
        if v.ndim == 3:
            assert forced.get(minor, v.shape[2]) == v.shape[2], name
            forced[minor] = v.shape[2]
    if out_chunked:
        assert forced.get("n", n // N_DEV) == n // N_DEV, name
        forced["n"] = n // N_DEV
    tm = forced.get("m") or _pick(m, (1024, 512, 256, 128))
    tn = forced.get("n") or _pick(n, (1408, 1280, 1024, 512, 256, 128))

    def blocks_bytes(tm_, tn_, t):
        io = tm_ * t * a.dtype.itemsize + t * tn_ * b.dtype.itemsize
        return 2 * (io + tm_ * tn_ * (jnp.dtype(out_dtype).itemsize + (4 if add is not None else 0))) + tm_ * tn_ * 4

    kc = forced.get("k")
    a_k_minor, b_k_minor = mode != "tn", mode == "nt"
    if kc:
        aligned = all(v.ndim == 3 or kc % (LANES if minor else 16) == 0 for v, minor in ((a, a_k_minor), (b, b_k_minor)))
        fits = [(tm_, tn_) for tm_, tn_ in ((tm, tn), (512, tn), (512, 512))
                if m % tm_ == 0 and n % tn_ == 0 and forced.get("m", tm_) == tm_ and forced.get("n", tn_) == tn_
                and blocks_bytes(tm_, tn_, k) <= MM_BLOCK_BUDGET_BYTES]
        if aligned and fits:
            (tm, tn), tk = fits[0], k
        else:
            tk, kc = kc, None
    else:
        tk = k if blocks_bytes(tm, tn, k) <= MM_BLOCK_BUDGET_BYTES else _pick(k, (512, 640, 256, 128))
    nk = k // tk
    dims = {"nn": NN, "nt": NT, "tn": TN}[mode]
    gi, gj, gk = (lambda i, j, kk: i), (lambda i, j, kk: j), (lambda i, j, kk: kk)

    def spec(v, t_major, t_minor, g_major, g_minor, all_chunks=False):
        if v.ndim == 3:
            if all_chunks:
                return pl.BlockSpec((N_DEV, t_major, v.shape[2]), lambda i, j, kk: (0, g_major(i, j, kk), 0))
            return pl.BlockSpec((None, t_major, v.shape[2]), lambda i, j, kk: (g_minor(i, j, kk), g_major(i, j, kk), 0))
        return pl.BlockSpec((t_major, t_minor), lambda i, j, kk: (g_major(i, j, kk), g_minor(i, j, kk)))

    a_spec = spec(a, tk, tm, gk, gi) if mode == "tn" else spec(a, tm, tk, gi, gk, bool(kc))
    b_spec = spec(b, tn, tk, gj, gk, bool(kc)) if mode == "nt" else spec(b, tk, tn, gk, gj)

    def k_chunk(ref, minor, c_):
        if len(ref.shape) == 3:
            return ref[c_]
        return ref[:, c_ * kc:(c_ + 1) * kc] if minor else ref[c_ * kc:(c_ + 1) * kc, :]
    if out_chunked:
        o_spec = pl.BlockSpec((None, tm, tn), lambda i, j, kk: (j, i, 0))
        out_shape = jax.ShapeDtypeStruct((N_DEV, m, tn), out_dtype)
    else:
        o_spec = pl.BlockSpec((tm, tn), lambda i, j, kk: (i, j))
        out_shape = jax.ShapeDtypeStruct((m, n), out_dtype)
    has_add = add is not None
    assert not (has_add and out_chunked), name

    def body(*refs):
        a_ref, b_ref = refs[0], refs[1]
        add_ref = refs[2] if has_add else None
        o_ref = refs[3 if has_add else 2]
        kk = pl.program_id(2)

        def finish(r):
            if scale != 1.0:
                r = r * scale
            if has_add:
                r = r + add_ref[...]
            o_ref[...] = r.astype(o_ref.dtype)

        if kc:
            r = _dot(k_chunk(a_ref, a_k_minor, 0), k_chunk(b_ref, b_k_minor, 0), dims)
            for c_ in range(1, k // kc):
                r = r + _dot(k_chunk(a_ref, a_k_minor, c_), k_chunk(b_ref, b_k_minor, c_), dims)
            finish(r)
        elif nk == 1:
            finish(_dot(a_ref[...], b_ref[...], dims))
        else:
            acc_ref = refs[-1]

            @pl.when(kk == 0)
            def _():
                acc_ref[...] = jnp.zeros_like(acc_ref)

            acc_ref[...] += _dot(a_ref[...], b_ref[...], dims)

            @pl.when(kk == nk - 1)
            def _():
                finish(acc_ref[...])

    res, carried = _call(
        body, name=name, grid=(m // tm, n // tn, nk),
        in_specs=[a_spec, b_spec] + ([o_spec] if has_add else []), out_specs=[o_spec], out_shape=[out_shape],
        scratch_shapes=[pltpu.VMEM((tm, tn), F32)] if nk > 1 else [],
        semantics=("parallel", "parallel", "arbitrary"), operands=[a, b] + ([add] if has_add else []), comm=comm)
    return (res[0], carried) if comm else res[0]


def _call(body, *, name, grid, in_specs, out_specs, out_shape, scratch_shapes, semantics, operands, comm=()):
    n_in, n_out, n_scr, n = len(in_specs), len(out_specs), len(scratch_shapes), len(comm)
    if not comm:
        res = pl.pallas_call(body, name=name, grid=grid, in_specs=in_specs, out_specs=out_specs, out_shape=out_shape,
                             scratch_shapes=scratch_shapes, compiler_params=_cparams(semantics))(*operands)
        return res, []

    def carrying(*refs):
        ins, c_in = refs[:n_in], refs[n_in:n_in + n]
        outs, c_out = refs[n_in + n:n_in + n + n_out], refs[n_in + n + n_out:n_in + 2 * n + n_out]
        scratch, sems = refs[n_in + 2 * n + n_out:][:n_scr], refs[n_in + 2 * n + n_out + n_scr:]
        tasks = [_comm_ops(kind, c_in[t], c_out[t], *sems[3 * t:3 * t + 3]) for t, (kind, _) in enumerate(comm)]
        ids = [pl.program_id(ax) for ax in range(len(grid))]

        @pl.when(functools.reduce(lambda p, q: p & q, [i == 0 for i in ids]))
        def _():
            for start, _ in tasks:
                start()

        body(*ins, *outs, *scratch)

        @pl.when(functools.reduce(lambda p, q: p & q, [i == g - 1 for i, g in zip(ids, grid)]))
        def _():
            for _, finish in tasks:
                finish()

    any_spec = pl.BlockSpec(memory_space=pl.ANY)
    res = pl.pallas_call(
        carrying, name=name, grid=grid,
        in_specs=list(in_specs) + [any_spec] * n, out_specs=list(out_specs) + [any_spec] * n,
        out_shape=list(out_shape) + [_comm_out_shape(kind, v) for kind, v in comm],
        scratch_shapes=list(scratch_shapes) + _comm_scratch(n),
        compiler_params=_cparams(("arbitrary",) * len(grid)),
    )(*operands, *[v for _, v in comm])
    return res[:n_out], res[n_out:]


def _rowwise(fn, rows, consts, outs, accs=(), *, name, rc=None):
    rows = [r if isinstance(r, tuple) else (r, r.shape[1], 0) for r in rows]
    m = rows[0][0].shape[0]
    widths = [w for _, w, _ in rows] + [n for n, _ in outs]
    row_bytes = sum(w * r.dtype.itemsize for r, w, _ in rows) + sum(n * jnp.dtype(d).itemsize for n, d in outs)
    tm = 1024
    while tm > 16 and (m % tm or 2 * tm * row_bytes > 24 * 1024 * 1024):
        tm //= 2
    if m % tm:
        tm = m
    if rc is None:
        rc = 16
        while rc * 2 <= tm and rc * 2 * max(widths) <= 32768:
            rc *= 2
    rc = min(rc, tm)
    nr, nc, no = len(rows), len(consts), len(outs)

    def body(*refs):
        in_refs, c_refs = refs[:nr], refs[nr:nr + nc]
        o_refs, a_refs = refs[nr + nc:nr + nc + no], refs[nr + nc + no:]

        @pl.when(pl.program_id(0) == 0)
        def _():
            for r in a_refs:
                r[...] = jnp.zeros_like(r)

        cvals = [c[...] for c in c_refs]

        def chunk(ci, carry):
            sl = pl.ds(pl.multiple_of(ci * rc, rc), rc)
            res = fn(*[r[sl, :] for r in in_refs], *cvals)
            if not isinstance(res, (tuple, list)):
                res = (res,)
            for r, v in zip(o_refs, res[:no]):
                r[sl, :] = v.astype(r.dtype)
            for r, v in zip(a_refs, res[no:]):
                r[...] += v
            return carry

        lax.fori_loop(0, tm // rc, chunk, 0)

    in_specs = [pl.BlockSpec((tm, w), functools.partial(lambda i, cb: (i, cb), cb=cb)) for _, w, cb in rows]
    in_specs += [pl.BlockSpec(c.shape, lambda i: (0, 0)) for c in consts]
    out_specs = [pl.BlockSpec((tm, n), lambda i: (i, 0)) for n, _ in outs]
    out_specs += [pl.BlockSpec((1, n), lambda i: (0, 0)) for n in accs]
    out_shape = [jax.ShapeDtypeStruct((m, n), d) for n, d in outs] + [jax.ShapeDtypeStruct((1, n), F32) for n in accs]
    res = pl.pallas_call(
        body, name=name, grid=(m // tm,),
        in_specs=in_specs, out_specs=out_specs, out_shape=out_shape,
        compiler_params=_cparams(("arbitrary",)),
    )(*[r for r, _, _ in rows], *consts)
    return res


def _colsum(v):
    return jnp.sum(v, axis=0, keepdims=True)


def _rms_parts(xv):
    r = lax.rsqrt(jnp.mean(xv * xv, axis=-1, keepdims=True) + EPS)
    return r, xv * r


def _rms_fwd(xv, g, *, name):
    d = xv.shape[1]

    def fn(xb, gb):
        _, xh = _rms_parts(xb)
        return xh * gb

    return _rowwise(fn, [xv], [g], [(d, BF16)], name=name)[0]


def _rms_bwd(xv, dy, g, add, *, name):
    d = xv.shape[1]

    def fn(xb, dyb, addb, gb):
        r, xh = _rms_parts(xb)
        t = dyb * gb
        dx = r * (t - xh * jnp.mean(t * xh, axis=-1, keepdims=True)) + addb
        return dx, dx, _colsum(dyb * xh)

    return _rowwise(fn, [xv, dy, add], [g], [(d, F32), (d, BF16)], [d], name=name)


def _silu_parts(a):
    sg = _sigmoid(a)
    return a * sg, sg * (1.0 + a * (1.0 - sg))


def _rows(v):
    return v.reshape(v.shape[0] * v.shape[1], v.shape[2])


def _ffn_fwd(h, g, ov, *, tag):
    s = h.shape[0]
    n = _rms_fwd(h, g, name=f"{tag}_norm")
    a = ov.mm(n, ov.weight(f"{tag}_w_gate"), mode="nn", name=f"{tag}_gate", out_dtype=BF16, out_chunked=True)
    b = ov.mm(n, ov.weight(f"{tag}_w_up"), mode="nn", name=f"{tag}_up", out_dtype=BF16, out_chunked=True)
    c = a.shape[2]
    hm = _rowwise(lambda av, bv: _silu_parts(av.astype(F32))[0] * bv.astype(F32), [_rows(a), _rows(b)], [], [(c, BF16)],
                  name=f"{tag}_act")[0]
    hm = hm.reshape(N_DEV, s, c)
    out = ov.mm(hm, ov.weight(f"{tag}_w_down"), mode="nn", name=f"{tag}_down", add=h, scale=0.5)
    return out, (n, a, b, hm)


def _ffn_bwd(h, g, ov, saved, dout, dout_bf, *, tag):
    n, a, b, hm = saved
    wg, wu, wd = ov.weight(f"{tag}_w_gate"), ov.weight(f"{tag}_w_up"), ov.weight(f"{tag}_w_down")
    s, c = h.shape[0], wg.shape[2]
    d_wd = ov.mm(hm, dout_bf, mode="tn", name=f"{tag}_down_dw", scale=0.5, out_dtype=BF16)
    ov.grad(f"{tag}_w_down", d_wd)
    dhm = ov.mm(dout_bf, wd, mode="nt", name=f"{tag}_down_dx", scale=0.5, out_dtype=BF16, out_chunked=True)

    def act_bwd(av, bv, dv):
        av, bv, dv = av.astype(F32), bv.astype(F32), dv.astype(F32)
        si, dsi = _silu_parts(av)
        return dv * bv * dsi, dv * si

    da, db = _rowwise(act_bwd, [_rows(a), _rows(b), _rows(dhm)], [], [(c, BF16), (c, BF16)], name=f"{tag}_act_bwd")
    da, db = da.reshape(N_DEV, s, c), db.reshape(N_DEV, s, c)
    ov.grad(f"{tag}_w_gate", ov.mm(n, da, mode="tn", name=f"{tag}_gate_dw", out_dtype=BF16, out_chunked=True))
    ov.grad(f"{tag}_w_up", ov.mm(n, db, mode="tn", name=f"{tag}_up_dw", out_dtype=BF16, out_chunked=True))
    dn = ov.mm(da, wg, mode="nt", name=f"{tag}_gate_dx")
    dn = ov.mm(db, wu, mode="nt", name=f"{tag}_up_dx", add=dn)
    dh, dh_bf, dg = _rms_bwd(h, dn, g, dout, name=f"{tag}_norm_bwd")
    return dh, dh_bf, dg


def _cumsum(xv, *, reverse, name):
    h, s = xv.shape
    t = _pick(s, (512, 256, 128))
    nb = s // t

    def blk(j):
        return (0, nb - 1 - j) if reverse else (0, j)

    def body(x_ref, o_ref, carry):
        @pl.when(pl.program_id(0) == 0)
        def _():
            carry[...] = jnp.zeros_like(carry)

        i0 = lax.broadcasted_iota(jnp.int32, (t, t), 0)
        i1 = lax.broadcasted_iota(jnp.int32, (t, t), 1)
        tri = ((i0 >= i1) if reverse else (i0 <= i1)).astype(F32)
        xb = x_ref[...]
        o_ref[...] = jnp.dot(xb, tri, precision=HIGHEST, preferred_element_type=F32) + carry[...]
        carry[...] += jnp.sum(xb, axis=1, keepdims=True)

    return pl.pallas_call(
        body, name=name, grid=(nb,),
        in_specs=[pl.BlockSpec((h, t), blk)], out_specs=pl.BlockSpec((h, t), blk),
        out_shape=jax.ShapeDtypeStruct((h, s), F32),
        scratch_shapes=[pltpu.VMEM((h, 1), F32)],
        compiler_params=_cparams(("arbitrary",)),
    )(xv)


def _fox_tiles(s):
    t = _pick(s, (512, 256, 128))
    return t, t


def _causal(sc):
    t = sc.shape[0]
    keep = lax.broadcasted_iota(jnp.int32, (t, t), 1) <= lax.broadcasted_iota(jnp.int32, (t, t), 0)
    return jnp.where(keep, sc, -jnp.inf)


ATTN_HEADS_PER_STEP = 2


def _head_cols(hh):
    return slice(hh * HEAD_DIM, (hh + 1) * HEAD_DIM)


def _across(rowstat, width):
    return jnp.tile(rowstat, (1, width // HEAD_DIM))


def _fox_fwd(zf, f_rep, f_row, *, heads, name, comm=()):
    s = zf.shape[0]
    tq, tk = _fox_tiles(s)
    assert tq == tk
    nq, nk = s // tq, s // tk
    hp = ATTN_HEADS_PER_STEP if heads % ATTN_HEADS_PER_STEP == 0 else 1
    wb = hp * HEAD_DIM
    scale = HEAD_DIM ** -0.5
    w = heads * HEAD_DIM

    def body(q_ref, k_ref, v_ref, fq_ref, fk_ref, o32_ref, o16_ref, lse_ref, m_sc, l_sc, acc_sc):
        i, j = pl.program_id(1), pl.program_id(2)

        @pl.when(j == 0)
        def _():
            m_sc[...] = jnp.full_like(m_sc, -jnp.inf)
            l_sc[...] = jnp.zeros_like(l_sc)
            acc_sc[...] = jnp.zeros_like(acc_sc)

        def step(diagonal):
            for hh in range(hp):
                cols = _head_cols(hh)
                sc = _dot(q_ref[:, cols], k_ref[:, cols], NT) * scale + _across(fq_ref[:, cols], tk) - fk_ref[hh]
                if diagonal:
                    sc = _causal(sc)
                m_old = m_sc[hh]
                m_new = jnp.maximum(m_old, jnp.max(sc, axis=-1, keepdims=True))
                alpha = jnp.exp(m_old - m_new)
                pr = jnp.exp(sc - _across(m_new, tk))
                l_sc[hh] = alpha * l_sc[hh] + jnp.sum(pr, axis=-1, keepdims=True)
                acc_sc[hh] = alpha * acc_sc[hh] + _dot(pr, v_ref[:, cols], NN)
                m_sc[hh] = m_new

        @pl.when(j < i)
        def _():
            step(False)

        @pl.when(j == i)
        def _():
            step(True)

        @pl.when(j == nk - 1)
        def _():
            for hh in range(hp):
                cols = _head_cols(hh)
                o = acc_sc[hh] / l_sc[hh]
                o32_ref[:, cols] = o
                o16_ref[:, cols] = o.astype(BF16)
                lse_ref[:, cols] = m_sc[hh] + jnp.log(l_sc[hh])

    def kv_blk(off):
        return lambda h, i, j: (jnp.minimum(j, i), off + h)

    o_spec = pl.BlockSpec((tq, wb), lambda h, i, j: (i, h))
    stat = pltpu.VMEM((hp, tq, HEAD_DIM), F32)
    return _call(
        body, name=name, grid=(heads // hp, nq, nk),
        in_specs=[
            o_spec,
            pl.BlockSpec((tk, wb), kv_blk(heads // hp)),
            pl.BlockSpec((tk, wb), kv_blk(2 * heads // hp)),
            o_spec,
            pl.BlockSpec((hp, 1, tk), lambda h, i, j: (h, 0, jnp.minimum(j, i))),
        ],
        out_specs=[o_spec, o_spec, o_spec],
        out_shape=[jax.ShapeDtypeStruct((s, w), F32), jax.ShapeDtypeStruct((s, w), BF16),
                   jax.ShapeDtypeStruct((s, w), F32)],
        scratch_shapes=[stat, stat, stat],
        semantics=("parallel", "parallel", "arbitrary"), operands=[zf, zf, zf, f_rep, f_row], comm=comm)


def _fox_bwd(zf, do, lse, delta, f_rep, f_row, *, heads, name, comm=()):
    s = zf.shape[0]
    tq, tk = _fox_tiles(s)
    assert tq == tk
    nq, nk = s // tq, s // tk
    hp = ATTN_HEADS_PER_STEP if heads % ATTN_HEADS_PER_STEP == 0 else 1
    wb = hp * HEAD_DIM
    scale = HEAD_DIM ** -0.5
    w = heads * HEAD_DIM

    def body(q_ref, k_ref, v_ref, do_ref, lse_ref, dl_ref, fq_ref, fk_ref, dq_ref, dk_ref, dv_ref, dfq_ref, dfk_ref):
        j, i = pl.program_id(1), pl.program_id(2)

        @pl.when((j == 0) & (i == 0))
        def _():
            dq_ref[...] = jnp.zeros_like(dq_ref)
            dfq_ref[...] = jnp.zeros_like(dfq_ref)

        @pl.when(i == 0)
        def _():
            dk_ref[...] = jnp.zeros_like(dk_ref)
            dv_ref[...] = jnp.zeros_like(dv_ref)
            dfk_ref[...] = jnp.zeros_like(dfk_ref)

        def step(diagonal):
            rows = pl.ds(pl.multiple_of(i * tq, tq), tq)
            for hh in range(hp):
                cols = _head_cols(hh)
                q, k, v = q_ref[:, cols], k_ref[:, cols], v_ref[:, cols]
                dob = do_ref[:, cols].astype(BF16)
                sc = _dot(q, k, NT) * scale + _across(fq_ref[:, cols], tk) - fk_ref[hh]
                if diagonal:
                    sc = _causal(sc)
                pr = jnp.exp(sc - _across(lse_ref[:, cols], tk))
                dv_ref[:, cols] += _dot(pr, dob, TN)
                dp = _dot(dob, v, NT)
                ds = pr * (dp - _across(dl_ref[:, cols], tk))
                dsb = ds.astype(BF16)
                dk_ref[:, cols] += _dot(dsb, q, TN) * scale
                dq_ref[rows, cols] += _dot(dsb, k, NN) * scale
                dfq_ref[rows, cols] += jnp.broadcast_to(jnp.sum(ds, axis=1, keepdims=True), (tq, HEAD_DIM))
                dfk_ref[hh] -= jnp.sum(ds, axis=0, keepdims=True)

        @pl.when(i > j)
        def _():
            step(False)

        @pl.when(i == j)
        def _():
            step(True)

    q_spec = pl.BlockSpec((tq, wb), lambda h, j, i: (jnp.maximum(i, j), h))
    k_spec = pl.BlockSpec((tk, wb), lambda h, j, i: (j, h))
    whole = pl.BlockSpec((s, wb), lambda h, j, i: (0, h))
    row_spec = pl.BlockSpec((hp, 1, tk), lambda h, j, i: (h, 0, j))
    return _call(
        body, name=name, grid=(heads // hp, nk, nq),
        in_specs=[
            q_spec,
            pl.BlockSpec((tk, wb), lambda h, j, i: (j, heads // hp + h)),
            pl.BlockSpec((tk, wb), lambda h, j, i: (j, 2 * heads // hp + h)),
            q_spec, q_spec, q_spec, q_spec, row_spec,
        ],
        out_specs=[whole, k_spec, k_spec, whole, row_spec],
        out_shape=[jax.ShapeDtypeStruct((s, w), F32), jax.ShapeDtypeStruct((s, w), F32),
                   jax.ShapeDtypeStruct((s, w), F32), jax.ShapeDtypeStruct((s, w), F32),
                   jax.ShapeDtypeStruct((heads, 1, s), F32)],
        scratch_shapes=[], semantics=("parallel", "arbitrary", "arbitrary"),
        operands=[zf, zf, zf, do, lse, delta, f_rep, f_row], comm=comm)


def _gla_rows(s):
    return _pick(s, (256, 128, 64))


def _gla_chunk_terms(la_c, tri):
    a_cum = jnp.dot(tri, la_c, precision=HIGHEST, preferred_element_type=F32)
    a_tot = jnp.sum(la_c, axis=0, keepdims=True)
    return jnp.exp(a_tot - a_cum), jnp.exp(a_tot)


def _gla_fwd(zr, la, *, heads, q_blk, k_blk, name):
    s = zr.shape[0]
    c = GLA_CHUNK
    rows = _gla_rows(s)
    nsteps, ncs = s // rows, rows // c
    scale = HEAD_DIM ** -0.5

    assert q_blk % heads == 0 and k_blk % heads == 0

    def body(q_ref, k_ref, v_ref, la_ref, o_ref, st_ref, state):
        @pl.when(pl.program_id(0) == 0)
        def _():
            state[...] = jnp.zeros_like(state)

        tri = (lax.broadcasted_iota(jnp.int32, (c, c), 0) >= lax.broadcasted_iota(jnp.int32, (c, c), 1)).astype(F32)
        for t in range(ncs):
            sl = slice(t * c, (t + 1) * c)
            for h in range(heads):
                kc, vc = _head_cols(h), slice(h * GLA_VAL_DIM, (h + 1) * GLA_VAL_DIM)
                dec, e_tot = _gla_chunk_terms(la_ref[sl, kc], tri)
                kd = k_ref[sl, kc] * dec
                st_ref[h, t] = state[h]
                new = state[h] * e_tot + _dot(v_ref[sl, vc], kd, TN)
                state[h] = new
                o_ref[sl, vc] = _dot(q_ref[sl, kc] * scale, new, NT)

    kw, vw = heads * HEAD_DIM, heads * GLA_VAL_DIM
    return pl.pallas_call(
        body, name=name, grid=(nsteps,),
        in_specs=[
            pl.BlockSpec((rows, kw), lambda i: (i, q_blk // heads)),
            pl.BlockSpec((rows, kw), lambda i: (i, k_blk // heads)),
            pl.BlockSpec((rows, vw), lambda i: (i, 0)),
            pl.BlockSpec((rows, kw), lambda i: (i, 0)),
        ],
        out_specs=[
            pl.BlockSpec((rows, vw), lambda i: (i, 0)),
            pl.BlockSpec((heads, ncs, GLA_VAL_DIM, HEAD_DIM), lambda i: (0, i, 0, 0)),
        ],
        out_shape=[jax.ShapeDtypeStruct((s, vw), F32),
                   jax.ShapeDtypeStruct((heads, s // c, GLA_VAL_DIM, HEAD_DIM), F32)],
        scratch_shapes=[pltpu.VMEM((heads, GLA_VAL_DIM, HEAD_DIM), F32)],
        compiler_params=_cparams(("arbitrary",)),
    )(zr, zr, zr, la)


def _gla_bwd(zr, la, do, states, *, heads, q_blk, k_blk, name):
    s = zr.shape[0]
    c = GLA_CHUNK
    rows = _gla_rows(s)
    nsteps, ncs = s // rows, rows // c
    scale = HEAD_DIM ** -0.5

    assert q_blk % heads == 0 and k_blk % heads == 0

    def body(q_ref, k_ref, v_ref, la_ref, do_ref, st_ref, dq_ref, dk_ref, dv_ref, dla_ref, dstate):
        @pl.when(pl.program_id(0) == 0)
        def _():
            dstate[...] = jnp.zeros_like(dstate)

        i0 = lax.broadcasted_iota(jnp.int32, (c, c), 0)
        i1 = lax.broadcasted_iota(jnp.int32, (c, c), 1)
        tri = (i0 >= i1).astype(F32)
        strict = (i0 > i1).astype(F32)
        for t in reversed(range(ncs)):
            sl = slice(t * c, (t + 1) * c)
            for h in range(heads):
                kc, vc = _head_cols(h), slice(h * GLA_VAL_DIM, (h + 1) * GLA_VAL_DIM)
                dec, e_tot = _gla_chunk_terms(la_ref[sl, kc], tri)
                kd = k_ref[sl, kc] * dec
                kdb = kd.astype(BF16)
                vb = v_ref[sl, vc].astype(BF16)
                dob = do_ref[sl, vc].astype(BF16)
                prev = st_ref[h, t]
                cur = prev * e_tot + _dot(vb, kdb, TN)
                d_cur = dstate[h] + _dot(dob, q_ref[sl, kc] * scale, TN)
                d_cur_b = d_cur.astype(BF16)
                dq_ref[sl, kc] = _dot(dob, cur, NN) * scale
                dv_ref[sl, vc] = _dot(kdb, d_cur_b, NT)
                dkd = _dot(vb, d_cur_b, NN)
                d_tot = e_tot * jnp.sum(d_cur * prev, axis=0, keepdims=True)
                dk_ref[sl, kc] = dkd * dec
                dla_ref[sl, kc] = d_tot + jnp.dot(strict, dkd * kd, precision=HIGHEST, preferred_element_type=F32)
                dstate[h] = d_cur * e_tot

    def rev(i):
        return nsteps - 1 - i

    kw, vw = heads * HEAD_DIM, heads * GLA_VAL_DIM
    kq_spec = pl.BlockSpec((rows, kw), lambda i: (rev(i), 0))
    v_spec = pl.BlockSpec((rows, vw), lambda i: (rev(i), 0))
    return pl.pallas_call(
        body, name=name, grid=(nsteps,),
        in_specs=[
            pl.BlockSpec((rows, kw), lambda i: (rev(i), q_blk // heads)),
            pl.BlockSpec((rows, kw), lambda i: (rev(i), k_blk // heads)),
            v_spec, kq_spec, v_spec,
            pl.BlockSpec((heads, ncs, GLA_VAL_DIM, HEAD_DIM), lambda i: (0, rev(i), 0, 0)),
        ],
        out_specs=[kq_spec, kq_spec, v_spec, kq_spec],
        out_shape=[jax.ShapeDtypeStruct((s, kw), F32), jax.ShapeDtypeStruct((s, kw), F32),
                   jax.ShapeDtypeStruct((s, vw), F32), jax.ShapeDtypeStruct((s, kw), F32)],
        scratch_shapes=[pltpu.VMEM((heads, GLA_VAL_DIM, HEAD_DIM), F32)],
        compiler_params=_cparams(("arbitrary",)),
    )(zr, zr, zr, la, do, states)


def _my_place():
    x, y, c = lax.axis_index("x"), lax.axis_index("y"), lax.axis_index("c")
    return x, y, c


def _gather_ops(x_ref, out_ref, send_sems, recv_sems, local_sem):
    def plan():
        x, y, c = _my_place()
        me, sibling = (x, y, c), (x, y, 1 - c)
        chips = [(1 - x, y), (x, 1 - y), (1 - x, 1 - y)]

        def blk(px, py, pc):
            return out_ref.at[4 * px + 2 * py + pc]

        def copy(k, block, to, src=None):
            return pltpu.make_async_remote_copy(
                src_ref=blk(*block) if src is None else src, dst_ref=blk(*block),
                send_sem=send_sems.at[k], recv_sem=recv_sems.at[k], device_id=to, device_id_type=MESH)

        mine = pltpu.make_async_copy(x_ref, blk(*me), local_sem)
        first = [copy(0, me, sibling, src=x_ref)]
        first += [copy(1 + j, me, (*chip, c), src=x_ref) for j, chip in enumerate(chips)]
        passed = [copy(4 + j, (*chip, c), sibling) for j, chip in enumerate(chips)]
        landed = [copy(1 + j, (*chip, c), me) for j, chip in enumerate(chips)]
        from_sibling = [copy(0, sibling, me)] + [copy(4 + j, (*chip, 1 - c), me) for j, chip in enumerate(chips)]
        return mine, first, passed, landed, from_sibling

    def start():
        mine, first, _, _, _ = plan()
        mine.start()
        for cp in first:
            cp.start()

    def finish():
        mine, first, passed, landed, from_sibling = plan()
        for cp, fwd in zip(landed, passed):
            cp.wait_recv()
            fwd.start()
        for cp in from_sibling:
            cp.wait_recv()
        for cp in first + passed:
            cp.wait_send()
        mine.wait()

    return start, finish


def _exchange_ops(scatter, s_ref, r_ref, send_sems, recv_sems, local_sem):
    def plan():
        x, y, c = _my_place()
        me = 4 * x + 2 * y + c
        mine = pltpu.make_async_copy(s_ref.at[me] if scatter else s_ref, r_ref.at[me], local_sem)
        sends, recvs = [], []
        for k in range(1, N_DEV):
            px, py, pc = x ^ ((k >> 2) & 1), y ^ ((k >> 1) & 1), c ^ (k & 1)
            peer = 4 * px + 2 * py + pc
            src = s_ref.at[peer] if scatter else s_ref
            for dst, out in ((r_ref.at[me], sends), (r_ref.at[peer], recvs)):
                out.append(pltpu.make_async_remote_copy(
                    src_ref=src, dst_ref=dst, send_sem=send_sems.at[k - 1], recv_sem=recv_sems.at[k - 1],
                    device_id=(px, py, pc), device_id_type=MESH))
        return mine, sends, recvs

    def start():
        mine, sends, _ = plan()
        mine.start()
        for cp in sends:
            cp.start()

    def finish():
        mine, sends, recvs = plan()
        for cp in recvs:
            cp.wait_recv()
        for cp in sends:
            cp.wait_send()
        mine.wait()

    return start, finish


def _pair_ops(s_ref, r_ref, send_sems, recv_sems):
    def plan():
        x, y, c = _my_place()
        return [pltpu.make_async_remote_copy(
            src_ref=s_ref.at[2 * q + 1 - c], dst_ref=r_ref.at[q], send_sem=send_sems.at[q], recv_sem=recv_sems.at[q],
            device_id=(x, y, 1 - c), device_id_type=MESH) for q in range(N_DEV // 2)]

    def start():
        for cp in plan():
            cp.start()

    def finish():
        copies = plan()
        for cp in copies:
            cp.wait_recv()
        for cp in copies:
            cp.wait_send()

    return start, finish


def _chips_ops(p_ref, r_ref, send_sems, recv_sems, local_sem):
    def plan():
        x, y, c = _my_place()
        chip = 2 * x + y
        mine = pltpu.make_async_copy(p_ref.at[chip], r_ref.at[chip], local_sem)
        sends, recvs = [], []
        for k in range(1, N_DEV // 2):
            px, py = x ^ (k >> 1), y ^ (k & 1)
            peer = 2 * px + py
            for dst, out in ((r_ref.at[chip], sends), (r_ref.at[peer], recvs)):
                out.append(pltpu.make_async_remote_copy(
                    src_ref=p_ref.at[peer], dst_ref=dst, send_sem=send_sems.at[k - 1], recv_sem=recv_sems.at[k - 1],
                    device_id=(px, py, c), device_id_type=MESH))
        return mine, sends, recvs

    def start():
        mine, sends, _ = plan()
        mine.start()
        for cp in sends:
            cp.start()

    def finish():
        mine, sends, recvs = plan()
        for cp in recvs:
            cp.wait_recv()
        for cp in sends:
            cp.wait_send()
        mine.wait()

    return start, finish


def _comm_ops(kind, src_ref, dst_ref, send_sems, recv_sems, local_sem):
    if kind == "gather":
        return _gather_ops(src_ref, dst_ref, send_sems, recv_sems, local_sem)
    if kind == "pair":
        return _pair_ops(src_ref, dst_ref, send_sems, recv_sems)
    if kind == "chips":
        return _chips_ops(src_ref, dst_ref, send_sems, recv_sems, local_sem)
    return _exchange_ops(False, src_ref, dst_ref, send_sems, recv_sems, local_sem)


def _comm_out_shape(kind, v):
    shape = {"pair": (N_DEV // 2,) + v.shape[1:], "chips": v.shape}.get(kind, (N_DEV,) + v.shape)
    return jax.ShapeDtypeStruct(shape, v.dtype)


def _comm_scratch(n_tasks):
    return [pltpu.SemaphoreType.DMA((N_DEV - 1,)), pltpu.SemaphoreType.DMA((N_DEV - 1,)), pltpu.SemaphoreType.DMA] * n_tasks


def _pair_sum(s, r, *, name):
    _, rows, cdim = s.shape
    tr = _pick(rows, (1024, 704, 512, 256, 128, 64, 32, 16, 8))
    rc = min(16, tr)

    def body(c_ref, s_ref, r_ref, o_ref):
        def chunk(ci, carry):
            sl = pl.ds(pl.multiple_of(ci * rc, rc), rc)
            o_ref[sl, :] = (s_ref[sl, :].astype(F32) + r_ref[sl, :].astype(F32)).astype(o_ref.dtype)
            return carry

        lax.fori_loop(0, tr // rc, chunk, 0)

    spec = pl.BlockSpec((None, tr, cdim), lambda q, i, c_ref: (q, i, 0))
    return pl.pallas_call(
        body, name=name,
        grid_spec=pltpu.PrefetchScalarGridSpec(
            num_scalar_prefetch=1, grid=(N_DEV // 2, rows // tr),
            in_specs=[pl.BlockSpec((None, None, tr, cdim), lambda q, i, c_ref: (q, c_ref[0], i, 0)), spec],
            out_specs=spec),
        out_shape=jax.ShapeDtypeStruct(r.shape, s.dtype),
        compiler_params=_cparams(("parallel", "parallel")),
    )(lax.axis_index("c").astype(jnp.int32).reshape(1), s.reshape(N_DEV // 2, 2, rows, cdim), r)


def _comm(kind, v, *, name):
    def body(s_ref, r_ref, send_sems, recv_sems, local_sem):
        start, finish = _comm_ops(kind, s_ref, r_ref, send_sems, recv_sems, local_sem)
        start()
        finish()

    return pl.pallas_call(
        body, name=name,
        out_shape=_comm_out_shape(kind, v),
        in_specs=[pl.BlockSpec(memory_space=pl.ANY)],
        out_specs=pl.BlockSpec(memory_space=pl.ANY),
        scratch_shapes=_comm_scratch(1),
    )(v)


class _Overlap:
    US_PER_MB = {"gather": 52.0, "pair": 1.0, "chips": 13.0, "bcast": 97.0}
    MM_FLOPS_PER_US = 6.0e8

    def __init__(self):
        self.queue, self.results, self.then = [], {}, {}

    def add(self, key, kind, v):
        self.queue.append((key, kind, v))

    def _cost(self, kind, v):
        return v.size * v.dtype.itemsize / 2 ** 20 * self.US_PER_MB[kind]

    def take(self, budget_us):
        taken, cum = [], 0.0
        while self.queue:
            cost = self._cost(*self.queue[0][1:])
            if taken and cum + cost > 1.25 * budget_us:
                break
            taken.append(self.queue.pop(0))
            cum += cost
        return taken

    def put(self, taken, res):
        for (key, _, _), r in zip(taken, res):
            self.results[key] = r
            if key in self.then:
                self.then.pop(key)(r)

    def carry(self, budget_us, fn):
        taken = self.take(budget_us)
        out, res = fn([(kind, v) for _, kind, v in taken])
        self.put(taken, res)
        return out

    def mm(self, a, b, *, mode, **kw):
        la, lb = _logical(a), _logical(b)
        budget = 2.0 * la[0] * la[1] * (lb[0] if mode == "nt" else lb[1]) / self.MM_FLOPS_PER_US

        def fn(comm):
            return _mm(a, b, mode=mode, comm=comm, **kw) if comm else (_mm(a, b, mode=mode, **kw), [])

        return self.carry(budget, fn)

    def get(self, key):
        while key not in self.results:
            keys = [k for k, _, _ in self.queue]
            task = self.queue.pop(keys.index(key if key in keys else "pair_" + key))
            self.put([task], [_comm(task[1], task[2], name=f"alone_{task[0]}")])
        return self.results[key]

    def weight(self, n):
        g = self.get(n)
        return g.reshape(-1, g.shape[2]) if n in ROW_SHARDED else g

    def grad(self, n, g):
        g = g if g.ndim == 3 else g.reshape(N_DEV, g.shape[0] // N_DEV, g.shape[1])
        self.add("pair_d_" + n, "pair", g)
        self.then["pair_d_" + n] = lambda r: self.add("d_" + n, "chips", _pair_sum(g, r, name=f"pair_sum_{n}"))


def _sel_tables(dest, ws, wp, tw):
    n_tiles = (int(dest.max()) + tw) // tw
    tbl = np.full((N_DEV, wp), -1, np.int32)
    for j in range(N_DEV):
        tbl[j, :ws] = dest[j * ws:(j + 1) * ws]
    by_tile = [sorted({j for j in range(N_DEV) if ((tbl[j] // tw) == t).any()}) for t in range(n_tiles)]
    by_shard = [sorted({int(t) for t in np.unique(tbl[j, :ws] // tw)}) for j in range(N_DEV)]

    def table(lists):
        width = max(len(v) for v in lists)
        idx = np.array([(v + [v[-1]] * width)[:width] if v else [0] * width for v in lists], np.int32)
        val = np.array([[1] * len(v) + [0] * (width - len(v)) for v in lists], np.int32)
        return idx.reshape(-1), val.reshape(-1), width

    return tbl[:, :, None], table(by_tile), table(by_shard)


def _sel_matrix(d_ref, t, wp, tw):
    cols = t * tw + lax.broadcasted_iota(jnp.int32, (wp, tw), 1)
    return (d_ref[...] == cols).astype(BF16)


def _win_unshard(g, tbl, idx, val, width, *, tw, padded, name):
    _, dm, wp = g.shape
    tm = _pick(dm, (1024, 512, 256, 128))

    def body(idx_ref, val_ref, g_ref, d_ref, o_ref, acc):
        t, s_ = pl.program_id(1), pl.program_id(2)

        @pl.when(s_ == 0)
        def _():
            acc[...] = jnp.zeros_like(acc)

        @pl.when(val_ref[t * width + s_] == 1)
        def _():
            acc[...] += _dot(g_ref[...], _sel_matrix(d_ref, t, wp, tw), NN)

        @pl.when(s_ == width - 1)
        def _():
            o_ref[...] = acc[...].astype(BF16)

    return pl.pallas_call(
        body, name=name,
        grid_spec=pltpu.PrefetchScalarGridSpec(
            num_scalar_prefetch=2, grid=(dm // tm, padded // tw, width),
            in_specs=[pl.BlockSpec((None, tm, wp), lambda i, t, s_, ix, vl: (ix[t * width + s_], i, 0)),
                      pl.BlockSpec((None, wp, 1), lambda i, t, s_, ix, vl: (ix[t * width + s_], 0, 0))],
            out_specs=pl.BlockSpec((tm, tw), lambda i, t, s_, ix, vl: (i, t)),
            scratch_shapes=[pltpu.VMEM((tm, tw), F32)]),
        out_shape=jax.ShapeDtypeStruct((dm, padded), BF16),
        compiler_params=_cparams(("parallel", "parallel", "arbitrary")),
    )(idx, val, g, tbl)


def _win_to_shards(dw, tbl, idx, val, width, *, tw, wp, name):
    dm = dw.shape[0]
    tm = _pick(dm, (1024, 512, 256, 128))

    def body(idx_ref, val_ref, w_ref, d_ref, o_ref, acc):
        j, s_ = pl.program_id(1), pl.program_id(2)

        @pl.when(s_ == 0)
        def _():
            acc[...] = jnp.zeros_like(acc)

        @pl.when(val_ref[j * width + s_] == 1)
        def _():
            acc[...] += _dot(w_ref[...], _sel_matrix(d_ref, idx_ref[j * width + s_], wp, tw), NT)

        @pl.when(s_ == width - 1)
        def _():
            o_ref[...] = acc[...].astype(BF16)

    return pl.pallas_call(
        body, name=name,
        grid_spec=pltpu.PrefetchScalarGridSpec(
            num_scalar_prefetch=2, grid=(dm // tm, N_DEV, width),
            in_specs=[pl.BlockSpec((tm, tw), lambda i, j, s_, ix, vl: (i, ix[j * width + s_])),
                      pl.BlockSpec((None, wp, 1), lambda i, j, s_, ix, vl: (j, 0, 0))],
            out_specs=pl.BlockSpec((None, tm, wp), lambda i, j, s_, ix, vl: (j, i, 0)),
            scratch_shapes=[pltpu.VMEM((tm, wp), F32)]),
        out_shape=jax.ShapeDtypeStruct((N_DEV, dm, wp), BF16),
        compiler_params=_cparams(("parallel", "parallel", "arbitrary")),
    )(idx, val, dw, tbl)


def _adamw(parts, w, m, v, *, name):
    r, cdim = w.shape
    n_parts = parts.shape[0]
    tr = _pick(r, (256, 128, 64, 32, 16, 8))
    rc = min(16, tr)
    c1 = 1.0 - ADAM_B1 ** ADAM_STEP
    c2 = 1.0 - ADAM_B2 ** ADAM_STEP

    def body(p_ref, w_ref, m_ref, v_ref, g_ref, d_ref, mo_ref, vo_ref):
        def chunk(ci, carry):
            sl = pl.ds(pl.multiple_of(ci * rc, rc), rc)
            g = p_ref[0, sl, :].astype(F32)
            for i in range(1, n_parts):
                g = g + p_ref[i, sl, :].astype(F32)
            mn = ADAM_B1 * m_ref[sl, :] + (1.0 - ADAM_B1) * g
            vn = ADAM_B2 * v_ref[sl, :] + (1.0 - ADAM_B2) * jnp.square(g)
            m_hat = mn / c1
            v_hat = vn / c2
            g_ref[sl, :] = g
            d_ref[sl, :] = -ADAM_LR * (m_hat / (jnp.sqrt(v_hat) + ADAM_EPS) + ADAM_WD * w_ref[sl, :])
            mo_ref[sl, :] = mn
            vo_ref[sl, :] = vn
            return carry

        lax.fori_loop(0, tr // rc, chunk, 0)

    spec = pl.BlockSpec((tr, cdim), lambda i: (i, 0))
    return pl.pallas_call(
        body, name=name, grid=(r // tr,),
        in_specs=[pl.BlockSpec((n_parts, tr, cdim), lambda i: (0, i, 0)), spec, spec, spec],
        out_specs=[spec] * 4,
        out_shape=[jax.ShapeDtypeStruct((r, cdim), F32)] * 4,
        compiler_params=_cparams(("parallel",)),
    )(parts, w, m, v)


def _pad_to(v, n):
    return v if v.shape[0] == n else jnp.concatenate([v, jnp.zeros((n - v.shape[0],), v.dtype)])


def _pad_cols(v, n):
    return v if v.shape[-1] == n else jnp.concatenate([v, jnp.zeros(v.shape[:-1] + (n - v.shape[-1],), v.dtype)], axis=-1)


def _pack(vs, cols, row_mult, dtype):
    offs, o = [], 0
    for v in vs:
        offs.append(o)
        o += v.size
    rows = -(-o // cols)
    rows = -(-rows // row_mult) * row_mult
    flat = jnp.concatenate([v.reshape(-1).astype(dtype) for v in vs])
    return _pad_to(flat, rows * cols).reshape(rows, cols), offs


SHARDED = ("ffn1_w_gate", "ffn1_w_up", "ffn1_w_down", "w_in", "w_merge_gate", "gla_gate_up", "w_branch_fox",
           "w_branch_gla", "w_out", "ffn2_w_gate", "ffn2_w_up", "ffn2_w_down", "w_ple_gate", "w_ple_proj")
ROW_SHARDED = ("ffn1_w_down", "w_out", "ffn2_w_down", "w_ple_gate")
REPLICATED = ("ffn1_norm", "mix_norm", "fox_forget_bias", "gla_gate_bias", "gla_head_norm", "b_merge_gate",
              "ffn2_norm", "ple_norm", "final_norm")
WEIGHTS = ("ffn1_norm", "ffn1_w_gate", "ffn1_w_up", "ffn1_w_down", "mix_norm", "w_in", "fox_forget_bias",
           "gla_gate_up", "gla_gate_bias", "gla_head_norm", "w_branch_fox", "w_branch_gla", "w_merge_gate",
           "b_merge_gate", "w_out", "ffn2_norm", "ffn2_w_gate", "ffn2_w_up", "ffn2_w_down", "ple_norm",
           "w_ple_proj", "w_ple_gate", "final_norm")


def kernel(x, p, ffn1_norm, ffn1_w_gate, ffn1_w_up, ffn1_w_down, mix_norm, w_in, fox_forget_bias, gla_gate_up, gla_gate_bias, gla_head_norm, w_branch_fox, w_branch_gla, w_merge_gate, b_merge_gate, w_out, ffn2_norm, ffn2_w_gate, ffn2_w_up, ffn2_w_down, ple_norm, w_ple_proj, w_ple_gate, final_norm, loss_target, m_ffn1_norm, m_ffn1_w_gate, m_ffn1_w_up, m_ffn1_w_down, m_mix_norm, m_w_in, m_fox_forget_bias, m_gla_gate_up, m_gla_gate_bias, m_gla_head_norm, m_w_branch_fox, m_w_branch_gla, m_w_merge_gate, m_b_merge_gate, m_w_out, m_ffn2_norm, m_ffn2_w_gate, m_ffn2_w_up, m_ffn2_w_down, m_ple_norm, m_w_ple_proj, m_w_ple_gate, m_final_norm, v_ffn1_norm, v_ffn1_w_gate, v_ffn1_w_up, v_ffn1_w_down, v_mix_norm, v_w_in, v_fox_forget_bias, v_gla_gate_up, v_gla_gate_bias, v_gla_head_norm, v_w_branch_fox, v_w_branch_gla, v_w_merge_gate, v_b_merge_gate, v_w_out, v_ffn2_norm, v_ffn2_w_gate, v_ffn2_w_up, v_ffn2_w_down, v_ple_norm, v_w_ple_proj, v_w_ple_gate, v_final_norm):
    args = dict(locals())
    wts = {n: args[n] for n in WEIGHTS}
    mom_m = {n: args["m_" + n] for n in WEIGHTS}
    mom_v = {n: args["v_" + n] for n in WEIGHTS}

    xs, ps, tgt = x[0], p[0, 0], loss_target[0]
    s, d = xs.shape
    fox_w = w_branch_fox.shape[1]
    gla_vw = w_branch_gla.shape[1]
    fox_heads = fox_w // HEAD_DIM
    gla_heads = gla_vw // GLA_VAL_DIM
    gla_kw = gla_heads * HEAD_DIM
    rank = gla_gate_up.shape[1]

    c_fl = 3 * fox_w
    o_gr, o_gq, o_gk = gla_vw, 2 * gla_vw, 2 * gla_vw + gla_kw
    o_fl = o_gk + gla_kw
    o_gd = o_fl + LANES
    rest_w = o_gd + LANES
    padded = c_fl + rest_w
    seg = [(c_fl, 0), (fox_heads, c_fl + o_fl), (gla_kw, c_fl + o_gq), (gla_kw, c_fl + o_gk), (gla_vw, c_fl),
           (gla_vw, c_fl + o_gr), (rank, c_fl + o_gd)]
    dest = np.concatenate([np.arange(w_, dtype=np.int32) + o_ for w_, o_ in seg])
    ws = w_in.shape[2]
    wp = -(-ws // LANES) * LANES
    tw = 256 if padded % 256 == 0 else LANES
    tbl, (t_idx, t_val, t_width), (s_idx, s_val, s_width) = _sel_tables(dest, ws, wp, tw)
    tbl = jnp.asarray(tbl)

    ov = _Overlap()
    for n in SHARDED:
        sh = wts[n][0].astype(BF16)
        ov.add(n, "gather", _pad_cols(sh, wp) if n == "w_in" else sh)
    fbias = _pad_cols(fox_forget_bias, LANES)
    bmg_f, bmg_g = b_merge_gate[:, :d], b_merge_gate[:, d:]
    ghn = jnp.tile(gla_head_norm, (1, gla_heads))

    h1, ffn1_saved = _ffn_fwd(xs, ffn1_norm, ov, tag="ffn1")
    u = _rms_fwd(h1, mix_norm, name="mix_norm")
    win = _win_unshard(ov.weight("w_in"), tbl, jnp.asarray(t_idx), jnp.asarray(t_val), t_width, tw=tw, padded=padded,
                       name="in_proj_unshard")
    win_fox, win_rest = win[:, :c_fl], win[:, c_fl:]
    zf = ov.mm(u, win_fox, mode="nn", name="in_proj_fox", out_dtype=BF16)
    zr = ov.mm(u, win_rest, mode="nn", name="in_proj_rest")
    gz = ov.mm(u, ov.weight("w_merge_gate"), mode="nn", name="merge_gate")
    gup = ov.weight("gla_gate_up").transpose(1, 0, 2).reshape(rank, gla_kw)
    gup = jnp.concatenate([gup, jnp.zeros((LANES - rank, gla_kw), BF16)], axis=0)

    log_f = _rowwise(lambda fl, b: _log_sigmoid(fl + b), [(zr, LANES, o_fl // LANES)], [fbias], [(LANES, F32)],
                     name="forget_gate")[0]
    f_cum = _cumsum(log_f[:, :fox_heads].T, reverse=False, name="forget_cumsum")
    f_rep = jnp.broadcast_to(f_cum.T[:, :, None], (s, fox_heads, HEAD_DIM)).reshape(s, fox_w)
    f_row = f_cum[:, None, :]
    attn_us = 2.0 * s * s * HEAD_DIM * fox_heads / _Overlap.MM_FLOPS_PER_US
    y_fox, y_fox_bf, lse = ov.carry(
        ATTN_FWD_MATMULS * attn_us, lambda comm: _fox_fwd(zf, f_rep, f_row, heads=fox_heads, name="fox_fwd", comm=comm))

    def decay_fn(gd, gupv, gb):
        return _log_sigmoid(_dot(gd, gupv, NN) + gb) * (1.0 / GLA_GATE_TAU)

    la = _rowwise(decay_fn, [(zr, LANES, o_gd // LANES)], [gup, gla_gate_bias], [(gla_kw, F32)], name="gla_decay", rc=128)[0]
    q_blk, k_blk = o_gq // HEAD_DIM, o_gk // HEAD_DIM
    o_gla, states = _gla_fwd(zr, la, heads=gla_heads, q_blk=q_blk, k_blk=k_blk, name="gla_fwd")

    def gla_out_fn(o, gr, g):
        outs = []
        for hh in range(gla_heads):
            sl = slice(hh * GLA_VAL_DIM, (hh + 1) * GLA_VAL_DIM)
            _, oh = _rms_parts(o[:, sl])
            outs.append(oh * g[:, sl] * _silu_parts(gr[:, sl])[0])
        return jnp.concatenate(outs, axis=1)

    y_gla = _rowwise(gla_out_fn, [o_gla, (zr, gla_vw, o_gr // gla_vw)], [ghn], [(gla_vw, BF16)], name="gla_out")[0]
    br_f = ov.mm(y_fox_bf, ov.weight("w_branch_fox"), mode="nn", name="branch_fox")
    br_g = ov.mm(y_gla, ov.weight("w_branch_gla"), mode="nn", name="branch_gla")

    def merge_fn(zf_, zg_, bf_, bg_, b1, b2):
        return _sigmoid(zf_ + b1) * bf_ + _sigmoid(zg_ + b2) * bg_

    merged = _rowwise(merge_fn, [(gz, d, 0), (gz, d, 1), br_f, br_g], [bmg_f, bmg_g], [(d, BF16)], name="merge")[0]
    h2 = ov.mm(merged, ov.weight("w_out"), mode="nn", name="out_proj", add=h1)
    h3, ffn2_saved = _ffn_fwd(h2, ffn2_norm, ov, tag="ffn2")
    n3 = _rms_fwd(h3, ple_norm, name="ple_norm")
    gl = ov.mm(n3, ov.weight("w_ple_gate"), mode="nn", name="ple_gate")
    pe = ov.mm(ps, ov.weight("w_ple_proj"), mode="nn", name="ple_proj")

    def head_fn(h3b, glb, peb, tb, gfin):
        pg = _sigmoid(glb)
        h4 = h3b + pg * peb
        r, xh = _rms_parts(h4)
        err = xh * gfin - tb
        dy = err * (1.0 / d)
        t = dy * gfin
        dh4 = r * (t - xh * jnp.mean(t * xh, axis=-1, keepdims=True))
        return dh4, dh4 * pg, dh4 * peb * pg * (1.0 - pg), _colsum(err * err), _colsum(dy * xh)

    dh4, dpe, dgl, loss_cols, d_final = _rowwise(
        head_fn, [h3, gl, pe, tgt], [final_norm.reshape(1, d)], [(d, F32), (d, BF16), (d, BF16)], [d, d], name="loss_head")
    loss = lax.psum(0.5 * jnp.sum(loss_cols) / d, AXES)

    grads = {"final_norm": d_final.reshape(d)}
    ov.grad("w_ple_proj", ov.mm(ps, dpe, mode="tn", name="ple_proj_dw", out_dtype=BF16, out_chunked=True))
    ov.grad("w_ple_gate", ov.mm(n3, dgl, mode="tn", name="ple_gate_dw", out_dtype=BF16))
    dn3 = ov.mm(dgl, ov.weight("w_ple_gate"), mode="nt", name="ple_gate_dx")
    dh3, dh3_bf, grads["ple_norm"] = _rms_bwd(h3, dn3, ple_norm, dh4, name="ple_norm_bwd")
    dh2, dh2_bf, grads["ffn2_norm"] = _ffn_bwd(h2, ffn2_norm, ov, ffn2_saved, dh3, dh3_bf, tag="ffn2")

    ov.grad("w_out", ov.mm(merged, dh2_bf, mode="tn", name="out_proj_dw", out_dtype=BF16))
    dmerged = ov.mm(dh2_bf, ov.weight("w_out"), mode="nt", name="out_proj_dx")

    def merge_bwd_fn(zf_, zg_, bf_, bg_, dm, b1, b2):
        sf, sg = _sigmoid(zf_ + b1), _sigmoid(zg_ + b2)
        dz = jnp.concatenate([dm * bf_ * sf * (1.0 - sf), dm * bg_ * sg * (1.0 - sg)], axis=1)
        return dm * sf, dm * sg, dz, _colsum(dz)

    dbr_f, dbr_g, dgz, grads["b_merge_gate"] = _rowwise(
        merge_bwd_fn, [(gz, d, 0), (gz, d, 1), br_f, br_g, dmerged], [bmg_f, bmg_g],
        [(d, BF16), (d, BF16), (2 * d, BF16)], [2 * d], name="merge_bwd")
    ov.grad("w_merge_gate", ov.mm(u, dgz, mode="tn", name="merge_gate_dw", out_dtype=BF16, out_chunked=True))
    ov.grad("w_branch_fox", ov.mm(y_fox_bf, dbr_f, mode="tn", name="branch_fox_dw", out_dtype=BF16, out_chunked=True))
    ov.grad("w_branch_gla", ov.mm(y_gla, dbr_g, mode="tn", name="branch_gla_dw", out_dtype=BF16, out_chunked=True))
    dy_fox = ov.mm(dbr_f, ov.weight("w_branch_fox"), mode="nt", name="branch_fox_dx")
    dy_gla = ov.mm(dbr_g, ov.weight("w_branch_gla"), mode="nt", name="branch_gla_dx")

    def gla_out_bwd_fn(o, gr, dy, g):
        dos, dgrs, dgs = [], [], []
        for hh in range(gla_heads):
            sl = slice(hh * GLA_VAL_DIM, (hh + 1) * GLA_VAL_DIM)
            r, oh = _rms_parts(o[:, sl])
            si, dsi = _silu_parts(gr[:, sl])
            don = dy[:, sl] * si
            dgrs.append(dy[:, sl] * oh * g[:, sl] * dsi)
            t = don * g[:, sl]
            dos.append(r * (t - oh * jnp.mean(t * oh, axis=-1, keepdims=True)))
            dgs.append(_colsum(don * oh))
        return jnp.concatenate(dos, axis=1), jnp.concatenate(dgrs, axis=1), jnp.concatenate(dgs, axis=1)

    do_gla, dgr, d_ghn = _rowwise(gla_out_bwd_fn, [o_gla, (zr, gla_vw, o_gr // gla_vw), dy_gla], [ghn],
                                  [(gla_vw, F32), (gla_vw, F32)], [gla_vw], name="gla_out_bwd")
    grads["gla_head_norm"] = d_ghn.reshape(gla_heads, GLA_VAL_DIM).sum(axis=0, keepdims=True)
    dgq, dgk, dgv, dla = _gla_bwd(zr, la, do_gla, states, heads=gla_heads, q_blk=q_blk, k_blk=k_blk, name="gla_bwd")

    def decay_bwd_fn(dl, gd, gupv, gb):
        pre = _dot(gd, gupv, NN) + gb
        dpre = dl * (1.0 / GLA_GATE_TAU) * _sigmoid(-pre)
        return _dot(dpre, gupv, NT), dpre, _colsum(dpre)

    dgd, dpre_bf, grads["gla_gate_bias"] = _rowwise(
        decay_bwd_fn, [dla, (zr, LANES, o_gd // LANES)], [gup, gla_gate_bias], [(LANES, F32), (gla_kw, BF16)], [gla_kw],
        name="gla_decay_bwd", rc=128)
    d_gup = ov.mm(zr[:, o_gd:o_gd + LANES], dpre_bf, mode="tn", name="gla_gate_up_dw")[:rank]
    ov.grad("gla_gate_up", d_gup.reshape(rank, N_DEV, gla_kw // N_DEV).transpose(1, 0, 2).astype(BF16))

    def delta_fn(dyv, yv):
        outs = []
        for hh in range(fox_heads):
            sl = slice(hh * HEAD_DIM, (hh + 1) * HEAD_DIM)
            outs.append(jnp.broadcast_to(jnp.sum(dyv[:, sl] * yv[:, sl], axis=-1, keepdims=True), (dyv.shape[0], HEAD_DIM)))
        return jnp.concatenate(outs, axis=1)

    delta = _rowwise(delta_fn, [dy_fox, y_fox], [], [(fox_w, F32)], name="fox_delta")[0]
    dfq, dfk, dfv, d_fcol, d_frow = ov.carry(
        ATTN_BWD_MATMULS * attn_us,
        lambda comm: _fox_bwd(zf, dy_fox, lse, delta, f_rep, f_row, heads=fox_heads, name="fox_bwd", comm=comm))
    d_fcum = d_fcol[:, ::HEAD_DIM].T + d_frow.reshape(fox_heads, s)
    d_logf = _cumsum(d_fcum, reverse=True, name="forget_cumsum_bwd")
    d_logf = _pad_cols(d_logf.T, LANES)

    def forget_bwd_fn(dl, fl, b):
        dfl_ = dl * _sigmoid(-(fl + b))
        return dfl_, _colsum(dfl_)

    dfl, d_fbias = _rowwise(forget_bwd_fn, [d_logf, (zr, LANES, o_fl // LANES)], [fbias], [(LANES, F32)], [LANES],
                            name="forget_gate_bwd")
    grads["fox_forget_bias"] = d_fbias[:, :fox_heads]

    dz = jnp.concatenate([dfq, dfk, dfv, dgv, dgr, dgq, dgk, dfl, dgd], axis=1).astype(BF16)
    dwin = ov.mm(u, dz, mode="tn", name="in_proj_dw", out_dtype=BF16)
    ov.grad("w_in", _win_to_shards(dwin, tbl, jnp.asarray(s_idx), jnp.asarray(s_val), s_width, tw=tw, wp=wp,
                                   name="in_proj_dw_shards"))
    du = ov.mm(dgz, ov.weight("w_merge_gate"), mode="nt", name="merge_gate_dx")
    du = ov.mm(dz, win, mode="nt", name="in_proj_dx", add=du)
    dh1, dh1_bf, grads["mix_norm"] = _rms_bwd(h1, du, mix_norm, dh2, name="mix_norm_bwd")
    dx, _, grads["ffn1_norm"] = _ffn_bwd(xs, ffn1_norm, ov, ffn1_saved, dh1, dh1_bf, tag="ffn1")

    outs = {}
    for n in SHARDED:
        parts = ov.get("d_" + n)
        state = [_pad_cols(t_[n][0], parts.shape[2]) for t_ in (wts, mom_m, mom_v)]
        res4 = _adamw(parts, *state, name=f"adamw_{n}")
        for kind, r_ in zip(("grad", "delta", "new_m", "new_v"), res4):
            outs[f"{kind}_{n}"] = r_[:, :wts[n].shape[2]][None]

    send_small, small_offs = _pack([grads[n] for n in REPLICATED], LANES, 8, F32)
    recv_small = _comm("bcast", send_small, name="exchange_replicated")
    w_sm, _ = _pack([wts[n] for n in REPLICATED], LANES, 8, F32)
    m_sm, _ = _pack([mom_m[n] for n in REPLICATED], LANES, 8, F32)
    v_sm, _ = _pack([mom_v[n] for n in REPLICATED], LANES, 8, F32)
    small = _adamw(recv_small, w_sm, m_sm, v_sm, name="adamw_replicated")
    for kind, buf in zip(("grad", "delta", "new_m", "new_v"), small):
        fs = buf.reshape(-1)
        for n, o in zip(REPLICATED, small_offs):
            outs[f"{kind}_{n}"] = fs[o:o + wts[n].size].reshape(wts[n].shape)

    res = [loss, dx[None]]
    for kind in ("grad", "delta", "new_m", "new_v"):
        res += [outs[f"{kind}_{n}"] for n in WEIGHTS]
    return tuple(res)
```

```python
import functools

import jax
import jax.numpy as jnp
import numpy as np
from jax import lax
from jax.experimental import pallas as pl
from jax.experimental.pallas import tpu as pltpu

F32 = jnp.float32
BF16 = jnp.bfloat16
MESH = pl.DeviceIdType.MESH
AXES = ("x", "y", "c")
N_DEV = 8

VMEM_LIMIT_BYTES = 56 * 1024 * 1024
MM_BLOCK_BUDGET_BYTES = 40 * 1024 * 1024
LANES = 128

EPS = 1e-6
HEAD_DIM = 128
GLA_VAL_DIM = 256
GLA_CHUNK = 64
GLA_GATE_TAU = 16.0
ADAM_LR, ADAM_B1, ADAM_B2, ADAM_EPS, ADAM_WD, ADAM_STEP = 0.001, 0.9, 0.999, 1e-08, 0.01, 10

ATTN_FWD_MATMULS = 7.0
ATTN_BWD_MATMULS = 7.0

HIGHEST = lax.Precision.HIGHEST
NN = (((1,), (0,)), ((), ()))
NT = (((1,), (1,)), ((), ()))
TN = (((0,), (0,)), ((), ()))


def _cparams(sem):
    return pltpu.CompilerParams(dimension_semantics=sem, vmem_limit_bytes=VMEM_LIMIT_BYTES)


def _pick(dim, cands):
    for c in cands:
        if dim % c == 0:
            return c
    return dim


def _bf(v):
    return v if v.dtype == BF16 else v.astype(BF16)


def _dot(a, b, dims):
    return lax.dot_general(_bf(a), _bf(b), dims, preferred_element_type=F32)


def _sigmoid(v):
    return 1.0 / (1.0 + jnp.exp(-v))


def _log_sigmoid(v):
    return jnp.minimum(v, 0.0) - jnp.log(1.0 + jnp.exp(-jnp.abs(v)))


def _logical(v):
    return (v.shape[1], v.shape[0] * v.shape[2]) if v.ndim == 3 else v.shape


def _mm(a, b, *, mode, name, out_dtype=F32, add=None, scale=1.0, out_chunked=False, comm=()):
    la, lb = _logical(a), _logical(b)
    if mode == "nn":
        (m, k), (k2, n) = la, lb
        a_minor, b_minor = "k", "n"
    elif mode == "nt":
        (m, k), (n, k2) = la, lb
        a_minor, b_minor = "k", "k"
    else:
        (k, m), (k2, n) = la, lb
        a_minor, b_minor = "m", "n"
    assert k == k2, (name, a.shape, b.shape)
    forced = {}
    for v, minor in ((a, a_minor), (b, b_minor)):
        if v.ndim == 3:
            assert forced.get(minor, v.shape[2]) == v.shape[2], name
            forced[minor] = v.shape[2]
    if out_chunked:
        assert forced.get("n", n // N_DEV) == n // N_DEV, name
        forced["n"] = n // N_DEV
    tm = forced.get("m") or _pick(m, (1024, 512, 256, 128))
    tn = forced.get("n") or _pick(n, (1408, 1280, 1024, 512, 256, 128))

    def blocks_bytes(tm_, tn_, t):
        io = tm_ * t * a.dtype.itemsize + t * tn_ * b.dtype.itemsize
        return 2 * (io + tm_ * tn_ * (jnp.dtype(out_dtype).itemsize + (4 if add is not None else 0))) + tm_ * tn_ * 4

    kc = forced.get("k")
    a_k_minor, b_k_minor = mode != "tn", mode == "nt"
    if kc:
        aligned = all(v.ndim == 3 or kc % (LANES if minor else 16) == 0 for v, minor in ((a, a_k_minor), (b, b_k_minor)))
        fits = [(tm_, tn_) for tm_, tn_ in ((tm, tn), (512, tn), (512, 512))
                if m % tm_ == 0 and n % tn_ == 0 and forced.get("m", tm_) == tm_ and forced.get("n", tn_) == tn_
                and blocks_bytes(tm_, tn_, k) <= MM_BLOCK_BUDGET_BYTES]
        if aligned and fits:
            (tm, tn), tk = fits[0], k
        else:
            tk, kc = kc, None
    else:
        tk = k if blocks_bytes(tm, tn, k) <= MM_BLOCK_BUDGET_BYTES else _pick(k, (512, 640, 256, 128))
    nk = k // tk
    dims = {"nn": NN, "nt": NT, "tn": TN}[mode]
    gi, gj, gk = (lambda i, j, kk: i), (lambda i, j, kk: j), (lambda i, j, kk: kk)

    def spec(v, t_major, t_minor, g_major, g_minor, all_chunks=False):
        if v.ndim == 3:
            if all_chunks:
                return pl.BlockSpec((N_DEV, t_major, v.shape[2]), lambda i, j, kk: (0, g_major(i, j, kk), 0))
            return pl.BlockSpec((None, t_major, v.shape[2]), lambda i, j, kk: (g_minor(i, j, kk), g_major(i, j, kk), 0))
        return pl.BlockSpec((t_major, t_minor), lambda i, j, kk: (g_major(i, j, kk), g_minor(i, j, kk)))

    a_spec = spec(a, tk, tm, gk, gi) if mode == "tn" else spec(a, tm, tk, gi, gk, bool(kc))
    b_spec = spec(b, tn, tk, gj, gk, bool(kc)) if mode == "nt" else spec(b, tk, tn, gk, gj)

    def k_chunk(ref, minor, c_):
        if len(ref.shape) == 3:
            return ref[c_]
        return ref[:, c_ * kc:(c_ + 1) * kc] if minor else ref[c_ * kc:(c_ + 1) * kc, :]
    if out_chunked:
        o_spec = pl.BlockSpec((None, tm, tn), lambda i, j, kk: (j, i, 0))
        out_shape = jax.ShapeDtypeStruct((N_DEV, m, tn), out_dtype)
    else:
        o_spec = pl.BlockSpec((tm, tn), lambda i, j, kk: (i, j))
        out_shape = jax.ShapeDtypeStruct((m, n), out_dtype)
    has_add = add is not None
    assert not (has_add and out_chunked), name

    def body(*refs):
        a_ref, b_ref = refs[0], refs[1]
        add_ref = refs[2] if has_add else None
        o_ref = refs[3 if has_add else 2]
        kk = pl.program_id(2)

        def finish(r):
            if scale != 1.0:
                r = r * scale
            if has_add:
                r = r + add_ref[...]
            o_ref[...] = r.astype(o_ref.dtype)

        if kc:
            r = _dot(k_chunk(a_ref, a_k_minor, 0), k_chunk(b_ref, b_k_minor, 0), dims)
            for c_ in range(1, k // kc):
                r = r + _dot(k_chunk(a_ref, a_k_minor, c_), k_chunk(b_ref, b_k_minor, c_), dims)
            finish(r)
        elif nk == 1:
            finish(_dot(a_ref[...], b_ref[...], dims))
        else:
            acc_ref = refs[-1]

            @pl.when(kk == 0)
            def _():
                acc_ref[...] = jnp.zeros_like(acc_ref)

            acc_ref[...] += _dot(a_ref[...], b_ref[...], dims)

            @pl.when(kk == nk - 1)
            def _():
                finish(acc_ref[...])

    res, carried = _call(
        body, name=name, grid=(m // tm, n // tn, nk),
        in_specs=[a_spec, b_spec] + ([o_spec] if has_add else []), out_specs=[o_spec], out_shape=[out_shape],
        scratch_shapes=[pltpu.VMEM((tm, tn), F32)] if nk > 1 else [],
        semantics=("parallel", "parallel", "arbitrary"), operands=[a, b] + ([add] if has_add else []), comm=comm)
    return (res[0], carried) if comm else res[0]


def _call(body, *, name, grid, in_specs, out_specs, out_shape, scratch_shapes, semantics, operands, comm=()):
    n_in, n_out, n_scr, n = len(in_specs), len(out_specs), len(scratch_shapes), len(comm)
    if not comm:
        res = pl.pallas_call(body, name=name, grid=grid, in_specs=in_specs, out_specs=out_specs, out_shape=out_shape,
                             scratch_shapes=scratch_shapes, compiler_params=_cparams(semantics))(*operands)
        return res, []

    def carrying(*refs):
        ins, c_in = refs[:n_in], refs[n_in:n_in + n]
        outs, c_out = refs[n_in + n:n_in + n + n_out], refs[n_in + n + n_out:n_in + 2 * n + n_out]
        scratch, sems = refs[n_in + 2 * n + n_out:][:n_scr], refs[n_in + 2 * n + n_out + n_scr:]
        tasks = [_comm_ops(kind, c_in[t], c_out[t], *sems[3 * t:3 * t + 3]) for t, (kind, _) in enumerate(comm)]
        ids = [pl.program_id(ax) for ax in range(len(grid))]

        @pl.when(functools.reduce(lambda p, q: p & q, [i == 0 for i in ids]))
        def _():
            for start, _ in tasks:
                start()

        body(*ins, *outs, *scratch)

        @pl.when(functools.reduce(lambda p, q: p & q, [i == g - 1 for i, g in zip(ids, grid)]))
        def _():
            for _, finish in tasks:
                finish()

    any_spec = pl.BlockSpec(memory_space=pl.ANY)
    res = pl.pallas_call(
        carrying, name=name, grid=grid,
        in_specs=list(in_specs) + [any_spec] * n, out_specs=list(out_specs) + [any_spec] * n,
        out_shape=list(out_shape) + [_comm_out_shape(kind, v) for kind, v in comm],
        scratch_shapes=list(scratch_shapes) + _comm_scratch(n),
        compiler_params=_cparams(("arbitrary",) * len(grid)),
    )(*operands, *[v for _, v in comm])
    return res[:n_out], res[n_out:]


def _rowwise(fn, rows, consts, outs, accs=(), *, name, rc=None):
    rows = [r if isinstance(r, tuple) else (r, r.shape[1], 0) for r in rows]
    m = rows[0][0].shape[0]
    widths = [w for _, w, _ in rows] + [n for n, _ in outs]
    row_bytes = sum(w * r.dtype.itemsize for r, w, _ in rows) + sum(n * jnp.dtype(d).itemsize for n, d in outs)
    tm = 1024
    while tm > 16 and (m % tm or 2 * tm * row_bytes > 24 * 1024 * 1024):
        tm //= 2
    if m % tm:
        tm = m
    if rc is None:
        rc = 16
        while rc * 2 <= tm and rc * 2 * max(widths) <= 32768:
            rc *= 2
    rc = min(rc, tm)
    nr, nc, no = len(rows), len(consts), len(outs)

    def body(*refs):
        in_refs, c_refs = refs[:nr], refs[nr:nr + nc]
        o_refs, a_refs = refs[nr + nc:nr + nc + no], refs[nr + nc + no:]

        @pl.when(pl.program_id(0) == 0)
        def _():
            for r in a_refs:
                r[...] = jnp.zeros_like(r)

        cvals = [c[...] for c in c_refs]

        def chunk(ci, carry):
            sl = pl.ds(pl.multiple_of(ci * rc, rc), rc)
            res = fn(*[r[sl, :] for r in in_refs], *cvals)
            if not isinstance(res, (tuple, list)):
                res = (res,)
            for r, v in zip(o_refs, res[:no]):
                r[sl, :] = v.astype(r.dtype)
            for r, v in zip(a_refs, res[no:]):
                r[...] += v
            return carry

        lax.fori_loop(0, tm // rc, chunk, 0)

    in_specs = [pl.BlockSpec((tm, w), functools.partial(lambda i, cb: (i, cb), cb=cb)) for _, w, cb in rows]
    in_specs += [pl.BlockSpec(c.shape, lambda i: (0, 0)) for c in consts]
    out_specs = [pl.BlockSpec((tm, n), lambda i: (i, 0)) for n, _ in outs]
    out_specs += [pl.BlockSpec((1, n), lambda i: (0, 0)) for n in accs]
    out_shape = [jax.ShapeDtypeStruct((m, n), d) for n, d in outs] + [jax.ShapeDtypeStruct((1, n), F32) for n in accs]
    res = pl.pallas_call(
        body, name=name, grid=(m // tm,),
        in_specs=in_specs, out_specs=out_specs, out_shape=out_shape,
        compiler_params=_cparams(("arbitrary",)),
    )(*[r for r, _, _ in rows], *consts)
    return res


def _colsum(v):
    return jnp.sum(v, axis=0, keepdims=True)


def _rms_parts(xv):
    r = lax.rsqrt(jnp.mean(xv * xv, axis=-1, keepdims=True) + EPS)
    return r, xv * r


def _rms_fwd(xv, g, *, name):
    d = xv.shape[1]

    def fn(xb, gb):
        _, xh = _rms_parts(xb)
        return xh * gb

    return _rowwise(fn, [xv], [g], [(d, BF16)], name=name)[0]


def _rms_bwd(xv, dy, g, add, *, name):
    d = xv.shape[1]

    def fn(xb, dyb, addb, gb):
        r, xh = _rms_parts(xb)
        dyb = dyb.astype(F32)
        t = dyb * gb
        dx = r * (t - xh * jnp.mean(t * xh, axis=-1, keepdims=True)) + addb
        return dx, dx, _colsum(dyb * xh)

    return _rowwise(fn, [xv, dy, add], [g], [(d, F32), (d, BF16)], [d], name=name)


def _silu_parts(a):
    sg = _sigmoid(a)
    return a * sg, sg * (1.0 + a * (1.0 - sg))


def _rows(v):
    return v.reshape(v.shape[0] * v.shape[1], v.shape[2])


def _ffn_fwd(h, g, ov, *, tag):
    s = h.shape[0]
    n = _rms_fwd(h, g, name=f"{tag}_norm")
    a = ov.mm(n, ov.weight(f"{tag}_w_gate"), mode="nn", name=f"{tag}_gate", out_dtype=BF16, out_chunked=True)
    b = ov.mm(n, ov.weight(f"{tag}_w_up"), mode="nn", name=f"{tag}_up", out_dtype=BF16, out_chunked=True)
    c = a.shape[2]
    hm = _rowwise(lambda av, bv: _silu_parts(av.astype(F32))[0] * bv.astype(F32), [_rows(a), _rows(b)], [], [(c, BF16)],
                  name=f"{tag}_act")[0]
    hm = hm.reshape(N_DEV, s, c)
    out = ov.mm(hm, ov.weight(f"{tag}_w_down"), mode="nn", name=f"{tag}_down", add=h, scale=0.5)
    return out, (n, a, b, hm)


def _ffn_bwd(h, g, ov, saved, dout, dout_bf, *, tag):
    n, a, b, hm = saved
    wg, wu, wd = ov.weight(f"{tag}_w_gate"), ov.weight(f"{tag}_w_up"), ov.weight(f"{tag}_w_down")
    s, c = h.shape[0], wg.shape[2]
    d_wd = ov.mm(hm, dout_bf, mode="tn", name=f"{tag}_down_dw", scale=0.5, out_dtype=BF16)
    ov.grad(f"{tag}_w_down", d_wd)
    dhm = ov.mm(dout_bf, wd, mode="nt", name=f"{tag}_down_dx", scale=0.5, out_dtype=BF16, out_chunked=True)

    def act_bwd(av, bv, dv):
        av, bv, dv = av.astype(F32), bv.astype(F32), dv.astype(F32)
        si, dsi = _silu_parts(av)
        return dv * bv * dsi, dv * si

    da, db = _rowwise(act_bwd, [_rows(a), _rows(b), _rows(dhm)], [], [(c, BF16), (c, BF16)], name=f"{tag}_act_bwd")
    da, db = da.reshape(N_DEV, s, c), db.reshape(N_DEV, s, c)
    ov.grad(f"{tag}_w_gate", ov.mm(n, da, mode="tn", name=f"{tag}_gate_dw", out_dtype=BF16, out_chunked=True))
    ov.grad(f"{tag}_w_up", ov.mm(n, db, mode="tn", name=f"{tag}_up_dw", out_dtype=BF16, out_chunked=True))
    dn = ov.mm(da, wg, mode="nt", name=f"{tag}_gate_dx")
    dn = ov.mm(db, wu, mode="nt", name=f"{tag}_up_dx", add=dn, out_dtype=BF16)
    dh, dh_bf, dg = _rms_bwd(h, dn, g, dout, name=f"{tag}_norm_bwd")
    return dh, dh_bf, dg


def _cumsum(xv, *, reverse, name):
    h, s = xv.shape
    t = _pick(s, (512, 256, 128))
    nb = s // t

    def blk(j):
        return (0, nb - 1 - j) if reverse else (0, j)

    def body(x_ref, o_ref, carry):
        @pl.when(pl.program_id(0) == 0)
        def _():
            carry[...] = jnp.zeros_like(carry)

        i0 = lax.broadcasted_iota(jnp.int32, (t, t), 0)
        i1 = lax.broadcasted_iota(jnp.int32, (t, t), 1)
        tri = ((i0 >= i1) if reverse else (i0 <= i1)).astype(F32)
        xb = x_ref[...]
        o_ref[...] = jnp.dot(xb, tri, precision=HIGHEST, preferred_element_type=F32) + carry[...]
        carry[...] += jnp.sum(xb, axis=1, keepdims=True)

    return pl.pallas_call(
        body, name=name, grid=(nb,),
        in_specs=[pl.BlockSpec((h, t), blk)], out_specs=pl.BlockSpec((h, t), blk),
        out_shape=jax.ShapeDtypeStruct((h, s), F32),
        scratch_shapes=[pltpu.VMEM((h, 1), F32)],
        compiler_params=_cparams(("arbitrary",)),
    )(xv)


def _fox_tiles(s):
    t = _pick(s, (512, 256, 128))
    return t, t


def _causal(sc):
    t = sc.shape[0]
    keep = lax.broadcasted_iota(jnp.int32, (t, t), 1) <= lax.broadcasted_iota(jnp.int32, (t, t), 0)
    return jnp.where(keep, sc, -jnp.inf)


ATTN_HEADS_PER_STEP = 2


def _head_cols(hh):
    return slice(hh * HEAD_DIM, (hh + 1) * HEAD_DIM)


def _across(rowstat, width):
    return jnp.tile(rowstat, (1, width // HEAD_DIM))


def _fox_fwd(zf, f_rep, f_row, *, heads, name, comm=()):
    s = zf.shape[0]
    tq, tk = _fox_tiles(s)
    assert tq == tk
    nq, nk = s // tq, s // tk
    hp = ATTN_HEADS_PER_STEP if heads % ATTN_HEADS_PER_STEP == 0 else 1
    wb = hp * HEAD_DIM
    scale = HEAD_DIM ** -0.5
    w = heads * HEAD_DIM

    def body(q_ref, k_ref, v_ref, fq_ref, fk_ref, o32_ref, o16_ref, lse_ref, m_sc, l_sc, acc_sc):
        i, j = pl.program_id(1), pl.program_id(2)

        @pl.when(j == 0)
        def _():
            m_sc[...] = jnp.full_like(m_sc, -jnp.inf)
            l_sc[...] = jnp.zeros_like(l_sc)
            acc_sc[...] = jnp.zeros_like(acc_sc)

        def step(diagonal):
            for hh in range(hp):
                cols = _head_cols(hh)
                sc = _dot(q_ref[:, cols], k_ref[:, cols], NT) * scale + _across(fq_ref[:, cols], tk) - fk_ref[hh]
                if diagonal:
                    sc = _causal(sc)
                m_old = m_sc[hh]
                m_new = jnp.maximum(m_old, jnp.max(sc, axis=-1, keepdims=True))
                alpha = jnp.exp(m_old - m_new)
                pr = jnp.exp(sc - _across(m_new, tk))
                l_sc[hh] = alpha * l_sc[hh] + jnp.sum(pr, axis=-1, keepdims=True)
                acc_sc[hh] = alpha * acc_sc[hh] + _dot(pr, v_ref[:, cols], NN)
                m_sc[hh] = m_new

        @pl.when(j < i)
        def _():
            step(False)

        @pl.when(j == i)
        def _():
            step(True)

        @pl.when(j == nk - 1)
        def _():
            for hh in range(hp):
                cols = _head_cols(hh)
                o = acc_sc[hh] / l_sc[hh]
                o32_ref[:, cols] = o
                o16_ref[:, cols] = o.astype(BF16)
                lse_ref[:, cols] = m_sc[hh] + jnp.log(l_sc[hh])

    def kv_blk(off):
        return lambda h, i, j: (jnp.minimum(j, i), off + h)

    o_spec = pl.BlockSpec((tq, wb), lambda h, i, j: (i, h))
    stat = pltpu.VMEM((hp, tq, HEAD_DIM), F32)
    return _call(
        body, name=name, grid=(heads // hp, nq, nk),
        in_specs=[
            o_spec,
            pl.BlockSpec((tk, wb), kv_blk(heads // hp)),
            pl.BlockSpec((tk, wb), kv_blk(2 * heads // hp)),
            o_spec,
            pl.BlockSpec((hp, 1, tk), lambda h, i, j: (h, 0, jnp.minimum(j, i))),
        ],
        out_specs=[o_spec, o_spec, o_spec],
        out_shape=[jax.ShapeDtypeStruct((s, w), F32), jax.ShapeDtypeStruct((s, w), BF16),
                   jax.ShapeDtypeStruct((s, w), F32)],
        scratch_shapes=[stat, stat, stat],
        semantics=("parallel", "parallel", "arbitrary"), operands=[zf, zf, zf, f_rep, f_row], comm=comm)


def _fox_bwd(zf, do, lse, delta, f_rep, f_row, *, heads, name, comm=()):
    s = zf.shape[0]
    tq, tk = _fox_tiles(s)
    assert tq == tk
    nq, nk = s // tq, s // tk
    hp = ATTN_HEADS_PER_STEP if heads % ATTN_HEADS_PER_STEP == 0 else 1
    wb = hp * HEAD_DIM
    scale = HEAD_DIM ** -0.5
    w = heads * HEAD_DIM

    def body(q_ref, k_ref, v_ref, do_ref, lse_ref, dl_ref, fq_ref, fk_ref, dq_ref, dk_ref, dv_ref, dfq_ref, dfk_ref):
        j, i = pl.program_id(1), pl.program_id(2)

        @pl.when((j == 0) & (i == 0))
        def _():
            dq_ref[...] = jnp.zeros_like(dq_ref)
            dfq_ref[...] = jnp.zeros_like(dfq_ref)

        @pl.when(i == 0)
        def _():
            dk_ref[...] = jnp.zeros_like(dk_ref)
            dv_ref[...] = jnp.zeros_like(dv_ref)
            dfk_ref[...] = jnp.zeros_like(dfk_ref)

        def step(diagonal):
            rows = pl.ds(pl.multiple_of(i * tq, tq), tq)
            for hh in range(hp):
                cols = _head_cols(hh)
                q, k, v = q_ref[:, cols], k_ref[:, cols], v_ref[:, cols]
                dob = do_ref[:, cols].astype(BF16)
                sc = _dot(q, k, NT) * scale + _across(fq_ref[:, cols], tk) - fk_ref[hh]
                if diagonal:
                    sc = _causal(sc)
                pr = jnp.exp(sc - _across(lse_ref[:, cols], tk))
                dv_ref[:, cols] += _dot(pr, dob, TN)
                dp = _dot(dob, v, NT)
                ds = pr * (dp - _across(dl_ref[:, cols], tk))
                dsb = ds.astype(BF16)
                dk_ref[:, cols] += _dot(dsb, q, TN) * scale
                dq_ref[rows, cols] += _dot(dsb, k, NN) * scale
                dfq_ref[rows, cols] += jnp.broadcast_to(jnp.sum(ds, axis=1, keepdims=True), (tq, HEAD_DIM))
                dfk_ref[hh] -= jnp.sum(ds, axis=0, keepdims=True)

        @pl.when(i > j)
        def _():
            step(False)

        @pl.when(i == j)
        def _():
            step(True)

    q_spec = pl.BlockSpec((tq, wb), lambda h, j, i: (jnp.maximum(i, j), h))
    k_spec = pl.BlockSpec((tk, wb), lambda h, j, i: (j, h))
    whole = pl.BlockSpec((s, wb), lambda h, j, i: (0, h))
    row_spec = pl.BlockSpec((hp, 1, tk), lambda h, j, i: (h, 0, j))
    return _call(
        body, name=name, grid=(heads // hp, nk, nq),
        in_specs=[
            q_spec,
            pl.BlockSpec((tk, wb), lambda h, j, i: (j, heads // hp + h)),
            pl.BlockSpec((tk, wb), lambda h, j, i: (j, 2 * heads // hp + h)),
            q_spec, q_spec, q_spec, q_spec, row_spec,
        ],
        out_specs=[whole, k_spec, k_spec, whole, row_spec],
        out_shape=[jax.ShapeDtypeStruct((s, w), F32), jax.ShapeDtypeStruct((s, w), F32),
                   jax.ShapeDtypeStruct((s, w), F32), jax.ShapeDtypeStruct((s, w), F32),
                   jax.ShapeDtypeStruct((heads, 1, s), F32)],
        scratch_shapes=[], semantics=("parallel", "arbitrary", "arbitrary"),
        operands=[zf, zf, zf, do, lse, delta, f_rep, f_row], comm=comm)


def _gla_rows(s):
    return _pick(s, (256, 128, 64))


def _gla_chunk_terms(la_c, tri):
    a_cum = jnp.dot(tri, la_c, precision=HIGHEST, preferred_element_type=F32)
    a_tot = jnp.sum(la_c, axis=0, keepdims=True)
    return jnp.exp(a_tot - a_cum), jnp.exp(a_tot)


def _gla_fwd(zr, la, *, heads, q_blk, k_blk, name):
    s = zr.shape[0]
    c = GLA_CHUNK
    rows = _gla_rows(s)
    nsteps, ncs = s // rows, rows // c
    scale = HEAD_DIM ** -0.5

    assert q_blk % heads == 0 and k_blk % heads == 0

    def body(q_ref, k_ref, v_ref, la_ref, o_ref, st_ref, state):
        @pl.when(pl.program_id(0) == 0)
        def _():
            state[...] = jnp.zeros_like(state)

        tri = (lax.broadcasted_iota(jnp.int32, (c, c), 0) >= lax.broadcasted_iota(jnp.int32, (c, c), 1)).astype(F32)
        for t in range(ncs):
            sl = slice(t * c, (t + 1) * c)
            for h in range(heads):
                kc, vc = _head_cols(h), slice(h * GLA_VAL_DIM, (h + 1) * GLA_VAL_DIM)
                dec, e_tot = _gla_chunk_terms(la_ref[sl, kc], tri)
                kd = k_ref[sl, kc] * dec
                st_ref[h, t] = state[h]
                new = state[h] * e_tot + _dot(v_ref[sl, vc], kd, TN)
                state[h] = new
                o_ref[sl, vc] = _dot(q_ref[sl, kc] * scale, new, NT)

    kw, vw = heads * HEAD_DIM, heads * GLA_VAL_DIM
    return pl.pallas_call(
        body, name=name, grid=(nsteps,),
        in_specs=[
            pl.BlockSpec((rows, kw), lambda i: (i, q_blk // heads)),
            pl.BlockSpec((rows, kw), lambda i: (i, k_blk // heads)),
            pl.BlockSpec((rows, vw), lambda i: (i, 0)),
            pl.BlockSpec((rows, kw), lambda i: (i, 0)),
        ],
        out_specs=[
            pl.BlockSpec((rows, vw), lambda i: (i, 0)),
            pl.BlockSpec((heads, ncs, GLA_VAL_DIM, HEAD_DIM), lambda i: (0, i, 0, 0)),
        ],
        out_shape=[jax.ShapeDtypeStruct((s, vw), F32),
                   jax.ShapeDtypeStruct((heads, s // c, GLA_VAL_DIM, HEAD_DIM), F32)],
        scratch_shapes=[pltpu.VMEM((heads, GLA_VAL_DIM, HEAD_DIM), F32)],
        compiler_params=_cparams(("arbitrary",)),
    )(zr, zr, zr, la)


def _gla_bwd(zr, la, do, states, *, heads, q_blk, k_blk, name):
    s = zr.shape[0]
    c = GLA_CHUNK
    rows = _gla_rows(s)
    nsteps, ncs = s // rows, rows // c
    scale = HEAD_DIM ** -0.5

    assert q_blk % heads == 0 and k_blk % heads == 0

    def body(q_ref, k_ref, v_ref, la_ref, do_ref, st_ref, dq_ref, dk_ref, dv_ref, dla_ref, dstate):
        @pl.when(pl.program_id(0) == 0)
        def _():
            dstate[...] = jnp.zeros_like(dstate)

        i0 = lax.broadcasted_iota(jnp.int32, (c, c), 0)
        i1 = lax.broadcasted_iota(jnp.int32, (c, c), 1)
        tri = (i0 >= i1).astype(F32)
        strict = (i0 > i1).astype(F32)
        for t in reversed(range(ncs)):
            sl = slice(t * c, (t + 1) * c)
            for h in range(heads):
                kc, vc = _head_cols(h), slice(h * GLA_VAL_DIM, (h + 1) * GLA_VAL_DIM)
                dec, e_tot = _gla_chunk_terms(la_ref[sl, kc], tri)
                kd = k_ref[sl, kc] * dec
                kdb = kd.astype(BF16)
                vb = v_ref[sl, vc].astype(BF16)
                dob = do_ref[sl, vc].astype(BF16)
                prev = st_ref[h, t]
                cur = prev * e_tot + _dot(vb, kdb, TN)
                d_cur = dstate[h] + _dot(dob, q_ref[sl, kc] * scale, TN)
                d_cur_b = d_cur.astype(BF16)
                dq_ref[sl, kc] = _dot(dob, cur, NN) * scale
                dv_ref[sl, vc] = _dot(kdb, d_cur_b, NT)
                dkd = _dot(vb, d_cur_b, NN)
                d_tot = e_tot * jnp.sum(d_cur * prev, axis=0, keepdims=True)
                dk_ref[sl, kc] = dkd * dec
                dla_ref[sl, kc] = d_tot + jnp.dot(strict, dkd * kd, precision=HIGHEST, preferred_element_type=F32)
                dstate[h] = d_cur * e_tot

    def rev(i):
        return nsteps - 1 - i

    kw, vw = heads * HEAD_DIM, heads * GLA_VAL_DIM
    kq_spec = pl.BlockSpec((rows, kw), lambda i: (rev(i), 0))
    v_spec = pl.BlockSpec((rows, vw), lambda i: (rev(i), 0))
    return pl.pallas_call(
        body, name=name, grid=(nsteps,),
        in_specs=[
            pl.BlockSpec((rows, kw), lambda i: (rev(i), q_blk // heads)),
            pl.BlockSpec((rows, kw), lambda i: (rev(i), k_blk // heads)),
            v_spec, kq_spec, v_spec,
            pl.BlockSpec((heads, ncs, GLA_VAL_DIM, HEAD_DIM), lambda i: (0, rev(i), 0, 0)),
        ],
        out_specs=[kq_spec, kq_spec, v_spec, kq_spec],
        out_shape=[jax.ShapeDtypeStruct((s, kw), F32), jax.ShapeDtypeStruct((s, kw), F32),
                   jax.ShapeDtypeStruct((s, vw), F32), jax.ShapeDtypeStruct((s, kw), F32)],
        scratch_shapes=[pltpu.VMEM((heads, GLA_VAL_DIM, HEAD_DIM), F32)],
        compiler_params=_cparams(("arbitrary",)),
    )(zr, zr, zr, la, do, states)


def _my_place():
    x, y, c = lax.axis_index("x"), lax.axis_index("y"), lax.axis_index("c")
    return x, y, c


def _gather_ops(x_ref, out_ref, send_sems, recv_sems, local_sem):
    def plan():
        x, y, c = _my_place()
        me, sibling = (x, y, c), (x, y, 1 - c)
        chips = [(1 - x, y), (x, 1 - y), (1 - x, 1 - y)]

        def blk(px, py, pc):
            return out_ref.at[4 * px + 2 * py + pc]

        def copy(k, block, to, src=None):
            return pltpu.make_async_remote_copy(
                src_ref=blk(*block) if src is None else src, dst_ref=blk(*block),
                send_sem=send_sems.at[k], recv_sem=recv_sems.at[k], device_id=to, device_id_type=MESH)

        mine = pltpu.make_async_copy(x_ref, blk(*me), local_sem)
        first = [copy(0, me, sibling, src=x_ref)]
        first += [copy(1 + j, me, (*chip, c), src=x_ref) for j, chip in enumerate(chips)]
        passed = [copy(4 + j, (*chip, c), sibling) for j, chip in enumerate(chips)]
        landed = [copy(1 + j, (*chip, c), me) for j, chip in enumerate(chips)]
        from_sibling = [copy(0, sibling, me)] + [copy(4 + j, (*chip, 1 - c), me) for j, chip in enumerate(chips)]
        return mine, first, passed, landed, from_sibling

    def start():
        mine, first, _, _, _ = plan()
        mine.start()
        for cp in first:
            cp.start()

    def finish():
        mine, first, passed, landed, from_sibling = plan()
        for cp, fwd in zip(landed, passed):
            cp.wait_recv()
            fwd.start()
        for cp in from_sibling:
            cp.wait_recv()
        for cp in first + passed:
            cp.wait_send()
        mine.wait()

    return start, finish


def _exchange_ops(scatter, s_ref, r_ref, send_sems, recv_sems, local_sem):
    def plan():
        x, y, c = _my_place()
        me = 4 * x + 2 * y + c
        mine = pltpu.make_async_copy(s_ref.at[me] if scatter else s_ref, r_ref.at[me], local_sem)
        sends, recvs = [], []
        for k in range(1, N_DEV):
            px, py, pc = x ^ ((k >> 2) & 1), y ^ ((k >> 1) & 1), c ^ (k & 1)
            peer = 4 * px + 2 * py + pc
            src = s_ref.at[peer] if scatter else s_ref
            for dst, out in ((r_ref.at[me], sends), (r_ref.at[peer], recvs)):
                out.append(pltpu.make_async_remote_copy(
                    src_ref=src, dst_ref=dst, send_sem=send_sems.at[k - 1], recv_sem=recv_sems.at[k - 1],
                    device_id=(px, py, pc), device_id_type=MESH))
        return mine, sends, recvs

    def start():
        mine, sends, _ = plan()
        mine.start()
        for cp in sends:
            cp.start()

    def finish():
        mine, sends, recvs = plan()
        for cp in recvs:
            cp.wait_recv()
        for cp in sends:
            cp.wait_send()
        mine.wait()

    return start, finish


def _pair_ops(s_ref, r_ref, send_sems, recv_sems):
    def plan():
        x, y, c = _my_place()
        return [pltpu.make_async_remote_copy(
            src_ref=s_ref.at[2 * q + 1 - c], dst_ref=r_ref.at[q], send_sem=send_sems.at[q], recv_sem=recv_sems.at[q],
            device_id=(x, y, 1 - c), device_id_type=MESH) for q in range(N_DEV // 2)]

    def start():
        for cp in plan():
            cp.start()

    def finish():
        copies = plan()
        for cp in copies:
            cp.wait_recv()
        for cp in copies:
            cp.wait_send()

    return start, finish


def _chips_ops(p_ref, r_ref, send_sems, recv_sems, local_sem):
    def plan():
        x, y, c = _my_place()
        chip = 2 * x + y
        mine = pltpu.make_async_copy(p_ref.at[chip], r_ref.at[chip], local_sem)
        sends, recvs = [], []
        for k in range(1, N_DEV // 2):
            px, py = x ^ (k >> 1), y ^ (k & 1)
            peer = 2 * px + py
            for dst, out in ((r_ref.at[chip], sends), (r_ref.at[peer], recvs)):
                out.append(pltpu.make_async_remote_copy(
                    src_ref=p_ref.at[peer], dst_ref=dst, send_sem=send_sems.at[k - 1], recv_sem=recv_sems.at[k - 1],
                    device_id=(px, py, c), device_id_type=MESH))
        return mine, sends, recvs

    def start():
        mine, sends, _ = plan()
        mine.start()
        for cp in sends:
            cp.start()

    def finish():
        mine, sends, recvs = plan()
        for cp in recvs:
            cp.wait_recv()
        for cp in sends:
            cp.wait_send()
        mine.wait()

    return start, finish


def _comm_ops(kind, src_ref, dst_ref, send_sems, recv_sems, local_sem):
    if kind == "gather":
        return _gather_ops(src_ref, dst_ref, send_sems, recv_sems, local_sem)
    if kind == "pair":
        return _pair_ops(src_ref, dst_ref, send_sems, recv_sems)
    if kind == "chips":
        return _chips_ops(src_ref, dst_ref, send_sems, recv_sems, local_sem)
    return _exchange_ops(False, src_ref, dst_ref, send_sems, recv_sems, local_sem)


def _comm_out_shape(kind, v):
    shape = {"pair": (N_DEV // 2,) + v.shape[1:], "chips": v.shape}.get(kind, (N_DEV,) + v.shape)
    return jax.ShapeDtypeStruct(shape, v.dtype)


def _comm_scratch(n_tasks):
    return [pltpu.SemaphoreType.DMA((N_DEV - 1,)), pltpu.SemaphoreType.DMA((N_DEV - 1,)), pltpu.SemaphoreType.DMA] * n_tasks


def _pair_sum(s, r, *, name):
    _, rows, cdim = s.shape
    tr = _pick(rows, (1024, 704, 512, 256, 128, 64, 32, 16, 8))
    rc = min(16, tr)

    def body(c_ref, s_ref, r_ref, o_ref):
        def chunk(ci, carry):
            sl = pl.ds(pl.multiple_of(ci * rc, rc), rc)
            o_ref[sl, :] = (s_ref[sl, :].astype(F32) + r_ref[sl, :].astype(F32)).astype(o_ref.dtype)
            return carry

        lax.fori_loop(0, tr // rc, chunk, 0)

    spec = pl.BlockSpec((None, tr, cdim), lambda q, i, c_ref: (q, i, 0))
    return pl.pallas_call(
        body, name=name,
        grid_spec=pltpu.PrefetchScalarGridSpec(
            num_scalar_prefetch=1, grid=(N_DEV // 2, rows // tr),
            in_specs=[pl.BlockSpec((None, None, tr, cdim), lambda q, i, c_ref: (q, c_ref[0], i, 0)), spec],
            out_specs=spec),
        out_shape=jax.ShapeDtypeStruct(r.shape, s.dtype),
        compiler_params=_cparams(("parallel", "parallel")),
    )(lax.axis_index("c").astype(jnp.int32).reshape(1), s.reshape(N_DEV // 2, 2, rows, cdim), r)


def _comm(kind, v, *, name):
    def body(s_ref, r_ref, send_sems, recv_sems, local_sem):
        start, finish = _comm_ops(kind, s_ref, r_ref, send_sems, recv_sems, local_sem)
        start()
        finish()

    return pl.pallas_call(
        body, name=name,
        out_shape=_comm_out_shape(kind, v),
        in_specs=[pl.BlockSpec(memory_space=pl.ANY)],
        out_specs=pl.BlockSpec(memory_space=pl.ANY),
        scratch_shapes=_comm_scratch(1),
    )(v)


class _Overlap:
    US_PER_MB = {"gather": 52.0, "pair": 1.0, "chips": 13.0, "bcast": 97.0}
    MM_FLOPS_PER_US = 6.0e8

    def __init__(self):
        self.queue, self.results, self.then = [], {}, {}

    def add(self, key, kind, v):
        self.queue.append((key, kind, v))

    def _cost(self, kind, v):
        return v.size * v.dtype.itemsize / 2 ** 20 * self.US_PER_MB[kind]

    def take(self, budget_us):
        taken, cum = [], 0.0
        while self.queue:
            cost = self._cost(*self.queue[0][1:])
            if taken and cum + cost > 1.25 * budget_us:
                break
            taken.append(self.queue.pop(0))
            cum += cost
        return taken

    def put(self, taken, res):
        for (key, _, _), r in zip(taken, res):
            self.results[key] = r
            if key in self.then:
                self.then.pop(key)(r)

    def carry(self, budget_us, fn):
        taken = self.take(budget_us)
        out, res = fn([(kind, v) for _, kind, v in taken])
        self.put(taken, res)
        return out

    def mm(self, a, b, *, mode, **kw):
        la, lb = _logical(a), _logical(b)
        budget = 2.0 * la[0] * la[1] * (lb[0] if mode == "nt" else lb[1]) / self.MM_FLOPS_PER_US

        def fn(comm):
            return _mm(a, b, mode=mode, comm=comm, **kw) if comm else (_mm(a, b, mode=mode, **kw), [])

        return self.carry(budget, fn)

    def get(self, key):
        while key not in self.results:
            keys = [k for k, _, _ in self.queue]
            task = self.queue.pop(keys.index(key if key in keys else "pair_" + key))
            self.put([task], [_comm(task[1], task[2], name=f"alone_{task[0]}")])
        return self.results[key]

    def weight(self, n):
        g = self.get(n)
        return g.reshape(-1, g.shape[2]) if n in ROW_SHARDED else g

    def grad(self, n, g):
        g = g if g.ndim == 3 else g.reshape(N_DEV, g.shape[0] // N_DEV, g.shape[1])
        self.add("pair_d_" + n, "pair", g)
        self.then["pair_d_" + n] = lambda r: self.add("d_" + n, "chips", _pair_sum(g, r, name=f"pair_sum_{n}"))


def _sel_tables(dest, ws, wp, tw):
    n_tiles = (int(dest.max()) + tw) // tw
    tbl = np.full((N_DEV, wp), -1, np.int32)
    for j in range(N_DEV):
        tbl[j, :ws] = dest[j * ws:(j + 1) * ws]
    by_tile = [sorted({j for j in range(N_DEV) if ((tbl[j] // tw) == t).any()}) for t in range(n_tiles)]
    by_shard = [sorted({int(t) for t in np.unique(tbl[j, :ws] // tw)}) for j in range(N_DEV)]

    def table(lists):
        width = max(len(v) for v in lists)
        idx = np.array([(v + [v[-1]] * width)[:width] if v else [0] * width for v in lists], np.int32)
        val = np.array([[1] * len(v) + [0] * (width - len(v)) for v in lists], np.int32)
        return idx.reshape(-1), val.reshape(-1), width

    return tbl[:, :, None], table(by_tile), table(by_shard)


def _sel_matrix(d_ref, t, wp, tw):
    cols = t * tw + lax.broadcasted_iota(jnp.int32, (wp, tw), 1)
    return (d_ref[...] == cols).astype(BF16)


def _win_unshard(g, tbl, idx, val, width, *, tw, padded, name):
    _, dm, wp = g.shape
    tm = _pick(dm, (1024, 512, 256, 128))

    def body(idx_ref, val_ref, g_ref, d_ref, o_ref, acc):
        t, s_ = pl.program_id(1), pl.program_id(2)

        @pl.when(s_ == 0)
        def _():
            acc[...] = jnp.zeros_like(acc)

        @pl.when(val_ref[t * width + s_] == 1)
        def _():
            acc[...] += _dot(g_ref[...], _sel_matrix(d_ref, t, wp, tw), NN)

        @pl.when(s_ == width - 1)
        def _():
            o_ref[...] = acc[...].astype(BF16)

    return pl.pallas_call(
        body, name=name,
        grid_spec=pltpu.PrefetchScalarGridSpec(
            num_scalar_prefetch=2, grid=(dm // tm, padded // tw, width),
            in_specs=[pl.BlockSpec((None, tm, wp), lambda i, t, s_, ix, vl: (ix[t * width + s_], i, 0)),
                      pl.BlockSpec((None, wp, 1), lambda i, t, s_, ix, vl: (ix[t * width + s_], 0, 0))],
            out_specs=pl.BlockSpec((tm, tw), lambda i, t, s_, ix, vl: (i, t)),
            scratch_shapes=[pltpu.VMEM((tm, tw), F32)]),
        out_shape=jax.ShapeDtypeStruct((dm, padded), BF16),
        compiler_params=_cparams(("parallel", "parallel", "arbitrary")),
    )(idx, val, g, tbl)


def _win_to_shards(dw, tbl, idx, val, width, *, tw, wp, name):
    dm = dw.shape[0]
    tm = _pick(dm, (1024, 512, 256, 128))

    def body(idx_ref, val_ref, w_ref, d_ref, o_ref, acc):
        j, s_ = pl.program_id(1), pl.program_id(2)

        @pl.when(s_ == 0)
        def _():
            acc[...] = jnp.zeros_like(acc)

        @pl.when(val_ref[j * width + s_] == 1)
        def _():
            acc[...] += _dot(w_ref[...], _sel_matrix(d_ref, idx_ref[j * width + s_], wp, tw), NT)

        @pl.when(s_ == width - 1)
        def _():
            o_ref[...] = acc[...].astype(BF16)

    return pl.pallas_call(
        body, name=name,
        grid_spec=pltpu.PrefetchScalarGridSpec(
            num_scalar_prefetch=2, grid=(dm // tm, N_DEV, width),
            in_specs=[pl.BlockSpec((tm, tw), lambda i, j, s_, ix, vl: (i, ix[j * width + s_])),
                      pl.BlockSpec((None, wp, 1), lambda i, j, s_, ix, vl: (j, 0, 0))],
            out_specs=pl.BlockSpec((None, tm, wp), lambda i, j, s_, ix, vl: (j, i, 0)),
            scratch_shapes=[pltpu.VMEM((tm, wp), F32)]),
        out_shape=jax.ShapeDtypeStruct((N_DEV, dm, wp), BF16),
        compiler_params=_cparams(("parallel", "parallel", "arbitrary")),
    )(idx, val, dw, tbl)


def _adamw(parts, w, m, v, *, name):
    r, cdim = w.shape
    n_parts = parts.shape[0]
    tr = _pick(r, (256, 128, 64, 32, 16, 8))
    rc = min(16, tr)
    c1 = 1.0 - ADAM_B1 ** ADAM_STEP
    c2 = 1.0 - ADAM_B2 ** ADAM_STEP

    def body(p_ref, w_ref, m_ref, v_ref, g_ref, d_ref, mo_ref, vo_ref):
        def chunk(ci, carry):
            sl = pl.ds(pl.multiple_of(ci * rc, rc), rc)
            g = p_ref[0, sl, :].astype(F32)
            for i in range(1, n_parts):
                g = g + p_ref[i, sl, :].astype(F32)
            mn = ADAM_B1 * m_ref[sl, :] + (1.0 - ADAM_B1) * g
            vn = ADAM_B2 * v_ref[sl, :] + (1.0 - ADAM_B2) * jnp.square(g)
            m_hat = mn / c1
            v_hat = vn / c2
            g_ref[sl, :] = g
            d_ref[sl, :] = -ADAM_LR * (m_hat / (jnp.sqrt(v_hat) + ADAM_EPS) + ADAM_WD * w_ref[sl, :])
            mo_ref[sl, :] = mn
            vo_ref[sl, :] = vn
            return carry

        lax.fori_loop(0, tr // rc, chunk, 0)

    spec = pl.BlockSpec((tr, cdim), lambda i: (i, 0))
    return pl.pallas_call(
        body, name=name, grid=(r // tr,),
        in_specs=[pl.BlockSpec((n_parts, tr, cdim), lambda i: (0, i, 0)), spec, spec, spec],
        out_specs=[spec] * 4,
        out_shape=[jax.ShapeDtypeStruct((r, cdim), F32)] * 4,
        compiler_params=_cparams(("parallel",)),
    )(parts, w, m, v)


def _pad_to(v, n):
    return v if v.shape[0] == n else jnp.concatenate([v, jnp.zeros((n - v.shape[0],), v.dtype)])


def _pad_cols(v, n):
    return v if v.shape[-1] == n else jnp.concatenate([v, jnp.zeros(v.shape[:-1] + (n - v.shape[-1],), v.dtype)], axis=-1)


def _pack(vs, cols, row_mult, dtype):
    offs, o = [], 0
    for v in vs:
        offs.append(o)
        o += v.size
    rows = -(-o // cols)
    rows = -(-rows // row_mult) * row_mult
    flat = jnp.concatenate([v.reshape(-1).astype(dtype) for v in vs])
    return _pad_to(flat, rows * cols).reshape(rows, cols), offs


SHARDED = ("ffn1_w_gate", "ffn1_w_up", "ffn1_w_down", "w_in", "w_merge_gate", "gla_gate_up", "w_branch_fox",
           "w_branch_gla", "w_out", "ffn2_w_gate", "ffn2_w_up", "ffn2_w_down", "w_ple_gate", "w_ple_proj")
ROW_SHARDED = ("ffn1_w_down", "w_out", "ffn2_w_down", "w_ple_gate")
REPLICATED = ("ffn1_norm", "mix_norm", "fox_forget_bias", "gla_gate_bias", "gla_head_norm", "b_merge_gate",
              "ffn2_norm", "ple_norm", "final_norm")
WEIGHTS = ("ffn1_norm", "ffn1_w_gate", "ffn1_w_up", "ffn1_w_down", "mix_norm", "w_in", "fox_forget_bias",
           "gla_gate_up", "gla_gate_bias", "gla_head_norm", "w_branch_fox", "w_branch_gla", "w_merge_gate",
           "b_merge_gate", "w_out", "ffn2_norm", "ffn2_w_gate", "ffn2_w_up", "ffn2_w_down", "ple_norm",
           "w_ple_proj", "w_ple_gate", "final_norm")


def kernel(x, p, ffn1_norm, ffn1_w_gate, ffn1_w_up, ffn1_w_down, mix_norm, w_in, fox_forget_bias, gla_gate_up, gla_gate_bias, gla_head_norm, w_branch_fox, w_branch_gla, w_merge_gate, b_merge_gate, w_out, ffn2_norm, ffn2_w_gate, ffn2_w_up, ffn2_w_down, ple_norm, w_ple_proj, w_ple_gate, final_norm, loss_target, m_ffn1_norm, m_ffn1_w_gate, m_ffn1_w_up, m_ffn1_w_down, m_mix_norm, m_w_in, m_fox_forget_bias, m_gla_gate_up, m_gla_gate_bias, m_gla_head_norm, m_w_branch_fox, m_w_branch_gla, m_w_merge_gate, m_b_merge_gate, m_w_out, m_ffn2_norm, m_ffn2_w_gate, m_ffn2_w_up, m_ffn2_w_down, m_ple_norm, m_w_ple_proj, m_w_ple_gate, m_final_norm, v_ffn1_norm, v_ffn1_w_gate, v_ffn1_w_up, v_ffn1_w_down, v_mix_norm, v_w_in, v_fox_forget_bias, v_gla_gate_up, v_gla_gate_bias, v_gla_head_norm, v_w_branch_fox, v_w_branch_gla, v_w_merge_gate, v_b_merge_gate, v_w_out, v_ffn2_norm, v_ffn2_w_gate, v_ffn2_w_up, v_ffn2_w_down, v_ple_norm, v_w_ple_proj, v_w_ple_gate, v_final_norm):
    args = dict(locals())
    wts = {n: args[n] for n in WEIGHTS}
    mom_m = {n: args["m_" + n] for n in WEIGHTS}
    mom_v = {n: args["v_" + n] for n in WEIGHTS}

    xs, ps, tgt = x[0], p[0, 0], loss_target[0]
    s, d = xs.shape
    fox_w = w_branch_fox.shape[1]
    gla_vw = w_branch_gla.shape[1]
    fox_heads = fox_w // HEAD_DIM
    gla_heads = gla_vw // GLA_VAL_DIM
    gla_kw = gla_heads * HEAD_DIM
    rank = gla_gate_up.shape[1]

    c_fl = 3 * fox_w
    o_gr, o_gq, o_gk = gla_vw, 2 * gla_vw, 2 * gla_vw + gla_kw
    o_fl = o_gk + gla_kw
    o_gd = o_fl + LANES
    rest_w = o_gd + LANES
    padded = c_fl + rest_w
    seg = [(c_fl, 0), (fox_heads, c_fl + o_fl), (gla_kw, c_fl + o_gq), (gla_kw, c_fl + o_gk), (gla_vw, c_fl),
           (gla_vw, c_fl + o_gr), (rank, c_fl + o_gd)]
    dest = np.concatenate([np.arange(w_, dtype=np.int32) + o_ for w_, o_ in seg])
    ws = w_in.shape[2]
    wp = -(-ws // LANES) * LANES
    tw = 256 if padded % 256 == 0 else LANES
    tbl, (t_idx, t_val, t_width), (s_idx, s_val, s_width) = _sel_tables(dest, ws, wp, tw)
    tbl = jnp.asarray(tbl)

    ov = _Overlap()
    for n in SHARDED:
        sh = wts[n][0].astype(BF16)
        ov.add(n, "gather", _pad_cols(sh, wp) if n == "w_in" else sh)
    fbias = _pad_cols(fox_forget_bias, LANES)
    bmg_f, bmg_g = b_merge_gate[:, :d], b_merge_gate[:, d:]
    ghn = jnp.tile(gla_head_norm, (1, gla_heads))

    h1, ffn1_saved = _ffn_fwd(xs, ffn1_norm, ov, tag="ffn1")
    u = _rms_fwd(h1, mix_norm, name="mix_norm")
    win = _win_unshard(ov.weight("w_in"), tbl, jnp.asarray(t_idx), jnp.asarray(t_val), t_width, tw=tw, padded=padded,
                       name="in_proj_unshard")
    win_fox, win_rest = win[:, :c_fl], win[:, c_fl:]
    zf = ov.mm(u, win_fox, mode="nn", name="in_proj_fox", out_dtype=BF16)
    zr = ov.mm(u, win_rest, mode="nn", name="in_proj_rest")
    gz = ov.mm(u, ov.weight("w_merge_gate"), mode="nn", name="merge_gate")
    gup = ov.weight("gla_gate_up").transpose(1, 0, 2).reshape(rank, gla_kw)
    gup = jnp.concatenate([gup, jnp.zeros((LANES - rank, gla_kw), BF16)], axis=0)

    log_f = _rowwise(lambda fl, b: _log_sigmoid(fl + b), [(zr, LANES, o_fl // LANES)], [fbias], [(LANES, F32)],
                     name="forget_gate")[0]
    f_cum = _cumsum(log_f[:, :fox_heads].T, reverse=False, name="forget_cumsum")
    f_rep = jnp.broadcast_to(f_cum.T[:, :, None], (s, fox_heads, HEAD_DIM)).reshape(s, fox_w)
    f_row = f_cum[:, None, :]
    attn_us = 2.0 * s * s * HEAD_DIM * fox_heads / _Overlap.MM_FLOPS_PER_US
    y_fox, y_fox_bf, lse = ov.carry(
        ATTN_FWD_MATMULS * attn_us, lambda comm: _fox_fwd(zf, f_rep, f_row, heads=fox_heads, name="fox_fwd", comm=comm))

    def decay_fn(gd, gupv, gb):
        return _log_sigmoid(_dot(gd, gupv, NN) + gb) * (1.0 / GLA_GATE_TAU)

    la = _rowwise(decay_fn, [(zr, LANES, o_gd // LANES)], [gup, gla_gate_bias], [(gla_kw, F32)], name="gla_decay", rc=128)[0]
    q_blk, k_blk = o_gq // HEAD_DIM, o_gk // HEAD_DIM
    o_gla, states = _gla_fwd(zr, la, heads=gla_heads, q_blk=q_blk, k_blk=k_blk, name="gla_fwd")

    def gla_out_fn(o, gr, g):
        outs = []
        for hh in range(gla_heads):
            sl = slice(hh * GLA_VAL_DIM, (hh + 1) * GLA_VAL_DIM)
            _, oh = _rms_parts(o[:, sl])
            outs.append(oh * g[:, sl] * _silu_parts(gr[:, sl])[0])
        return jnp.concatenate(outs, axis=1)

    y_gla = _rowwise(gla_out_fn, [o_gla, (zr, gla_vw, o_gr // gla_vw)], [ghn], [(gla_vw, BF16)], name="gla_out")[0]
    br_f = ov.mm(y_fox_bf, ov.weight("w_branch_fox"), mode="nn", name="branch_fox")
    br_g = ov.mm(y_gla, ov.weight("w_branch_gla"), mode="nn", name="branch_gla")

    def merge_fn(zf_, zg_, bf_, bg_, b1, b2):
        return _sigmoid(zf_ + b1) * bf_ + _sigmoid(zg_ + b2) * bg_

    merged = _rowwise(merge_fn, [(gz, d, 0), (gz, d, 1), br_f, br_g], [bmg_f, bmg_g], [(d, BF16)], name="merge")[0]
    h2 = ov.mm(merged, ov.weight("w_out"), mode="nn", name="out_proj", add=h1)
    h3, ffn2_saved = _ffn_fwd(h2, ffn2_norm, ov, tag="ffn2")
    n3 = _rms_fwd(h3, ple_norm, name="ple_norm")
    gl = ov.mm(n3, ov.weight("w_ple_gate"), mode="nn", name="ple_gate")
    pe = ov.mm(ps, ov.weight("w_ple_proj"), mode="nn", name="ple_proj")

    def head_fn(h3b, glb, peb, tb, gfin):
        pg = _sigmoid(glb)
        h4 = h3b + pg * peb
        r, xh = _rms_parts(h4)
        err = xh * gfin - tb
        dy = err * (1.0 / d)
        t = dy * gfin
        dh4 = r * (t - xh * jnp.mean(t * xh, axis=-1, keepdims=True))
        return dh4, dh4 * pg, dh4 * peb * pg * (1.0 - pg), _colsum(err * err), _colsum(dy * xh)

    dh4, dpe, dgl, loss_cols, d_final = _rowwise(
        head_fn, [h3, gl, pe, tgt], [final_norm.reshape(1, d)], [(d, F32), (d, BF16), (d, BF16)], [d, d], name="loss_head")
    loss = lax.psum(0.5 * jnp.sum(loss_cols) / d, AXES)

    grads = {"final_norm": d_final.reshape(d)}
    ov.grad("w_ple_proj", ov.mm(ps, dpe, mode="tn", name="ple_proj_dw", out_dtype=BF16, out_chunked=True))
    ov.grad("w_ple_gate", ov.mm(n3, dgl, mode="tn", name="ple_gate_dw", out_dtype=BF16))
    dn3 = ov.mm(dgl, ov.weight("w_ple_gate"), mode="nt", name="ple_gate_dx", out_dtype=BF16)
    dh3, dh3_bf, grads["ple_norm"] = _rms_bwd(h3, dn3, ple_norm, dh4, name="ple_norm_bwd")
    dh2, dh2_bf, grads["ffn2_norm"] = _ffn_bwd(h2, ffn2_norm, ov, ffn2_saved, dh3, dh3_bf, tag="ffn2")

    ov.grad("w_out", ov.mm(merged, dh2_bf, mode="tn", name="out_proj_dw", out_dtype=BF16))
    dmerged = ov.mm(dh2_bf, ov.weight("w_out"), mode="nt", name="out_proj_dx")

    def merge_bwd_fn(zf_, zg_, bf_, bg_, dm, b1, b2):
        sf, sg = _sigmoid(zf_ + b1), _sigmoid(zg_ + b2)
        dz = jnp.concatenate([dm * bf_ * sf * (1.0 - sf), dm * bg_ * sg * (1.0 - sg)], axis=1)
        return dm * sf, dm * sg, dz, _colsum(dz)

    dbr_f, dbr_g, dgz, grads["b_merge_gate"] = _rowwise(
        merge_bwd_fn, [(gz, d, 0), (gz, d, 1), br_f, br_g, dmerged], [bmg_f, bmg_g],
        [(d, BF16), (d, BF16), (2 * d, BF16)], [2 * d], name="merge_bwd")
    ov.grad("w_merge_gate", ov.mm(u, dgz, mode="tn", name="merge_gate_dw", out_dtype=BF16, out_chunked=True))
    ov.grad("w_branch_fox", ov.mm(y_fox_bf, dbr_f, mode="tn", name="branch_fox_dw", out_dtype=BF16, out_chunked=True))
    ov.grad("w_branch_gla", ov.mm(y_gla, dbr_g, mode="tn", name="branch_gla_dw", out_dtype=BF16, out_chunked=True))
    dy_fox = ov.mm(dbr_f, ov.weight("w_branch_fox"), mode="nt", name="branch_fox_dx")
    dy_gla = ov.mm(dbr_g, ov.weight("w_branch_gla"), mode="nt", name="branch_gla_dx")

    def gla_out_bwd_fn(o, gr, dy, g):
        dos, dgrs, dgs = [], [], []
        for hh in range(gla_heads):
            sl = slice(hh * GLA_VAL_DIM, (hh + 1) * GLA_VAL_DIM)
            r, oh = _rms_parts(o[:, sl])
            si, dsi = _silu_parts(gr[:, sl])
            don = dy[:, sl] * si
            dgrs.append(dy[:, sl] * oh * g[:, sl] * dsi)
            t = don * g[:, sl]
            dos.append(r * (t - oh * jnp.mean(t * oh, axis=-1, keepdims=True)))
            dgs.append(_colsum(don * oh))
        return jnp.concatenate(dos, axis=1), jnp.concatenate(dgrs, axis=1), jnp.concatenate(dgs, axis=1)

    do_gla, dgr, d_ghn = _rowwise(gla_out_bwd_fn, [o_gla, (zr, gla_vw, o_gr // gla_vw), dy_gla], [ghn],
                                  [(gla_vw, F32), (gla_vw, F32)], [gla_vw], name="gla_out_bwd")
    grads["gla_head_norm"] = d_ghn.reshape(gla_heads, GLA_VAL_DIM).sum(axis=0, keepdims=True)
    dgq, dgk, dgv, dla = _gla_bwd(zr, la, do_gla, states, heads=gla_heads, q_blk=q_blk, k_blk=k_blk, name="gla_bwd")

    def decay_bwd_fn(dl, gd, gupv, gb):
        pre = _dot(gd, gupv, NN) + gb
        dpre = dl * (1.0 / GLA_GATE_TAU) * _sigmoid(-pre)
        return _dot(dpre, gupv, NT), dpre, _colsum(dpre)

    dgd, dpre_bf, grads["gla_gate_bias"] = _rowwise(
        decay_bwd_fn, [dla, (zr, LANES, o_gd // LANES)], [gup, gla_gate_bias], [(LANES, F32), (gla_kw, BF16)], [gla_kw],
        name="gla_decay_bwd", rc=128)
    d_gup = ov.mm(zr[:, o_gd:o_gd + LANES], dpre_bf, mode="tn", name="gla_gate_up_dw")[:rank]
    ov.grad("gla_gate_up", d_gup.reshape(rank, N_DEV, gla_kw // N_DEV).transpose(1, 0, 2).astype(BF16))

    def delta_fn(dyv, yv):
        outs = []
        for hh in range(fox_heads):
            sl = slice(hh * HEAD_DIM, (hh + 1) * HEAD_DIM)
            outs.append(jnp.broadcast_to(jnp.sum(dyv[:, sl] * yv[:, sl], axis=-1, keepdims=True), (dyv.shape[0], HEAD_DIM)))
        return jnp.concatenate(outs, axis=1)

    delta = _rowwise(delta_fn, [dy_fox, y_fox], [], [(fox_w, F32)], name="fox_delta")[0]
    dfq, dfk, dfv, d_fcol, d_frow = ov.carry(
        ATTN_BWD_MATMULS * attn_us,
        lambda comm: _fox_bwd(zf, dy_fox, lse, delta, f_rep, f_row, heads=fox_heads, name="fox_bwd", comm=comm))
    d_fcum = d_fcol[:, ::HEAD_DIM].T + d_frow.reshape(fox_heads, s)
    d_logf = _cumsum(d_fcum, reverse=True, name="forget_cumsum_bwd")
    d_logf = _pad_cols(d_logf.T, LANES)

    def forget_bwd_fn(dl, fl, b):
        dfl_ = dl * _sigmoid(-(fl + b))
        return dfl_, _colsum(dfl_)

    dfl, d_fbias = _rowwise(forget_bwd_fn, [d_logf, (zr, LANES, o_fl // LANES)], [fbias], [(LANES, F32)], [LANES],
                            name="forget_gate_bwd")
    grads["fox_forget_bias"] = d_fbias[:, :fox_heads]

    dz = jnp.concatenate([dfq, dfk, dfv, dgv, dgr, dgq, dgk, dfl, dgd], axis=1).astype(BF16)
    dwin = ov.mm(u, dz, mode="tn", name="in_proj_dw", out_dtype=BF16)
    ov.grad("w_in", _win_to_shards(dwin, tbl, jnp.asarray(s_idx), jnp.asarray(s_val), s_width, tw=tw, wp=wp,
                                   name="in_proj_dw_shards"))
    du = ov.mm(dgz, ov.weight("w_merge_gate"), mode="nt", name="merge_gate_dx")
    du = ov.mm(dz, win, mode="nt", name="in_proj_dx", add=du, out_dtype=BF16)
    dh1, dh1_bf, grads["mix_norm"] = _rms_bwd(h1, du, mix_norm, dh2, name="mix_norm_bwd")
    dx, _, grads["ffn1_norm"] = _ffn_bwd(xs, ffn1_norm, ov, ffn1_saved, dh1, dh1_bf, tag="ffn1")

    outs = {}
    for n in SHARDED:
        parts = ov.get("d_" + n)
        state = [_pad_cols(t_[n][0], parts.shape[2]) for t_ in (wts, mom_m, mom_v)]
        res4 = _adamw(parts, *state, name=f"adamw_{n}")
        for kind, r_ in zip(("grad", "delta", "new_m", "new_v"), res4):
            outs[f"{kind}_{n}"] = r_[:, :wts[n].shape[2]][None]

    send_small, small_offs = _pack([grads[n] for n in REPLICATED], LANES, 8, F32)
    recv_small = _comm("bcast", send_small, name="exchange_replicated")
    w_sm, _ = _pack([wts[n] for n in REPLICATED], LANES, 8, F32)
    m_sm, _ = _pack([mom_m[n] for n in REPLICATED], LANES, 8, F32)
    v_sm, _ = _pack([mom_v[n] for n in REPLICATED], LANES, 8, F32)
    small = _adamw(recv_small, w_sm, m_sm, v_sm, name="adamw_replicated")
    for kind, buf in zip(("grad", "delta", "new_m", "new_v"), small):
        fs = buf.reshape(-1)
        for n, o in zip(REPLICATED, small_offs):
            outs[f"{kind}_{n}"] = fs[o:o + wts[n].size].reshape(wts[n].shape)

    res = [loss, dx[None]]
    for kind in ("grad", "delta", "new_m", "new_v"):
        res += [outs[f"{kind}_{n}"] for n in WEIGHTS]
    return tuple(res)
```

```python
import functools

import jax
import jax.numpy as jnp
import numpy as np
from jax import lax
from jax.experimental import pallas as pl
from jax.experimental.pallas import tpu as pltpu

F32 = jnp.float32
BF16 = jnp.bfloat16
MESH = pl.DeviceIdType.MESH
AXES = ("x", "y", "c")
N_DEV = 8

VMEM_LIMIT_BYTES = 56 * 1024 * 1024
MM_BLOCK_BUDGET_BYTES = 40 * 1024 * 1024
LANES = 128

EPS = 1e-6
HEAD_DIM = 128
GLA_VAL_DIM = 256
GLA_CHUNK = 64
GLA_GATE_TAU = 16.0
ADAM_LR, ADAM_B1, ADAM_B2, ADAM_EPS, ADAM_WD, ADAM_STEP = 0.001, 0.9, 0.999, 1e-08, 0.01, 10

ATTN_FWD_MATMULS = 7.0
ATTN_BWD_MATMULS = 7.0

HIGHEST = lax.Precision.HIGHEST
NN = (((1,), (0,)), ((), ()))
NT = (((1,), (1,)), ((), ()))
TN = (((0,), (0,)), ((), ()))


def _cparams(sem):
    return pltpu.CompilerParams(dimension_semantics=sem, vmem_limit_bytes=VMEM_LIMIT_BYTES)


def _pick(dim, cands):
    for c in cands:
        if dim % c == 0:
            return c
    return dim


def _bf(v):
    return v if v.dtype == BF16 else v.astype(BF16)


def _dot(a, b, dims):
    return lax.dot_general(_bf(a), _bf(b), dims, preferred_element_type=F32)


def _sigmoid(v):
    return 1.0 / (1.0 + jnp.exp(-v))


def _log_sigmoid(v):
    return jnp.minimum(v, 0.0) - jnp.log(1.0 + jnp.exp(-jnp.abs(v)))


def _logical(v):
    return (v.shape[1], v.shape[0] * v.shape[2]) if v.ndim == 3 else v.shape


def _mm(a, b, *, mode, name, out_dtype=F32, add=None, scale=1.0, out_chunked=False, comm=()):
    la, lb = _logical(a), _logical(b)
    if mode == "nn":
        (m, k), (k2, n) = la, lb
        a_minor, b_minor = "k", "n"
    elif mode == "nt":
        (m, k), (n, k2) = la, lb
        a_minor, b_minor = "k", "k"
    else:
        (k, m), (k2, n) = la, lb
        a_minor, b_minor = "m", "n"
    assert k == k2, (name, a.shape, b.shape)
    forced = {}
    for v, minor in ((a, a_minor), (b, b_minor)):
        if v.ndim == 3:
            assert forced.get(minor, v.shape[2]) == v.shape[2], name
            forced[minor] = v.shape[2]
    if out_chunked:
        assert forced.get("n", n // N_DEV) == n // N_DEV, name
        forced["n"] = n // N_DEV
    tm = forced.get("m") or _pick(m, (1024, 512, 256, 128))
    tn = forced.get("n") or _pick(n, (1408, 1280, 1024, 512, 256, 128))

    def blocks_bytes(tm_, tn_, t):
        io = tm_ * t * a.dtype.itemsize + t * tn_ * b.dtype.itemsize
        return 2 * (io + tm_ * tn_ * (jnp.dtype(out_dtype).itemsize + (4 if add is not None else 0))) + tm_ * tn_ * 4

    kc = forced.get("k")
    a_k_minor, b_k_minor = mode != "tn", mode == "nt"
    if kc:
        aligned = all(v.ndim == 3 or kc % (LANES if minor else 16) == 0 for v, minor in ((a, a_k_minor), (b, b_k_minor)))
        fits = [(tm_, tn_) for tm_, tn_ in ((tm, tn), (512, tn), (512, 512))
                if m % tm_ == 0 and n % tn_ == 0 and forced.get("m", tm_) == tm_ and forced.get("n", tn_) == tn_
                and blocks_bytes(tm_, tn_, k) <= MM_BLOCK_BUDGET_BYTES]
        if aligned and fits:
            (tm, tn), tk = fits[0], k
        else:
            tk, kc = kc, None
    else:
        tk = k if blocks_bytes(tm, tn, k) <= MM_BLOCK_BUDGET_BYTES else _pick(k, (512, 640, 256, 128))
    nk = k // tk
    dims = {"nn": NN, "nt": NT, "tn": TN}[mode]
    gi, gj, gk = (lambda i, j, kk: i), (lambda i, j, kk: j), (lambda i, j, kk: kk)

    def spec(v, t_major, t_minor, g_major, g_minor, all_chunks=False):
        if v.ndim == 3:
            if all_chunks:
                return pl.BlockSpec((N_DEV, t_major, v.shape[2]), lambda i, j, kk: (0, g_major(i, j, kk), 0))
            return pl.BlockSpec((None, t_major, v.shape[2]), lambda i, j, kk: (g_minor(i, j, kk), g_major(i, j, kk), 0))
        return pl.BlockSpec((t_major, t_minor), lambda i, j, kk: (g_major(i, j, kk), g_minor(i, j, kk)))

    a_spec = spec(a, tk, tm, gk, gi) if mode == "tn" else spec(a, tm, tk, gi, gk, bool(kc))
    b_spec = spec(b, tn, tk, gj, gk, bool(kc)) if mode == "nt" else spec(b, tk, tn, gk, gj)

    def k_chunk(ref, minor, c_):
        if len(ref.shape) == 3:
            return ref[c_]
        return ref[:, c_ * kc:(c_ + 1) * kc] if minor else ref[c_ * kc:(c_ + 1) * kc, :]
    if out_chunked:
        o_spec = pl.BlockSpec((None, tm, tn), lambda i, j, kk: (j, i, 0))
        out_shape = jax.ShapeDtypeStruct((N_DEV, m, tn), out_dtype)
    else:
        o_spec = pl.BlockSpec((tm, tn), lambda i, j, kk: (i, j))
        out_shape = jax.ShapeDtypeStruct((m, n), out_dtype)
    has_add = add is not None
    assert not (has_add and out_chunked), name

    def body(*refs):
        a_ref, b_ref = refs[0], refs[1]
        add_ref = refs[2] if has_add else None
        o_ref = refs[3 if has_add else 2]
        kk = pl.program_id(2)

        def finish(r):
            if scale != 1.0:
                r = r * scale
            if has_add:
                r = r + add_ref[...]
            o_ref[...] = r.astype(o_ref.dtype)

        if kc:
            r = _dot(k_chunk(a_ref, a_k_minor, 0), k_chunk(b_ref, b_k_minor, 0), dims)
            for c_ in range(1, k // kc):
                r = r + _dot(k_chunk(a_ref, a_k_minor, c_), k_chunk(b_ref, b_k_minor, c_), dims)
            finish(r)
        elif nk == 1:
            finish(_dot(a_ref[...], b_ref[...], dims))
        else:
            acc_ref = refs[-1]

            @pl.when(kk == 0)
            def _():
                acc_ref[...] = jnp.zeros_like(acc_ref)

            acc_ref[...] += _dot(a_ref[...], b_ref[...], dims)

            @pl.when(kk == nk - 1)
            def _():
                finish(acc_ref[...])

    res, carried = _call(
        body, name=name, grid=(m // tm, n // tn, nk),
        in_specs=[a_spec, b_spec] + ([o_spec] if has_add else []), out_specs=[o_spec], out_shape=[out_shape],
        scratch_shapes=[pltpu.VMEM((tm, tn), F32)] if nk > 1 else [],
        semantics=("parallel", "parallel", "arbitrary"), operands=[a, b] + ([add] if has_add else []), comm=comm)
    return (res[0], carried) if comm else res[0]


def _call(body, *, name, grid, in_specs, out_specs, out_shape, scratch_shapes, semantics, operands, comm=()):
    n_in, n_out, n_scr, n = len(in_specs), len(out_specs), len(scratch_shapes), len(comm)
    if not comm:
        res = pl.pallas_call(body, name=name, grid=grid, in_specs=in_specs, out_specs=out_specs, out_shape=out_shape,
                             scratch_shapes=scratch_shapes, compiler_params=_cparams(semantics))(*operands)
        return res, []

    def carrying(*refs):
        ins, c_in = refs[:n_in], refs[n_in:n_in + n]
        outs, c_out = refs[n_in + n:n_in + n + n_out], refs[n_in + n + n_out:n_in + 2 * n + n_out]
        scratch, sems = refs[n_in + 2 * n + n_out:][:n_scr], refs[n_in + 2 * n + n_out + n_scr:]
        tasks = [_comm_ops(kind, c_in[t], c_out[t], *sems[3 * t:3 * t + 3]) for t, (kind, _) in enumerate(comm)]
        ids = [pl.program_id(ax) for ax in range(len(grid))]

        @pl.when(functools.reduce(lambda p, q: p & q, [i == 0 for i in ids]))
        def _():
            for start, _ in tasks:
                start()

        body(*ins, *outs, *scratch)

        @pl.when(functools.reduce(lambda p, q: p & q, [i == g - 1 for i, g in zip(ids, grid)]))
        def _():
            for _, finish in tasks:
                finish()

    any_spec = pl.BlockSpec(memory_space=pl.ANY)
    res = pl.pallas_call(
        carrying, name=name, grid=grid,
        in_specs=list(in_specs) + [any_spec] * n, out_specs=list(out_specs) + [any_spec] * n,
        out_shape=list(out_shape) + [_comm_out_shape(kind, v) for kind, v in comm],
        scratch_shapes=list(scratch_shapes) + _comm_scratch(n),
        compiler_params=_cparams(("arbitrary",) * len(grid)),
    )(*operands, *[v for _, v in comm])
    return res[:n_out], res[n_out:]


def _rowwise(fn, rows, consts, outs, accs=(), *, name, rc=None):
    rows = [r if isinstance(r, tuple) else (r, r.shape[1], 0) for r in rows]
    m = rows[0][0].shape[0]
    widths = [w for _, w, _ in rows] + [n for n, _ in outs]
    row_bytes = sum(w * r.dtype.itemsize for r, w, _ in rows) + sum(n * jnp.dtype(d).itemsize for n, d in outs)
    tm = 1024
    while tm > 16 and (m % tm or 2 * tm * row_bytes > 24 * 1024 * 1024):
        tm //= 2
    if m % tm:
        tm = m
    if rc is None:
        rc = 16
        while rc * 2 <= tm and rc * 2 * max(widths) <= 32768:
            rc *= 2
    rc = min(rc, tm)
    nr, nc, no = len(rows), len(consts), len(outs)

    def body(*refs):
        in_refs, c_refs = refs[:nr], refs[nr:nr + nc]
        o_refs, a_refs = refs[nr + nc:nr + nc + no], refs[nr + nc + no:]

        @pl.when(pl.program_id(0) == 0)
        def _():
            for r in a_refs:
                r[...] = jnp.zeros_like(r)

        cvals = [c[...] for c in c_refs]

        def chunk(ci, carry):
            sl = pl.ds(pl.multiple_of(ci * rc, rc), rc)
            res = fn(*[r[sl, :] for r in in_refs], *cvals)
            if not isinstance(res, (tuple, list)):
                res = (res,)
            for r, v in zip(o_refs, res[:no]):
                r[sl, :] = v.astype(r.dtype)
            for r, v in zip(a_refs, res[no:]):
                r[...] += v
            return carry

        lax.fori_loop(0, tm // rc, chunk, 0)

    in_specs = [pl.BlockSpec((tm, w), functools.partial(lambda i, cb: (i, cb), cb=cb)) for _, w, cb in rows]
    in_specs += [pl.BlockSpec(c.shape, lambda i: (0, 0)) for c in consts]
    out_specs = [pl.BlockSpec((tm, n), lambda i: (i, 0)) for n, _ in outs]
    out_specs += [pl.BlockSpec((1, n), lambda i: (0, 0)) for n in accs]
    out_shape = [jax.ShapeDtypeStruct((m, n), d) for n, d in outs] + [jax.ShapeDtypeStruct((1, n), F32) for n in accs]
    res = pl.pallas_call(
        body, name=name, grid=(m // tm,),
        in_specs=in_specs, out_specs=out_specs, out_shape=out_shape,
        compiler_params=_cparams(("arbitrary",)),
    )(*[r for r, _, _ in rows], *consts)
    return res


def _colsum(v):
    return jnp.sum(v, axis=0, keepdims=True)


def _rms_parts(xv):
    r = lax.rsqrt(jnp.mean(xv * xv, axis=-1, keepdims=True) + EPS)
    return r, xv * r


def _rms_fwd(xv, g, *, name):
    d = xv.shape[1]

    def fn(xb, gb):
        _, xh = _rms_parts(xb)
        return xh * gb

    return _rowwise(fn, [xv], [g], [(d, BF16)], name=name)[0]


def _rms_bwd(xv, dy, g, add, *, name):
    d = xv.shape[1]

    def fn(xb, dyb, addb, gb):
        r, xh = _rms_parts(xb)
        t = dyb * gb
        dx = r * (t - xh * jnp.mean(t * xh, axis=-1, keepdims=True)) + addb
        return dx, dx, _colsum(dyb * xh)

    return _rowwise(fn, [xv, dy, add], [g], [(d, F32), (d, BF16)], [d], name=name)


def _silu_parts(a):
    sg = _sigmoid(a)
    return a * sg, sg * (1.0 + a * (1.0 - sg))


def _rows(v):
    return v.reshape(v.shape[0] * v.shape[1], v.shape[2])


def _ffn_fwd(h, g, ov, *, tag):
    s = h.shape[0]
    n = _rms_fwd(h, g, name=f"{tag}_norm")
    a = ov.mm(n, ov.weight(f"{tag}_w_gate"), mode="nn", name=f"{tag}_gate", out_dtype=BF16, out_chunked=True)
    b = ov.mm(n, ov.weight(f"{tag}_w_up"), mode="nn", name=f"{tag}_up", out_dtype=BF16, out_chunked=True)
    c = a.shape[2]
    hm = _rowwise(lambda av, bv: _silu_parts(av.astype(F32))[0] * bv.astype(F32), [_rows(a), _rows(b)], [], [(c, BF16)],
                  name=f"{tag}_act")[0]
    hm = hm.reshape(N_DEV, s, c)
    out = ov.mm(hm, ov.weight(f"{tag}_w_down"), mode="nn", name=f"{tag}_down", add=h, scale=0.5)
    return out, (n, a, b, hm)


def _ffn_bwd(h, g, ov, saved, dout, dout_bf, *, tag):
    n, a, b, hm = saved
    wg, wu, wd = ov.weight(f"{tag}_w_gate"), ov.weight(f"{tag}_w_up"), ov.weight(f"{tag}_w_down")
    s, c = h.shape[0], wg.shape[2]
    d_wd = ov.mm(hm, dout_bf, mode="tn", name=f"{tag}_down_dw", scale=0.5, out_dtype=BF16)
    ov.grad(f"{tag}_w_down", d_wd)
    dhm = ov.mm(dout_bf, wd, mode="nt", name=f"{tag}_down_dx", scale=0.5, out_dtype=BF16, out_chunked=True)

    def act_bwd(av, bv, dv):
        av, bv, dv = av.astype(F32), bv.astype(F32), dv.astype(F32)
        si, dsi = _silu_parts(av)
        return dv * bv * dsi, dv * si

    da, db = _rowwise(act_bwd, [_rows(a), _rows(b), _rows(dhm)], [], [(c, BF16), (c, BF16)], name=f"{tag}_act_bwd")
    da, db = da.reshape(N_DEV, s, c), db.reshape(N_DEV, s, c)
    ov.grad(f"{tag}_w_gate", ov.mm(n, da, mode="tn", name=f"{tag}_gate_dw", out_dtype=BF16, out_chunked=True))
    ov.grad(f"{tag}_w_up", ov.mm(n, db, mode="tn", name=f"{tag}_up_dw", out_dtype=BF16, out_chunked=True))
    dn = ov.mm(da, wg, mode="nt", name=f"{tag}_gate_dx")
    dn = ov.mm(db, wu, mode="nt", name=f"{tag}_up_dx", add=dn)
    dh, dh_bf, dg = _rms_bwd(h, dn, g, dout, name=f"{tag}_norm_bwd")
    return dh, dh_bf, dg


def _cumsum(xv, *, reverse, name):
    h, s = xv.shape
    t = _pick(s, (512, 256, 128))
    nb = s // t

    def blk(j):
        return (0, nb - 1 - j) if reverse else (0, j)

    def body(x_ref, o_ref, carry):
        @pl.when(pl.program_id(0) == 0)
        def _():
            carry[...] = jnp.zeros_like(carry)

        i0 = lax.broadcasted_iota(jnp.int32, (t, t), 0)
        i1 = lax.broadcasted_iota(jnp.int32, (t, t), 1)
        tri = ((i0 >= i1) if reverse else (i0 <= i1)).astype(F32)
        xb = x_ref[...]
        o_ref[...] = jnp.dot(xb, tri, precision=HIGHEST, preferred_element_type=F32) + carry[...]
        carry[...] += jnp.sum(xb, axis=1, keepdims=True)

    return pl.pallas_call(
        body, name=name, grid=(nb,),
        in_specs=[pl.BlockSpec((h, t), blk)], out_specs=pl.BlockSpec((h, t), blk),
        out_shape=jax.ShapeDtypeStruct((h, s), F32),
        scratch_shapes=[pltpu.VMEM((h, 1), F32)],
        compiler_params=_cparams(("arbitrary",)),
    )(xv)


def _fox_tiles(s):
    t = _pick(s, (512, 256, 128))
    return t, t


def _causal(sc):
    t = sc.shape[0]
    keep = lax.broadcasted_iota(jnp.int32, (t, t), 1) <= lax.broadcasted_iota(jnp.int32, (t, t), 0)
    return jnp.where(keep, sc, -jnp.inf)


ATTN_HEADS_PER_STEP = 2


def _head_cols(hh):
    return slice(hh * HEAD_DIM, (hh + 1) * HEAD_DIM)


def _across(rowstat, width):
    return jnp.tile(rowstat, (1, width // HEAD_DIM))


def _fox_fwd(zf, f_rep, f_row, *, heads, name, comm=()):
    s = zf.shape[0]
    tq, tk = _fox_tiles(s)
    assert tq == tk
    nq, nk = s // tq, s // tk
    hp = 2 * ATTN_HEADS_PER_STEP if heads % (2 * ATTN_HEADS_PER_STEP) == 0 else 1
    wb = hp * HEAD_DIM
    scale = HEAD_DIM ** -0.5
    w = heads * HEAD_DIM

    def body(q_ref, k_ref, v_ref, fq_ref, fk_ref, o32_ref, o16_ref, lse_ref, m_sc, l_sc, acc_sc):
        i, j = pl.program_id(1), pl.program_id(2)

        @pl.when(j == 0)
        def _():
            m_sc[...] = jnp.full_like(m_sc, -jnp.inf)
            l_sc[...] = jnp.zeros_like(l_sc)
            acc_sc[...] = jnp.zeros_like(acc_sc)

        def step(diagonal):
            for hh in range(hp):
                cols = _head_cols(hh)
                sc = _dot(q_ref[:, cols], k_ref[:, cols], NT) * scale + _across(fq_ref[:, cols], tk) - fk_ref[hh]
                if diagonal:
                    sc = _causal(sc)
                m_old = m_sc[hh]
                m_new = jnp.maximum(m_old, jnp.max(sc, axis=-1, keepdims=True))
                alpha = jnp.exp(m_old - m_new)
                pr = jnp.exp(sc - _across(m_new, tk))
                l_sc[hh] = alpha * l_sc[hh] + jnp.sum(pr, axis=-1, keepdims=True)
                acc_sc[hh] = alpha * acc_sc[hh] + _dot(pr, v_ref[:, cols], NN)
                m_sc[hh] = m_new

        @pl.when(j < i)
        def _():
            step(False)

        @pl.when(j == i)
        def _():
            step(True)

        @pl.when(j == nk - 1)
        def _():
            for hh in range(hp):
                cols = _head_cols(hh)
                o = acc_sc[hh] / l_sc[hh]
                o32_ref[:, cols] = o
                o16_ref[:, cols] = o.astype(BF16)
                lse_ref[:, cols] = m_sc[hh] + jnp.log(l_sc[hh])

    def kv_blk(off):
        return lambda h, i, j: (jnp.minimum(j, i), off + h)

    o_spec = pl.BlockSpec((tq, wb), lambda h, i, j: (i, h))
    stat = pltpu.VMEM((hp, tq, HEAD_DIM), F32)
    return _call(
        body, name=name, grid=(heads // hp, nq, nk),
        in_specs=[
            o_spec,
            pl.BlockSpec((tk, wb), kv_blk(heads // hp)),
            pl.BlockSpec((tk, wb), kv_blk(2 * heads // hp)),
            o_spec,
            pl.BlockSpec((hp, 1, tk), lambda h, i, j: (h, 0, jnp.minimum(j, i))),
        ],
        out_specs=[o_spec, o_spec, o_spec],
        out_shape=[jax.ShapeDtypeStruct((s, w), F32), jax.ShapeDtypeStruct((s, w), BF16),
                   jax.ShapeDtypeStruct((s, w), F32)],
        scratch_shapes=[stat, stat, stat],
        semantics=("parallel", "parallel", "arbitrary"), operands=[zf, zf, zf, f_rep, f_row], comm=comm)


def _fox_bwd(zf, do, lse, delta, f_rep, f_row, *, heads, name, comm=()):
    s = zf.shape[0]
    tq, tk = _fox_tiles(s)
    assert tq == tk
    nq, nk = s // tq, s // tk
    hp = ATTN_HEADS_PER_STEP if heads % ATTN_HEADS_PER_STEP == 0 else 1
    wb = hp * HEAD_DIM
    scale = HEAD_DIM ** -0.5
    w = heads * HEAD_DIM

    def body(q_ref, k_ref, v_ref, do_ref, lse_ref, dl_ref, fq_ref, fk_ref, dq_ref, dk_ref, dv_ref, dfq_ref, dfk_ref):
        j, i = pl.program_id(1), pl.program_id(2)

        @pl.when((j == 0) & (i == 0))
        def _():
            dq_ref[...] = jnp.zeros_like(dq_ref)
            dfq_ref[...] = jnp.zeros_like(dfq_ref)

        @pl.when(i == 0)
        def _():
            dk_ref[...] = jnp.zeros_like(dk_ref)
            dv_ref[...] = jnp.zeros_like(dv_ref)
            dfk_ref[...] = jnp.zeros_like(dfk_ref)

        def step(diagonal):
            rows = pl.ds(pl.multiple_of(i * tq, tq), tq)
            for hh in range(hp):
                cols = _head_cols(hh)
                q, k, v = q_ref[:, cols], k_ref[:, cols], v_ref[:, cols]
                dob = do_ref[:, cols].astype(BF16)
                sc = _dot(q, k, NT) * scale + _across(fq_ref[:, cols], tk) - fk_ref[hh]
                if diagonal:
                    sc = _causal(sc)
                pr = jnp.exp(sc - _across(lse_ref[:, cols], tk))
                dv_ref[:, cols] += _dot(pr, dob, TN)
                dp = _dot(dob, v, NT)
                ds = pr * (dp - _across(dl_ref[:, cols], tk))
                dsb = ds.astype(BF16)
                dk_ref[:, cols] += _dot(dsb, q, TN) * scale
                dq_ref[rows, cols] += _dot(dsb, k, NN) * scale
                dfq_ref[rows, cols] += jnp.broadcast_to(jnp.sum(ds, axis=1, keepdims=True), (tq, HEAD_DIM))
                dfk_ref[hh] -= jnp.sum(ds, axis=0, keepdims=True)

        @pl.when(i > j)
        def _():
            step(False)

        @pl.when(i == j)
        def _():
            step(True)

    q_spec = pl.BlockSpec((tq, wb), lambda h, j, i: (jnp.maximum(i, j), h))
    k_spec = pl.BlockSpec((tk, wb), lambda h, j, i: (j, h))
    whole = pl.BlockSpec((s, wb), lambda h, j, i: (0, h))
    row_spec = pl.BlockSpec((hp, 1, tk), lambda h, j, i: (h, 0, j))
    return _call(
        body, name=name, grid=(heads // hp, nk, nq),
        in_specs=[
            q_spec,
            pl.BlockSpec((tk, wb), lambda h, j, i: (j, heads // hp + h)),
            pl.BlockSpec((tk, wb), lambda h, j, i: (j, 2 * heads // hp + h)),
            q_spec, q_spec, q_spec, q_spec, row_spec,
        ],
        out_specs=[whole, k_spec, k_spec, whole, row_spec],
        out_shape=[jax.ShapeDtypeStruct((s, w), F32), jax.ShapeDtypeStruct((s, w), F32),
                   jax.ShapeDtypeStruct((s, w), F32), jax.ShapeDtypeStruct((s, w), F32),
                   jax.ShapeDtypeStruct((heads, 1, s), F32)],
        scratch_shapes=[], semantics=("parallel", "arbitrary", "arbitrary"),
        operands=[zf, zf, zf, do, lse, delta, f_rep, f_row], comm=comm)


def _gla_rows(s):
    return _pick(s, (256, 128, 64))


def _gla_chunk_terms(la_c, tri):
    a_cum = jnp.dot(tri, la_c, precision=HIGHEST, preferred_element_type=F32)
    a_tot = jnp.sum(la_c, axis=0, keepdims=True)
    return jnp.exp(a_tot - a_cum), jnp.exp(a_tot)


def _gla_fwd(zr, la, *, heads, q_blk, k_blk, name):
    s = zr.shape[0]
    c = GLA_CHUNK
    rows = _gla_rows(s)
    nsteps, ncs = s // rows, rows // c
    scale = HEAD_DIM ** -0.5

    assert q_blk % heads == 0 and k_blk % heads == 0

    def body(q_ref, k_ref, v_ref, la_ref, o_ref, st_ref, state):
        @pl.when(pl.program_id(0) == 0)
        def _():
            state[...] = jnp.zeros_like(state)

        tri = (lax.broadcasted_iota(jnp.int32, (c, c), 0) >= lax.broadcasted_iota(jnp.int32, (c, c), 1)).astype(F32)
        for t in range(ncs):
            sl = slice(t * c, (t + 1) * c)
            for h in range(heads):
                kc, vc = _head_cols(h), slice(h * GLA_VAL_DIM, (h + 1) * GLA_VAL_DIM)
                dec, e_tot = _gla_chunk_terms(la_ref[sl, kc], tri)
                kd = k_ref[sl, kc] * dec
                st_ref[h, t] = state[h]
                new = state[h] * e_tot + _dot(v_ref[sl, vc], kd, TN)
                state[h] = new
                o_ref[sl, vc] = _dot(q_ref[sl, kc] * scale, new, NT)

    kw, vw = heads * HEAD_DIM, heads * GLA_VAL_DIM
    return pl.pallas_call(
        body, name=name, grid=(nsteps,),
        in_specs=[
            pl.BlockSpec((rows, kw), lambda i: (i, q_blk // heads)),
            pl.BlockSpec((rows, kw), lambda i: (i, k_blk // heads)),
            pl.BlockSpec((rows, vw), lambda i: (i, 0)),
            pl.BlockSpec((rows, kw), lambda i: (i, 0)),
        ],
        out_specs=[
            pl.BlockSpec((rows, vw), lambda i: (i, 0)),
            pl.BlockSpec((heads, ncs, GLA_VAL_DIM, HEAD_DIM), lambda i: (0, i, 0, 0)),
        ],
        out_shape=[jax.ShapeDtypeStruct((s, vw), F32),
                   jax.ShapeDtypeStruct((heads, s // c, GLA_VAL_DIM, HEAD_DIM), F32)],
        scratch_shapes=[pltpu.VMEM((heads, GLA_VAL_DIM, HEAD_DIM), F32)],
        compiler_params=_cparams(("arbitrary",)),
    )(zr, zr, zr, la)


def _gla_bwd(zr, la, do, states, *, heads, q_blk, k_blk, name):
    s = zr.shape[0]
    c = GLA_CHUNK
    rows = _gla_rows(s)
    nsteps, ncs = s // rows, rows // c
    scale = HEAD_DIM ** -0.5

    assert q_blk % heads == 0 and k_blk % heads == 0

    def body(q_ref, k_ref, v_ref, la_ref, do_ref, st_ref, dq_ref, dk_ref, dv_ref, dla_ref, dstate):
        @pl.when(pl.program_id(0) == 0)
        def _():
            dstate[...] = jnp.zeros_like(dstate)

        i0 = lax.broadcasted_iota(jnp.int32, (c, c), 0)
        i1 = lax.broadcasted_iota(jnp.int32, (c, c), 1)
        tri = (i0 >= i1).astype(F32)
        strict = (i0 > i1).astype(F32)
        for t in reversed(range(ncs)):
            sl = slice(t * c, (t + 1) * c)
            for h in range(heads):
                kc, vc = _head_cols(h), slice(h * GLA_VAL_DIM, (h + 1) * GLA_VAL_DIM)
                dec, e_tot = _gla_chunk_terms(la_ref[sl, kc], tri)
                kd = k_ref[sl, kc] * dec
                kdb = kd.astype(BF16)
                vb = v_ref[sl, vc].astype(BF16)
                dob = do_ref[sl, vc].astype(BF16)
                prev = st_ref[h, t]
                cur = prev * e_tot + _dot(vb, kdb, TN)
                d_cur = dstate[h] + _dot(dob, q_ref[sl, kc] * scale, TN)
                d_cur_b = d_cur.astype(BF16)
                dq_ref[sl, kc] = _dot(dob, cur, NN) * scale
                dv_ref[sl, vc] = _dot(kdb, d_cur_b, NT)
                dkd = _dot(vb, d_cur_b, NN)
                d_tot = e_tot * jnp.sum(d_cur * prev, axis=0, keepdims=True)
                dk_ref[sl, kc] = dkd * dec
                dla_ref[sl, kc] = d_tot + jnp.dot(strict, dkd * kd, precision=HIGHEST, preferred_element_type=F32)
                dstate[h] = d_cur * e_tot

    def rev(i):
        return nsteps - 1 - i

    kw, vw = heads * HEAD_DIM, heads * GLA_VAL_DIM
    kq_spec = pl.BlockSpec((rows, kw), lambda i: (rev(i), 0))
    v_spec = pl.BlockSpec((rows, vw), lambda i: (rev(i), 0))
    return pl.pallas_call(
        body, name=name, grid=(nsteps,),
        in_specs=[
            pl.BlockSpec((rows, kw), lambda i: (rev(i), q_blk // heads)),
            pl.BlockSpec((rows, kw), lambda i: (rev(i), k_blk // heads)),
            v_spec, kq_spec, v_spec,
            pl.BlockSpec((heads, ncs, GLA_VAL_DIM, HEAD_DIM), lambda i: (0, rev(i), 0, 0)),
        ],
        out_specs=[kq_spec, kq_spec, v_spec, kq_spec],
        out_shape=[jax.ShapeDtypeStruct((s, kw), F32), jax.ShapeDtypeStruct((s, kw), F32),
                   jax.ShapeDtypeStruct((s, vw), F32), jax.ShapeDtypeStruct((s, kw), F32)],
        scratch_shapes=[pltpu.VMEM((heads, GLA_VAL_DIM, HEAD_DIM), F32)],
        compiler_params=_cparams(("arbitrary",)),
    )(zr, zr, zr, la, do, states)


def _my_place():
    x, y, c = lax.axis_index("x"), lax.axis_index("y"), lax.axis_index("c")
    return x, y, c


def _gather_ops(x_ref, out_ref, send_sems, recv_sems, local_sem):
    def plan():
        x, y, c = _my_place()
        me, sibling = (x, y, c), (x, y, 1 - c)
        chips = [(1 - x, y), (x, 1 - y), (1 - x, 1 - y)]

        def blk(px, py, pc):
            return out_ref.at[4 * px + 2 * py + pc]

        def copy(k, block, to, src=None):
            return pltpu.make_async_remote_copy(
                src_ref=blk(*block) if src is None else src, dst_ref=blk(*block),
                send_sem=send_sems.at[k], recv_sem=recv_sems.at[k], device_id=to, device_id_type=MESH)

        mine = pltpu.make_async_copy(x_ref, blk(*me), local_sem)
        first = [copy(0, me, sibling, src=x_ref)]
        first += [copy(1 + j, me, (*chip, c), src=x_ref) for j, chip in enumerate(chips)]
        passed = [copy(4 + j, (*chip, c), sibling) for j, chip in enumerate(chips)]
        landed = [copy(1 + j, (*chip, c), me) for j, chip in enumerate(chips)]
        from_sibling = [copy(0, sibling, me)] + [copy(4 + j, (*chip, 1 - c), me) for j, chip in enumerate(chips)]
        return mine, first, passed, landed, from_sibling

    def start():
        mine, first, _, _, _ = plan()
        mine.start()
        for cp in first:
            cp.start()

    def finish():
        mine, first, passed, landed, from_sibling = plan()
        for cp, fwd in zip(landed, passed):
            cp.wait_recv()
            fwd.start()
        for cp in from_sibling:
            cp.wait_recv()
        for cp in first + passed:
            cp.wait_send()
        mine.wait()

    return start, finish


def _exchange_ops(scatter, s_ref, r_ref, send_sems, recv_sems, local_sem):
    def plan():
        x, y, c = _my_place()
        me = 4 * x + 2 * y + c
        mine = pltpu.make_async_copy(s_ref.at[me] if scatter else s_ref, r_ref.at[me], local_sem)
        sends, recvs = [], []
        for k in range(1, N_DEV):
            px, py, pc = x ^ ((k >> 2) & 1), y ^ ((k >> 1) & 1), c ^ (k & 1)
            peer = 4 * px + 2 * py + pc
            src = s_ref.at[peer] if scatter else s_ref
            for dst, out in ((r_ref.at[me], sends), (r_ref.at[peer], recvs)):
                out.append(pltpu.make_async_remote_copy(
                    src_ref=src, dst_ref=dst, send_sem=send_sems.at[k - 1], recv_sem=recv_sems.at[k - 1],
                    device_id=(px, py, pc), device_id_type=MESH))
        return mine, sends, recvs

    def start():
        mine, sends, _ = plan()
        mine.start()
        for cp in sends:
            cp.start()

    def finish():
        mine, sends, recvs = plan()
        for cp in recvs:
            cp.wait_recv()
        for cp in sends:
            cp.wait_send()
        mine.wait()

    return start, finish


def _pair_ops(s_ref, r_ref, send_sems, recv_sems):
    def plan():
        x, y, c = _my_place()
        return [pltpu.make_async_remote_copy(
            src_ref=s_ref.at[2 * q + 1 - c], dst_ref=r_ref.at[q], send_sem=send_sems.at[q], recv_sem=recv_sems.at[q],
            device_id=(x, y, 1 - c), device_id_type=MESH) for q in range(N_DEV // 2)]

    def start():
        for cp in plan():
            cp.start()

    def finish():
        copies = plan()
        for cp in copies:
            cp.wait_recv()
        for cp in copies:
            cp.wait_send()

    return start, finish


def _chips_ops(p_ref, r_ref, send_sems, recv_sems, local_sem):
    def plan():
        x, y, c = _my_place()
        chip = 2 * x + y
        mine = pltpu.make_async_copy(p_ref.at[chip], r_ref.at[chip], local_sem)
        sends, recvs = [], []
        for k in range(1, N_DEV // 2):
            px, py = x ^ (k >> 1), y ^ (k & 1)
            peer = 2 * px + py
            for dst, out in ((r_ref.at[chip], sends), (r_ref.at[peer], recvs)):
                out.append(pltpu.make_async_remote_copy(
                    src_ref=p_ref.at[peer], dst_ref=dst, send_sem=send_sems.at[k - 1], recv_sem=recv_sems.at[k - 1],
                    device_id=(px, py, c), device_id_type=MESH))
        return mine, sends, recvs

    def start():
        mine, sends, _ = plan()
        mine.start()
        for cp in sends:
            cp.start()

    def finish():
        mine, sends, recvs = plan()
        for cp in recvs:
            cp.wait_recv()
        for cp in sends:
            cp.wait_send()
        mine.wait()

    return start, finish


def _comm_ops(kind, src_ref, dst_ref, send_sems, recv_sems, local_sem):
    if kind == "gather":
        return _gather_ops(src_ref, dst_ref, send_sems, recv_sems, local_sem)
    if kind == "pair":
        return _pair_ops(src_ref, dst_ref, send_sems, recv_sems)
    if kind == "chips":
        return _chips_ops(src_ref, dst_ref, send_sems, recv_sems, local_sem)
    return _exchange_ops(False, src_ref, dst_ref, send_sems, recv_sems, local_sem)


def _comm_out_shape(kind, v):
    shape = {"pair": (N_DEV // 2,) + v.shape[1:], "chips": v.shape}.get(kind, (N_DEV,) + v.shape)
    return jax.ShapeDtypeStruct(shape, v.dtype)


def _comm_scratch(n_tasks):
    return [pltpu.SemaphoreType.DMA((N_DEV - 1,)), pltpu.SemaphoreType.DMA((N_DEV - 1,)), pltpu.SemaphoreType.DMA] * n_tasks


def _pair_sum(s, r, *, name):
    _, rows, cdim = s.shape
    tr = _pick(rows, (1024, 704, 512, 256, 128, 64, 32, 16, 8))
    rc = min(16, tr)

    def body(c_ref, s_ref, r_ref, o_ref):
        def chunk(ci, carry):
            sl = pl.ds(pl.multiple_of(ci * rc, rc), rc)
            o_ref[sl, :] = (s_ref[sl, :].astype(F32) + r_ref[sl, :].astype(F32)).astype(o_ref.dtype)
            return carry

        lax.fori_loop(0, tr // rc, chunk, 0)

    spec = pl.BlockSpec((None, tr, cdim), lambda q, i, c_ref: (q, i, 0))
    return pl.pallas_call(
        body, name=name,
        grid_spec=pltpu.PrefetchScalarGridSpec(
            num_scalar_prefetch=1, grid=(N_DEV // 2, rows // tr),
            in_specs=[pl.BlockSpec((None, None, tr, cdim), lambda q, i, c_ref: (q, c_ref[0], i, 0)), spec],
            out_specs=spec),
        out_shape=jax.ShapeDtypeStruct(r.shape, s.dtype),
        compiler_params=_cparams(("parallel", "parallel")),
    )(lax.axis_index("c").astype(jnp.int32).reshape(1), s.reshape(N_DEV // 2, 2, rows, cdim), r)


def _comm(kind, v, *, name):
    def body(s_ref, r_ref, send_sems, recv_sems, local_sem):
        start, finish = _comm_ops(kind, s_ref, r_ref, send_sems, recv_sems, local_sem)
        start()
        finish()

    return pl.pallas_call(
        body, name=name,
        out_shape=_comm_out_shape(kind, v),
        in_specs=[pl.BlockSpec(memory_space=pl.ANY)],
        out_specs=pl.BlockSpec(memory_space=pl.ANY),
        scratch_shapes=_comm_scratch(1),
    )(v)


class _Overlap:
    US_PER_MB = {"gather": 52.0, "pair": 1.0, "chips": 13.0, "bcast": 97.0}
    MM_FLOPS_PER_US = 6.0e8

    def __init__(self):
        self.queue, self.results, self.then = [], {}, {}

    def add(self, key, kind, v):
        self.queue.append((key, kind, v))

    def _cost(self, kind, v):
        return v.size * v.dtype.itemsize / 2 ** 20 * self.US_PER_MB[kind]

    def take(self, budget_us):
        taken, cum = [], 0.0
        while self.queue:
            cost = self._cost(*self.queue[0][1:])
            if taken and cum + cost > 1.25 * budget_us:
                break
            taken.append(self.queue.pop(0))
            cum += cost
        return taken

    def put(self, taken, res):
        for (key, _, _), r in zip(taken, res):
            self.results[key] = r
            if key in self.then:
                self.then.pop(key)(r)

    def carry(self, budget_us, fn):
        taken = self.take(budget_us)
        out, res = fn([(kind, v) for _, kind, v in taken])
        self.put(taken, res)
        return out

    def mm(self, a, b, *, mode, **kw):
        la, lb = _logical(a), _logical(b)
        budget = 2.0 * la[0] * la[1] * (lb[0] if mode == "nt" else lb[1]) / self.MM_FLOPS_PER_US

        def fn(comm):
            return _mm(a, b, mode=mode, comm=comm, **kw) if comm else (_mm(a, b, mode=mode, **kw), [])

        return self.carry(budget, fn)

    def get(self, key):
        while key not in self.results:
            keys = [k for k, _, _ in self.queue]
            task = self.queue.pop(keys.index(key if key in keys else "pair_" + key))
            self.put([task], [_comm(task[1], task[2], name=f"alone_{task[0]}")])
        return self.results[key]

    def weight(self, n):
        g = self.get(n)
        return g.reshape(-1, g.shape[2]) if n in ROW_SHARDED else g

    def grad(self, n, g):
        g = g if g.ndim == 3 else g.reshape(N_DEV, g.shape[0] // N_DEV, g.shape[1])
        self.add("pair_d_" + n, "pair", g)
        self.then["pair_d_" + n] = lambda r: self.add("d_" + n, "chips", _pair_sum(g, r, name=f"pair_sum_{n}"))


def _sel_tables(dest, ws, wp, tw):
    n_tiles = (int(dest.max()) + tw) // tw
    tbl = np.full((N_DEV, wp), -1, np.int32)
    for j in range(N_DEV):
        tbl[j, :ws] = dest[j * ws:(j + 1) * ws]
    by_tile = [sorted({j for j in range(N_DEV) if ((tbl[j] // tw) == t).any()}) for t in range(n_tiles)]
    by_shard = [sorted({int(t) for t in np.unique(tbl[j, :ws] // tw)}) for j in range(N_DEV)]

    def table(lists):
        width = max(len(v) for v in lists)
        idx = np.array([(v + [v[-1]] * width)[:width] if v else [0] * width for v in lists], np.int32)
        val = np.array([[1] * len(v) + [0] * (width - len(v)) for v in lists], np.int32)
        return idx.reshape(-1), val.reshape(-1), width

    return tbl[:, :, None], table(by_tile), table(by_shard)


def _sel_matrix(d_ref, t, wp, tw):
    cols = t * tw + lax.broadcasted_iota(jnp.int32, (wp, tw), 1)
    return (d_ref[...] == cols).astype(BF16)


def _win_unshard(g, tbl, idx, val, width, *, tw, padded, name):
    _, dm, wp = g.shape
    tm = _pick(dm, (1024, 512, 256, 128))

    def body(idx_ref, val_ref, g_ref, d_ref, o_ref, acc):
        t, s_ = pl.program_id(1), pl.program_id(2)

        @pl.when(s_ == 0)
        def _():
            acc[...] = jnp.zeros_like(acc)

        @pl.when(val_ref[t * width + s_] == 1)
        def _():
            acc[...] += _dot(g_ref[...], _sel_matrix(d_ref, t, wp, tw), NN)

        @pl.when(s_ == width - 1)
        def _():
            o_ref[...] = acc[...].astype(BF16)

    return pl.pallas_call(
        body, name=name,
        grid_spec=pltpu.PrefetchScalarGridSpec(
            num_scalar_prefetch=2, grid=(dm // tm, padded // tw, width),
            in_specs=[pl.BlockSpec((None, tm, wp), lambda i, t, s_, ix, vl: (ix[t * width + s_], i, 0)),
                      pl.BlockSpec((None, wp, 1), lambda i, t, s_, ix, vl: (ix[t * width + s_], 0, 0))],
            out_specs=pl.BlockSpec((tm, tw), lambda i, t, s_, ix, vl: (i, t)),
            scratch_shapes=[pltpu.VMEM((tm, tw), F32)]),
        out_shape=jax.ShapeDtypeStruct((dm, padded), BF16),
        compiler_params=_cparams(("parallel", "parallel", "arbitrary")),
    )(idx, val, g, tbl)


def _win_to_shards(dw, tbl, idx, val, width, *, tw, wp, name):
    dm = dw.shape[0]
    tm = _pick(dm, (1024, 512, 256, 128))

    def body(idx_ref, val_ref, w_ref, d_ref, o_ref, acc):
        j, s_ = pl.program_id(1), pl.program_id(2)

        @pl.when(s_ == 0)
        def _():
            acc[...] = jnp.zeros_like(acc)

        @pl.when(val_ref[j * width + s_] == 1)
        def _():
            acc[...] += _dot(w_ref[...], _sel_matrix(d_ref, idx_ref[j * width + s_], wp, tw), NT)

        @pl.when(s_ == width - 1)
        def _():
            o_ref[...] = acc[...].astype(BF16)

    return pl.pallas_call(
        body, name=name,
        grid_spec=pltpu.PrefetchScalarGridSpec(
            num_scalar_prefetch=2, grid=(dm // tm, N_DEV, width),
            in_specs=[pl.BlockSpec((tm, tw), lambda i, j, s_, ix, vl: (i, ix[j * width + s_])),
                      pl.BlockSpec((None, wp, 1), lambda i, j, s_, ix, vl: (j, 0, 0))],
            out_specs=pl.BlockSpec((None, tm, wp), lambda i, j, s_, ix, vl: (j, i, 0)),
            scratch_shapes=[pltpu.VMEM((tm, wp), F32)]),
        out_shape=jax.ShapeDtypeStruct((N_DEV, dm, wp), BF16),
        compiler_params=_cparams(("parallel", "parallel", "arbitrary")),
    )(idx, val, dw, tbl)


def _adamw(parts, w, m, v, *, name):
    r, cdim = w.shape
    n_parts = parts.shape[0]
    tr = _pick(r, (256, 128, 64, 32, 16, 8))
    rc = min(16, tr)
    c1 = 1.0 - ADAM_B1 ** ADAM_STEP
    c2 = 1.0 - ADAM_B2 ** ADAM_STEP

    def body(p_ref, w_ref, m_ref, v_ref, g_ref, d_ref, mo_ref, vo_ref):
        def chunk(ci, carry):
            sl = pl.ds(pl.multiple_of(ci * rc, rc), rc)
            g = p_ref[0, sl, :].astype(F32)
            for i in range(1, n_parts):
                g = g + p_ref[i, sl, :].astype(F32)
            mn = ADAM_B1 * m_ref[sl, :] + (1.0 - ADAM_B1) * g
            vn = ADAM_B2 * v_ref[sl, :] + (1.0 - ADAM_B2) * jnp.square(g)
            m_hat = mn / c1
            v_hat = vn / c2
            g_ref[sl, :] = g
            d_ref[sl, :] = -ADAM_LR * (m_hat / (jnp.sqrt(v_hat) + ADAM_EPS) + ADAM_WD * w_ref[sl, :])
            mo_ref[sl, :] = mn
            vo_ref[sl, :] = vn
            return carry

        lax.fori_loop(0, tr // rc, chunk, 0)

    spec = pl.BlockSpec((tr, cdim), lambda i: (i, 0))
    return pl.pallas_call(
        body, name=name, grid=(r // tr,),
        in_specs=[pl.BlockSpec((n_parts, tr, cdim), lambda i: (0, i, 0)), spec, spec, spec],
        out_specs=[spec] * 4,
        out_shape=[jax.ShapeDtypeStruct((r, cdim), F32)] * 4,
        compiler_params=_cparams(("parallel",)),
    )(parts, w, m, v)


def _pad_to(v, n):
    return v if v.shape[0] == n else jnp.concatenate([v, jnp.zeros((n - v.shape[0],), v.dtype)])


def _pad_cols(v, n):
    return v if v.shape[-1] == n else jnp.concatenate([v, jnp.zeros(v.shape[:-1] + (n - v.shape[-1],), v.dtype)], axis=-1)


def _pack(vs, cols, row_mult, dtype):
    offs, o = [], 0
    for v in vs:
        offs.append(o)
        o += v.size
    rows = -(-o // cols)
    rows = -(-rows // row_mult) * row_mult
    flat = jnp.concatenate([v.reshape(-1).astype(dtype) for v in vs])
    return _pad_to(flat, rows * cols).reshape(rows, cols), offs


SHARDED = ("ffn1_w_gate", "ffn1_w_up", "ffn1_w_down", "w_in", "w_merge_gate", "gla_gate_up", "w_branch_fox",
           "w_branch_gla", "w_out", "ffn2_w_gate", "ffn2_w_up", "ffn2_w_down", "w_ple_gate", "w_ple_proj")
ROW_SHARDED = ("ffn1_w_down", "w_out", "ffn2_w_down", "w_ple_gate")
REPLICATED = ("ffn1_norm", "mix_norm", "fox_forget_bias", "gla_gate_bias", "gla_head_norm", "b_merge_gate",
              "ffn2_norm", "ple_norm", "final_norm")
WEIGHTS = ("ffn1_norm", "ffn1_w_gate", "ffn1_w_up", "ffn1_w_down", "mix_norm", "w_in", "fox_forget_bias",
           "gla_gate_up", "gla_gate_bias", "gla_head_norm", "w_branch_fox", "w_branch_gla", "w_merge_gate",
           "b_merge_gate", "w_out", "ffn2_norm", "ffn2_w_gate", "ffn2_w_up", "ffn2_w_down", "ple_norm",
           "w_ple_proj", "w_ple_gate", "final_norm")


def kernel(x, p, ffn1_norm, ffn1_w_gate, ffn1_w_up, ffn1_w_down, mix_norm, w_in, fox_forget_bias, gla_gate_up, gla_gate_bias, gla_head_norm, w_branch_fox, w_branch_gla, w_merge_gate, b_merge_gate, w_out, ffn2_norm, ffn2_w_gate, ffn2_w_up, ffn2_w_down, ple_norm, w_ple_proj, w_ple_gate, final_norm, loss_target, m_ffn1_norm, m_ffn1_w_gate, m_ffn1_w_up, m_ffn1_w_down, m_mix_norm, m_w_in, m_fox_forget_bias, m_gla_gate_up, m_gla_gate_bias, m_gla_head_norm, m_w_branch_fox, m_w_branch_gla, m_w_merge_gate, m_b_merge_gate, m_w_out, m_ffn2_norm, m_ffn2_w_gate, m_ffn2_w_up, m_ffn2_w_down, m_ple_norm, m_w_ple_proj, m_w_ple_gate, m_final_norm, v_ffn1_norm, v_ffn1_w_gate, v_ffn1_w_up, v_ffn1_w_down, v_mix_norm, v_w_in, v_fox_forget_bias, v_gla_gate_up, v_gla_gate_bias, v_gla_head_norm, v_w_branch_fox, v_w_branch_gla, v_w_merge_gate, v_b_merge_gate, v_w_out, v_ffn2_norm, v_ffn2_w_gate, v_ffn2_w_up, v_ffn2_w_down, v_ple_norm, v_w_ple_proj, v_w_ple_gate, v_final_norm):
    args = dict(locals())
    wts = {n: args[n] for n in WEIGHTS}
    mom_m = {n: args["m_" + n] for n in WEIGHTS}
    mom_v = {n: args["v_" + n] for n in WEIGHTS}

    xs, ps, tgt = x[0], p[0, 0], loss_target[0]
    s, d = xs.shape
    fox_w = w_branch_fox.shape[1]
    gla_vw = w_branch_gla.shape[1]
    fox_heads = fox_w // HEAD_DIM
    gla_heads = gla_vw // GLA_VAL_DIM
    gla_kw = gla_heads * HEAD_DIM
    rank = gla_gate_up.shape[1]

    c_fl = 3 * fox_w
    o_gr, o_gq, o_gk = gla_vw, 2 * gla_vw, 2 * gla_vw + gla_kw
    o_fl = o_gk + gla_kw
    o_gd = o_fl + LANES
    rest_w = o_gd + LANES
    padded = c_fl + rest_w
    seg = [(c_fl, 0), (fox_heads, c_fl + o_fl), (gla_kw, c_fl + o_gq), (gla_kw, c_fl + o_gk), (gla_vw, c_fl),
           (gla_vw, c_fl + o_gr), (rank, c_fl + o_gd)]
    dest = np.concatenate([np.arange(w_, dtype=np.int32) + o_ for w_, o_ in seg])
    ws = w_in.shape[2]
    wp = -(-ws // LANES) * LANES
    tw = 256 if padded % 256 == 0 else LANES
    tbl, (t_idx, t_val, t_width), (s_idx, s_val, s_width) = _sel_tables(dest, ws, wp, tw)
    tbl = jnp.asarray(tbl)

    ov = _Overlap()
    for n in SHARDED:
        sh = wts[n][0].astype(BF16)
        ov.add(n, "gather", _pad_cols(sh, wp) if n == "w_in" else sh)
    fbias = _pad_cols(fox_forget_bias, LANES)
    bmg_f, bmg_g = b_merge_gate[:, :d], b_merge_gate[:, d:]
    ghn = jnp.tile(gla_head_norm, (1, gla_heads))

    h1, ffn1_saved = _ffn_fwd(xs, ffn1_norm, ov, tag="ffn1")
    u = _rms_fwd(h1, mix_norm, name="mix_norm")
    win = _win_unshard(ov.weight("w_in"), tbl, jnp.asarray(t_idx), jnp.asarray(t_val), t_width, tw=tw, padded=padded,
                       name="in_proj_unshard")
    win_fox, win_rest = win[:, :c_fl], win[:, c_fl:]
    zf = ov.mm(u, win_fox, mode="nn", name="in_proj_fox", out_dtype=BF16)
    zr = ov.mm(u, win_rest, mode="nn", name="in_proj_rest")
    gz = ov.mm(u, ov.weight("w_merge_gate"), mode="nn", name="merge_gate")
    gup = ov.weight("gla_gate_up").transpose(1, 0, 2).reshape(rank, gla_kw)
    gup = jnp.concatenate([gup, jnp.zeros((LANES - rank, gla_kw), BF16)], axis=0)

    log_f = _rowwise(lambda fl, b: _log_sigmoid(fl + b), [(zr, LANES, o_fl // LANES)], [fbias], [(LANES, F32)],
                     name="forget_gate")[0]
    f_cum = _cumsum(log_f[:, :fox_heads].T, reverse=False, name="forget_cumsum")
    f_rep = jnp.broadcast_to(f_cum.T[:, :, None], (s, fox_heads, HEAD_DIM)).reshape(s, fox_w)
    f_row = f_cum[:, None, :]
    attn_us = 2.0 * s * s * HEAD_DIM * fox_heads / _Overlap.MM_FLOPS_PER_US
    y_fox, y_fox_bf, lse = ov.carry(
        ATTN_FWD_MATMULS * attn_us, lambda comm: _fox_fwd(zf, f_rep, f_row, heads=fox_heads, name="fox_fwd", comm=comm))

    def decay_fn(gd, gupv, gb):
        return _log_sigmoid(_dot(gd, gupv, NN) + gb) * (1.0 / GLA_GATE_TAU)

    la = _rowwise(decay_fn, [(zr, LANES, o_gd // LANES)], [gup, gla_gate_bias], [(gla_kw, F32)], name="gla_decay", rc=128)[0]
    q_blk, k_blk = o_gq // HEAD_DIM, o_gk // HEAD_DIM
    o_gla, states = _gla_fwd(zr, la, heads=gla_heads, q_blk=q_blk, k_blk=k_blk, name="gla_fwd")

    def gla_out_fn(o, gr, g):
        outs = []
        for hh in range(gla_heads):
            sl = slice(hh * GLA_VAL_DIM, (hh + 1) * GLA_VAL_DIM)
            _, oh = _rms_parts(o[:, sl])
            outs.append(oh * g[:, sl] * _silu_parts(gr[:, sl])[0])
        return jnp.concatenate(outs, axis=1)

    y_gla = _rowwise(gla_out_fn, [o_gla, (zr, gla_vw, o_gr // gla_vw)], [ghn], [(gla_vw, BF16)], name="gla_out")[0]
    br_f = ov.mm(y_fox_bf, ov.weight("w_branch_fox"), mode="nn", name="branch_fox")
    br_g = ov.mm(y_gla, ov.weight("w_branch_gla"), mode="nn", name="branch_gla")

    def merge_fn(zf_, zg_, bf_, bg_, b1, b2):
        return _sigmoid(zf_ + b1) * bf_ + _sigmoid(zg_ + b2) * bg_

    merged = _rowwise(merge_fn, [(gz, d, 0), (gz, d, 1), br_f, br_g], [bmg_f, bmg_g], [(d, BF16)], name="merge")[0]
    h2 = ov.mm(merged, ov.weight("w_out"), mode="nn", name="out_proj", add=h1)
    h3, ffn2_saved = _ffn_fwd(h2, ffn2_norm, ov, tag="ffn2")
    n3 = _rms_fwd(h3, ple_norm, name="ple_norm")
    gl = ov.mm(n3, ov.weight("w_ple_gate"), mode="nn", name="ple_gate")
    pe = ov.mm(ps, ov.weight("w_ple_proj"), mode="nn", name="ple_proj")

    def head_fn(h3b, glb, peb, tb, gfin):
        pg = _sigmoid(glb)
        h4 = h3b + pg * peb
        r, xh = _rms_parts(h4)
        err = xh * gfin - tb
        dy = err * (1.0 / d)
        t = dy * gfin
        dh4 = r * (t - xh * jnp.mean(t * xh, axis=-1, keepdims=True))
        return dh4, dh4 * pg, dh4 * peb * pg * (1.0 - pg), _colsum(err * err), _colsum(dy * xh)

    dh4, dpe, dgl, loss_cols, d_final = _rowwise(
        head_fn, [h3, gl, pe, tgt], [final_norm.reshape(1, d)], [(d, F32), (d, BF16), (d, BF16)], [d, d], name="loss_head")
    loss = lax.psum(0.5 * jnp.sum(loss_cols) / d, AXES)

    grads = {"final_norm": d_final.reshape(d)}
    ov.grad("w_ple_proj", ov.mm(ps, dpe, mode="tn", name="ple_proj_dw", out_dtype=BF16, out_chunked=True))
    ov.grad("w_ple_gate", ov.mm(n3, dgl, mode="tn", name="ple_gate_dw", out_dtype=BF16))
    dn3 = ov.mm(dgl, ov.weight("w_ple_gate"), mode="nt", name="ple_gate_dx")
    dh3, dh3_bf, grads["ple_norm"] = _rms_bwd(h3, dn3, ple_norm, dh4, name="ple_norm_bwd")
    dh2, dh2_bf, grads["ffn2_norm"] = _ffn_bwd(h2, ffn2_norm, ov, ffn2_saved, dh3, dh3_bf, tag="ffn2")

    ov.grad("w_out", ov.mm(merged, dh2_bf, mode="tn", name="out_proj_dw", out_dtype=BF16))
    dmerged = ov.mm(dh2_bf, ov.weight("w_out"), mode="nt", name="out_proj_dx")

    def merge_bwd_fn(zf_, zg_, bf_, bg_, dm, b1, b2):
        sf, sg = _sigmoid(zf_ + b1), _sigmoid(zg_ + b2)
        dz = jnp.concatenate([dm * bf_ * sf * (1.0 - sf), dm * bg_ * sg * (1.0 - sg)], axis=1)
        return dm * sf, dm * sg, dz, _colsum(dz)

    dbr_f, dbr_g, dgz, grads["b_merge_gate"] = _rowwise(
        merge_bwd_fn, [(gz, d, 0), (gz, d, 1), br_f, br_g, dmerged], [bmg_f, bmg_g],
        [(d, BF16), (d, BF16), (2 * d, BF16)], [2 * d], name="merge_bwd")
    ov.grad("w_merge_gate", ov.mm(u, dgz, mode="tn", name="merge_gate_dw", out_dtype=BF16, out_chunked=True))
    ov.grad("w_branch_fox", ov.mm(y_fox_bf, dbr_f, mode="tn", name="branch_fox_dw", out_dtype=BF16, out_chunked=True))
    ov.grad("w_branch_gla", ov.mm(y_gla, dbr_g, mode="tn", name="branch_gla_dw", out_dtype=BF16, out_chunked=True))
    dy_fox = ov.mm(dbr_f, ov.weight("w_branch_fox"), mode="nt", name="branch_fox_dx")
    dy_gla = ov.mm(dbr_g, ov.weight("w_branch_gla"), mode="nt", name="branch_gla_dx")

    def gla_out_bwd_fn(o, gr, dy, g):
        dos, dgrs, dgs = [], [], []
        for hh in range(gla_heads):
            sl = slice(hh * GLA_VAL_DIM, (hh + 1) * GLA_VAL_DIM)
            r, oh = _rms_parts(o[:, sl])
            si, dsi = _silu_parts(gr[:, sl])
            don = dy[:, sl] * si
            dgrs.append(dy[:, sl] * oh * g[:, sl] * dsi)
            t = don * g[:, sl]
            dos.append(r * (t - oh * jnp.mean(t * oh, axis=-1, keepdims=True)))
            dgs.append(_colsum(don * oh))
        return jnp.concatenate(dos, axis=1), jnp.concatenate(dgrs, axis=1), jnp.concatenate(dgs, axis=1)

    do_gla, dgr, d_ghn = _rowwise(gla_out_bwd_fn, [o_gla, (zr, gla_vw, o_gr // gla_vw), dy_gla], [ghn],
                                  [(gla_vw, F32), (gla_vw, F32)], [gla_vw], name="gla_out_bwd")
    grads["gla_head_norm"] = d_ghn.reshape(gla_heads, GLA_VAL_DIM).sum(axis=0, keepdims=True)
    dgq, dgk, dgv, dla = _gla_bwd(zr, la, do_gla, states, heads=gla_heads, q_blk=q_blk, k_blk=k_blk, name="gla_bwd")

    def decay_bwd_fn(dl, gd, gupv, gb):
        pre = _dot(gd, gupv, NN) + gb
        dpre = dl * (1.0 / GLA_GATE_TAU) * _sigmoid(-pre)
        return _dot(dpre, gupv, NT), dpre, _colsum(dpre)

    dgd, dpre_bf, grads["gla_gate_bias"] = _rowwise(
        decay_bwd_fn, [dla, (zr, LANES, o_gd // LANES)], [gup, gla_gate_bias], [(LANES, F32), (gla_kw, BF16)], [gla_kw],
        name="gla_decay_bwd", rc=128)
    d_gup = ov.mm(zr[:, o_gd:o_gd + LANES], dpre_bf, mode="tn", name="gla_gate_up_dw")[:rank]
    ov.grad("gla_gate_up", d_gup.reshape(rank, N_DEV, gla_kw // N_DEV).transpose(1, 0, 2).astype(BF16))

    def delta_fn(dyv, yv):
        outs = []
        for hh in range(fox_heads):
            sl = slice(hh * HEAD_DIM, (hh + 1) * HEAD_DIM)
            outs.append(jnp.broadcast_to(jnp.sum(dyv[:, sl] * yv[:, sl], axis=-1, keepdims=True), (dyv.shape[0], HEAD_DIM)))
        return jnp.concatenate(outs, axis=1)

    delta = _rowwise(delta_fn, [dy_fox, y_fox], [], [(fox_w, F32)], name="fox_delta")[0]
    dfq, dfk, dfv, d_fcol, d_frow = ov.carry(
        ATTN_BWD_MATMULS * attn_us,
        lambda comm: _fox_bwd(zf, dy_fox, lse, delta, f_rep, f_row, heads=fox_heads, name="fox_bwd", comm=comm))
    d_fcum = d_fcol[:, ::HEAD_DIM].T + d_frow.reshape(fox_heads, s)
    d_logf = _cumsum(d_fcum, reverse=True, name="forget_cumsum_bwd")
    d_logf = _pad_cols(d_logf.T, LANES)

    def forget_bwd_fn(dl, fl, b):
        dfl_ = dl * _sigmoid(-(fl + b))
        return dfl_, _colsum(dfl_)

    dfl, d_fbias = _rowwise(forget_bwd_fn, [d_logf, (zr, LANES, o_fl // LANES)], [fbias], [(LANES, F32)], [LANES],
                            name="forget_gate_bwd")
    grads["fox_forget_bias"] = d_fbias[:, :fox_heads]

    dz = jnp.concatenate([dfq, dfk, dfv, dgv, dgr, dgq, dgk, dfl, dgd], axis=1).astype(BF16)
    dwin = ov.mm(u, dz, mode="tn", name="in_proj_dw", out_dtype=BF16)
    ov.grad("w_in", _win_to_shards(dwin, tbl, jnp.asarray(s_idx), jnp.asarray(s_val), s_width, tw=tw, wp=wp,
                                   name="in_proj_dw_shards"))
    du = ov.mm(dgz, ov.weight("w_merge_gate"), mode="nt", name="merge_gate_dx")
    du = ov.mm(dz, win, mode="nt", name="in_proj_dx", add=du)
    dh1, dh1_bf, grads["mix_norm"] = _rms_bwd(h1, du, mix_norm, dh2, name="mix_norm_bwd")
    dx, _, grads["ffn1_norm"] = _ffn_bwd(xs, ffn1_norm, ov, ffn1_saved, dh1, dh1_bf, tag="ffn1")

    outs = {}
    for n in SHARDED:
        parts = ov.get("d_" + n)
        state = [_pad_cols(t_[n][0], parts.shape[2]) for t_ in (wts, mom_m, mom_v)]
        res4 = _adamw(parts, *state, name=f"adamw_{n}")
        for kind, r_ in zip(("grad", "delta", "new_m", "new_v"), res4):
            outs[f"{kind}_{n}"] = r_[:, :wts[n].shape[2]][None]

    send_small, small_offs = _pack([grads[n] for n in REPLICATED], LANES, 8, F32)
    recv_small = _comm("bcast", send_small, name="exchange_replicated")
    w_sm, _ = _pack([wts[n] for n in REPLICATED], LANES, 8, F32)
    m_sm, _ = _pack([mom_m[n] for n in REPLICATED], LANES, 8, F32)
    v_sm, _ = _pack([mom_v[n] for n in REPLICATED], LANES, 8, F32)
    small = _adamw(recv_small, w_sm, m_sm, v_sm, name="adamw_replicated")
    for kind, buf in zip(("grad", "delta", "new_m", "new_v"), small):
        fs = buf.reshape(-1)
        for n, o in zip(REPLICATED, small_offs):
            outs[f"{kind}_{n}"] = fs[o:o + wts[n].size].reshape(wts[n].shape)

    res = [loss, dx[None]]
    for kind in ("grad", "delta", "new_m", "new_v"):
        res += [outs[f"{kind}_{n}"] for n in WEIGHTS]
    return tuple(res)
```

```python
import functools

import jax
import jax.numpy as jnp
import numpy as np
from jax import lax
from jax.experimental import pallas as pl
from jax.experimental.pallas import tpu as pltpu

F32 = jnp.float32
BF16 = jnp.bfloat16
MESH = pl.DeviceIdType.MESH
AXES = ("x", "y", "c")
N_DEV = 8

VMEM_LIMIT_BYTES = 56 * 1024 * 1024
MM_BLOCK_BUDGET_BYTES = 40 * 1024 * 1024
LANES = 128

EPS = 1e-6
HEAD_DIM = 128
GLA_VAL_DIM = 256
GLA_CHUNK = 64
GLA_GATE_TAU = 16.0
ADAM_LR, ADAM_B1, ADAM_B2, ADAM_EPS, ADAM_WD, ADAM_STEP = 0.001, 0.9, 0.999, 1e-08, 0.01, 10

ATTN_FWD_MATMULS = 7.0
ATTN_BWD_MATMULS = 7.0

HIGHEST = lax.Precision.HIGHEST
NN = (((1,), (0,)), ((), ()))
NT = (((1,), (1,)), ((), ()))
TN = (((0,), (0,)), ((), ()))


def _cparams(sem):
    return pltpu.CompilerParams(dimension_semantics=sem, vmem_limit_bytes=VMEM_LIMIT_BYTES)


def _pick(dim, cands):
    for c in cands:
        if dim % c == 0:
            return c
    return dim


def _bf(v):
    return v if v.dtype == BF16 else v.astype(BF16)


def _dot(a, b, dims):
    return lax.dot_general(_bf(a), _bf(b), dims, preferred_element_type=F32)


def _sigmoid(v):
    return 1.0 / (1.0 + jnp.exp(-v))


def _log_sigmoid(v):
    return jnp.minimum(v, 0.0) - jnp.log(1.0 + jnp.exp(-jnp.abs(v)))


def _logical(v):
    return (v.shape[1], v.shape[0] * v.shape[2]) if v.ndim == 3 else v.shape


def _mm(a, b, *, mode, name, out_dtype=F32, add=None, scale=1.0, out_chunked=False, comm=()):
    la, lb = _logical(a), _logical(b)
    if mode == "nn":
        (m, k), (k2, n) = la, lb
        a_minor, b_minor = "k", "n"
    elif mode == "nt":
        (m, k), (n, k2) = la, lb
        a_minor, b_minor = "k", "k"
    else:
        (k, m), (k2, n) = la, lb
        a_minor, b_minor = "m", "n"
    assert k == k2, (name, a.shape, b.shape)
    forced = {}
    for v, minor in ((a, a_minor), (b, b_minor)):
        if v.ndim == 3:
            assert forced.get(minor, v.shape[2]) == v.shape[2], name
            forced[minor] = v.shape[2]
    if out_chunked:
        assert forced.get("n", n // N_DEV) == n // N_DEV, name
        forced["n"] = n // N_DEV
    tm = forced.get("m") or _pick(m, (1024, 512, 256, 128))
    tn = forced.get("n") or _pick(n, (1408, 1280, 1024, 512, 256, 128))

    def blocks_bytes(tm_, tn_, t):
        io = tm_ * t * a.dtype.itemsize + t * tn_ * b.dtype.itemsize
        return 2 * (io + tm_ * tn_ * (jnp.dtype(out_dtype).itemsize + (4 if add is not None else 0))) + tm_ * tn_ * 4

    kc = forced.get("k")
    a_k_minor, b_k_minor = mode != "tn", mode == "nt"
    if kc:
        aligned = all(v.ndim == 3 or kc % (LANES if minor else 16) == 0 for v, minor in ((a, a_k_minor), (b, b_k_minor)))
        fits = [(tm_, tn_) for tm_, tn_ in ((tm, tn), (512, tn), (512, 512))
                if m % tm_ == 0 and n % tn_ == 0 and forced.get("m", tm_) == tm_ and forced.get("n", tn_) == tn_
                and blocks_bytes(tm_, tn_, k) <= MM_BLOCK_BUDGET_BYTES]
        if aligned and fits:
            (tm, tn), tk = fits[0], k
        else:
            tk, kc = kc, None
    else:
        tk = k if blocks_bytes(tm, tn, k) <= MM_BLOCK_BUDGET_BYTES else _pick(k, (512, 640, 256, 128))
    nk = k // tk
    dims = {"nn": NN, "nt": NT, "tn": TN}[mode]
    gi, gj, gk = (lambda i, j, kk: i), (lambda i, j, kk: j), (lambda i, j, kk: kk)

    def spec(v, t_major, t_minor, g_major, g_minor, all_chunks=False):
        if v.ndim == 3:
            if all_chunks:
                return pl.BlockSpec((N_DEV, t_major, v.shape[2]), lambda i, j, kk: (0, g_major(i, j, kk), 0))
            return pl.BlockSpec((None, t_major, v.shape[2]), lambda i, j, kk: (g_minor(i, j, kk), g_major(i, j, kk), 0))
        return pl.BlockSpec((t_major, t_minor), lambda i, j, kk: (g_major(i, j, kk), g_minor(i, j, kk)))

    a_spec = spec(a, tk, tm, gk, gi) if mode == "tn" else spec(a, tm, tk, gi, gk, bool(kc))
    b_spec = spec(b, tn, tk, gj, gk, bool(kc)) if mode == "nt" else spec(b, tk, tn, gk, gj)

    def k_chunk(ref, minor, c_):
        if len(ref.shape) == 3:
            return ref[c_]
        return ref[:, c_ * kc:(c_ + 1) * kc] if minor else ref[c_ * kc:(c_ + 1) * kc, :]
    if out_chunked:
        o_spec = pl.BlockSpec((None, tm, tn), lambda i, j, kk: (j, i, 0))
        out_shape = jax.ShapeDtypeStruct((N_DEV, m, tn), out_dtype)
    else:
        o_spec = pl.BlockSpec((tm, tn), lambda i, j, kk: (i, j))
        out_shape = jax.ShapeDtypeStruct((m, n), out_dtype)
    has_add = add is not None
    assert not (has_add and out_chunked), name

    def body(*refs):
        a_ref, b_ref = refs[0], refs[1]
        add_ref = refs[2] if has_add else None
        o_ref = refs[3 if has_add else 2]
        kk = pl.program_id(2)

        def finish(r):
            if scale != 1.0:
                r = r * scale
            if has_add:
                r = r + add_ref[...]
            o_ref[...] = r.astype(o_ref.dtype)

        if kc:
            r = _dot(k_chunk(a_ref, a_k_minor, 0), k_chunk(b_ref, b_k_minor, 0), dims)
            for c_ in range(1, k // kc):
                r = r + _dot(k_chunk(a_ref, a_k_minor, c_), k_chunk(b_ref, b_k_minor, c_), dims)
            finish(r)
        elif nk == 1:
            finish(_dot(a_ref[...], b_ref[...], dims))
        else:
            acc_ref = refs[-1]

            @pl.when(kk == 0)
            def _():
                acc_ref[...] = jnp.zeros_like(acc_ref)

            acc_ref[...] += _dot(a_ref[...], b_ref[...], dims)

            @pl.when(kk == nk - 1)
            def _():
                finish(acc_ref[...])

    res, carried = _call(
        body, name=name, grid=(m // tm, n // tn, nk),
        in_specs=[a_spec, b_spec] + ([o_spec] if has_add else []), out_specs=[o_spec], out_shape=[out_shape],
        scratch_shapes=[pltpu.VMEM((tm, tn), F32)] if nk > 1 else [],
        semantics=("parallel", "parallel", "arbitrary"), operands=[a, b] + ([add] if has_add else []), comm=comm)
    return (res[0], carried) if comm else res[0]


def _call(body, *, name, grid, in_specs, out_specs, out_shape, scratch_shapes, semantics, operands, comm=()):
    n_in, n_out, n_scr, n = len(in_specs), len(out_specs), len(scratch_shapes), len(comm)
    if not comm:
        res = pl.pallas_call(body, name=name, grid=grid, in_specs=in_specs, out_specs=out_specs, out_shape=out_shape,
                             scratch_shapes=scratch_shapes, compiler_params=_cparams(semantics))(*operands)
        return res, []

    def carrying(*refs):
        ins, c_in = refs[:n_in], refs[n_in:n_in + n]
        outs, c_out = refs[n_in + n:n_in + n + n_out], refs[n_in + n + n_out:n_in + 2 * n + n_out]
        scratch, sems = refs[n_in + 2 * n + n_out:][:n_scr], refs[n_in + 2 * n + n_out + n_scr:]
        tasks = [_comm_ops(kind, c_in[t], c_out[t], *sems[3 * t:3 * t + 3]) for t, (kind, _) in enumerate(comm)]
        ids = [pl.program_id(ax) for ax in range(len(grid))]

        @pl.when(functools.reduce(lambda p, q: p & q, [i == 0 for i in ids]))
        def _():
            for start, _ in tasks:
                start()

        body(*ins, *outs, *scratch)

        @pl.when(functools.reduce(lambda p, q: p & q, [i == g - 1 for i, g in zip(ids, grid)]))
        def _():
            for _, finish in tasks:
                finish()

    any_spec = pl.BlockSpec(memory_space=pl.ANY)
    res = pl.pallas_call(
        carrying, name=name, grid=grid,
        in_specs=list(in_specs) + [any_spec] * n, out_specs=list(out_specs) + [any_spec] * n,
        out_shape=list(out_shape) + [_comm_out_shape(kind, v) for kind, v in comm],
        scratch_shapes=list(scratch_shapes) + _comm_scratch(n),
        compiler_params=_cparams(("arbitrary",) * len(grid)),
    )(*operands, *[v for _, v in comm])
    return res[:n_out], res[n_out:]


def _rowwise(fn, rows, consts, outs, accs=(), *, name, rc=None):
    rows = [r if isinstance(r, tuple) else (r, r.shape[1], 0) for r in rows]
    m = rows[0][0].shape[0]
    widths = [w for _, w, _ in rows] + [n for n, _ in outs]
    row_bytes = sum(w * r.dtype.itemsize for r, w, _ in rows) + sum(n * jnp.dtype(d).itemsize for n, d in outs)
    tm = 1024
    while tm > 16 and (m % tm or 2 * tm * row_bytes > 24 * 1024 * 1024):
        tm //= 2
    if m % tm:
        tm = m
    if rc is None:
        rc = 16
        while rc * 2 <= tm and rc * 2 * max(widths) <= 32768:
            rc *= 2
    rc = min(rc, tm)
    nr, nc, no = len(rows), len(consts), len(outs)

    def body(*refs):
        in_refs, c_refs = refs[:nr], refs[nr:nr + nc]
        o_refs, a_refs = refs[nr + nc:nr + nc + no], refs[nr + nc + no:]

        @pl.when(pl.program_id(0) == 0)
        def _():
            for r in a_refs:
                r[...] = jnp.zeros_like(r)

        cvals = [c[...] for c in c_refs]

        def chunk(ci, carry):
            sl = pl.ds(pl.multiple_of(ci * rc, rc), rc)
            res = fn(*[r[sl, :] for r in in_refs], *cvals)
            if not isinstance(res, (tuple, list)):
                res = (res,)
            for r, v in zip(o_refs, res[:no]):
                r[sl, :] = v.astype(r.dtype)
            for r, v in zip(a_refs, res[no:]):
                r[...] += v
            return carry

        lax.fori_loop(0, tm // rc, chunk, 0)

    in_specs = [pl.BlockSpec((tm, w), functools.partial(lambda i, cb: (i, cb), cb=cb)) for _, w, cb in rows]
    in_specs += [pl.BlockSpec(c.shape, lambda i: (0, 0)) for c in consts]
    out_specs = [pl.BlockSpec((tm, n), lambda i: (i, 0)) for n, _ in outs]
    out_specs += [pl.BlockSpec((1, n), lambda i: (0, 0)) for n in accs]
    out_shape = [jax.ShapeDtypeStruct((m, n), d) for n, d in outs] + [jax.ShapeDtypeStruct((1, n), F32) for n in accs]
    res = pl.pallas_call(
        body, name=name, grid=(m // tm,),
        in_specs=in_specs, out_specs=out_specs, out_shape=out_shape,
        compiler_params=_cparams(("arbitrary",)),
    )(*[r for r, _, _ in rows], *consts)
    return res


def _colsum(v):
    return jnp.sum(v, axis=0, keepdims=True)


def _rms_parts(xv):
    r = lax.rsqrt(jnp.mean(xv * xv, axis=-1, keepdims=True) + EPS)
    return r, xv * r


def _rms_fwd(xv, g, *, name):
    d = xv.shape[1]

    def fn(xb, gb):
        _, xh = _rms_parts(xb)
        return xh * gb

    return _rowwise(fn, [xv], [g], [(d, BF16)], name=name)[0]


def _rms_bwd(xv, dy, g, add, *, name):
    d = xv.shape[1]

    def fn(xb, dyb, addb, gb):
        r, xh = _rms_parts(xb)
        t = dyb * gb
        dx = r * (t - xh * jnp.mean(t * xh, axis=-1, keepdims=True)) + addb
        return dx, dx, _colsum(dyb * xh)

    return _rowwise(fn, [xv, dy, add], [g], [(d, F32), (d, BF16)], [d], name=name)


def _silu_parts(a):
    sg = _sigmoid(a)
    return a * sg, sg * (1.0 + a * (1.0 - sg))


def _rows(v):
    return v.reshape(v.shape[0] * v.shape[1], v.shape[2])


def _ffn_fwd(h, g, ov, *, tag):
    s = h.shape[0]
    n = _rms_fwd(h, g, name=f"{tag}_norm")
    a = ov.mm(n, ov.weight(f"{tag}_w_gate"), mode="nn", name=f"{tag}_gate", out_dtype=BF16, out_chunked=True)
    b = ov.mm(n, ov.weight(f"{tag}_w_up"), mode="nn", name=f"{tag}_up", out_dtype=BF16, out_chunked=True)
    c = a.shape[2]
    hm = _rowwise(lambda av, bv: _silu_parts(av.astype(F32))[0] * bv.astype(F32), [_rows(a), _rows(b)], [], [(c, BF16)],
                  name=f"{tag}_act")[0]
    hm = hm.reshape(N_DEV, s, c)
    out = ov.mm(hm, ov.weight(f"{tag}_w_down"), mode="nn", name=f"{tag}_down", add=h, scale=0.5)
    return out, (n, a, b, hm)


def _ffn_bwd(h, g, ov, saved, dout, dout_bf, *, tag):
    n, a, b, hm = saved
    wg, wu, wd = ov.weight(f"{tag}_w_gate"), ov.weight(f"{tag}_w_up"), ov.weight(f"{tag}_w_down")
    s, c = h.shape[0], wg.shape[2]
    d_wd = ov.mm(hm, dout_bf, mode="tn", name=f"{tag}_down_dw", scale=0.5, out_dtype=BF16)
    ov.grad(f"{tag}_w_down", d_wd)
    dhm = ov.mm(dout_bf, wd, mode="nt", name=f"{tag}_down_dx", scale=0.5, out_dtype=BF16, out_chunked=True)

    def act_bwd(av, bv, dv):
        av, bv, dv = av.astype(F32), bv.astype(F32), dv.astype(F32)
        si, dsi = _silu_parts(av)
        return dv * bv * dsi, dv * si

    da, db = _rowwise(act_bwd, [_rows(a), _rows(b), _rows(dhm)], [], [(c, BF16), (c, BF16)], name=f"{tag}_act_bwd")
    da, db = da.reshape(N_DEV, s, c), db.reshape(N_DEV, s, c)
    ov.grad(f"{tag}_w_gate", ov.mm(n, da, mode="tn", name=f"{tag}_gate_dw", out_dtype=BF16, out_chunked=True))
    ov.grad(f"{tag}_w_up", ov.mm(n, db, mode="tn", name=f"{tag}_up_dw", out_dtype=BF16, out_chunked=True))
    dn = ov.mm(da, wg, mode="nt", name=f"{tag}_gate_dx")
    dn = ov.mm(db, wu, mode="nt", name=f"{tag}_up_dx", add=dn)
    dh, dh_bf, dg = _rms_bwd(h, dn, g, dout, name=f"{tag}_norm_bwd")
    return dh, dh_bf, dg


def _cumsum(xv, *, reverse, name):
    h, s = xv.shape
    t = _pick(s, (512, 256, 128))
    nb = s // t

    def blk(j):
        return (0, nb - 1 - j) if reverse else (0, j)

    def body(x_ref, o_ref, carry):
        @pl.when(pl.program_id(0) == 0)
        def _():
            carry[...] = jnp.zeros_like(carry)

        i0 = lax.broadcasted_iota(jnp.int32, (t, t), 0)
        i1 = lax.broadcasted_iota(jnp.int32, (t, t), 1)
        tri = ((i0 >= i1) if reverse else (i0 <= i1)).astype(F32)
        xb = x_ref[...]
        o_ref[...] = jnp.dot(xb, tri, precision=HIGHEST, preferred_element_type=F32) + carry[...]
        carry[...] += jnp.sum(xb, axis=1, keepdims=True)

    return pl.pallas_call(
        body, name=name, grid=(nb,),
        in_specs=[pl.BlockSpec((h, t), blk)], out_specs=pl.BlockSpec((h, t), blk),
        out_shape=jax.ShapeDtypeStruct((h, s), F32),
        scratch_shapes=[pltpu.VMEM((h, 1), F32)],
        compiler_params=_cparams(("arbitrary",)),
    )(xv)


def _fox_tiles(s):
    t = _pick(s, (512, 256, 128))
    return t, t


def _causal(sc):
    t = sc.shape[0]
    keep = lax.broadcasted_iota(jnp.int32, (t, t), 1) <= lax.broadcasted_iota(jnp.int32, (t, t), 0)
    return jnp.where(keep, sc, -jnp.inf)


ATTN_HEADS_PER_STEP = 2


def _head_cols(hh):
    return slice(hh * HEAD_DIM, (hh + 1) * HEAD_DIM)


def _across(rowstat, width):
    return jnp.tile(rowstat, (1, width // HEAD_DIM))


def _fox_fwd(zf, f_rep, f_row, *, heads, name, comm=()):
    s = zf.shape[0]
    tq, tk = _fox_tiles(s)
    assert tq == tk
    nq, nk = s // tq, s // tk
    hp = ATTN_HEADS_PER_STEP if heads % ATTN_HEADS_PER_STEP == 0 else 1
    wb = hp * HEAD_DIM
    scale = HEAD_DIM ** -0.5
    w = heads * HEAD_DIM

    def body(q_ref, k_ref, v_ref, fq_ref, fk_ref, o32_ref, o16_ref, lse_ref, m_sc, l_sc, acc_sc):
        i, j = pl.program_id(1), pl.program_id(2)

        @pl.when(j == 0)
        def _():
            m_sc[...] = jnp.full_like(m_sc, -jnp.inf)
            l_sc[...] = jnp.zeros_like(l_sc)
            acc_sc[...] = jnp.zeros_like(acc_sc)

        def step(diagonal):
            for hh in range(hp):
                cols = _head_cols(hh)
                sc = _dot(q_ref[:, cols], k_ref[:, cols], NT) * scale + _across(fq_ref[:, cols], tk) - fk_ref[hh]
                if diagonal:
                    sc = _causal(sc)
                m_old = m_sc[hh]
                m_new = jnp.maximum(m_old, jnp.max(sc, axis=-1, keepdims=True))
                alpha = jnp.exp(m_old - m_new)
                pr = jnp.exp(sc - _across(m_new, tk))
                l_sc[hh] = alpha * l_sc[hh] + jnp.sum(pr, axis=-1, keepdims=True)
                acc_sc[hh] = alpha * acc_sc[hh] + _dot(pr, v_ref[:, cols], NN)
                m_sc[hh] = m_new

        @pl.when(j < i)
        def _():
            step(False)

        @pl.when(j == i)
        def _():
            step(True)

        @pl.when(j == nk - 1)
        def _():
            for hh in range(hp):
                cols = _head_cols(hh)
                o = acc_sc[hh] / l_sc[hh]
                o32_ref[:, cols] = o
                o16_ref[:, cols] = o.astype(BF16)
                lse_ref[:, cols] = m_sc[hh] + jnp.log(l_sc[hh])

    def kv_blk(off):
        return lambda h, i, j: (jnp.minimum(j, i), off + h)

    o_spec = pl.BlockSpec((tq, wb), lambda h, i, j: (i, h))
    stat = pltpu.VMEM((hp, tq, HEAD_DIM), F32)
    return _call(
        body, name=name, grid=(heads // hp, nq, nk),
        in_specs=[
            o_spec,
            pl.BlockSpec((tk, wb), kv_blk(heads // hp)),
            pl.BlockSpec((tk, wb), kv_blk(2 * heads // hp)),
            o_spec,
            pl.BlockSpec((hp, 1, tk), lambda h, i, j: (h, 0, jnp.minimum(j, i))),
        ],
        out_specs=[o_spec, o_spec, o_spec],
        out_shape=[jax.ShapeDtypeStruct((s, w), F32), jax.ShapeDtypeStruct((s, w), BF16),
                   jax.ShapeDtypeStruct((s, w), F32)],
        scratch_shapes=[stat, stat, stat],
        semantics=("parallel", "parallel", "arbitrary"), operands=[zf, zf, zf, f_rep, f_row], comm=comm)


def _fox_bwd(zf, do, lse, delta, f_rep, f_row, *, heads, name, comm=()):
    s = zf.shape[0]
    tq, tk = _fox_tiles(s)
    assert tq == tk
    nq, nk = s // tq, s // tk
    hp = ATTN_HEADS_PER_STEP if heads % ATTN_HEADS_PER_STEP == 0 else 1
    wb = hp * HEAD_DIM
    scale = HEAD_DIM ** -0.5
    w = heads * HEAD_DIM

    def body(q_ref, k_ref, v_ref, do_ref, lse_ref, dl_ref, fq_ref, fk_ref, dq_ref, dk_ref, dv_ref, dfq_ref, dfk_ref):
        j, i = pl.program_id(1), pl.program_id(2)

        @pl.when((j == 0) & (i == 0))
        def _():
            dq_ref[...] = jnp.zeros_like(dq_ref)
            dfq_ref[...] = jnp.zeros_like(dfq_ref)

        @pl.when(i == 0)
        def _():
            dk_ref[...] = jnp.zeros_like(dk_ref)
            dv_ref[...] = jnp.zeros_like(dv_ref)
            dfk_ref[...] = jnp.zeros_like(dfk_ref)

        def step(diagonal):
            rows = pl.ds(pl.multiple_of(i * tq, tq), tq)
            for hh in range(hp):
                cols = _head_cols(hh)
                q, k, v = q_ref[:, cols], k_ref[:, cols], v_ref[:, cols]
                dob = do_ref[:, cols].astype(BF16)
                sc = _dot(q, k, NT) * scale + _across(fq_ref[:, cols], tk) - fk_ref[hh]
                if diagonal:
                    sc = _causal(sc)
                pr = jnp.exp(sc - _across(lse_ref[:, cols], tk))
                dv_ref[:, cols] += _dot(pr, dob, TN)
                dp = _dot(dob, v, NT)
                ds = pr * (dp - _across(dl_ref[:, cols], tk))
                dsb = ds.astype(BF16)
                dk_ref[:, cols] += _dot(dsb, q, TN) * scale
                dq_ref[rows, cols] += _dot(dsb, k, NN) * scale
                dfq_ref[rows, cols] += jnp.broadcast_to(jnp.sum(ds, axis=1, keepdims=True), (tq, HEAD_DIM))
                dfk_ref[hh] -= jnp.sum(ds, axis=0, keepdims=True)

        @pl.when(i > j)
        def _():
            step(False)

        @pl.when(i == j)
        def _():
            step(True)

    q_spec = pl.BlockSpec((tq, wb), lambda h, j, i: (jnp.maximum(i, j), h))
    k_spec = pl.BlockSpec((tk, wb), lambda h, j, i: (j, h))
    whole = pl.BlockSpec((s, wb), lambda h, j, i: (0, h))
    row_spec = pl.BlockSpec((hp, 1, tk), lambda h, j, i: (h, 0, j))
    return _call(
        body, name=name, grid=(heads // hp, nk, nq),
        in_specs=[
            q_spec,
            pl.BlockSpec((tk, wb), lambda h, j, i: (j, heads // hp + h)),
            pl.BlockSpec((tk, wb), lambda h, j, i: (j, 2 * heads // hp + h)),
            q_spec, q_spec, q_spec, q_spec, row_spec,
        ],
        out_specs=[whole, k_spec, k_spec, whole, row_spec],
        out_shape=[jax.ShapeDtypeStruct((s, w), F32), jax.ShapeDtypeStruct((s, w), F32),
                   jax.ShapeDtypeStruct((s, w), F32), jax.ShapeDtypeStruct((s, w), F32),
                   jax.ShapeDtypeStruct((heads, 1, s), F32)],
        scratch_shapes=[], semantics=("parallel", "arbitrary", "arbitrary"),
        operands=[zf, zf, zf, do, lse, delta, f_rep, f_row], comm=comm)


def _gla_rows(s):
    return _pick(s, (256, 128, 64))


def _gla_chunk_terms(la_c, tri):
    a_cum = jnp.dot(tri, la_c, precision=HIGHEST, preferred_element_type=F32)
    a_tot = jnp.sum(la_c, axis=0, keepdims=True)
    return jnp.exp(a_tot - a_cum), jnp.exp(a_tot)


def _gla_fwd(zr, la, *, heads, q_blk, k_blk, name):
    s = zr.shape[0]
    c = GLA_CHUNK
    rows = _gla_rows(s)
    nsteps, ncs = s // rows, rows // c
    scale = HEAD_DIM ** -0.5

    assert q_blk % heads == 0 and k_blk % heads == 0

    def body(q_ref, k_ref, v_ref, la_ref, o_ref, st_ref, state):
        @pl.when(pl.program_id(0) == 0)
        def _():
            state[...] = jnp.zeros_like(state)

        tri = (lax.broadcasted_iota(jnp.int32, (c, c), 0) >= lax.broadcasted_iota(jnp.int32, (c, c), 1)).astype(F32)
        for t in range(ncs):
            sl = slice(t * c, (t + 1) * c)
            for h in range(heads):
                kc, vc = _head_cols(h), slice(h * GLA_VAL_DIM, (h + 1) * GLA_VAL_DIM)
                dec, e_tot = _gla_chunk_terms(la_ref[sl, kc], tri)
                kd = k_ref[sl, kc] * dec
                st_ref[h, t] = state[h]
                new = state[h] * e_tot + _dot(v_ref[sl, vc], kd, TN)
                state[h] = new
                o_ref[sl, vc] = _dot(q_ref[sl, kc] * scale, new, NT)

    kw, vw = heads * HEAD_DIM, heads * GLA_VAL_DIM
    return pl.pallas_call(
        body, name=name, grid=(nsteps,),
        in_specs=[
            pl.BlockSpec((rows, kw), lambda i: (i, q_blk // heads)),
            pl.BlockSpec((rows, kw), lambda i: (i, k_blk // heads)),
            pl.BlockSpec((rows, vw), lambda i: (i, 0)),
            pl.BlockSpec((rows, kw), lambda i: (i, 0)),
        ],
        out_specs=[
            pl.BlockSpec((rows, vw), lambda i: (i, 0)),
            pl.BlockSpec((heads, ncs, GLA_VAL_DIM, HEAD_DIM), lambda i: (0, i, 0, 0)),
        ],
        out_shape=[jax.ShapeDtypeStruct((s, vw), F32),
                   jax.ShapeDtypeStruct((heads, s // c, GLA_VAL_DIM, HEAD_DIM), F32)],
        scratch_shapes=[pltpu.VMEM((heads, GLA_VAL_DIM, HEAD_DIM), F32)],
        compiler_params=_cparams(("arbitrary",)),
    )(zr, zr, zr, la)


def _gla_bwd(zr, la, do, states, *, heads, q_blk, k_blk, name):
    s = zr.shape[0]
    c = GLA_CHUNK
    rows = _gla_rows(s)
    nsteps, ncs = s // rows, rows // c
    scale = HEAD_DIM ** -0.5

    assert q_blk % heads == 0 and k_blk % heads == 0

    def body(q_ref, k_ref, v_ref, la_ref, do_ref, st_ref, dq_ref, dk_ref, dv_ref, dla_ref, dstate):
        @pl.when(pl.program_id(0) == 0)
        def _():
            dstate[...] = jnp.zeros_like(dstate)

        i0 = lax.broadcasted_iota(jnp.int32, (c, c), 0)
        i1 = lax.broadcasted_iota(jnp.int32, (c, c), 1)
        tri = (i0 >= i1).astype(F32)
        strict = (i0 > i1).astype(F32)
        for t in reversed(range(ncs)):
            sl = slice(t * c, (t + 1) * c)
            for h in range(heads):
                kc, vc = _head_cols(h), slice(h * GLA_VAL_DIM, (h + 1) * GLA_VAL_DIM)
                dec, e_tot = _gla_chunk_terms(la_ref[sl, kc], tri)
                kd = k_ref[sl, kc] * dec
                kdb = kd.astype(BF16)
                vb = v_ref[sl, vc].astype(BF16)
                dob = do_ref[sl, vc].astype(BF16)
                prev = st_ref[h, t]
                cur = prev * e_tot + _dot(vb, kdb, TN)
                d_cur = dstate[h] + _dot(dob, q_ref[sl, kc] * scale, TN)
                d_cur_b = d_cur.astype(BF16)
                dq_ref[sl, kc] = _dot(dob, cur, NN) * scale
                dv_ref[sl, vc] = _dot(kdb, d_cur_b, NT)
                dkd = _dot(vb, d_cur_b, NN)
                d_tot = e_tot * jnp.sum(d_cur * prev, axis=0, keepdims=True)
                dk_ref[sl, kc] = dkd * dec
                dla_ref[sl, kc] = d_tot + jnp.dot(strict, dkd * kd, precision=HIGHEST, preferred_element_type=F32)
                dstate[h] = d_cur * e_tot

    def rev(i):
        return nsteps - 1 - i

    kw, vw = heads * HEAD_DIM, heads * GLA_VAL_DIM
    kq_spec = pl.BlockSpec((rows, kw), lambda i: (rev(i), 0))
    v_spec = pl.BlockSpec((rows, vw), lambda i: (rev(i), 0))
    return pl.pallas_call(
        body, name=name, grid=(nsteps,),
        in_specs=[
            pl.BlockSpec((rows, kw), lambda i: (rev(i), q_blk // heads)),
            pl.BlockSpec((rows, kw), lambda i: (rev(i), k_blk // heads)),
            v_spec, kq_spec, v_spec,
            pl.BlockSpec((heads, ncs, GLA_VAL_DIM, HEAD_DIM), lambda i: (0, rev(i), 0, 0)),
        ],
        out_specs=[kq_spec, kq_spec, v_spec, kq_spec],
        out_shape=[jax.ShapeDtypeStruct((s, kw), F32), jax.ShapeDtypeStruct((s, kw), F32),
                   jax.ShapeDtypeStruct((s, vw), F32), jax.ShapeDtypeStruct((s, kw), F32)],
        scratch_shapes=[pltpu.VMEM((heads, GLA_VAL_DIM, HEAD_DIM), F32)],
        compiler_params=_cparams(("arbitrary",)),
    )(zr, zr, zr, la, do, states)


def _my_place():
    x, y, c = lax.axis_index("x"), lax.axis_index("y"), lax.axis_index("c")
    return x, y, c


def _gather_ops(x_ref, out_ref, send_sems, recv_sems, local_sem):
    def plan():
        x, y, c = _my_place()
        me, sibling = (x, y, c), (x, y, 1 - c)
        chips = [(1 - x, y), (x, 1 - y), (1 - x, 1 - y)]

        def blk(px, py, pc):
            return out_ref.at[4 * px + 2 * py + pc]

        def copy(k, block, to, src=None):
            return pltpu.make_async_remote_copy(
                src_ref=blk(*block) if src is None else src, dst_ref=blk(*block),
                send_sem=send_sems.at[k], recv_sem=recv_sems.at[k], device_id=to, device_id_type=MESH)

        mine = pltpu.make_async_copy(x_ref, blk(*me), local_sem)
        first = [copy(0, me, sibling, src=x_ref)]
        first += [copy(1 + j, me, (*chip, c), src=x_ref) for j, chip in enumerate(chips)]
        passed = [copy(4 + j, (*chip, c), sibling) for j, chip in enumerate(chips)]
        landed = [copy(1 + j, (*chip, c), me) for j, chip in enumerate(chips)]
        from_sibling = [copy(0, sibling, me)] + [copy(4 + j, (*chip, 1 - c), me) for j, chip in enumerate(chips)]
        return mine, first, passed, landed, from_sibling

    def start():
        mine, first, _, _, _ = plan()
        mine.start()
        for cp in first:
            cp.start()

    def finish():
        mine, first, passed, landed, from_sibling = plan()
        for cp, fwd in zip(landed, passed):
            cp.wait_recv()
            fwd.start()
        for cp in from_sibling:
            cp.wait_recv()
        for cp in first + passed:
            cp.wait_send()
        mine.wait()

    return start, finish


def _exchange_ops(scatter, s_ref, r_ref, send_sems, recv_sems, local_sem):
    def plan():
        x, y, c = _my_place()
        me = 4 * x + 2 * y + c
        mine = pltpu.make_async_copy(s_ref.at[me] if scatter else s_ref, r_ref.at[me], local_sem)
        sends, recvs = [], []
        for k in range(1, N_DEV):
            px, py, pc = x ^ ((k >> 2) & 1), y ^ ((k >> 1) & 1), c ^ (k & 1)
            peer = 4 * px + 2 * py + pc
            src = s_ref.at[peer] if scatter else s_ref
            for dst, out in ((r_ref.at[me], sends), (r_ref.at[peer], recvs)):
                out.append(pltpu.make_async_remote_copy(
                    src_ref=src, dst_ref=dst, send_sem=send_sems.at[k - 1], recv_sem=recv_sems.at[k - 1],
                    device_id=(px, py, pc), device_id_type=MESH))
        return mine, sends, recvs

    def start():
        mine, sends, _ = plan()
        mine.start()
        for cp in sends:
            cp.start()

    def finish():
        mine, sends, recvs = plan()
        for cp in recvs:
            cp.wait_recv()
        for cp in sends:
            cp.wait_send()
        mine.wait()

    return start, finish


def _pair_ops(s_ref, r_ref, send_sems, recv_sems):
    def plan():
        x, y, c = _my_place()
        return [pltpu.make_async_remote_copy(
            src_ref=s_ref.at[2 * q + 1 - c], dst_ref=r_ref.at[q], send_sem=send_sems.at[q], recv_sem=recv_sems.at[q],
            device_id=(x, y, 1 - c), device_id_type=MESH) for q in range(N_DEV // 2)]

    def start():
        for cp in plan():
            cp.start()

    def finish():
        copies = plan()
        for cp in copies:
            cp.wait_recv()
        for cp in copies:
            cp.wait_send()

    return start, finish


def _chips_ops(p_ref, r_ref, send_sems, recv_sems, local_sem):
    def plan():
        x, y, c = _my_place()
        chip = 2 * x + y
        mine = pltpu.make_async_copy(p_ref.at[chip], r_ref.at[chip], local_sem)
        sends, recvs = [], []
        for k in range(1, N_DEV // 2):
            px, py = x ^ (k >> 1), y ^ (k & 1)
            peer = 2 * px + py
            for dst, out in ((r_ref.at[chip], sends), (r_ref.at[peer], recvs)):
                out.append(pltpu.make_async_remote_copy(
                    src_ref=p_ref.at[peer], dst_ref=dst, send_sem=send_sems.at[k - 1], recv_sem=recv_sems.at[k - 1],
                    device_id=(px, py, c), device_id_type=MESH))
        return mine, sends, recvs

    def start():
        mine, sends, _ = plan()
        mine.start()
        for cp in sends:
            cp.start()

    def finish():
        mine, sends, recvs = plan()
        for cp in recvs:
            cp.wait_recv()
        for cp in sends:
            cp.wait_send()
        mine.wait()

    return start, finish


def _comm_ops(kind, src_ref, dst_ref, send_sems, recv_sems, local_sem):
    if kind == "gather":
        return _gather_ops(src_ref, dst_ref, send_sems, recv_sems, local_sem)
    if kind == "pair":
        return _pair_ops(src_ref, dst_ref, send_sems, recv_sems)
    if kind == "chips":
        return _chips_ops(src_ref, dst_ref, send_sems, recv_sems, local_sem)
    return _exchange_ops(False, src_ref, dst_ref, send_sems, recv_sems, local_sem)


def _comm_out_shape(kind, v):
    shape = {"pair": (N_DEV // 2,) + v.shape[1:], "chips": v.shape}.get(kind, (N_DEV,) + v.shape)
    return jax.ShapeDtypeStruct(shape, v.dtype)


def _comm_scratch(n_tasks):
    return [pltpu.SemaphoreType.DMA((N_DEV - 1,)), pltpu.SemaphoreType.DMA((N_DEV - 1,)), pltpu.SemaphoreType.DMA] * n_tasks


def _pair_sum(s, r, *, name):
    _, rows, cdim = s.shape
    tr = _pick(rows, (1024, 704, 512, 256, 128, 64, 32, 16, 8))
    rc = min(16, tr)

    def body(c_ref, s_ref, r_ref, o_ref):
        def chunk(ci, carry):
            sl = pl.ds(pl.multiple_of(ci * rc, rc), rc)
            o_ref[sl, :] = (s_ref[sl, :].astype(F32) + r_ref[sl, :].astype(F32)).astype(o_ref.dtype)
            return carry

        lax.fori_loop(0, tr // rc, chunk, 0)

    spec = pl.BlockSpec((None, tr, cdim), lambda q, i, c_ref: (q, i, 0))
    return pl.pallas_call(
        body, name=name,
        grid_spec=pltpu.PrefetchScalarGridSpec(
            num_scalar_prefetch=1, grid=(N_DEV // 2, rows // tr),
            in_specs=[pl.BlockSpec((None, None, tr, cdim), lambda q, i, c_ref: (q, c_ref[0], i, 0)), spec],
            out_specs=spec),
        out_shape=jax.ShapeDtypeStruct(r.shape, s.dtype),
        compiler_params=_cparams(("parallel", "parallel")),
    )(lax.axis_index("c").astype(jnp.int32).reshape(1), s.reshape(N_DEV // 2, 2, rows, cdim), r)


def _comm(kind, v, *, name):
    def body(s_ref, r_ref, send_sems, recv_sems, local_sem):
        start, finish = _comm_ops(kind, s_ref, r_ref, send_sems, recv_sems, local_sem)
        start()
        finish()

    return pl.pallas_call(
        body, name=name,
        out_shape=_comm_out_shape(kind, v),
        in_specs=[pl.BlockSpec(memory_space=pl.ANY)],
        out_specs=pl.BlockSpec(memory_space=pl.ANY),
        scratch_shapes=_comm_scratch(1),
    )(v)


class _Overlap:
    US_PER_MB = {"gather": 52.0, "pair": 1.0, "chips": 13.0, "bcast": 97.0}
    MM_FLOPS_PER_US = 6.0e8

    def __init__(self):
        self.queue, self.results, self.then = [], {}, {}

    def add(self, key, kind, v):
        self.queue.append((key, kind, v))

    def _cost(self, kind, v):
        return v.size * v.dtype.itemsize / 2 ** 20 * self.US_PER_MB[kind]

    def take(self, budget_us):
        taken, rest, cum = [], [], 0.0
        for task in self.queue:
            cost = self._cost(*task[1:])
            if (cum + cost <= 1.25 * budget_us) if taken else (cost <= 2.0 * budget_us):
                taken.append(task)
                cum += cost
            else:
                rest.append(task)
        self.queue = rest
        return taken

    def put(self, taken, res):
        for (key, _, _), r in zip(taken, res):
            self.results[key] = r
            if key in self.then:
                self.then.pop(key)(r)

    def carry(self, budget_us, fn):
        taken = self.take(budget_us)
        out, res = fn([(kind, v) for _, kind, v in taken])
        self.put(taken, res)
        return out

    def mm(self, a, b, *, mode, **kw):
        la, lb = _logical(a), _logical(b)
        budget = 2.0 * la[0] * la[1] * (lb[0] if mode == "nt" else lb[1]) / self.MM_FLOPS_PER_US

        def fn(comm):
            return _mm(a, b, mode=mode, comm=comm, **kw) if comm else (_mm(a, b, mode=mode, **kw), [])

        return self.carry(budget, fn)

    def get(self, key):
        while key not in self.results:
            keys = [k for k, _, _ in self.queue]
            task = self.queue.pop(keys.index(key if key in keys else "pair_" + key))
            self.put([task], [_comm(task[1], task[2], name=f"alone_{task[0]}")])
        return self.results[key]

    def weight(self, n):
        g = self.get(n)
        return g.reshape(-1, g.shape[2]) if n in ROW_SHARDED else g

    def grad(self, n, g):
        g = g if g.ndim == 3 else g.reshape(N_DEV, g.shape[0] // N_DEV, g.shape[1])
        self.add("pair_d_" + n, "pair", g)
        self.then["pair_d_" + n] = lambda r: self.add("d_" + n, "chips", _pair_sum(g, r, name=f"pair_sum_{n}"))


def _sel_tables(dest, ws, wp, tw):
    n_tiles = (int(dest.max()) + tw) // tw
    tbl = np.full((N_DEV, wp), -1, np.int32)
    for j in range(N_DEV):
        tbl[j, :ws] = dest[j * ws:(j + 1) * ws]
    by_tile = [sorted({j for j in range(N_DEV) if ((tbl[j] // tw) == t).any()}) for t in range(n_tiles)]
    by_shard = [sorted({int(t) for t in np.unique(tbl[j, :ws] // tw)}) for j in range(N_DEV)]

    def table(lists):
        width = max(len(v) for v in lists)
        idx = np.array([(v + [v[-1]] * width)[:width] if v else [0] * width for v in lists], np.int32)
        val = np.array([[1] * len(v) + [0] * (width - len(v)) for v in lists], np.int32)
        return idx.reshape(-1), val.reshape(-1), width

    return tbl[:, :, None], table(by_tile), table(by_shard)


def _sel_matrix(d_ref, t, wp, tw):
    cols = t * tw + lax.broadcasted_iota(jnp.int32, (wp, tw), 1)
    return (d_ref[...] == cols).astype(BF16)


def _win_unshard(g, tbl, idx, val, width, *, tw, padded, name):
    _, dm, wp = g.shape
    tm = _pick(dm, (1024, 512, 256, 128))

    def body(idx_ref, val_ref, g_ref, d_ref, o_ref, acc):
        t, s_ = pl.program_id(1), pl.program_id(2)

        @pl.when(s_ == 0)
        def _():
            acc[...] = jnp.zeros_like(acc)

        @pl.when(val_ref[t * width + s_] == 1)
        def _():
            acc[...] += _dot(g_ref[...], _sel_matrix(d_ref, t, wp, tw), NN)

        @pl.when(s_ == width - 1)
        def _():
            o_ref[...] = acc[...].astype(BF16)

    return pl.pallas_call(
        body, name=name,
        grid_spec=pltpu.PrefetchScalarGridSpec(
            num_scalar_prefetch=2, grid=(dm // tm, padded // tw, width),
            in_specs=[pl.BlockSpec((None, tm, wp), lambda i, t, s_, ix, vl: (ix[t * width + s_], i, 0)),
                      pl.BlockSpec((None, wp, 1), lambda i, t, s_, ix, vl: (ix[t * width + s_], 0, 0))],
            out_specs=pl.BlockSpec((tm, tw), lambda i, t, s_, ix, vl: (i, t)),
            scratch_shapes=[pltpu.VMEM((tm, tw), F32)]),
        out_shape=jax.ShapeDtypeStruct((dm, padded), BF16),
        compiler_params=_cparams(("parallel", "parallel", "arbitrary")),
    )(idx, val, g, tbl)


def _win_to_shards(dw, tbl, idx, val, width, *, tw, wp, name):
    dm = dw.shape[0]
    tm = _pick(dm, (1024, 512, 256, 128))

    def body(idx_ref, val_ref, w_ref, d_ref, o_ref, acc):
        j, s_ = pl.program_id(1), pl.program_id(2)

        @pl.when(s_ == 0)
        def _():
            acc[...] = jnp.zeros_like(acc)

        @pl.when(val_ref[j * width + s_] == 1)
        def _():
            acc[...] += _dot(w_ref[...], _sel_matrix(d_ref, idx_ref[j * width + s_], wp, tw), NT)

        @pl.when(s_ == width - 1)
        def _():
            o_ref[...] = acc[...].astype(BF16)

    return pl.pallas_call(
        body, name=name,
        grid_spec=pltpu.PrefetchScalarGridSpec(
            num_scalar_prefetch=2, grid=(dm // tm, N_DEV, width),
            in_specs=[pl.BlockSpec((tm, tw), lambda i, j, s_, ix, vl: (i, ix[j * width + s_])),
                      pl.BlockSpec((None, wp, 1), lambda i, j, s_, ix, vl: (j, 0, 0))],
            out_specs=pl.BlockSpec((None, tm, wp), lambda i, j, s_, ix, vl: (j, i, 0)),
            scratch_shapes=[pltpu.VMEM((tm, wp), F32)]),
        out_shape=jax.ShapeDtypeStruct((N_DEV, dm, wp), BF16),
        compiler_params=_cparams(("parallel", "parallel", "arbitrary")),
    )(idx, val, dw, tbl)


def _adamw(parts, w, m, v, *, name):
    r, cdim = w.shape
    n_parts = parts.shape[0]
    tr = _pick(r, (256, 128, 64, 32, 16, 8))
    rc = min(16, tr)
    c1 = 1.0 - ADAM_B1 ** ADAM_STEP
    c2 = 1.0 - ADAM_B2 ** ADAM_STEP

    def body(p_ref, w_ref, m_ref, v_ref, g_ref, d_ref, mo_ref, vo_ref):
        def chunk(ci, carry):
            sl = pl.ds(pl.multiple_of(ci * rc, rc), rc)
            g = p_ref[0, sl, :].astype(F32)
            for i in range(1, n_parts):
                g = g + p_ref[i, sl, :].astype(F32)
            mn = ADAM_B1 * m_ref[sl, :] + (1.0 - ADAM_B1) * g
            vn = ADAM_B2 * v_ref[sl, :] + (1.0 - ADAM_B2) * jnp.square(g)
            m_hat = mn / c1
            v_hat = vn / c2
            g_ref[sl, :] = g
            d_ref[sl, :] = -ADAM_LR * (m_hat / (jnp.sqrt(v_hat) + ADAM_EPS) + ADAM_WD * w_ref[sl, :])
            mo_ref[sl, :] = mn
            vo_ref[sl, :] = vn
            return carry

        lax.fori_loop(0, tr // rc, chunk, 0)

    spec = pl.BlockSpec((tr, cdim), lambda i: (i, 0))
    return pl.pallas_call(
        body, name=name, grid=(r // tr,),
        in_specs=[pl.BlockSpec((n_parts, tr, cdim), lambda i: (0, i, 0)), spec, spec, spec],
        out_specs=[spec] * 4,
        out_shape=[jax.ShapeDtypeStruct((r, cdim), F32)] * 4,
        compiler_params=_cparams(("parallel",)),
    )(parts, w, m, v)


def _pad_to(v, n):
    return v if v.shape[0] == n else jnp.concatenate([v, jnp.zeros((n - v.shape[0],), v.dtype)])


def _pad_cols(v, n):
    return v if v.shape[-1] == n else jnp.concatenate([v, jnp.zeros(v.shape[:-1] + (n - v.shape[-1],), v.dtype)], axis=-1)


def _pack(vs, cols, row_mult, dtype):
    offs, o = [], 0
    for v in vs:
        offs.append(o)
        o += v.size
    rows = -(-o // cols)
    rows = -(-rows // row_mult) * row_mult
    flat = jnp.concatenate([v.reshape(-1).astype(dtype) for v in vs])
    return _pad_to(flat, rows * cols).reshape(rows, cols), offs


SHARDED = ("ffn1_w_gate", "ffn1_w_up", "ffn1_w_down", "w_in", "w_merge_gate", "gla_gate_up", "w_branch_fox",
           "w_branch_gla", "w_out", "ffn2_w_gate", "ffn2_w_up", "ffn2_w_down", "w_ple_gate", "w_ple_proj")
ROW_SHARDED = ("ffn1_w_down", "w_out", "ffn2_w_down", "w_ple_gate")
REPLICATED = ("ffn1_norm", "mix_norm", "fox_forget_bias", "gla_gate_bias", "gla_head_norm", "b_merge_gate",
              "ffn2_norm", "ple_norm", "final_norm")
WEIGHTS = ("ffn1_norm", "ffn1_w_gate", "ffn1_w_up", "ffn1_w_down", "mix_norm", "w_in", "fox_forget_bias",
           "gla_gate_up", "gla_gate_bias", "gla_head_norm", "w_branch_fox", "w_branch_gla", "w_merge_gate",
           "b_merge_gate", "w_out", "ffn2_norm", "ffn2_w_gate", "ffn2_w_up", "ffn2_w_down", "ple_norm",
           "w_ple_proj", "w_ple_gate", "final_norm")


def kernel(x, p, ffn1_norm, ffn1_w_gate, ffn1_w_up, ffn1_w_down, mix_norm, w_in, fox_forget_bias, gla_gate_up, gla_gate_bias, gla_head_norm, w_branch_fox, w_branch_gla, w_merge_gate, b_merge_gate, w_out, ffn2_norm, ffn2_w_gate, ffn2_w_up, ffn2_w_down, ple_norm, w_ple_proj, w_ple_gate, final_norm, loss_target, m_ffn1_norm, m_ffn1_w_gate, m_ffn1_w_up, m_ffn1_w_down, m_mix_norm, m_w_in, m_fox_forget_bias, m_gla_gate_up, m_gla_gate_bias, m_gla_head_norm, m_w_branch_fox, m_w_branch_gla, m_w_merge_gate, m_b_merge_gate, m_w_out, m_ffn2_norm, m_ffn2_w_gate, m_ffn2_w_up, m_ffn2_w_down, m_ple_norm, m_w_ple_proj, m_w_ple_gate, m_final_norm, v_ffn1_norm, v_ffn1_w_gate, v_ffn1_w_up, v_ffn1_w_down, v_mix_norm, v_w_in, v_fox_forget_bias, v_gla_gate_up, v_gla_gate_bias, v_gla_head_norm, v_w_branch_fox, v_w_branch_gla, v_w_merge_gate, v_b_merge_gate, v_w_out, v_ffn2_norm, v_ffn2_w_gate, v_ffn2_w_up, v_ffn2_w_down, v_ple_norm, v_w_ple_proj, v_w_ple_gate, v_final_norm):
    args = dict(locals())
    wts = {n: args[n] for n in WEIGHTS}
    mom_m = {n: args["m_" + n] for n in WEIGHTS}
    mom_v = {n: args["v_" + n] for n in WEIGHTS}

    xs, ps, tgt = x[0], p[0, 0], loss_target[0]
    s, d = xs.shape
    fox_w = w_branch_fox.shape[1]
    gla_vw = w_branch_gla.shape[1]
    fox_heads = fox_w // HEAD_DIM
    gla_heads = gla_vw // GLA_VAL_DIM
    gla_kw = gla_heads * HEAD_DIM
    rank = gla_gate_up.shape[1]

    c_fl = 3 * fox_w
    o_gr, o_gq, o_gk = gla_vw, 2 * gla_vw, 2 * gla_vw + gla_kw
    o_fl = o_gk + gla_kw
    o_gd = o_fl + LANES
    rest_w = o_gd + LANES
    padded = c_fl + rest_w
    seg = [(c_fl, 0), (fox_heads, c_fl + o_fl), (gla_kw, c_fl + o_gq), (gla_kw, c_fl + o_gk), (gla_vw, c_fl),
           (gla_vw, c_fl + o_gr), (rank, c_fl + o_gd)]
    dest = np.concatenate([np.arange(w_, dtype=np.int32) + o_ for w_, o_ in seg])
    ws = w_in.shape[2]
    wp = -(-ws // LANES) * LANES
    tw = 256 if padded % 256 == 0 else LANES
    tbl, (t_idx, t_val, t_width), (s_idx, s_val, s_width) = _sel_tables(dest, ws, wp, tw)
    tbl = jnp.asarray(tbl)

    ov = _Overlap()
    for n in SHARDED:
        sh = wts[n][0].astype(BF16)
        ov.add(n, "gather", _pad_cols(sh, wp) if n == "w_in" else sh)
    fbias = _pad_cols(fox_forget_bias, LANES)
    bmg_f, bmg_g = b_merge_gate[:, :d], b_merge_gate[:, d:]
    ghn = jnp.tile(gla_head_norm, (1, gla_heads))

    h1, ffn1_saved = _ffn_fwd(xs, ffn1_norm, ov, tag="ffn1")
    u = _rms_fwd(h1, mix_norm, name="mix_norm")
    win = _win_unshard(ov.weight("w_in"), tbl, jnp.asarray(t_idx), jnp.asarray(t_val), t_width, tw=tw, padded=padded,
                       name="in_proj_unshard")
    win_fox, win_rest = win[:, :c_fl], win[:, c_fl:]
    zf = ov.mm(u, win_fox, mode="nn", name="in_proj_fox", out_dtype=BF16)
    zr = ov.mm(u, win_rest, mode="nn", name="in_proj_rest")
    gz = ov.mm(u, ov.weight("w_merge_gate"), mode="nn", name="merge_gate")
    gup = ov.weight("gla_gate_up").transpose(1, 0, 2).reshape(rank, gla_kw)
    gup = jnp.concatenate([gup, jnp.zeros((LANES - rank, gla_kw), BF16)], axis=0)

    log_f = _rowwise(lambda fl, b: _log_sigmoid(fl + b), [(zr, LANES, o_fl // LANES)], [fbias], [(LANES, F32)],
                     name="forget_gate")[0]
    f_cum = _cumsum(log_f[:, :fox_heads].T, reverse=False, name="forget_cumsum")
    f_rep = jnp.broadcast_to(f_cum.T[:, :, None], (s, fox_heads, HEAD_DIM)).reshape(s, fox_w)
    f_row = f_cum[:, None, :]
    attn_us = 2.0 * s * s * HEAD_DIM * fox_heads / _Overlap.MM_FLOPS_PER_US
    y_fox, y_fox_bf, lse = ov.carry(
        ATTN_FWD_MATMULS * attn_us, lambda comm: _fox_fwd(zf, f_rep, f_row, heads=fox_heads, name="fox_fwd", comm=comm))

    def decay_fn(gd, gupv, gb):
        return _log_sigmoid(_dot(gd, gupv, NN) + gb) * (1.0 / GLA_GATE_TAU)

    la = _rowwise(decay_fn, [(zr, LANES, o_gd // LANES)], [gup, gla_gate_bias], [(gla_kw, F32)], name="gla_decay", rc=128)[0]
    q_blk, k_blk = o_gq // HEAD_DIM, o_gk // HEAD_DIM
    o_gla, states = _gla_fwd(zr, la, heads=gla_heads, q_blk=q_blk, k_blk=k_blk, name="gla_fwd")

    def gla_out_fn(o, gr, g):
        outs = []
        for hh in range(gla_heads):
            sl = slice(hh * GLA_VAL_DIM, (hh + 1) * GLA_VAL_DIM)
            _, oh = _rms_parts(o[:, sl])
            outs.append(oh * g[:, sl] * _silu_parts(gr[:, sl])[0])
        return jnp.concatenate(outs, axis=1)

    y_gla = _rowwise(gla_out_fn, [o_gla, (zr, gla_vw, o_gr // gla_vw)], [ghn], [(gla_vw, BF16)], name="gla_out")[0]
    br_f = ov.mm(y_fox_bf, ov.weight("w_branch_fox"), mode="nn", name="branch_fox")
    br_g = ov.mm(y_gla, ov.weight("w_branch_gla"), mode="nn", name="branch_gla")

    def merge_fn(zf_, zg_, bf_, bg_, b1, b2):
        return _sigmoid(zf_ + b1) * bf_ + _sigmoid(zg_ + b2) * bg_

    merged = _rowwise(merge_fn, [(gz, d, 0), (gz, d, 1), br_f, br_g], [bmg_f, bmg_g], [(d, BF16)], name="merge")[0]
    h2 = ov.mm(merged, ov.weight("w_out"), mode="nn", name="out_proj", add=h1)
    h3, ffn2_saved = _ffn_fwd(h2, ffn2_norm, ov, tag="ffn2")
    n3 = _rms_fwd(h3, ple_norm, name="ple_norm")
    gl = ov.mm(n3, ov.weight("w_ple_gate"), mode="nn", name="ple_gate")
    pe = ov.mm(ps, ov.weight("w_ple_proj"), mode="nn", name="ple_proj")

    def head_fn(h3b, glb, peb, tb, gfin):
        pg = _sigmoid(glb)
        h4 = h3b + pg * peb
        r, xh = _rms_parts(h4)
        err = xh * gfin - tb
        dy = err * (1.0 / d)
        t = dy * gfin
        dh4 = r * (t - xh * jnp.mean(t * xh, axis=-1, keepdims=True))
        return dh4, dh4 * pg, dh4 * peb * pg * (1.0 - pg), _colsum(err * err), _colsum(dy * xh)

    dh4, dpe, dgl, loss_cols, d_final = _rowwise(
        head_fn, [h3, gl, pe, tgt], [final_norm.reshape(1, d)], [(d, F32), (d, BF16), (d, BF16)], [d, d], name="loss_head")
    loss = lax.psum(0.5 * jnp.sum(loss_cols) / d, AXES)

    grads = {"final_norm": d_final.reshape(d)}
    ov.grad("w_ple_proj", ov.mm(ps, dpe, mode="tn", name="ple_proj_dw", out_dtype=BF16, out_chunked=True))
    ov.grad("w_ple_gate", ov.mm(n3, dgl, mode="tn", name="ple_gate_dw", out_dtype=BF16))
    dn3 = ov.mm(dgl, ov.weight("w_ple_gate"), mode="nt", name="ple_gate_dx")
    dh3, dh3_bf, grads["ple_norm"] = _rms_bwd(h3, dn3, ple_norm, dh4, name="ple_norm_bwd")
    dh2, dh2_bf, grads["ffn2_norm"] = _ffn_bwd(h2, ffn2_norm, ov, ffn2_saved, dh3, dh3_bf, tag="ffn2")

    ov.grad("w_out", ov.mm(merged, dh2_bf, mode="tn", name="out_proj_dw", out_dtype=BF16))
    dmerged = ov.mm(dh2_bf, ov.weight("w_out"), mode="nt", name="out_proj_dx")

    def merge_bwd_fn(zf_, zg_, bf_, bg_, dm, b1, b2):
        sf, sg = _sigmoid(zf_ + b1), _sigmoid(zg_ + b2)
        dz = jnp.concatenate([dm * bf_ * sf * (1.0 - sf), dm * bg_ * sg * (1.0 - sg)], axis=1)
        return dm * sf, dm * sg, dz, _colsum(dz)

    dbr_f, dbr_g, dgz, grads["b_merge_gate"] = _rowwise(
        merge_bwd_fn, [(gz, d, 0), (gz, d, 1), br_f, br_g, dmerged], [bmg_f, bmg_g],
        [(d, BF16), (d, BF16), (2 * d, BF16)], [2 * d], name="merge_bwd")
    ov.grad("w_merge_gate", ov.mm(u, dgz, mode="tn", name="merge_gate_dw", out_dtype=BF16, out_chunked=True))
    ov.grad("w_branch_fox", ov.mm(y_fox_bf, dbr_f, mode="tn", name="branch_fox_dw", out_dtype=BF16, out_chunked=True))
    ov.grad("w_branch_gla", ov.mm(y_gla, dbr_g, mode="tn", name="branch_gla_dw", out_dtype=BF16, out_chunked=True))
    dy_fox = ov.mm(dbr_f, ov.weight("w_branch_fox"), mode="nt", name="branch_fox_dx")
    dy_gla = ov.mm(dbr_g, ov.weight("w_branch_gla"), mode="nt", name="branch_gla_dx")

    def gla_out_bwd_fn(o, gr, dy, g):
        dos, dgrs, dgs = [], [], []
        for hh in range(gla_heads):
            sl = slice(hh * GLA_VAL_DIM, (hh + 1) * GLA_VAL_DIM)
            r, oh = _rms_parts(o[:, sl])
            si, dsi = _silu_parts(gr[:, sl])
            don = dy[:, sl] * si
            dgrs.append(dy[:, sl] * oh * g[:, sl] * dsi)
            t = don * g[:, sl]
            dos.append(r * (t - oh * jnp.mean(t * oh, axis=-1, keepdims=True)))
            dgs.append(_colsum(don * oh))
        return jnp.concatenate(dos, axis=1), jnp.concatenate(dgrs, axis=1), jnp.concatenate(dgs, axis=1)

    do_gla, dgr, d_ghn = _rowwise(gla_out_bwd_fn, [o_gla, (zr, gla_vw, o_gr // gla_vw), dy_gla], [ghn],
                                  [(gla_vw, F32), (gla_vw, F32)], [gla_vw], name="gla_out_bwd")
    grads["gla_head_norm"] = d_ghn.reshape(gla_heads, GLA_VAL_DIM).sum(axis=0, keepdims=True)
    dgq, dgk, dgv, dla = _gla_bwd(zr, la, do_gla, states, heads=gla_heads, q_blk=q_blk, k_blk=k_blk, name="gla_bwd")

    def decay_bwd_fn(dl, gd, gupv, gb):
        pre = _dot(gd, gupv, NN) + gb
        dpre = dl * (1.0 / GLA_GATE_TAU) * _sigmoid(-pre)
        return _dot(dpre, gupv, NT), dpre, _colsum(dpre)

    dgd, dpre_bf, grads["gla_gate_bias"] = _rowwise(
        decay_bwd_fn, [dla, (zr, LANES, o_gd // LANES)], [gup, gla_gate_bias], [(LANES, F32), (gla_kw, BF16)], [gla_kw],
        name="gla_decay_bwd", rc=128)
    d_gup = ov.mm(zr[:, o_gd:o_gd + LANES], dpre_bf, mode="tn", name="gla_gate_up_dw")[:rank]
    ov.grad("gla_gate_up", d_gup.reshape(rank, N_DEV, gla_kw // N_DEV).transpose(1, 0, 2).astype(BF16))

    def delta_fn(dyv, yv):
        outs = []
        for hh in range(fox_heads):
            sl = slice(hh * HEAD_DIM, (hh + 1) * HEAD_DIM)
            outs.append(jnp.broadcast_to(jnp.sum(dyv[:, sl] * yv[:, sl], axis=-1, keepdims=True), (dyv.shape[0], HEAD_DIM)))
        return jnp.concatenate(outs, axis=1)

    delta = _rowwise(delta_fn, [dy_fox, y_fox], [], [(fox_w, F32)], name="fox_delta")[0]
    dfq, dfk, dfv, d_fcol, d_frow = ov.carry(
        ATTN_BWD_MATMULS * attn_us,
        lambda comm: _fox_bwd(zf, dy_fox, lse, delta, f_rep, f_row, heads=fox_heads, name="fox_bwd", comm=comm))
    d_fcum = d_fcol[:, ::HEAD_DIM].T + d_frow.reshape(fox_heads, s)
    d_logf = _cumsum(d_fcum, reverse=True, name="forget_cumsum_bwd")
    d_logf = _pad_cols(d_logf.T, LANES)

    def forget_bwd_fn(dl, fl, b):
        dfl_ = dl * _sigmoid(-(fl + b))
        return dfl_, _colsum(dfl_)

    dfl, d_fbias = _rowwise(forget_bwd_fn, [d_logf, (zr, LANES, o_fl // LANES)], [fbias], [(LANES, F32)], [LANES],
                            name="forget_gate_bwd")
    grads["fox_forget_bias"] = d_fbias[:, :fox_heads]

    dz = jnp.concatenate([dfq, dfk, dfv, dgv, dgr, dgq, dgk, dfl, dgd], axis=1).astype(BF16)
    dwin = ov.mm(u, dz, mode="tn", name="in_proj_dw", out_dtype=BF16)
    ov.grad("w_in", _win_to_shards(dwin, tbl, jnp.asarray(s_idx), jnp.asarray(s_val), s_width, tw=tw, wp=wp,
                                   name="in_proj_dw_shards"))
    du = ov.mm(dgz, ov.weight("w_merge_gate"), mode="nt", name="merge_gate_dx")
    du = ov.mm(dz, win, mode="nt", name="in_proj_dx", add=du)
    dh1, dh1_bf, grads["mix_norm"] = _rms_bwd(h1, du, mix_norm, dh2, name="mix_norm_bwd")
    dx, _, grads["ffn1_norm"] = _ffn_bwd(xs, ffn1_norm, ov, ffn1_saved, dh1, dh1_bf, tag="ffn1")

    outs = {}
    for n in SHARDED:
        parts = ov.get("d_" + n)
        state = [_pad_cols(t_[n][0], parts.shape[2]) for t_ in (wts, mom_m, mom_v)]
        res4 = _adamw(parts, *state, name=f"adamw_{n}")
        for kind, r_ in zip(("grad", "delta", "new_m", "new_v"), res4):
            outs[f"{kind}_{n}"] = r_[:, :wts[n].shape[2]][None]

    send_small, small_offs = _pack([grads[n] for n in REPLICATED], LANES, 8, F32)
    recv_small = _comm("bcast", send_small, name="exchange_replicated")
    w_sm, _ = _pack([wts[n] for n in REPLICATED], LANES, 8, F32)
    m_sm, _ = _pack([mom_m[n] for n in REPLICATED], LANES, 8, F32)
    v_sm, _ = _pack([mom_v[n] for n in REPLICATED], LANES, 8, F32)
    small = _adamw(recv_small, w_sm, m_sm, v_sm, name="adamw_replicated")
    for kind, buf in zip(("grad", "delta", "new_m", "new_v"), small):
        fs = buf.reshape(-1)
        for n, o in zip(REPLICATED, small_offs):
            outs[f"{kind}_{n}"] = fs[o:o + wts[n].size].reshape(wts[n].shape)

    res = [loss, dx[None]]
    for kind in ("grad", "delta", "new_m", "new_v"):
        res += [outs[f"{kind}_{n}"] for n in WEIGHTS]
    return tuple(res)
```

```python
import functools

import jax
import jax.numpy as jnp
import numpy as np
from jax import lax
from jax.experimental import pallas as pl
from jax.experimental.pallas import tpu as pltpu

F32 = jnp.float32
BF16 = jnp.bfloat16
MESH = pl.DeviceIdType.MESH
AXES = ("x", "y", "c")
N_DEV = 8

VMEM_LIMIT_BYTES = 56 * 1024 * 1024
MM_BLOCK_BUDGET_BYTES = 40 * 1024 * 1024
LANES = 128

EPS = 1e-6
HEAD_DIM = 128
GLA_VAL_DIM = 256
GLA_CHUNK = 64
GLA_GATE_TAU = 16.0
ADAM_LR, ADAM_B1, ADAM_B2, ADAM_EPS, ADAM_WD, ADAM_STEP = 0.001, 0.9, 0.999, 1e-08, 0.01, 10

ROWWISE_BYTES_PER_US = 2.5e6
ATTN_FWD_MATMULS = 7.0
ATTN_BWD_MATMULS = 7.0

HIGHEST = lax.Precision.HIGHEST
NN = (((1,), (0,)), ((), ()))
NT = (((1,), (1,)), ((), ()))
TN = (((0,), (0,)), ((), ()))


def _cparams(sem):
    return pltpu.CompilerParams(dimension_semantics=sem, vmem_limit_bytes=VMEM_LIMIT_BYTES)


def _pick(dim, cands):
    for c in cands:
        if dim % c == 0:
            return c
    return dim


def _bf(v):
    return v if v.dtype == BF16 else v.astype(BF16)


def _dot(a, b, dims):
    return lax.dot_general(_bf(a), _bf(b), dims, preferred_element_type=F32)


def _sigmoid(v):
    return 1.0 / (1.0 + jnp.exp(-v))


def _log_sigmoid(v):
    return jnp.minimum(v, 0.0) - jnp.log(1.0 + jnp.exp(-jnp.abs(v)))


def _logical(v):
    return (v.shape[1], v.shape[0] * v.shape[2]) if v.ndim == 3 else v.shape


def _mm(a, b, *, mode, name, out_dtype=F32, add=None, scale=1.0, out_chunked=False, comm=()):
    la, lb = _logical(a), _logical(b)
    if mode == "nn":
        (m, k), (k2, n) = la, lb
        a_minor, b_minor = "k", "n"
    elif mode == "nt":
        (m, k), (n, k2) = la, lb
        a_minor, b_minor = "k", "k"
    else:
        (k, m), (k2, n) = la, lb
        a_minor, b_minor = "m", "n"
    assert k == k2, (name, a.shape, b.shape)
    forced = {}
    for v, minor in ((a, a_minor), (b, b_minor)):
        if v.ndim == 3:
            assert forced.get(minor, v.shape[2]) == v.shape[2], name
            forced[minor] = v.shape[2]
    if out_chunked:
        assert forced.get("n", n // N_DEV) == n // N_DEV, name
        forced["n"] = n // N_DEV
    tm = forced.get("m") or _pick(m, (1024, 512, 256, 128))
    tn = forced.get("n") or _pick(n, (1408, 1280, 1024, 512, 256, 128))

    def blocks_bytes(tm_, tn_, t):
        io = tm_ * t * a.dtype.itemsize + t * tn_ * b.dtype.itemsize
        return 2 * (io + tm_ * tn_ * (jnp.dtype(out_dtype).itemsize + (4 if add is not None else 0))) + tm_ * tn_ * 4

    kc = forced.get("k")
    a_k_minor, b_k_minor = mode != "tn", mode == "nt"
    if kc:
        aligned = all(v.ndim == 3 or kc % (LANES if minor else 16) == 0 for v, minor in ((a, a_k_minor), (b, b_k_minor)))
        fits = [(tm_, tn_) for tm_, tn_ in ((tm, tn), (512, tn), (512, 512))
                if m % tm_ == 0 and n % tn_ == 0 and forced.get("m", tm_) == tm_ and forced.get("n", tn_) == tn_
                and blocks_bytes(tm_, tn_, k) <= MM_BLOCK_BUDGET_BYTES]
        if aligned and fits:
            (tm, tn), tk = fits[0], k
        else:
            tk, kc = kc, None
    else:
        tk = k if blocks_bytes(tm, tn, k) <= MM_BLOCK_BUDGET_BYTES else _pick(k, (512, 640, 256, 128))
    nk = k // tk
    dims = {"nn": NN, "nt": NT, "tn": TN}[mode]
    gi, gj, gk = (lambda i, j, kk: i), (lambda i, j, kk: j), (lambda i, j, kk: kk)

    def spec(v, t_major, t_minor, g_major, g_minor, all_chunks=False):
        if v.ndim == 3:
            if all_chunks:
                return pl.BlockSpec((N_DEV, t_major, v.shape[2]), lambda i, j, kk: (0, g_major(i, j, kk), 0))
            return pl.BlockSpec((None, t_major, v.shape[2]), lambda i, j, kk: (g_minor(i, j, kk), g_major(i, j, kk), 0))
        return pl.BlockSpec((t_major, t_minor), lambda i, j, kk: (g_major(i, j, kk), g_minor(i, j, kk)))

    a_spec = spec(a, tk, tm, gk, gi) if mode == "tn" else spec(a, tm, tk, gi, gk, bool(kc))
    b_spec = spec(b, tn, tk, gj, gk, bool(kc)) if mode == "nt" else spec(b, tk, tn, gk, gj)

    def k_chunk(ref, minor, c_):
        if len(ref.shape) == 3:
            return ref[c_]
        return ref[:, c_ * kc:(c_ + 1) * kc] if minor else ref[c_ * kc:(c_ + 1) * kc, :]
    if out_chunked:
        o_spec = pl.BlockSpec((None, tm, tn), lambda i, j, kk: (j, i, 0))
        out_shape = jax.ShapeDtypeStruct((N_DEV, m, tn), out_dtype)
    else:
        o_spec = pl.BlockSpec((tm, tn), lambda i, j, kk: (i, j))
        out_shape = jax.ShapeDtypeStruct((m, n), out_dtype)
    has_add = add is not None
    assert not (has_add and out_chunked), name

    def body(*refs):
        a_ref, b_ref = refs[0], refs[1]
        add_ref = refs[2] if has_add else None
        o_ref = refs[3 if has_add else 2]
        kk = pl.program_id(2)

        def finish(r):
            if scale != 1.0:
                r = r * scale
            if has_add:
                r = r + add_ref[...]
            o_ref[...] = r.astype(o_ref.dtype)

        if kc:
            r = _dot(k_chunk(a_ref, a_k_minor, 0), k_chunk(b_ref, b_k_minor, 0), dims)
            for c_ in range(1, k // kc):
                r = r + _dot(k_chunk(a_ref, a_k_minor, c_), k_chunk(b_ref, b_k_minor, c_), dims)
            finish(r)
        elif nk == 1:
            finish(_dot(a_ref[...], b_ref[...], dims))
        else:
            acc_ref = refs[-1]

            @pl.when(kk == 0)
            def _():
                acc_ref[...] = jnp.zeros_like(acc_ref)

            acc_ref[...] += _dot(a_ref[...], b_ref[...], dims)

            @pl.when(kk == nk - 1)
            def _():
                finish(acc_ref[...])

    res, carried = _call(
        body, name=name, grid=(m // tm, n // tn, nk),
        in_specs=[a_spec, b_spec] + ([o_spec] if has_add else []), out_specs=[o_spec], out_shape=[out_shape],
        scratch_shapes=[pltpu.VMEM((tm, tn), F32)] if nk > 1 else [],
        semantics=("parallel", "parallel", "arbitrary"), operands=[a, b] + ([add] if has_add else []), comm=comm)
    return (res[0], carried) if comm else res[0]


def _call(body, *, name, grid, in_specs, out_specs, out_shape, scratch_shapes, semantics, operands, comm=()):
    n_in, n_out, n_scr, n = len(in_specs), len(out_specs), len(scratch_shapes), len(comm)
    if not comm:
        res = pl.pallas_call(body, name=name, grid=grid, in_specs=in_specs, out_specs=out_specs, out_shape=out_shape,
                             scratch_shapes=scratch_shapes, compiler_params=_cparams(semantics))(*operands)
        return res, []

    def carrying(*refs):
        ins, c_in = refs[:n_in], refs[n_in:n_in + n]
        outs, c_out = refs[n_in + n:n_in + n + n_out], refs[n_in + n + n_out:n_in + 2 * n + n_out]
        scratch, sems = refs[n_in + 2 * n + n_out:][:n_scr], refs[n_in + 2 * n + n_out + n_scr:]
        tasks = [_comm_ops(kind, c_in[t], c_out[t], *sems[3 * t:3 * t + 3]) for t, (kind, _) in enumerate(comm)]
        ids = [pl.program_id(ax) for ax in range(len(grid))]

        @pl.when(functools.reduce(lambda p, q: p & q, [i == 0 for i in ids]))
        def _():
            for start, _ in tasks:
                start()

        body(*ins, *outs, *scratch)

        @pl.when(functools.reduce(lambda p, q: p & q, [i == g - 1 for i, g in zip(ids, grid)]))
        def _():
            for _, finish in tasks:
                finish()

    any_spec = pl.BlockSpec(memory_space=pl.ANY)
    res = pl.pallas_call(
        carrying, name=name, grid=grid,
        in_specs=list(in_specs) + [any_spec] * n, out_specs=list(out_specs) + [any_spec] * n,
        out_shape=list(out_shape) + [_comm_out_shape(kind, v) for kind, v in comm],
        scratch_shapes=list(scratch_shapes) + _comm_scratch(n),
        compiler_params=_cparams(("arbitrary",) * len(grid)),
    )(*operands, *[v for _, v in comm])
    return res[:n_out], res[n_out:]


def _rowwise(fn, rows, consts, outs, accs=(), *, name, rc=None, ov=None):
    rows = [r if isinstance(r, tuple) else (r, r.shape[1], 0) for r in rows]
    m = rows[0][0].shape[0]
    widths = [w for _, w, _ in rows] + [n for n, _ in outs]
    row_bytes = sum(w * r.dtype.itemsize for r, w, _ in rows) + sum(n * jnp.dtype(d).itemsize for n, d in outs)
    tm = 1024
    while tm > 16 and (m % tm or 2 * tm * row_bytes > 24 * 1024 * 1024):
        tm //= 2
    if m % tm:
        tm = m
    if rc is None:
        rc = 16
        while rc * 2 <= tm and rc * 2 * max(widths) <= 32768:
            rc *= 2
    rc = min(rc, tm)
    nr, nc, no = len(rows), len(consts), len(outs)

    def body(*refs):
        in_refs, c_refs = refs[:nr], refs[nr:nr + nc]
        o_refs, a_refs = refs[nr + nc:nr + nc + no], refs[nr + nc + no:]

        @pl.when(pl.program_id(0) == 0)
        def _():
            for r in a_refs:
                r[...] = jnp.zeros_like(r)

        cvals = [c[...] for c in c_refs]

        def chunk(ci, carry):
            sl = pl.ds(pl.multiple_of(ci * rc, rc), rc)
            res = fn(*[r[sl, :] for r in in_refs], *cvals)
            if not isinstance(res, (tuple, list)):
                res = (res,)
            for r, v in zip(o_refs, res[:no]):
                r[sl, :] = v.astype(r.dtype)
            for r, v in zip(a_refs, res[no:]):
                r[...] += v
            return carry

        lax.fori_loop(0, tm // rc, chunk, 0)

    in_specs = [pl.BlockSpec((tm, w), functools.partial(lambda i, cb: (i, cb), cb=cb)) for _, w, cb in rows]
    in_specs += [pl.BlockSpec(c.shape, lambda i: (0, 0)) for c in consts]
    out_specs = [pl.BlockSpec((tm, n), lambda i: (i, 0)) for n, _ in outs]
    out_specs += [pl.BlockSpec((1, n), lambda i: (0, 0)) for n in accs]
    out_shape = [jax.ShapeDtypeStruct((m, n), d) for n, d in outs] + [jax.ShapeDtypeStruct((1, n), F32) for n in accs]
    def run(comm):
        return _call(body, name=name, grid=(m // tm,), in_specs=in_specs, out_specs=out_specs, out_shape=out_shape,
                     scratch_shapes=[], semantics=("arbitrary",), operands=[r for r, _, _ in rows] + list(consts), comm=comm)

    if ov is None:
        return run(())[0]
    return ov.carry(m * row_bytes / ROWWISE_BYTES_PER_US, run, first=1.25)


def _colsum(v):
    return jnp.sum(v, axis=0, keepdims=True)


def _rms_parts(xv):
    r = lax.rsqrt(jnp.mean(xv * xv, axis=-1, keepdims=True) + EPS)
    return r, xv * r


def _rms_fwd(xv, g, *, name):
    d = xv.shape[1]

    def fn(xb, gb):
        _, xh = _rms_parts(xb)
        return xh * gb

    return _rowwise(fn, [xv], [g], [(d, BF16)], name=name)[0]


def _rms_bwd(xv, dy, g, add, *, name, ov=None):
    d = xv.shape[1]

    def fn(xb, dyb, addb, gb):
        r, xh = _rms_parts(xb)
        t = dyb * gb
        dx = r * (t - xh * jnp.mean(t * xh, axis=-1, keepdims=True)) + addb
        return dx, dx, _colsum(dyb * xh)

    return _rowwise(fn, [xv, dy, add], [g], [(d, F32), (d, BF16)], [d], name=name, ov=ov)


def _silu_parts(a):
    sg = _sigmoid(a)
    return a * sg, sg * (1.0 + a * (1.0 - sg))


def _rows(v):
    return v.reshape(v.shape[0] * v.shape[1], v.shape[2])


def _ffn_fwd(h, g, ov, *, tag):
    s = h.shape[0]
    n = _rms_fwd(h, g, name=f"{tag}_norm")
    a = ov.mm(n, ov.weight(f"{tag}_w_gate"), mode="nn", name=f"{tag}_gate", out_dtype=BF16, out_chunked=True)
    b = ov.mm(n, ov.weight(f"{tag}_w_up"), mode="nn", name=f"{tag}_up", out_dtype=BF16, out_chunked=True)
    c = a.shape[2]
    hm = _rowwise(lambda av, bv: _silu_parts(av.astype(F32))[0] * bv.astype(F32), [_rows(a), _rows(b)], [], [(c, BF16)],
                  name=f"{tag}_act", ov=ov)[0]
    hm = hm.reshape(N_DEV, s, c)
    out = ov.mm(hm, ov.weight(f"{tag}_w_down"), mode="nn", name=f"{tag}_down", add=h, scale=0.5)
    return out, (n, a, b, hm)


def _ffn_bwd(h, g, ov, saved, dout, dout_bf, *, tag):
    n, a, b, hm = saved
    wg, wu, wd = ov.weight(f"{tag}_w_gate"), ov.weight(f"{tag}_w_up"), ov.weight(f"{tag}_w_down")
    s, c = h.shape[0], wg.shape[2]
    d_wd = ov.mm(hm, dout_bf, mode="tn", name=f"{tag}_down_dw", scale=0.5, out_dtype=BF16)
    ov.grad(f"{tag}_w_down", d_wd)
    dhm = ov.mm(dout_bf, wd, mode="nt", name=f"{tag}_down_dx", scale=0.5, out_dtype=BF16, out_chunked=True)

    def act_bwd(av, bv, dv):
        av, bv, dv = av.astype(F32), bv.astype(F32), dv.astype(F32)
        si, dsi = _silu_parts(av)
        return dv * bv * dsi, dv * si

    da, db = _rowwise(act_bwd, [_rows(a), _rows(b), _rows(dhm)], [], [(c, BF16), (c, BF16)], name=f"{tag}_act_bwd",
                      ov=ov)
    da, db = da.reshape(N_DEV, s, c), db.reshape(N_DEV, s, c)
    ov.grad(f"{tag}_w_gate", ov.mm(n, da, mode="tn", name=f"{tag}_gate_dw", out_dtype=BF16, out_chunked=True))
    ov.grad(f"{tag}_w_up", ov.mm(n, db, mode="tn", name=f"{tag}_up_dw", out_dtype=BF16, out_chunked=True))
    dn = ov.mm(da, wg, mode="nt", name=f"{tag}_gate_dx")
    dn = ov.mm(db, wu, mode="nt", name=f"{tag}_up_dx", add=dn)
    dh, dh_bf, dg = _rms_bwd(h, dn, g, dout, name=f"{tag}_norm_bwd", ov=ov)
    return dh, dh_bf, dg


def _cumsum(xv, *, reverse, name):
    h, s = xv.shape
    t = _pick(s, (512, 256, 128))
    nb = s // t

    def blk(j):
        return (0, nb - 1 - j) if reverse else (0, j)

    def body(x_ref, o_ref, carry):
        @pl.when(pl.program_id(0) == 0)
        def _():
            carry[...] = jnp.zeros_like(carry)

        i0 = lax.broadcasted_iota(jnp.int32, (t, t), 0)
        i1 = lax.broadcasted_iota(jnp.int32, (t, t), 1)
        tri = ((i0 >= i1) if reverse else (i0 <= i1)).astype(F32)
        xb = x_ref[...]
        o_ref[...] = jnp.dot(xb, tri, precision=HIGHEST, preferred_element_type=F32) + carry[...]
        carry[...] += jnp.sum(xb, axis=1, keepdims=True)

    return pl.pallas_call(
        body, name=name, grid=(nb,),
        in_specs=[pl.BlockSpec((h, t), blk)], out_specs=pl.BlockSpec((h, t), blk),
        out_shape=jax.ShapeDtypeStruct((h, s), F32),
        scratch_shapes=[pltpu.VMEM((h, 1), F32)],
        compiler_params=_cparams(("arbitrary",)),
    )(xv)


def _fox_tiles(s):
    t = _pick(s, (512, 256, 128))
    return t, t


def _causal(sc):
    t = sc.shape[0]
    keep = lax.broadcasted_iota(jnp.int32, (t, t), 1) <= lax.broadcasted_iota(jnp.int32, (t, t), 0)
    return jnp.where(keep, sc, -jnp.inf)


ATTN_HEADS_PER_STEP = 2


def _head_cols(hh):
    return slice(hh * HEAD_DIM, (hh + 1) * HEAD_DIM)


def _across(rowstat, width):
    return jnp.tile(rowstat, (1, width // HEAD_DIM))


def _fox_fwd(zf, f_rep, f_row, *, heads, name, comm=()):
    s = zf.shape[0]
    tq, tk = _fox_tiles(s)
    assert tq == tk
    nq, nk = s // tq, s // tk
    hp = ATTN_HEADS_PER_STEP if heads % ATTN_HEADS_PER_STEP == 0 else 1
    wb = hp * HEAD_DIM
    scale = HEAD_DIM ** -0.5
    w = heads * HEAD_DIM

    def body(q_ref, k_ref, v_ref, fq_ref, fk_ref, o32_ref, o16_ref, lse_ref, m_sc, l_sc, acc_sc):
        i, j = pl.program_id(1), pl.program_id(2)

        @pl.when(j == 0)
        def _():
            m_sc[...] = jnp.full_like(m_sc, -jnp.inf)
            l_sc[...] = jnp.zeros_like(l_sc)
            acc_sc[...] = jnp.zeros_like(acc_sc)

        def step(diagonal):
            for hh in range(hp):
                cols = _head_cols(hh)
                sc = _dot(q_ref[:, cols], k_ref[:, cols], NT) * scale + _across(fq_ref[:, cols], tk) - fk_ref[hh]
                if diagonal:
                    sc = _causal(sc)
                m_old = m_sc[hh]
                m_new = jnp.maximum(m_old, jnp.max(sc, axis=-1, keepdims=True))
                alpha = jnp.exp(m_old - m_new)
                pr = jnp.exp(sc - _across(m_new, tk))
                l_sc[hh] = alpha * l_sc[hh] + jnp.sum(pr, axis=-1, keepdims=True)
                acc_sc[hh] = alpha * acc_sc[hh] + _dot(pr, v_ref[:, cols], NN)
                m_sc[hh] = m_new

        @pl.when(j < i)
        def _():
            step(False)

        @pl.when(j == i)
        def _():
            step(True)

        @pl.when(j == nk - 1)
        def _():
            for hh in range(hp):
                cols = _head_cols(hh)
                o = acc_sc[hh] / l_sc[hh]
                o32_ref[:, cols] = o
                o16_ref[:, cols] = o.astype(BF16)
                lse_ref[:, cols] = m_sc[hh] + jnp.log(l_sc[hh])

    def kv_blk(off):
        return lambda h, i, j: (jnp.minimum(j, i), off + h)

    o_spec = pl.BlockSpec((tq, wb), lambda h, i, j: (i, h))
    stat = pltpu.VMEM((hp, tq, HEAD_DIM), F32)
    return _call(
        body, name=name, grid=(heads // hp, nq, nk),
        in_specs=[
            o_spec,
            pl.BlockSpec((tk, wb), kv_blk(heads // hp)),
            pl.BlockSpec((tk, wb), kv_blk(2 * heads // hp)),
            o_spec,
            pl.BlockSpec((hp, 1, tk), lambda h, i, j: (h, 0, jnp.minimum(j, i))),
        ],
        out_specs=[o_spec, o_spec, o_spec],
        out_shape=[jax.ShapeDtypeStruct((s, w), F32), jax.ShapeDtypeStruct((s, w), BF16),
                   jax.ShapeDtypeStruct((s, w), F32)],
        scratch_shapes=[stat, stat, stat],
        semantics=("parallel", "parallel", "arbitrary"), operands=[zf, zf, zf, f_rep, f_row], comm=comm)


def _fox_bwd(zf, do, lse, delta, f_rep, f_row, *, heads, name, comm=()):
    s = zf.shape[0]
    tq, tk = _fox_tiles(s)
    assert tq == tk
    nq, nk = s // tq, s // tk
    hp = ATTN_HEADS_PER_STEP if heads % ATTN_HEADS_PER_STEP == 0 else 1
    wb = hp * HEAD_DIM
    scale = HEAD_DIM ** -0.5
    w = heads * HEAD_DIM

    def body(q_ref, k_ref, v_ref, do_ref, lse_ref, dl_ref, fq_ref, fk_ref, dq_ref, dk_ref, dv_ref, dfq_ref, dfk_ref):
        j, i = pl.program_id(1), pl.program_id(2)

        @pl.when((j == 0) & (i == 0))
        def _():
            dq_ref[...] = jnp.zeros_like(dq_ref)
            dfq_ref[...] = jnp.zeros_like(dfq_ref)

        @pl.when(i == 0)
        def _():
            dk_ref[...] = jnp.zeros_like(dk_ref)
            dv_ref[...] = jnp.zeros_like(dv_ref)
            dfk_ref[...] = jnp.zeros_like(dfk_ref)

        def step(diagonal):
            rows = pl.ds(pl.multiple_of(i * tq, tq), tq)
            for hh in range(hp):
                cols = _head_cols(hh)
                q, k, v = q_ref[:, cols], k_ref[:, cols], v_ref[:, cols]
                dob = do_ref[:, cols].astype(BF16)
                sc = _dot(q, k, NT) * scale + _across(fq_ref[:, cols], tk) - fk_ref[hh]
                if diagonal:
                    sc = _causal(sc)
                pr = jnp.exp(sc - _across(lse_ref[:, cols], tk))
                dv_ref[:, cols] += _dot(pr, dob, TN)
                dp = _dot(dob, v, NT)
                ds = pr * (dp - _across(dl_ref[:, cols], tk))
                dsb = ds.astype(BF16)
                dk_ref[:, cols] += _dot(dsb, q, TN) * scale
                dq_ref[rows, cols] += _dot(dsb, k, NN) * scale
                dfq_ref[rows, cols] += jnp.broadcast_to(jnp.sum(ds, axis=1, keepdims=True), (tq, HEAD_DIM))
                dfk_ref[hh] -= jnp.sum(ds, axis=0, keepdims=True)

        @pl.when(i > j)
        def _():
            step(False)

        @pl.when(i == j)
        def _():
            step(True)

    q_spec = pl.BlockSpec((tq, wb), lambda h, j, i: (jnp.maximum(i, j), h))
    k_spec = pl.BlockSpec((tk, wb), lambda h, j, i: (j, h))
    whole = pl.BlockSpec((s, wb), lambda h, j, i: (0, h))
    row_spec = pl.BlockSpec((hp, 1, tk), lambda h, j, i: (h, 0, j))
    return _call(
        body, name=name, grid=(heads // hp, nk, nq),
        in_specs=[
            q_spec,
            pl.BlockSpec((tk, wb), lambda h, j, i: (j, heads // hp + h)),
            pl.BlockSpec((tk, wb), lambda h, j, i: (j, 2 * heads // hp + h)),
            q_spec, q_spec, q_spec, q_spec, row_spec,
        ],
        out_specs=[whole, k_spec, k_spec, whole, row_spec],
        out_shape=[jax.ShapeDtypeStruct((s, w), F32), jax.ShapeDtypeStruct((s, w), F32),
                   jax.ShapeDtypeStruct((s, w), F32), jax.ShapeDtypeStruct((s, w), F32),
                   jax.ShapeDtypeStruct((heads, 1, s), F32)],
        scratch_shapes=[], semantics=("parallel", "arbitrary", "arbitrary"),
        operands=[zf, zf, zf, do, lse, delta, f_rep, f_row], comm=comm)


def _gla_rows(s):
    return _pick(s, (256, 128, 64))


def _gla_chunk_terms(la_c, tri):
    a_cum = jnp.dot(tri, la_c, precision=HIGHEST, preferred_element_type=F32)
    a_tot = jnp.sum(la_c, axis=0, keepdims=True)
    return jnp.exp(a_tot - a_cum), jnp.exp(a_tot)


def _gla_fwd(zr, la, *, heads, q_blk, k_blk, name):
    s = zr.shape[0]
    c = GLA_CHUNK
    rows = _gla_rows(s)
    nsteps, ncs = s // rows, rows // c
    scale = HEAD_DIM ** -0.5

    assert q_blk % heads == 0 and k_blk % heads == 0

    def body(q_ref, k_ref, v_ref, la_ref, o_ref, st_ref, state):
        @pl.when(pl.program_id(0) == 0)
        def _():
            state[...] = jnp.zeros_like(state)

        tri = (lax.broadcasted_iota(jnp.int32, (c, c), 0) >= lax.broadcasted_iota(jnp.int32, (c, c), 1)).astype(F32)
        for t in range(ncs):
            sl = slice(t * c, (t + 1) * c)
            for h in range(heads):
                kc, vc = _head_cols(h), slice(h * GLA_VAL_DIM, (h + 1) * GLA_VAL_DIM)
                dec, e_tot = _gla_chunk_terms(la_ref[sl, kc], tri)
                kd = k_ref[sl, kc] * dec
                st_ref[h, t] = state[h]
                new = state[h] * e_tot + _dot(v_ref[sl, vc], kd, TN)
                state[h] = new
                o_ref[sl, vc] = _dot(q_ref[sl, kc] * scale, new, NT)

    kw, vw = heads * HEAD_DIM, heads * GLA_VAL_DIM
    return pl.pallas_call(
        body, name=name, grid=(nsteps,),
        in_specs=[
            pl.BlockSpec((rows, kw), lambda i: (i, q_blk // heads)),
            pl.BlockSpec((rows, kw), lambda i: (i, k_blk // heads)),
            pl.BlockSpec((rows, vw), lambda i: (i, 0)),
            pl.BlockSpec((rows, kw), lambda i: (i, 0)),
        ],
        out_specs=[
            pl.BlockSpec((rows, vw), lambda i: (i, 0)),
            pl.BlockSpec((heads, ncs, GLA_VAL_DIM, HEAD_DIM), lambda i: (0, i, 0, 0)),
        ],
        out_shape=[jax.ShapeDtypeStruct((s, vw), F32),
                   jax.ShapeDtypeStruct((heads, s // c, GLA_VAL_DIM, HEAD_DIM), F32)],
        scratch_shapes=[pltpu.VMEM((heads, GLA_VAL_DIM, HEAD_DIM), F32)],
        compiler_params=_cparams(("arbitrary",)),
    )(zr, zr, zr, la)


def _gla_bwd(zr, la, do, states, *, heads, q_blk, k_blk, name):
    s = zr.shape[0]
    c = GLA_CHUNK
    rows = _gla_rows(s)
    nsteps, ncs = s // rows, rows // c
    scale = HEAD_DIM ** -0.5

    assert q_blk % heads == 0 and k_blk % heads == 0

    def body(q_ref, k_ref, v_ref, la_ref, do_ref, st_ref, dq_ref, dk_ref, dv_ref, dla_ref, dstate):
        @pl.when(pl.program_id(0) == 0)
        def _():
            dstate[...] = jnp.zeros_like(dstate)

        i0 = lax.broadcasted_iota(jnp.int32, (c, c), 0)
        i1 = lax.broadcasted_iota(jnp.int32, (c, c), 1)
        tri = (i0 >= i1).astype(F32)
        strict = (i0 > i1).astype(F32)
        for t in reversed(range(ncs)):
            sl = slice(t * c, (t + 1) * c)
            for h in range(heads):
                kc, vc = _head_cols(h), slice(h * GLA_VAL_DIM, (h + 1) * GLA_VAL_DIM)
                dec, e_tot = _gla_chunk_terms(la_ref[sl, kc], tri)
                kd = k_ref[sl, kc] * dec
                kdb = kd.astype(BF16)
                vb = v_ref[sl, vc].astype(BF16)
                dob = do_ref[sl, vc].astype(BF16)
                prev = st_ref[h, t]
                cur = prev * e_tot + _dot(vb, kdb, TN)
                d_cur = dstate[h] + _dot(dob, q_ref[sl, kc] * scale, TN)
                d_cur_b = d_cur.astype(BF16)
                dq_ref[sl, kc] = _dot(dob, cur, NN) * scale
                dv_ref[sl, vc] = _dot(kdb, d_cur_b, NT)
                dkd = _dot(vb, d_cur_b, NN)
                d_tot = e_tot * jnp.sum(d_cur * prev, axis=0, keepdims=True)
                dk_ref[sl, kc] = dkd * dec
                dla_ref[sl, kc] = d_tot + jnp.dot(strict, dkd * kd, precision=HIGHEST, preferred_element_type=F32)
                dstate[h] = d_cur * e_tot

    def rev(i):
        return nsteps - 1 - i

    kw, vw = heads * HEAD_DIM, heads * GLA_VAL_DIM
    kq_spec = pl.BlockSpec((rows, kw), lambda i: (rev(i), 0))
    v_spec = pl.BlockSpec((rows, vw), lambda i: (rev(i), 0))
    return pl.pallas_call(
        body, name=name, grid=(nsteps,),
        in_specs=[
            pl.BlockSpec((rows, kw), lambda i: (rev(i), q_blk // heads)),
            pl.BlockSpec((rows, kw), lambda i: (rev(i), k_blk // heads)),
            v_spec, kq_spec, v_spec,
            pl.BlockSpec((heads, ncs, GLA_VAL_DIM, HEAD_DIM), lambda i: (0, rev(i), 0, 0)),
        ],
        out_specs=[kq_spec, kq_spec, v_spec, kq_spec],
        out_shape=[jax.ShapeDtypeStruct((s, kw), F32), jax.ShapeDtypeStruct((s, kw), F32),
                   jax.ShapeDtypeStruct((s, vw), F32), jax.ShapeDtypeStruct((s, kw), F32)],
        scratch_shapes=[pltpu.VMEM((heads, GLA_VAL_DIM, HEAD_DIM), F32)],
        compiler_params=_cparams(("arbitrary",)),
    )(zr, zr, zr, la, do, states)


def _my_place():
    x, y, c = lax.axis_index("x"), lax.axis_index("y"), lax.axis_index("c")
    return x, y, c


def _gather_ops(x_ref, out_ref, send_sems, recv_sems, local_sem):
    def plan():
        x, y, c = _my_place()
        me, sibling = (x, y, c), (x, y, 1 - c)
        chips = [(1 - x, y), (x, 1 - y), (1 - x, 1 - y)]

        def blk(px, py, pc):
            return out_ref.at[4 * px + 2 * py + pc]

        def copy(k, block, to, src=None):
            return pltpu.make_async_remote_copy(
                src_ref=blk(*block) if src is None else src, dst_ref=blk(*block),
                send_sem=send_sems.at[k], recv_sem=recv_sems.at[k], device_id=to, device_id_type=MESH)

        mine = pltpu.make_async_copy(x_ref, blk(*me), local_sem)
        first = [copy(0, me, sibling, src=x_ref)]
        first += [copy(1 + j, me, (*chip, c), src=x_ref) for j, chip in enumerate(chips)]
        passed = [copy(4 + j, (*chip, c), sibling) for j, chip in enumerate(chips)]
        landed = [copy(1 + j, (*chip, c), me) for j, chip in enumerate(chips)]
        from_sibling = [copy(0, sibling, me)] + [copy(4 + j, (*chip, 1 - c), me) for j, chip in enumerate(chips)]
        return mine, first, passed, landed, from_sibling

    def start():
        mine, first, _, _, _ = plan()
        mine.start()
        for cp in first:
            cp.start()

    def finish():
        mine, first, passed, landed, from_sibling = plan()
        for cp, fwd in zip(landed, passed):
            cp.wait_recv()
            fwd.start()
        for cp in from_sibling:
            cp.wait_recv()
        for cp in first + passed:
            cp.wait_send()
        mine.wait()

    return start, finish


def _exchange_ops(scatter, s_ref, r_ref, send_sems, recv_sems, local_sem):
    def plan():
        x, y, c = _my_place()
        me = 4 * x + 2 * y + c
        mine = pltpu.make_async_copy(s_ref.at[me] if scatter else s_ref, r_ref.at[me], local_sem)
        sends, recvs = [], []
        for k in range(1, N_DEV):
            px, py, pc = x ^ ((k >> 2) & 1), y ^ ((k >> 1) & 1), c ^ (k & 1)
            peer = 4 * px + 2 * py + pc
            src = s_ref.at[peer] if scatter else s_ref
            for dst, out in ((r_ref.at[me], sends), (r_ref.at[peer], recvs)):
                out.append(pltpu.make_async_remote_copy(
                    src_ref=src, dst_ref=dst, send_sem=send_sems.at[k - 1], recv_sem=recv_sems.at[k - 1],
                    device_id=(px, py, pc), device_id_type=MESH))
        return mine, sends, recvs

    def start():
        mine, sends, _ = plan()
        mine.start()
        for cp in sends:
            cp.start()

    def finish():
        mine, sends, recvs = plan()
        for cp in recvs:
            cp.wait_recv()
        for cp in sends:
            cp.wait_send()
        mine.wait()

    return start, finish


def _pair_ops(s_ref, r_ref, send_sems, recv_sems):
    def plan():
        x, y, c = _my_place()
        return [pltpu.make_async_remote_copy(
            src_ref=s_ref.at[2 * q + 1 - c], dst_ref=r_ref.at[q], send_sem=send_sems.at[q], recv_sem=recv_sems.at[q],
            device_id=(x, y, 1 - c), device_id_type=MESH) for q in range(N_DEV // 2)]

    def start():
        for cp in plan():
            cp.start()

    def finish():
        copies = plan()
        for cp in copies:
            cp.wait_recv()
        for cp in copies:
            cp.wait_send()

    return start, finish


def _chips_ops(p_ref, r_ref, send_sems, recv_sems, local_sem):
    def plan():
        x, y, c = _my_place()
        chip = 2 * x + y
        mine = pltpu.make_async_copy(p_ref.at[chip], r_ref.at[chip], local_sem)
        sends, recvs = [], []
        for k in range(1, N_DEV // 2):
            px, py = x ^ (k >> 1), y ^ (k & 1)
            peer = 2 * px + py
            for dst, out in ((r_ref.at[chip], sends), (r_ref.at[peer], recvs)):
                out.append(pltpu.make_async_remote_copy(
                    src_ref=p_ref.at[peer], dst_ref=dst, send_sem=send_sems.at[k - 1], recv_sem=recv_sems.at[k - 1],
                    device_id=(px, py, c), device_id_type=MESH))
        return mine, sends, recvs

    def start():
        mine, sends, _ = plan()
        mine.start()
        for cp in sends:
            cp.start()

    def finish():
        mine, sends, recvs = plan()
        for cp in recvs:
            cp.wait_recv()
        for cp in sends:
            cp.wait_send()
        mine.wait()

    return start, finish


def _comm_ops(kind, src_ref, dst_ref, send_sems, recv_sems, local_sem):
    if kind == "gather":
        return _gather_ops(src_ref, dst_ref, send_sems, recv_sems, local_sem)
    if kind == "pair":
        return _pair_ops(src_ref, dst_ref, send_sems, recv_sems)
    if kind == "chips":
        return _chips_ops(src_ref, dst_ref, send_sems, recv_sems, local_sem)
    return _exchange_ops(False, src_ref, dst_ref, send_sems, recv_sems, local_sem)


def _comm_out_shape(kind, v):
    shape = {"pair": (N_DEV // 2,) + v.shape[1:], "chips": v.shape}.get(kind, (N_DEV,) + v.shape)
    return jax.ShapeDtypeStruct(shape, v.dtype)


def _comm_scratch(n_tasks):
    return [pltpu.SemaphoreType.DMA((N_DEV - 1,)), pltpu.SemaphoreType.DMA((N_DEV - 1,)), pltpu.SemaphoreType.DMA] * n_tasks


def _pair_sum(s, r, *, name):
    _, rows, cdim = s.shape
    tr = _pick(rows, (1024, 704, 512, 256, 128, 64, 32, 16, 8))
    rc = min(16, tr)

    def body(c_ref, s_ref, r_ref, o_ref):
        def chunk(ci, carry):
            sl = pl.ds(pl.multiple_of(ci * rc, rc), rc)
            o_ref[sl, :] = (s_ref[sl, :].astype(F32) + r_ref[sl, :].astype(F32)).astype(o_ref.dtype)
            return carry

        lax.fori_loop(0, tr // rc, chunk, 0)

    spec = pl.BlockSpec((None, tr, cdim), lambda q, i, c_ref: (q, i, 0))
    return pl.pallas_call(
        body, name=name,
        grid_spec=pltpu.PrefetchScalarGridSpec(
            num_scalar_prefetch=1, grid=(N_DEV // 2, rows // tr),
            in_specs=[pl.BlockSpec((None, None, tr, cdim), lambda q, i, c_ref: (q, c_ref[0], i, 0)), spec],
            out_specs=spec),
        out_shape=jax.ShapeDtypeStruct(r.shape, s.dtype),
        compiler_params=_cparams(("parallel", "parallel")),
    )(lax.axis_index("c").astype(jnp.int32).reshape(1), s.reshape(N_DEV // 2, 2, rows, cdim), r)


def _comm(kind, v, *, name):
    def body(s_ref, r_ref, send_sems, recv_sems, local_sem):
        start, finish = _comm_ops(kind, s_ref, r_ref, send_sems, recv_sems, local_sem)
        start()
        finish()

    return pl.pallas_call(
        body, name=name,
        out_shape=_comm_out_shape(kind, v),
        in_specs=[pl.BlockSpec(memory_space=pl.ANY)],
        out_specs=pl.BlockSpec(memory_space=pl.ANY),
        scratch_shapes=_comm_scratch(1),
    )(v)


class _Overlap:
    US_PER_MB = {"gather": 52.0, "pair": 1.0, "chips": 13.0, "bcast": 97.0}
    MM_FLOPS_PER_US = 6.0e8

    def __init__(self):
        self.queue, self.results, self.then = [], {}, {}

    def add(self, key, kind, v):
        self.queue.append((key, kind, v))

    def _cost(self, kind, v):
        return v.size * v.dtype.itemsize / 2 ** 20 * self.US_PER_MB[kind]

    def take(self, budget_us, first=2.0):
        taken, rest, cum = [], [], 0.0
        for task in self.queue:
            cost = self._cost(*task[1:])
            if (cum + cost <= 1.25 * budget_us) if taken else (cost <= first * budget_us):
                taken.append(task)
                cum += cost
            else:
                rest.append(task)
        self.queue = rest
        return taken

    def put(self, taken, res):
        for (key, _, _), r in zip(taken, res):
            self.results[key] = r
            if key in self.then:
                self.then.pop(key)(r)

    def carry(self, budget_us, fn, first=2.0):
        taken = self.take(budget_us, first)
        out, res = fn([(kind, v) for _, kind, v in taken])
        self.put(taken, res)
        return out

    def mm(self, a, b, *, mode, **kw):
        la, lb = _logical(a), _logical(b)
        budget = 2.0 * la[0] * la[1] * (lb[0] if mode == "nt" else lb[1]) / self.MM_FLOPS_PER_US

        def fn(comm):
            return _mm(a, b, mode=mode, comm=comm, **kw) if comm else (_mm(a, b, mode=mode, **kw), [])

        return self.carry(budget, fn)

    def get(self, key):
        while key not in self.results:
            keys = [k for k, _, _ in self.queue]
            task = self.queue.pop(keys.index(key if key in keys else "pair_" + key))
            self.put([task], [_comm(task[1], task[2], name=f"alone_{task[0]}")])
        return self.results[key]

    def weight(self, n):
        g = self.get(n)
        return g.reshape(-1, g.shape[2]) if n in ROW_SHARDED else g

    def grad(self, n, g):
        g = g if g.ndim == 3 else g.reshape(N_DEV, g.shape[0] // N_DEV, g.shape[1])
        self.add("pair_d_" + n, "pair", g)
        self.then["pair_d_" + n] = lambda r: self.add("d_" + n, "chips", _pair_sum(g, r, name=f"pair_sum_{n}"))


def _sel_tables(dest, ws, wp, tw):
    n_tiles = (int(dest.max()) + tw) // tw
    tbl = np.full((N_DEV, wp), -1, np.int32)
    for j in range(N_DEV):
        tbl[j, :ws] = dest[j * ws:(j + 1) * ws]
    by_tile = [sorted({j for j in range(N_DEV) if ((tbl[j] // tw) == t).any()}) for t in range(n_tiles)]
    by_shard = [sorted({int(t) for t in np.unique(tbl[j, :ws] // tw)}) for j in range(N_DEV)]

    def table(lists):
        width = max(len(v) for v in lists)
        idx = np.array([(v + [v[-1]] * width)[:width] if v else [0] * width for v in lists], np.int32)
        val = np.array([[1] * len(v) + [0] * (width - len(v)) for v in lists], np.int32)
        return idx.reshape(-1), val.reshape(-1), width

    return tbl[:, :, None], table(by_tile), table(by_shard)


def _sel_matrix(d_ref, t, wp, tw):
    cols = t * tw + lax.broadcasted_iota(jnp.int32, (wp, tw), 1)
    return (d_ref[...] == cols).astype(BF16)


def _win_unshard(g, tbl, idx, val, width, *, tw, padded, name):
    _, dm, wp = g.shape
    tm = _pick(dm, (1024, 512, 256, 128))

    def body(idx_ref, val_ref, g_ref, d_ref, o_ref, acc):
        t, s_ = pl.program_id(1), pl.program_id(2)

        @pl.when(s_ == 0)
        def _():
            acc[...] = jnp.zeros_like(acc)

        @pl.when(val_ref[t * width + s_] == 1)
        def _():
            acc[...] += _dot(g_ref[...], _sel_matrix(d_ref, t, wp, tw), NN)

        @pl.when(s_ == width - 1)
        def _():
            o_ref[...] = acc[...].astype(BF16)

    return pl.pallas_call(
        body, name=name,
        grid_spec=pltpu.PrefetchScalarGridSpec(
            num_scalar_prefetch=2, grid=(dm // tm, padded // tw, width),
            in_specs=[pl.BlockSpec((None, tm, wp), lambda i, t, s_, ix, vl: (ix[t * width + s_], i, 0)),
                      pl.BlockSpec((None, wp, 1), lambda i, t, s_, ix, vl: (ix[t * width + s_], 0, 0))],
            out_specs=pl.BlockSpec((tm, tw), lambda i, t, s_, ix, vl: (i, t)),
            scratch_shapes=[pltpu.VMEM((tm, tw), F32)]),
        out_shape=jax.ShapeDtypeStruct((dm, padded), BF16),
        compiler_params=_cparams(("parallel", "parallel", "arbitrary")),
    )(idx, val, g, tbl)


def _win_to_shards(dw, tbl, idx, val, width, *, tw, wp, name):
    dm = dw.shape[0]
    tm = _pick(dm, (1024, 512, 256, 128))

    def body(idx_ref, val_ref, w_ref, d_ref, o_ref, acc):
        j, s_ = pl.program_id(1), pl.program_id(2)

        @pl.when(s_ == 0)
        def _():
            acc[...] = jnp.zeros_like(acc)

        @pl.when(val_ref[j * width + s_] == 1)
        def _():
            acc[...] += _dot(w_ref[...], _sel_matrix(d_ref, idx_ref[j * width + s_], wp, tw), NT)

        @pl.when(s_ == width - 1)
        def _():
            o_ref[...] = acc[...].astype(BF16)

    return pl.pallas_call(
        body, name=name,
        grid_spec=pltpu.PrefetchScalarGridSpec(
            num_scalar_prefetch=2, grid=(dm // tm, N_DEV, width),
            in_specs=[pl.BlockSpec((tm, tw), lambda i, j, s_, ix, vl: (i, ix[j * width + s_])),
                      pl.BlockSpec((None, wp, 1), lambda i, j, s_, ix, vl: (j, 0, 0))],
            out_specs=pl.BlockSpec((None, tm, wp), lambda i, j, s_, ix, vl: (j, i, 0)),
            scratch_shapes=[pltpu.VMEM((tm, wp), F32)]),
        out_shape=jax.ShapeDtypeStruct((N_DEV, dm, wp), BF16),
        compiler_params=_cparams(("parallel", "parallel", "arbitrary")),
    )(idx, val, dw, tbl)


def _adamw(parts, w, m, v, *, name):
    r, cdim = w.shape
    n_parts = parts.shape[0]
    tr = _pick(r, (256, 128, 64, 32, 16, 8))
    rc = min(16, tr)
    c1 = 1.0 - ADAM_B1 ** ADAM_STEP
    c2 = 1.0 - ADAM_B2 ** ADAM_STEP

    def body(p_ref, w_ref, m_ref, v_ref, g_ref, d_ref, mo_ref, vo_ref):
        def chunk(ci, carry):
            sl = pl.ds(pl.multiple_of(ci * rc, rc), rc)
            g = p_ref[0, sl, :].astype(F32)
            for i in range(1, n_parts):
                g = g + p_ref[i, sl, :].astype(F32)
            mn = ADAM_B1 * m_ref[sl, :] + (1.0 - ADAM_B1) * g
            vn = ADAM_B2 * v_ref[sl, :] + (1.0 - ADAM_B2) * jnp.square(g)
            m_hat = mn / c1
            v_hat = vn / c2
            g_ref[sl, :] = g
            d_ref[sl, :] = -ADAM_LR * (m_hat / (jnp.sqrt(v_hat) + ADAM_EPS) + ADAM_WD * w_ref[sl, :])
            mo_ref[sl, :] = mn
            vo_ref[sl, :] = vn
            return carry

        lax.fori_loop(0, tr // rc, chunk, 0)

    spec = pl.BlockSpec((tr, cdim), lambda i: (i, 0))
    return pl.pallas_call(
        body, name=name, grid=(r // tr,),
        in_specs=[pl.BlockSpec((n_parts, tr, cdim), lambda i: (0, i, 0)), spec, spec, spec],
        out_specs=[spec] * 4,
        out_shape=[jax.ShapeDtypeStruct((r, cdim), F32)] * 4,
        compiler_params=_cparams(("parallel",)),
    )(parts, w, m, v)


def _pad_to(v, n):
    return v if v.shape[0] == n else jnp.concatenate([v, jnp.zeros((n - v.shape[0],), v.dtype)])


def _pad_cols(v, n):
    return v if v.shape[-1] == n else jnp.concatenate([v, jnp.zeros(v.shape[:-1] + (n - v.shape[-1],), v.dtype)], axis=-1)


def _pack(vs, cols, row_mult, dtype):
    offs, o = [], 0
    for v in vs:
        offs.append(o)
        o += v.size
    rows = -(-o // cols)
    rows = -(-rows // row_mult) * row_mult
    flat = jnp.concatenate([v.reshape(-1).astype(dtype) for v in vs])
    return _pad_to(flat, rows * cols).reshape(rows, cols), offs


SHARDED = ("ffn1_w_gate", "ffn1_w_up", "ffn1_w_down", "w_in", "w_merge_gate", "gla_gate_up", "w_branch_fox",
           "w_branch_gla", "w_out", "ffn2_w_gate", "ffn2_w_up", "ffn2_w_down", "w_ple_gate", "w_ple_proj")
ROW_SHARDED = ("ffn1_w_down", "w_out", "ffn2_w_down", "w_ple_gate")
REPLICATED = ("ffn1_norm", "mix_norm", "fox_forget_bias", "gla_gate_bias", "gla_head_norm", "b_merge_gate",
              "ffn2_norm", "ple_norm", "final_norm")
WEIGHTS = ("ffn1_norm", "ffn1_w_gate", "ffn1_w_up", "ffn1_w_down", "mix_norm", "w_in", "fox_forget_bias",
           "gla_gate_up", "gla_gate_bias", "gla_head_norm", "w_branch_fox", "w_branch_gla", "w_merge_gate",
           "b_merge_gate", "w_out", "ffn2_norm", "ffn2_w_gate", "ffn2_w_up", "ffn2_w_down", "ple_norm",
           "w_ple_proj", "w_ple_gate", "final_norm")


def kernel(x, p, ffn1_norm, ffn1_w_gate, ffn1_w_up, ffn1_w_down, mix_norm, w_in, fox_forget_bias, gla_gate_up, gla_gate_bias, gla_head_norm, w_branch_fox, w_branch_gla, w_merge_gate, b_merge_gate, w_out, ffn2_norm, ffn2_w_gate, ffn2_w_up, ffn2_w_down, ple_norm, w_ple_proj, w_ple_gate, final_norm, loss_target, m_ffn1_norm, m_ffn1_w_gate, m_ffn1_w_up, m_ffn1_w_down, m_mix_norm, m_w_in, m_fox_forget_bias, m_gla_gate_up, m_gla_gate_bias, m_gla_head_norm, m_w_branch_fox, m_w_branch_gla, m_w_merge_gate, m_b_merge_gate, m_w_out, m_ffn2_norm, m_ffn2_w_gate, m_ffn2_w_up, m_ffn2_w_down, m_ple_norm, m_w_ple_proj, m_w_ple_gate, m_final_norm, v_ffn1_norm, v_ffn1_w_gate, v_ffn1_w_up, v_ffn1_w_down, v_mix_norm, v_w_in, v_fox_forget_bias, v_gla_gate_up, v_gla_gate_bias, v_gla_head_norm, v_w_branch_fox, v_w_branch_gla, v_w_merge_gate, v_b_merge_gate, v_w_out, v_ffn2_norm, v_ffn2_w_gate, v_ffn2_w_up, v_ffn2_w_down, v_ple_norm, v_w_ple_proj, v_w_ple_gate, v_final_norm):
    args = dict(locals())
    wts = {n: args[n] for n in WEIGHTS}
    mom_m = {n: args["m_" + n] for n in WEIGHTS}
    mom_v = {n: args["v_" + n] for n in WEIGHTS}

    xs, ps, tgt = x[0], p[0, 0], loss_target[0]
    s, d = xs.shape
    fox_w = w_branch_fox.shape[1]
    gla_vw = w_branch_gla.shape[1]
    fox_heads = fox_w // HEAD_DIM
    gla_heads = gla_vw // GLA_VAL_DIM
    gla_kw = gla_heads * HEAD_DIM
    rank = gla_gate_up.shape[1]

    c_fl = 3 * fox_w
    o_gr, o_gq, o_gk = gla_vw, 2 * gla_vw, 2 * gla_vw + gla_kw
    o_fl = o_gk + gla_kw
    o_gd = o_fl + LANES
    rest_w = o_gd + LANES
    padded = c_fl + rest_w
    seg = [(c_fl, 0), (fox_heads, c_fl + o_fl), (gla_kw, c_fl + o_gq), (gla_kw, c_fl + o_gk), (gla_vw, c_fl),
           (gla_vw, c_fl + o_gr), (rank, c_fl + o_gd)]
    dest = np.concatenate([np.arange(w_, dtype=np.int32) + o_ for w_, o_ in seg])
    ws = w_in.shape[2]
    wp = -(-ws // LANES) * LANES
    tw = 256 if padded % 256 == 0 else LANES
    tbl, (t_idx, t_val, t_width), (s_idx, s_val, s_width) = _sel_tables(dest, ws, wp, tw)
    tbl = jnp.asarray(tbl)

    ov = _Overlap()
    for n in SHARDED:
        sh = wts[n][0].astype(BF16)
        ov.add(n, "gather", _pad_cols(sh, wp) if n == "w_in" else sh)
    fbias = _pad_cols(fox_forget_bias, LANES)
    bmg_f, bmg_g = b_merge_gate[:, :d], b_merge_gate[:, d:]
    ghn = jnp.tile(gla_head_norm, (1, gla_heads))

    h1, ffn1_saved = _ffn_fwd(xs, ffn1_norm, ov, tag="ffn1")
    u = _rms_fwd(h1, mix_norm, name="mix_norm")
    win = _win_unshard(ov.weight("w_in"), tbl, jnp.asarray(t_idx), jnp.asarray(t_val), t_width, tw=tw, padded=padded,
                       name="in_proj_unshard")
    win_fox, win_rest = win[:, :c_fl], win[:, c_fl:]
    zf = ov.mm(u, win_fox, mode="nn", name="in_proj_fox", out_dtype=BF16)
    zr = ov.mm(u, win_rest, mode="nn", name="in_proj_rest")
    gz = ov.mm(u, ov.weight("w_merge_gate"), mode="nn", name="merge_gate")
    gup = ov.weight("gla_gate_up").transpose(1, 0, 2).reshape(rank, gla_kw)
    gup = jnp.concatenate([gup, jnp.zeros((LANES - rank, gla_kw), BF16)], axis=0)

    log_f = _rowwise(lambda fl, b: _log_sigmoid(fl + b), [(zr, LANES, o_fl // LANES)], [fbias], [(LANES, F32)],
                     name="forget_gate")[0]
    f_cum = _cumsum(log_f[:, :fox_heads].T, reverse=False, name="forget_cumsum")
    f_rep = jnp.broadcast_to(f_cum.T[:, :, None], (s, fox_heads, HEAD_DIM)).reshape(s, fox_w)
    f_row = f_cum[:, None, :]
    attn_us = 2.0 * s * s * HEAD_DIM * fox_heads / _Overlap.MM_FLOPS_PER_US
    y_fox, y_fox_bf, lse = ov.carry(
        ATTN_FWD_MATMULS * attn_us, lambda comm: _fox_fwd(zf, f_rep, f_row, heads=fox_heads, name="fox_fwd", comm=comm))

    def decay_fn(gd, gupv, gb):
        return _log_sigmoid(_dot(gd, gupv, NN) + gb) * (1.0 / GLA_GATE_TAU)

    la = _rowwise(decay_fn, [(zr, LANES, o_gd // LANES)], [gup, gla_gate_bias], [(gla_kw, F32)], name="gla_decay", rc=128)[0]
    q_blk, k_blk = o_gq // HEAD_DIM, o_gk // HEAD_DIM
    o_gla, states = _gla_fwd(zr, la, heads=gla_heads, q_blk=q_blk, k_blk=k_blk, name="gla_fwd")

    def gla_out_fn(o, gr, g):
        outs = []
        for hh in range(gla_heads):
            sl = slice(hh * GLA_VAL_DIM, (hh + 1) * GLA_VAL_DIM)
            _, oh = _rms_parts(o[:, sl])
            outs.append(oh * g[:, sl] * _silu_parts(gr[:, sl])[0])
        return jnp.concatenate(outs, axis=1)

    y_gla = _rowwise(gla_out_fn, [o_gla, (zr, gla_vw, o_gr // gla_vw)], [ghn], [(gla_vw, BF16)], name="gla_out")[0]
    br_f = ov.mm(y_fox_bf, ov.weight("w_branch_fox"), mode="nn", name="branch_fox")
    br_g = ov.mm(y_gla, ov.weight("w_branch_gla"), mode="nn", name="branch_gla")

    def merge_fn(zf_, zg_, bf_, bg_, b1, b2):
        return _sigmoid(zf_ + b1) * bf_ + _sigmoid(zg_ + b2) * bg_

    merged = _rowwise(merge_fn, [(gz, d, 0), (gz, d, 1), br_f, br_g], [bmg_f, bmg_g], [(d, BF16)], name="merge",
                      ov=ov)[0]
    h2 = ov.mm(merged, ov.weight("w_out"), mode="nn", name="out_proj", add=h1)
    h3, ffn2_saved = _ffn_fwd(h2, ffn2_norm, ov, tag="ffn2")
    n3 = _rms_fwd(h3, ple_norm, name="ple_norm")
    gl = ov.mm(n3, ov.weight("w_ple_gate"), mode="nn", name="ple_gate")
    pe = ov.mm(ps, ov.weight("w_ple_proj"), mode="nn", name="ple_proj")

    def head_fn(h3b, glb, peb, tb, gfin):
        pg = _sigmoid(glb)
        h4 = h3b + pg * peb
        r, xh = _rms_parts(h4)
        err = xh * gfin - tb
        dy = err * (1.0 / d)
        t = dy * gfin
        dh4 = r * (t - xh * jnp.mean(t * xh, axis=-1, keepdims=True))
        return dh4, dh4 * pg, dh4 * peb * pg * (1.0 - pg), _colsum(err * err), _colsum(dy * xh)

    dh4, dpe, dgl, loss_cols, d_final = _rowwise(
        head_fn, [h3, gl, pe, tgt], [final_norm.reshape(1, d)], [(d, F32), (d, BF16), (d, BF16)], [d, d], name="loss_head",
        ov=ov)
    loss = lax.psum(0.5 * jnp.sum(loss_cols) / d, AXES)

    grads = {"final_norm": d_final.reshape(d)}
    ov.grad("w_ple_proj", ov.mm(ps, dpe, mode="tn", name="ple_proj_dw", out_dtype=BF16, out_chunked=True))
    ov.grad("w_ple_gate", ov.mm(n3, dgl, mode="tn", name="ple_gate_dw", out_dtype=BF16))
    dn3 = ov.mm(dgl, ov.weight("w_ple_gate"), mode="nt", name="ple_gate_dx")
    dh3, dh3_bf, grads["ple_norm"] = _rms_bwd(h3, dn3, ple_norm, dh4, name="ple_norm_bwd", ov=ov)
    dh2, dh2_bf, grads["ffn2_norm"] = _ffn_bwd(h2, ffn2_norm, ov, ffn2_saved, dh3, dh3_bf, tag="ffn2")

    ov.grad("w_out", ov.mm(merged, dh2_bf, mode="tn", name="out_proj_dw", out_dtype=BF16))
    dmerged = ov.mm(dh2_bf, ov.weight("w_out"), mode="nt", name="out_proj_dx")

    def merge_bwd_fn(zf_, zg_, bf_, bg_, dm, b1, b2):
        sf, sg = _sigmoid(zf_ + b1), _sigmoid(zg_ + b2)
        dz = jnp.concatenate([dm * bf_ * sf * (1.0 - sf), dm * bg_ * sg * (1.0 - sg)], axis=1)
        return dm * sf, dm * sg, dz, _colsum(dz)

    dbr_f, dbr_g, dgz, grads["b_merge_gate"] = _rowwise(
        merge_bwd_fn, [(gz, d, 0), (gz, d, 1), br_f, br_g, dmerged], [bmg_f, bmg_g],
        [(d, BF16), (d, BF16), (2 * d, BF16)], [2 * d], name="merge_bwd", ov=ov)
    ov.grad("w_merge_gate", ov.mm(u, dgz, mode="tn", name="merge_gate_dw", out_dtype=BF16, out_chunked=True))
    ov.grad("w_branch_fox", ov.mm(y_fox_bf, dbr_f, mode="tn", name="branch_fox_dw", out_dtype=BF16, out_chunked=True))
    ov.grad("w_branch_gla", ov.mm(y_gla, dbr_g, mode="tn", name="branch_gla_dw", out_dtype=BF16, out_chunked=True))
    dy_fox = ov.mm(dbr_f, ov.weight("w_branch_fox"), mode="nt", name="branch_fox_dx")
    dy_gla = ov.mm(dbr_g, ov.weight("w_branch_gla"), mode="nt", name="branch_gla_dx")

    def gla_out_bwd_fn(o, gr, dy, g):
        dos, dgrs, dgs = [], [], []
        for hh in range(gla_heads):
            sl = slice(hh * GLA_VAL_DIM, (hh + 1) * GLA_VAL_DIM)
            r, oh = _rms_parts(o[:, sl])
            si, dsi = _silu_parts(gr[:, sl])
            don = dy[:, sl] * si
            dgrs.append(dy[:, sl] * oh * g[:, sl] * dsi)
            t = don * g[:, sl]
            dos.append(r * (t - oh * jnp.mean(t * oh, axis=-1, keepdims=True)))
            dgs.append(_colsum(don * oh))
        return jnp.concatenate(dos, axis=1), jnp.concatenate(dgrs, axis=1), jnp.concatenate(dgs, axis=1)

    do_gla, dgr, d_ghn = _rowwise(gla_out_bwd_fn, [o_gla, (zr, gla_vw, o_gr // gla_vw), dy_gla], [ghn],
                                  [(gla_vw, F32), (gla_vw, F32)], [gla_vw], name="gla_out_bwd")
    grads["gla_head_norm"] = d_ghn.reshape(gla_heads, GLA_VAL_DIM).sum(axis=0, keepdims=True)
    dgq, dgk, dgv, dla = _gla_bwd(zr, la, do_gla, states, heads=gla_heads, q_blk=q_blk, k_blk=k_blk, name="gla_bwd")

    def decay_bwd_fn(dl, gd, gupv, gb):
        pre = _dot(gd, gupv, NN) + gb
        dpre = dl * (1.0 / GLA_GATE_TAU) * _sigmoid(-pre)
        return _dot(dpre, gupv, NT), dpre, _colsum(dpre)

    dgd, dpre_bf, grads["gla_gate_bias"] = _rowwise(
        decay_bwd_fn, [dla, (zr, LANES, o_gd // LANES)], [gup, gla_gate_bias], [(LANES, F32), (gla_kw, BF16)], [gla_kw],
        name="gla_decay_bwd", rc=128)
    d_gup = ov.mm(zr[:, o_gd:o_gd + LANES], dpre_bf, mode="tn", name="gla_gate_up_dw")[:rank]
    ov.grad("gla_gate_up", d_gup.reshape(rank, N_DEV, gla_kw // N_DEV).transpose(1, 0, 2).astype(BF16))

    def delta_fn(dyv, yv):
        outs = []
        for hh in range(fox_heads):
            sl = slice(hh * HEAD_DIM, (hh + 1) * HEAD_DIM)
            outs.append(jnp.broadcast_to(jnp.sum(dyv[:, sl] * yv[:, sl], axis=-1, keepdims=True), (dyv.shape[0], HEAD_DIM)))
        return jnp.concatenate(outs, axis=1)

    delta = _rowwise(delta_fn, [dy_fox, y_fox], [], [(fox_w, F32)], name="fox_delta")[0]
    dfq, dfk, dfv, d_fcol, d_frow = ov.carry(
        ATTN_BWD_MATMULS * attn_us,
        lambda comm: _fox_bwd(zf, dy_fox, lse, delta, f_rep, f_row, heads=fox_heads, name="fox_bwd", comm=comm))
    d_fcum = d_fcol[:, ::HEAD_DIM].T + d_frow.reshape(fox_heads, s)
    d_logf = _cumsum(d_fcum, reverse=True, name="forget_cumsum_bwd")
    d_logf = _pad_cols(d_logf.T, LANES)

    def forget_bwd_fn(dl, fl, b):
        dfl_ = dl * _sigmoid(-(fl + b))
        return dfl_, _colsum(dfl_)

    dfl, d_fbias = _rowwise(forget_bwd_fn, [d_logf, (zr, LANES, o_fl // LANES)], [fbias], [(LANES, F32)], [LANES],
                            name="forget_gate_bwd")
    grads["fox_forget_bias"] = d_fbias[:, :fox_heads]

    dz = jnp.concatenate([dfq, dfk, dfv, dgv, dgr, dgq, dgk, dfl, dgd], axis=1).astype(BF16)
    dwin = ov.mm(u, dz, mode="tn", name="in_proj_dw", out_dtype=BF16)
    ov.grad("w_in", _win_to_shards(dwin, tbl, jnp.asarray(s_idx), jnp.asarray(s_val), s_width, tw=tw, wp=wp,
                                   name="in_proj_dw_shards"))
    du = ov.mm(dgz, ov.weight("w_merge_gate"), mode="nt", name="merge_gate_dx")
    du = ov.mm(dz, win, mode="nt", name="in_proj_dx", add=du)
    dh1, dh1_bf, grads["mix_norm"] = _rms_bwd(h1, du, mix_norm, dh2, name="mix_norm_bwd", ov=ov)
    dx, _, grads["ffn1_norm"] = _ffn_bwd(xs, ffn1_norm, ov, ffn1_saved, dh1, dh1_bf, tag="ffn1")

    outs = {}
    for n in SHARDED:
        parts = ov.get("d_" + n)
        state = [_pad_cols(t_[n][0], parts.shape[2]) for t_ in (wts, mom_m, mom_v)]
        res4 = _adamw(parts, *state, name=f"adamw_{n}")
        for kind, r_ in zip(("grad", "delta", "new_m", "new_v"), res4):
            outs[f"{kind}_{n}"] = r_[:, :wts[n].shape[2]][None]

    send_small, small_offs = _pack([grads[n] for n in REPLICATED], LANES, 8, F32)
    recv_small = _comm("bcast", send_small, name="exchange_replicated")
    w_sm, _ = _pack([wts[n] for n in REPLICATED], LANES, 8, F32)
    m_sm, _ = _pack([mom_m[n] for n in REPLICATED], LANES, 8, F32)
    v_sm, _ = _pack([mom_v[n] for n in REPLICATED], LANES, 8, F32)
    small = _adamw(recv_small, w_sm, m_sm, v_sm, name="adamw_replicated")
    for kind, buf in zip(("grad", "delta", "new_m", "new_v"), small):
        fs = buf.reshape(-1)
        for n, o in zip(REPLICATED, small_offs):
            outs[f"{kind}_{n}"] = fs[o:o + wts[n].size].reshape(wts[n].shape)

    res = [loss, dx[None]]
    for kind in ("grad", "delta", "new_m", "new_v"):
        res += [outs[f"{kind}_{n}"] for n in WEIGHTS]
    return tuple(res)
```
